```python
import math
import jax, jax.numpy as jnp
from jax import lax
import numpy as np

D_MODEL = 1024
BATCH = 4
SEQ = 4096
DEPTH = 2

GRID_W = 64
CTX_LEN = 256
RMS_EPS = 1e-6

SSD_HEAD_DIM = 64
SSD_WIDTH = D_MODEL
SSD_HEADS = SSD_WIDTH // SSD_HEAD_DIM
SSD_GROUPS = 2
SSD_STATE = 128
SSD_CONV = 3
SSD_CHUNK = 128
SSD_XBC = SSD_WIDTH + 2 * SSD_GROUPS * SSD_STATE

POOL_WINDOWS = (2, 4, 8, 16)
POOL_WIDTH = D_MODEL // 2
POOL_GROUP_DIM = POOL_WIDTH // len(POOL_WINDOWS)

NA_HEAD_DIM = 64
NA_WIDTH = D_MODEL // 2
NA_HEADS = NA_WIDTH // NA_HEAD_DIM
NA_KH = 8
NA_KW = 16

MIX_WIDTH = SSD_WIDTH + POOL_WIDTH + NA_WIDTH
IN_WIDTH = SSD_WIDTH + SSD_XBC + 2 * SSD_HEADS + POOL_WIDTH + 3 * NA_WIDTH

PEER_HEADS = 8
PEER_NKEYS = 128
PEER_EXPERTS = PEER_NKEYS * PEER_NKEYS
PEER_QDIM = 256
PEER_TOPK = 16
PEER_BLOCK = 128

kernel_name = 'hybrid_ssd_pool_natten_peer_dit'


def rmsnorm(x, g):
    x32 = x.astype(jnp.float32)
    ms = jnp.mean(x32 * x32, axis=-1, keepdims=True)
    return (x32 * lax.rsqrt(ms + RMS_EPS)).astype(x.dtype) * g


def project(h, shift, scale, g, w_in):
    u = rmsnorm(h, g) * (1 + scale) + shift
    p = u @ w_in
    sizes = (SSD_WIDTH, SSD_XBC, 2 * SSD_HEADS, POOL_WIDTH, NA_WIDTH, NA_WIDTH, NA_WIDTH)
    points = [int(v) for v in np.cumsum(sizes)[:-1]]
    return jnp.split(p, points, axis=-1)


def to_heads(t):
    return t.reshape(t.shape[:-1] + (NA_HEADS, NA_HEAD_DIM))


def dwconv(x, w, b):
    k, ch = w.shape
    y = lax.conv_general_dilated(x, w[:, None, :].astype(x.dtype), window_strides=(1,),
                                 padding=[(k // 2, k // 2)], dimension_numbers=('NWC', 'WIO', 'NWC'),
                                 feature_group_count=ch)
    return y + b


def segsum(a):
    t = a.shape[-1]
    cs = jnp.cumsum(a, axis=-1)
    diff = cs[..., :, None] - cs[..., None, :]
    mask = jnp.tril(jnp.ones((t, t), dtype=bool))
    return jnp.where(mask, diff, -jnp.inf)


def ssd_chunked(x, dt, a_head, bm, cm, init, with_y):
    b, l, h, p = x.shape
    n = bm.shape[-1]
    nc = l // SSD_CHUNK
    xd = (x * dt[..., None]).reshape(b, nc, SSD_CHUNK, h, p)
    a = jnp.moveaxis((dt * a_head).reshape(b, nc, SSD_CHUNK, h), -1, 1)
    bc = bm.reshape(b, nc, SSD_CHUNK, h, n)
    a_cs = jnp.cumsum(a, axis=-1)
    decay_to_end = jnp.exp(a_cs[..., -1:] - a_cs)
    chunk_states = jnp.einsum('bclhn,bhcl,bclhp->bchpn', bc, decay_to_end, xd)
    states = jnp.concatenate([init[:, None], chunk_states], axis=1)
    chunk_decay = jnp.exp(segsum(jnp.pad(a_cs[..., -1], ((0, 0), (0, 0), (1, 0)))))
    states = jnp.einsum('bhzc,bchpn->bzhpn', chunk_decay, states)
    final = states[:, -1]
    if not with_y:
        return None, final
    cc = cm.reshape(b, nc, SSD_CHUNK, h, n)
    lmat = jnp.exp(segsum(a))
    cb = jnp.einsum('bclhn,bcshn->bhcls', cc, bc)
    y_diag = jnp.einsum('bhcls,bcshp->bclhp', cb * lmat, xd)
    in_decay = jnp.exp(a_cs).transpose(0, 2, 3, 1)[..., None]
    y_off = jnp.einsum('bclhn,bchpn->bclhp', cc, states[:, :-1]) * in_decay
    return (y_diag + y_off).reshape(b, l, h, p), final


def ssd_mixer(z, xbc_raw, dt_raw, conv_w, conv_b, a_log, dt_bias, d_skip, norm_g, init_f, init_b, with_y):
    xbc = jax.nn.silu(dwconv(xbc_raw, conv_w, conv_b))
    b, l, _ = xbc.shape
    gs = SSD_GROUPS * SSD_STATE
    xs = xbc[..., :SSD_WIDTH].reshape(b, l, SSD_HEADS, SSD_HEAD_DIM).astype(jnp.float32)
    hpg = SSD_HEADS // SSD_GROUPS
    bm = jnp.repeat(xbc[..., SSD_WIDTH:SSD_WIDTH + gs].reshape(b, l, SSD_GROUPS, SSD_STATE), hpg, axis=2).astype(jnp.float32)
    cm = jnp.repeat(xbc[..., SSD_WIDTH + gs:].reshape(b, l, SSD_GROUPS, SSD_STATE), hpg, axis=2).astype(jnp.float32)
    dt = jax.nn.softplus(dt_raw.astype(jnp.float32).reshape(b, l, 2, SSD_HEADS) + dt_bias.astype(jnp.float32))
    a_head = -jnp.exp(a_log.astype(jnp.float32))
    if init_f is None:
        init_f = jnp.zeros((b, SSD_HEADS, SSD_HEAD_DIM, SSD_STATE), jnp.float32)
        init_b = init_f
    flip = lambda t: jnp.flip(t, axis=1)
    y_f, s_f = ssd_chunked(xs, dt[:, :, 0], a_head[0], bm, cm, init_f, with_y)
    y_b, s_b = ssd_chunked(flip(xs), flip(dt[:, :, 1]), a_head[1], flip(bm), flip(cm), init_b, with_y)
    if not with_y:
        return None, s_f, s_b
    y = y_f + flip(y_b) + d_skip.astype(jnp.float32)[:, None] * xs
    y = y.reshape(b, l, SSD_WIDTH) * jax.nn.silu(z.astype(jnp.float32))
    return rmsnorm(y, norm_g).astype(z.dtype), s_f, s_b


def pool_mixer(u, w_pool, scale):
    b, l, _ = u.shape
    t = jnp.arange(l)
    cs = jnp.pad(jnp.cumsum(u.astype(jnp.float32), axis=1), ((0, 0), (1, 0), (0, 0)))
    outs = []
    for gi, w in enumerate(POOL_WINDOWS):
        sl = slice(gi * POOL_GROUP_DIM, (gi + 1) * POOL_GROUP_DIM)
        lo = jnp.clip(t - w // 2, 0, l - 1)
        hi = jnp.clip(t + w - w // 2 - 1, 0, l - 1)
        seg = cs[..., sl]
        cnt = (hi - lo + 1).astype(jnp.float32)[:, None]
        mean = (jnp.take(seg, hi + 1, axis=1) - jnp.take(seg, lo, axis=1)) / cnt
        outs.append(mean - u[..., sl].astype(jnp.float32))
    pooled = jnp.stack(outs, axis=2)
    y = jnp.einsum('blgi,gio->blgo', pooled, w_pool).reshape(b, l, POOL_WIDTH)
    return (y * scale).astype(u.dtype)


def context_attention(q, k, v):
    b, l, h, d = q.shape
    s = jnp.einsum('bqhd,bkhd->bhqk', q * (d ** -0.5), k).astype(jnp.float32)
    p = jax.nn.softmax(s, axis=-1).astype(v.dtype)
    return jnp.einsum('bhqk,bkhd->bqhd', p, v).reshape(b, l, h * d)


def neighbourhood_attention(q, k, v, k_ctx, v_ctx, rpb):
    b, s, h, d = q.shape
    rows = s // GRID_W
    kh = min(NA_KH, rows)
    qg = (q * (d ** -0.5)).reshape(b, rows, GRID_W, h, d)
    kg = k.reshape(b, rows, GRID_W, h, d)
    vg = v.reshape(b, rows, GRID_W, h, d)
    row_start = jnp.clip(jnp.arange(rows) - kh // 2, 0, rows - kh)
    col = jnp.arange(GRID_W)
    col_start = jnp.clip(col - NA_KW // 2, 0, GRID_W - NA_KW)
    in_window = (col[None, :] >= col_start[:, None]) & (col[None, :] < col_start[:, None] + NA_KW)
    dc = jnp.clip(col[None, :] - col[:, None], -(NA_KW - 1), NA_KW - 1) + NA_KW - 1
    col_bias = jnp.where(in_window[None, None], rpb[:, :, dc].astype(jnp.float32), -jnp.inf)

    def row_block(args):
        q_row, r0, r = args
        k_blk = lax.dynamic_slice_in_dim(kg, r0, kh, axis=1)
        v_blk = lax.dynamic_slice_in_dim(vg, r0, kh, axis=1)
        dr = r0 + jnp.arange(kh) - r + NA_KH - 1
        bias = jnp.take(col_bias, dr, axis=1).transpose(0, 2, 1, 3)
        s_loc = jnp.einsum('bqhd,bkwhd->bhqkw', q_row, k_blk).astype(jnp.float32) + bias[None]
        s_ctx = jnp.einsum('bqhd,bchd->bhqc', q_row, k_ctx).astype(jnp.float32)
        s_all = jnp.concatenate([s_loc.reshape(b, h, GRID_W, kh * GRID_W), s_ctx], axis=-1)
        p = jax.nn.softmax(s_all, axis=-1).astype(v.dtype)
        p_loc = p[..., :kh * GRID_W].reshape(b, h, GRID_W, kh, GRID_W)
        return (jnp.einsum('bhqkw,bkwhd->bqhd', p_loc, v_blk)
                + jnp.einsum('bhqc,bchd->bqhd', p[..., kh * GRID_W:], v_ctx))

    out = lax.map(row_block, (jnp.moveaxis(qg, 1, 0), row_start, jnp.arange(rows)))
    return jnp.moveaxis(out, 0, 1).reshape(b, s, h * d)


def peer_ffn(u, w_q, sub_keys, u_tab, v_tab):
    b, l, dm = u.shape
    t = u.reshape(b * l, dm)
    n_tok = b * l
    q = (t @ w_q).reshape(n_tok, PEER_HEADS, 2, PEER_QDIM // 2)
    s = jnp.einsum('thid,hikd->thik', q, sub_keys).astype(jnp.float32)
    v1, i1 = lax.top_k(s[:, :, 0], PEER_TOPK)
    v2, i2 = lax.top_k(s[:, :, 1], PEER_TOPK)
    cand = (v1[..., :, None] + v2[..., None, :]).reshape(n_tok, PEER_HEADS, PEER_TOPK * PEER_TOPK)
    cand_idx = (i1[..., :, None] * PEER_NKEYS + i2[..., None, :]).reshape(n_tok, PEER_HEADS, PEER_TOPK * PEER_TOPK)
    top_s, pos = lax.top_k(cand, PEER_TOPK)
    idx = jnp.take_along_axis(cand_idx, pos, axis=-1).reshape(n_tok, PEER_HEADS * PEER_TOPK)
    g = jax.nn.softmax(top_s, axis=-1).astype(u.dtype).reshape(n_tok, PEER_HEADS * PEER_TOPK)
    nb = n_tok // PEER_BLOCK

    def block(args):
        xb, ib, gb = args
        act = jax.nn.gelu(jnp.einsum('tkd,td->tk', jnp.take(u_tab, ib, axis=0), xb), approximate=False)
        return jnp.einsum('tk,tkd->td', gb * act, jnp.take(v_tab, ib, axis=0))

    out = lax.map(block, (t.reshape(nb, PEER_BLOCK, dm), idx.reshape(nb, PEER_BLOCK, -1), g.reshape(nb, PEER_BLOCK, -1)))
    return out.reshape(b, l, dm)


def setup_inputs(seed: int = 0) -> dict:
    key = jax.random.key(seed)
    ks = jax.random.split(key, 24)
    L, D = DEPTH, D_MODEL
    nrm = lambda k, shape, s: jax.random.normal(k, shape, jnp.float32) * s
    dt0 = jnp.exp(jax.random.uniform(ks[9], (L, 2, SSD_HEADS), jnp.float32, math.log(1e-3), math.log(1e-1)))
    dt_bias = dt0 + jnp.log(-jnp.expm1(-dt0))
    return {
        'x': nrm(ks[0], (BATCH, SEQ, D), 1.0),
        'c': nrm(ks[1], (BATCH, D), 1.0),
        'ctx': nrm(ks[2], (BATCH, CTX_LEN, D), 1.0),
        'c_ctx': nrm(ks[3], (D,), 0.5),
        'ada_w': nrm(ks[4], (L, D, 6 * D), 0.5 * D ** -0.5),
        'ada_b': nrm(ks[5], (L, 6 * D), 0.01),
        'norm1_g': 1.0 + nrm(ks[6], (L, D), 0.01),
        'w_in': nrm(ks[7], (L, D, IN_WIDTH), D ** -0.5),
        'conv_w': nrm(ks[8], (L, SSD_CONV, SSD_XBC), SSD_CONV ** -0.5),
        'conv_b': nrm(ks[10], (L, SSD_XBC), 0.01),
        'a_log': jnp.log(jax.random.uniform(ks[11], (L, 2, SSD_HEADS), jnp.float32, 1.0, 16.0)),
        'dt_bias': dt_bias,
        'd_skip': 1.0 + nrm(ks[12], (L, SSD_HEADS), 0.1),
        'ssd_norm_g': 1.0 + nrm(ks[13], (L, SSD_WIDTH), 0.01),
        'pool_w': nrm(ks[14], (L, len(POOL_WINDOWS), POOL_GROUP_DIM, POOL_GROUP_DIM), POOL_GROUP_DIM ** -0.5),
        'pool_scale': 1.0 + nrm(ks[15], (L, POOL_WIDTH), 0.1),
        'na_rpb': nrm(ks[16], (L, NA_HEADS, 2 * NA_KH - 1, 2 * NA_KW - 1), 0.02),
        'w_out': nrm(ks[17], (L, MIX_WIDTH, D), MIX_WIDTH ** -0.5),
        'norm2_g': 1.0 + nrm(ks[18], (L, D), 0.01),
        'peer_wq': nrm(ks[19], (L, D, PEER_HEADS * PEER_QDIM), D ** -0.5),
        'peer_keys': nrm(ks[20], (L, PEER_HEADS, 2, PEER_NKEYS, PEER_QDIM // 2), (PEER_QDIM // 2) ** -0.5),
        'peer_u': nrm(ks[21], (L, PEER_EXPERTS, D), D ** -0.5),
        'peer_v': nrm(ks[22], (L, PEER_EXPERTS, D), PEER_TOPK ** -0.5),
        'final_g': 1.0 + nrm(ks[23], (D,), 0.01),
    }


def reference(x, c, ctx, c_ctx, ada_w, ada_b, norm1_g, w_in, conv_w, conv_b, a_log, dt_bias, d_skip,
              ssd_norm_g, pool_w, pool_scale, na_rpb, w_out, norm2_g, peer_wq, peer_keys, peer_u, peer_v, final_g):
    h, hc = x, ctx
    for i in range(DEPTH):
        need_ctx_out = i < DEPTH - 1
        mod = jax.nn.silu(c) @ ada_w[i] + ada_b[i]
        sh1, sc1, g1, sh2, sc2, g2 = jnp.split(mod[:, None, :], 6, axis=-1)
        mod_c = jax.nn.silu(c_ctx) @ ada_w[i] + ada_b[i]
        csh1, csc1, cg1, csh2, csc2, cg2 = jnp.split(mod_c, 6, axis=-1)
        ssd_args = (conv_w[i], conv_b[i], a_log[i], dt_bias[i], d_skip[i], ssd_norm_g[i])

        zc, xbcc, dtc, poolc, qc, kc, vc = project(hc, csh1, csc1, norm1_g[i], w_in[i])
        kc, vc = to_heads(kc), to_heads(vc)
        ssd_c, st_f, st_b = ssd_mixer(zc, xbcc, dtc, *ssd_args, None, None, need_ctx_out)
        if need_ctx_out:
            mix_c = jnp.concatenate([ssd_c, pool_mixer(poolc, pool_w[i], pool_scale[i]),
                                     context_attention(to_heads(qc), kc, vc)], axis=-1) @ w_out[i]
            hc = hc + cg1 * mix_c
            uc = rmsnorm(hc, norm2_g[i]) * (1 + csc2) + csh2
            hc = hc + cg2 * peer_ffn(uc, peer_wq[i], peer_keys[i], peer_u[i], peer_v[i])

        z, xbc, dtr, pl, q, k, v = project(h, sh1, sc1, norm1_g[i], w_in[i])
        ssd_l, _, _ = ssd_mixer(z, xbc, dtr, *ssd_args, st_f, st_b, True)
        na = neighbourhood_attention(to_heads(q), to_heads(k), to_heads(v), kc, vc, na_rpb[i])
        mix = jnp.concatenate([ssd_l, pool_mixer(pl, pool_w[i], pool_scale[i]), na], axis=-1) @ w_out[i]
        h = h + g1 * mix
        u = rmsnorm(h, norm2_g[i]) * (1 + sc2) + sh2
        h = h + g2 * peer_ffn(u, peer_wq[i], peer_keys[i], peer_u[i], peer_v[i])
    return rmsnorm(h, final_g)
```

```python
import functools
import math

import jax
import jax.numpy as jnp
from jax import lax
from jax.experimental import pallas as pl
from jax.experimental.pallas import tpu as pltpu

F32 = jnp.float32
BF16 = jnp.bfloat16
HIGHEST = lax.Precision.HIGHEST

D_MODEL = 1024
DEPTH = 2
GRID_W = 64
RMS_EPS = 1e-6

SSD_HEAD_DIM = 64
SSD_HEADS = 16
SSD_GROUPS = 2
SSD_STATE = 128
SSD_CHUNK = 128
SSD_WIDTH = SSD_HEADS * SSD_HEAD_DIM
SSD_XBC = SSD_WIDTH + 2 * SSD_GROUPS * SSD_STATE

POOL_WINDOWS = (2, 4, 8, 16)
POOL_GROUP_DIM = 128
POOL_WIDTH = POOL_GROUP_DIM * len(POOL_WINDOWS)
POOL_PAD = 8

NA_HEAD_DIM = 64
NA_HEADS = 8
NA_WIDTH = NA_HEADS * NA_HEAD_DIM
NA_KH = 8
NA_KW = 16

PEER_HEADS = 8
PEER_NKEYS = 128
PEER_TOPK = 16
PEER_SUB = 128

LANES = 128
VMEM_LIMIT_BYTES = 56 * 1024 * 1024

_PROJ_SEGS = (SSD_WIDTH, SSD_XBC, POOL_WIDTH, NA_WIDTH, NA_WIDTH, NA_WIDTH, LANES)
_PROJ_DTYPES = (F32, F32, F32, BF16, BF16, BF16, F32)


def _params(*sem):
    return pltpu.CompilerParams(dimension_semantics=sem, vmem_limit_bytes=VMEM_LIMIT_BYTES)


def _rms_mod(x, g, scale, shift):
    ms = jnp.mean(x * x, axis=-1, keepdims=True)
    return (x * lax.rsqrt(ms + RMS_EPS)) * g * (1.0 + scale) + shift


def _silu(x):
    return x * jax.nn.sigmoid(x)


def _softplus(x):
    return jnp.maximum(x, 0.0) + jnp.log1p(jnp.exp(-jnp.abs(x)))


def _dot_nt(a, b):
    return lax.dot_general(a, b, (((1,), (1,)), ((), ())), preferred_element_type=F32)


def _batch_map(block_rows, seq_len, n_rows):
    if n_rows == 1:
        return lambda i, *_: (0, 0, 0)
    return lambda i, *_: ((i * block_rows) // seq_len, 0, 0)


def _mod_kernel(c_ref, w_ref, b_ref, o_ref):
    s = _silu(c_ref[...])
    o_ref[0] = jnp.dot(s, w_ref[0], precision=HIGHEST, preferred_element_type=F32) + b_ref[0]


def _modulation(c8, ada_w, ada_b):
    depth, d, six_d = ada_w.shape
    tn = 1024
    return pl.pallas_call(
        _mod_kernel,
        grid=(depth, six_d // tn),
        in_specs=[
            pl.BlockSpec((8, d), lambda l, j: (0, 0)),
            pl.BlockSpec((1, d, tn), lambda l, j: (l, 0, j)),
            pl.BlockSpec((1, 1, tn), lambda l, j: (l, 0, j)),
        ],
        out_specs=pl.BlockSpec((1, 8, tn), lambda l, j: (l, 0, j)),
        out_shape=jax.ShapeDtypeStruct((depth, 8, six_d), F32),
        compiler_params=_params("parallel", "parallel"),
        name="adaln_mod",
    )(c8, ada_w, ada_b.reshape(depth, 1, six_d))


def _proj_kernel(h_ref, g_ref, sh_ref, sc_ref, w_ref, *out_refs):
    u = _rms_mod(h_ref[...], g_ref[...], sc_ref[0], sh_ref[0]).astype(BF16)
    off = 0
    for o_ref, width in zip(out_refs, _PROJ_SEGS):
        o_ref[...] = jnp.dot(u, w_ref[:, off:off + width], preferred_element_type=F32).astype(o_ref.dtype)
        off += width


def _project(h2d, seq_len, g, shift, scale, w_perm):
    n, d = h2d.shape
    t = 256
    total = sum(_PROJ_SEGS)
    nb = shift.shape[0]
    row_map = lambda i: (i, 0)
    return pl.pallas_call(
        _proj_kernel,
        grid=(n // t,),
        in_specs=[
            pl.BlockSpec((t, d), row_map),
            pl.BlockSpec((1, d), lambda i: (0, 0)),
            pl.BlockSpec((1, 1, d), _batch_map(t, seq_len, nb)),
            pl.BlockSpec((1, 1, d), _batch_map(t, seq_len, nb)),
            pl.BlockSpec((d, total), lambda i: (0, 0)),
        ],
        out_specs=[pl.BlockSpec((t, w), row_map) for w in _PROJ_SEGS],
        out_shape=[jax.ShapeDtypeStruct((n, w), dt) for w, dt in zip(_PROJ_SEGS, _PROJ_DTYPES)],
        compiler_params=_params("parallel"),
        name="in_proj",
    )(h2d, g, shift, scale, w_perm)


def _permute_w_in(w_in):
    o = 0
    z = w_in[:, o:o + SSD_WIDTH]; o += SSD_WIDTH
    xbc = w_in[:, o:o + SSD_XBC]; o += SSD_XBC
    dt = w_in[:, o:o + 2 * SSD_HEADS]; o += 2 * SSD_HEADS
    rest = w_in[:, o:]
    dt = jnp.pad(dt, ((0, 0), (0, LANES - 2 * SSD_HEADS)))
    return jnp.concatenate([z, xbc, rest, dt], axis=1).astype(BF16)


def _conv_kernel(x_ref, w_ref, b_ref, o_ref):
    x = x_ref[0]
    n = x.shape[0]
    row = lax.broadcasted_iota(jnp.int32, x.shape, 0)
    prev = jnp.where(row == 0, 0.0, pltpu.roll(x, 1, 0))
    nxt = jnp.where(row == n - 1, 0.0, pltpu.roll(x, n - 1, 0))
    y = prev * w_ref[0:1, :] + x * w_ref[1:2, :] + nxt * w_ref[2:3, :] + b_ref[...]
    o_ref[0] = _silu(y)


def _conv_silu(xbc3, conv_w, conv_b):
    b, l, c = xbc3.shape
    tc = 256
    return pl.pallas_call(
        _conv_kernel,
        grid=(b, c // tc),
        in_specs=[
            pl.BlockSpec((1, l, tc), lambda i, j: (i, 0, j)),
            pl.BlockSpec((3, tc), lambda i, j: (0, j)),
            pl.BlockSpec((1, tc), lambda i, j: (0, j)),
        ],
        out_specs=pl.BlockSpec((1, l, tc), lambda i, j: (i, 0, j)),
        out_shape=jax.ShapeDtypeStruct((b, l, c), F32),
        compiler_params=_params("parallel", "parallel"),
        name="dwconv_silu",
    )(xbc3, conv_w, conv_b.reshape(1, c))


def _ssd_kernel(reverse, xbc_ref, dt_ref, dtt_ref, dtb_row, dtb_col, alog_row, alog_col, init_ref,
                y_ref, fin_ref, state_scr):
    c = pl.program_id(1)
    q = SSD_CHUNK

    @pl.when(c == 0)
    def _():
        state_scr[...] = init_ref[0]

    col0 = SSD_HEADS if reverse else 0
    dt_l = _softplus(dt_ref[0] + dtb_row[...])
    a_l = dt_l * (-jnp.exp(alog_row[...]))
    dt_t = _softplus(dtt_ref[0] + dtb_col[...])
    a_t = dt_t * (-jnp.exp(alog_col[...]))
    row = lax.broadcasted_iota(jnp.int32, (q, q), 0)
    col = lax.broadcasted_iota(jnp.int32, (q, q), 1)
    lower = (row >= col).astype(F32)
    upper = (row <= col).astype(F32)
    cs_l = jnp.dot(lower, a_l, precision=HIGHEST, preferred_element_type=F32)
    cs_t = jnp.dot(a_t, upper, precision=HIGHEST, preferred_element_type=F32)
    tot_l = cs_l[q - 1:q, :]
    if reverse:
        p_l, p_t = cs_l - a_l, cs_t - a_t
        tri = row <= col
    else:
        p_l, p_t = cs_l, cs_t
        tri = row >= col

    xbc = xbc_ref[0]
    for g in range(SSD_GROUPS):
        b_g = xbc[:, SSD_WIDTH + g * SSD_STATE:SSD_WIDTH + (g + 1) * SSD_STATE]
        c_g = xbc[:, SSD_WIDTH + (SSD_GROUPS + g) * SSD_STATE:SSD_WIDTH + (SSD_GROUPS + g + 1) * SSD_STATE]
        c_bf = c_g.astype(BF16)
        cb = _dot_nt(c_bf, b_g.astype(BF16))
        bt_bf = b_g.T.astype(BF16)
        for hh in range(SSD_HEADS // SSD_GROUPS):
            h = g * (SSD_HEADS // SSD_GROUPS) + hh
            k = col0 + h
            pcol = p_l[:, k:k + 1]
            prow = p_t[k:k + 1, :]
            tot = tot_l[:, k:k + 1]
            seg = (prow - pcol) if reverse else (pcol - prow)
            lmat = jnp.exp(jnp.where(tri, seg, -jnp.inf))
            xd = xbc[:, h * SSD_HEAD_DIM:(h + 1) * SSD_HEAD_DIM] * dt_l[:, k:k + 1]
            y_diag = jnp.dot((cb * lmat).astype(BF16), xd.astype(BF16), preferred_element_type=F32)
            s_prev = state_scr[h]
            if reverse:
                in_decay, to_end = jnp.exp(tot - pcol), jnp.exp(pcol)
            else:
                in_decay, to_end = jnp.exp(pcol), jnp.exp(tot - pcol)
            y_off = jnp.dot(c_bf, s_prev.astype(BF16), preferred_element_type=F32) * in_decay
            state_scr[h] = jnp.exp(tot) * s_prev + jnp.dot(bt_bf, (xd * to_end).astype(BF16),
                                                           preferred_element_type=F32)
            y_ref[0, :, h * SSD_HEAD_DIM:(h + 1) * SSD_HEAD_DIM] = y_diag + y_off

    @pl.when(c == pl.num_programs(1) - 1)
    def _():
        fin_ref[0] = state_scr[...]


def _ssd_scan(xbc_act, dt3, dtt3, dtb, alog, init, reverse):
    b, l, _ = xbc_act.shape
    nc = l // SSD_CHUNK
    cmap = (lambda c: nc - 1 - c) if reverse else (lambda c: c)
    dtb_row = jnp.pad(dtb.reshape(1, -1), ((0, 0), (0, LANES - 2 * SSD_HEADS)))
    alog_row = jnp.pad(alog.reshape(1, -1), ((0, 0), (0, LANES - 2 * SSD_HEADS)))
    small = lambda shape: pl.BlockSpec(shape, lambda i, c: (0, 0))
    st_shape = (b, SSD_HEADS, SSD_STATE, SSD_HEAD_DIM)
    st_spec = pl.BlockSpec((1,) + st_shape[1:], lambda i, c: (i, 0, 0, 0))
    return pl.pallas_call(
        functools.partial(_ssd_kernel, reverse),
        grid=(b, nc),
        in_specs=[
            pl.BlockSpec((1, SSD_CHUNK, SSD_XBC), lambda i, c: (i, cmap(c), 0)),
            pl.BlockSpec((1, SSD_CHUNK, LANES), lambda i, c: (i, cmap(c), 0)),
            pl.BlockSpec((1, 2 * SSD_HEADS, SSD_CHUNK), lambda i, c: (i, 0, cmap(c))),
            small((1, LANES)), small((2 * SSD_HEADS, 1)), small((1, LANES)), small((2 * SSD_HEADS, 1)),
            st_spec,
        ],
        out_specs=[pl.BlockSpec((1, SSD_CHUNK, SSD_WIDTH), lambda i, c: (i, cmap(c), 0)), st_spec],
        out_shape=[jax.ShapeDtypeStruct((b, l, SSD_WIDTH), F32), jax.ShapeDtypeStruct(st_shape, F32)],
        scratch_shapes=[pltpu.VMEM(st_shape[1:], F32)],
        compiler_params=_params("parallel", "arbitrary"),
        name="ssd_scan_bwd" if reverse else "ssd_scan_fwd",
    )(xbc_act, dt3, dtt3, dtb_row, dtb.reshape(-1, 1), alog_row, alog.reshape(-1, 1), init)


def _pool_kernel(x_ref, w_ref, sc_ref, o_ref, pad_scr):
    n = x_ref.shape[1]
    zeros = jnp.zeros((POOL_PAD, POOL_GROUP_DIM), F32)
    pad_scr[0:POOL_PAD, :] = zeros
    pad_scr[n + POOL_PAD:n + 2 * POOL_PAD, :] = zeros
    t = lax.broadcasted_iota(jnp.int32, (n, 1), 0)
    for gi, w in enumerate(POOL_WINDOWS):
        sl = slice(gi * POOL_GROUP_DIM, (gi + 1) * POOL_GROUP_DIM)
        x = x_ref[0, :, sl]
        pad_scr[POOL_PAD:n + POOL_PAD, :] = x
        acc = jnp.zeros_like(x)
        for o in range(-(w // 2), w - w // 2):
            acc = acc + pad_scr[POOL_PAD + o:POOL_PAD + o + n, :]
        lo = jnp.maximum(t - w // 2, 0)
        hi = jnp.minimum(t + (w - w // 2 - 1), n - 1)
        pooled = acc / (hi - lo + 1).astype(F32) - x
        y = jnp.dot(pooled.astype(BF16), w_ref[gi].astype(BF16), preferred_element_type=F32)
        o_ref[0, :, sl] = y * sc_ref[:, sl]


def _pool_mixer(u3, w_pool, scale):
    b, l, c = u3.shape
    return pl.pallas_call(
        _pool_kernel,
        grid=(b,),
        in_specs=[
            pl.BlockSpec((1, l, c), lambda i: (i, 0, 0)),
            pl.BlockSpec(w_pool.shape, lambda i: (0, 0, 0)),
            pl.BlockSpec((1, c), lambda i: (0, 0)),
        ],
        out_specs=pl.BlockSpec((1, l, c), lambda i: (i, 0, 0)),
        out_shape=jax.ShapeDtypeStruct((b, l, c), F32),
        scratch_shapes=[pltpu.VMEM((l + 2 * POOL_PAD, POOL_GROUP_DIM), F32)],
        compiler_params=_params("parallel"),
        name="pool_mixer",
    )(u3, w_pool, scale.reshape(1, c))


def _na_bias_kernel(rpb_ref, o_ref):
    h = pl.program_id(0)
    dr = pl.program_id(1)
    qi = lax.broadcasted_iota(jnp.int32, (GRID_W, LANES), 0)
    lane = lax.broadcasted_iota(jnp.int32, (GRID_W, LANES), 1)
    ki = lane % GRID_W
    second = lane >= GRID_W
    start = jnp.clip(qi - NA_KW // 2, 0, GRID_W - NA_KW)
    in_window = (ki >= start) & (ki < start + NA_KW)
    dc = jnp.clip(ki - qi, -(NA_KW - 1), NA_KW - 1) + NA_KW - 1
    val = jnp.zeros((GRID_W, LANES), F32)
    for j in range(2 * NA_KW - 1):
        pick = jnp.where(second, rpb_ref[h, dr + 1, j], rpb_ref[h, dr, j])
        val = jnp.where(dc == j, pick, val)
    o_ref[0, 0] = jnp.where(in_window, val, -jnp.inf)


def _na_bias(rpb):
    nh, ndr, ndc = rpb.shape
    return pl.pallas_call(
        _na_bias_kernel,
        grid=(nh, ndr - 1),
        in_specs=[pl.BlockSpec(memory_space=pltpu.SMEM)],
        out_specs=pl.BlockSpec((1, 1, GRID_W, LANES), lambda h, d: (h, d, 0, 0)),
        out_shape=jax.ShapeDtypeStruct((nh, ndr - 1, GRID_W, LANES), F32),
        compiler_params=_params("parallel", "parallel"),
        name="na_bias",
    )(rpb)


def _na_kernel(kh, q_ref, k_ref, v_ref, kc_ref, vc_ref, bias_ref, o_ref):
    r = pl.program_id(1)
    rows = pl.num_programs(1)
    r0 = jnp.clip(r - kh // 2, 0, rows - kh)
    start = pl.multiple_of(r0 * GRID_W, GRID_W)
    kblk = k_ref[0, pl.ds(start, kh * GRID_W), :]
    vblk = v_ref[0, pl.ds(start, kh * GRID_W), :]
    dr0 = r0 - r + NA_KH - 1
    scale = NA_HEAD_DIM ** -0.5
    for h in range(NA_HEADS):
        sl = slice(h * NA_HEAD_DIM, (h + 1) * NA_HEAD_DIM)
        qh = q_ref[0, :, sl] * scale
        s_loc = _dot_nt(qh, kblk[:, sl])
        bias = jnp.concatenate([bias_ref[h, dr0 + 2 * j] for j in range(kh // 2)], axis=1)
        s_loc = s_loc + bias
        s_ctx = _dot_nt(qh, kc_ref[0, :, sl])
        m = jnp.maximum(jnp.max(s_loc, axis=-1, keepdims=True), jnp.max(s_ctx, axis=-1, keepdims=True))
        p_loc = jnp.exp(s_loc - m)
        p_ctx = jnp.exp(s_ctx - m)
        den = jnp.sum(p_loc, axis=-1, keepdims=True) + jnp.sum(p_ctx, axis=-1, keepdims=True)
        acc = jnp.dot(p_loc.astype(BF16), vblk[:, sl], preferred_element_type=F32)
        acc = acc + jnp.dot(p_ctx.astype(BF16), vc_ref[0, :, sl], preferred_element_type=F32)
        o_ref[0, :, sl] = acc / den


def _neighbourhood_attention(q3, k3, v3, kc3, vc3, bias):
    b, s, c = q3.shape
    rows = s // GRID_W
    kh = min(NA_KH, rows)
    lc = kc3.shape[1]
    full = lambda n: pl.BlockSpec((1, n, c), lambda i, r: (i, 0, 0))
    return pl.pallas_call(
        functools.partial(_na_kernel, kh),
        grid=(b, rows),
        in_specs=[
            pl.BlockSpec((1, GRID_W, c), lambda i, r: (i, r, 0)),
            full(s), full(s), full(lc), full(lc),
            pl.BlockSpec(bias.shape, lambda i, r: (0, 0, 0, 0)),
        ],
        out_specs=pl.BlockSpec((1, GRID_W, c), lambda i, r: (i, r, 0)),
        out_shape=jax.ShapeDtypeStruct((b, s, c), F32),
        compiler_params=_params("parallel", "arbitrary"),
        name="na_attention",
    )(q3, k3, v3, kc3, vc3, bias)


def _ctx_attn_kernel(q_ref, k_ref, v_ref, o_ref):
    scale = NA_HEAD_DIM ** -0.5
    for h in range(NA_HEADS):
        sl = slice(h * NA_HEAD_DIM, (h + 1) * NA_HEAD_DIM)
        s = _dot_nt(q_ref[0, :, sl] * scale, k_ref[0, :, sl])
        p = jnp.exp(s - jnp.max(s, axis=-1, keepdims=True))
        den = jnp.sum(p, axis=-1, keepdims=True)
        o_ref[0, :, sl] = jnp.dot(p.astype(BF16), v_ref[0, :, sl], preferred_element_type=F32) / den


def _context_attention(q3, k3, v3):
    b, l, c = q3.shape
    spec = pl.BlockSpec((1, l, c), lambda i: (i, 0, 0))
    return pl.pallas_call(
        _ctx_attn_kernel,
        grid=(b,),
        in_specs=[spec, spec, spec],
        out_specs=spec,
        out_shape=jax.ShapeDtypeStruct((b, l, c), F32),
        compiler_params=_params("parallel"),
        name="ctx_attention",
    )(q3, k3, v3)


def _mix_out_kernel(yf_ref, yb_ref, xs_ref, z_ref, pool_ref, na_ref, h_ref, dsk_ref, ng_ref, g1_ref, w_ref, o_ref):
    y = yf_ref[...] + yb_ref[...] + dsk_ref[...] * xs_ref[...]
    y = y * _silu(z_ref[...])
    ms = jnp.mean(y * y, axis=-1, keepdims=True)
    yn = (y * lax.rsqrt(ms + RMS_EPS)) * ng_ref[...]
    mix = jnp.dot(yn.astype(BF16), w_ref[0:SSD_WIDTH, :], preferred_element_type=F32)
    mix = mix + jnp.dot(pool_ref[...].astype(BF16), w_ref[SSD_WIDTH:SSD_WIDTH + POOL_WIDTH, :],
                        preferred_element_type=F32)
    mix = mix + jnp.dot(na_ref[...].astype(BF16), w_ref[SSD_WIDTH + POOL_WIDTH:, :], preferred_element_type=F32)
    o_ref[...] = h_ref[...] + g1_ref[0] * mix


def _mix_out(yf, yb, xbc_act2d, z, pool_y, na_y, h2d, seq_len, d_skip, norm_g, g1, w_out_bf):
    n, d = h2d.shape
    t = 256
    row = lambda w: pl.BlockSpec((t, w), lambda i: (i, 0))
    vec = lambda w: pl.BlockSpec((1, w), lambda i: (0, 0))
    return pl.pallas_call(
        _mix_out_kernel,
        grid=(n // t,),
        in_specs=[
            row(SSD_WIDTH), row(SSD_WIDTH), row(SSD_WIDTH), row(SSD_WIDTH), row(POOL_WIDTH), row(NA_WIDTH), row(d),
            vec(SSD_WIDTH), vec(SSD_WIDTH),
            pl.BlockSpec((1, 1, d), _batch_map(t, seq_len, g1.shape[0])),
            pl.BlockSpec(w_out_bf.shape, lambda i: (0, 0)),
        ],
        out_specs=row(d),
        out_shape=jax.ShapeDtypeStruct((n, d), F32),
        compiler_params=_params("parallel"),
        name="mix_out",
    )(yf, yb, xbc_act2d, z, pool_y, na_y, h2d, jnp.repeat(d_skip, SSD_HEAD_DIM).reshape(1, -1),
      norm_g.reshape(1, -1), g1, w_out_bf)


_CAND_ROWS = 16 + 8 * 7 + 8


def _extract_top16(s):
    n, t = s.shape
    iota = lax.broadcasted_iota(jnp.int32, (n, t), 0).astype(F32)
    row16 = lax.broadcasted_iota(jnp.int32, (PEER_TOPK, t), 0)

    def body(j, carry):
        work, rank, vals = carry
        m = jnp.max(work, axis=0, keepdims=True)
        idx = jnp.min(jnp.where(work == m, iota, float(n)), axis=0, keepdims=True)
        sel = iota == idx
        rank = jnp.where(sel, j.astype(F32), rank)
        work = jnp.where(sel, -jnp.inf, work)
        vals = jnp.where(row16 == j, m, vals)
        return work, rank, vals

    init = (s, jnp.full((n, t), float(PEER_TOPK), F32), jnp.zeros((PEER_TOPK, t), F32))
    _, rank, vals = lax.fori_loop(0, PEER_TOPK, body, init)
    return rank, vals


def _peer_score_kernel(h_ref, g_ref, sh_ref, sc_ref, wq_ref, keys_ref,
                       x_out, cnt_out, e1_out, rk_out, e2_out, u_scr):
    hd = pl.program_id(1)

    @pl.when(hd == 0)
    def _():
        u = _rms_mod(h_ref[...], g_ref[...], sc_ref[0], sh_ref[0]).astype(BF16)
        u_scr[...] = u
        x_out[...] = u

    q = jnp.dot(u_scr[...], wq_ref[0], preferred_element_type=F32).astype(BF16)
    s1 = _dot_nt(keys_ref[0, 0], q[:, :PEER_SUB])
    s2 = _dot_nt(keys_ref[0, 1], q[:, PEER_SUB:])
    rank1, v1 = _extract_top16(s1)
    rank2, v2 = _extract_top16(s2)

    blocks = [v1[0:1] + v2]
    for j in range(1, 8):
        blocks.append(v1[j:j + 1] + v2[0:8])
    blocks.append(v1[8:16] + v2[0:1])
    cand = jnp.concatenate(blocks, axis=0)
    t = cand.shape[1]
    iota = lax.broadcasted_iota(jnp.int32, cand.shape, 0).astype(F32)

    def body(_, carry):
        work, sel_acc = carry
        m = jnp.max(work, axis=0, keepdims=True)
        idx = jnp.min(jnp.where(work == m, iota, float(_CAND_ROWS)), axis=0, keepdims=True)
        sel = iota == idx
        return jnp.where(sel, -jnp.inf, work), jnp.where(sel, 1.0, sel_acc)

    _, sel = lax.fori_loop(0, PEER_TOPK, body, (cand, jnp.zeros(cand.shape, F32)))
    z = jnp.sum(sel * jnp.exp(cand - cand[0:1]), axis=0, keepdims=True)

    cnt = jnp.zeros(rank1.shape, F32)
    for j in range(8):
        lo = 0 if j == 0 else 16 + 8 * (j - 1)
        n_j = jnp.sum(sel[lo:lo + (16 if j == 0 else 8)], axis=0, keepdims=True)
        cnt = cnt + jnp.where(rank1 == float(j), n_j, 0.0)
    for j in range(8, 16):
        cnt = cnt + jnp.where(rank1 == float(j), sel[64 + j:65 + j], 0.0)
    cnt_out[0] = cnt
    e1_out[0] = jnp.exp(s1 - v1[0:1])
    rk_out[0] = rank2
    e2_out[0] = jnp.exp(s2 - v2[0:1]) / z


def _peer_scores(h2d, seq_len, g, shift, scale, wq_heads, keys_bf):
    n, d = h2d.shape
    t = 256
    nb = shift.shape[0]
    bm = _batch_map(t, seq_len, nb)
    mod_spec = pl.BlockSpec((1, 1, d), lambda i, hd: bm(i))
    head_out = pl.BlockSpec((1, PEER_NKEYS, t), lambda i, hd: (hd, 0, i))
    head_shape = jax.ShapeDtypeStruct((PEER_HEADS, PEER_NKEYS, n), F32)
    return pl.pallas_call(
        _peer_score_kernel,
        grid=(n // t, PEER_HEADS),
        in_specs=[
            pl.BlockSpec((t, d), lambda i, hd: (i, 0)),
            pl.BlockSpec((1, d), lambda i, hd: (0, 0)),
            mod_spec, mod_spec,
            pl.BlockSpec((1, d, 2 * PEER_SUB), lambda i, hd: (hd, 0, 0)),
            pl.BlockSpec((1, 2, PEER_NKEYS, PEER_SUB), lambda i, hd: (hd, 0, 0, 0)),
        ],
        out_specs=[pl.BlockSpec((t, d), lambda i, hd: (i, 0)), head_out, head_out, head_out, head_out],
        out_shape=[jax.ShapeDtypeStruct((n, d), BF16), head_shape, head_shape, head_shape, head_shape],
        scratch_shapes=[pltpu.VMEM((t, d), BF16)],
        compiler_params=_params("parallel", "arbitrary"),
        name="peer_scores",
    )(h2d, g, shift, scale, wq_heads, keys_bf)


_PEER_EC = 1024
_INV_SQRT2 = 1.0 / math.sqrt(2.0)


def _peer_expert_kernel(x_ref, u_ref, vt_ref, cnt_ref, e1_ref, rk_ref, e2_ref, h_ref, g2_ref, o_ref,
                        ht_scr, g_scr, acc_scr):
    c = pl.program_id(1)
    t = x_ref.shape[0]

    @pl.when(c == 0)
    def _():
        acc_scr[...] = jnp.zeros_like(acc_scr)

    ht_scr[...] = _dot_nt(u_ref[...], x_ref[...])

    def a_body(ai, carry):
        row0 = pl.multiple_of(ai * PEER_NKEYS, PEER_NKEYS)
        for tc in range(t // LANES):
            ls = slice(tc * LANES, (tc + 1) * LANES)
            w = jnp.zeros((PEER_NKEYS, LANES), F32)
            for hd in range(PEER_HEADS):
                cn = cnt_ref[hd, ai, :, ls]
                e1 = e1_ref[hd, ai, :, ls]
                w = w + jnp.where(rk_ref[hd, :, ls] < cn, e2_ref[hd, :, ls], 0.0) * e1
            hs = ht_scr[pl.ds(row0, PEER_NKEYS), ls]
            act = 0.5 * hs * (1.0 + lax.erf(hs * _INV_SQRT2))
            g_scr[pl.ds(row0, PEER_NKEYS), ls] = (w * act).astype(BF16)
        return carry

    lax.fori_loop(0, _PEER_EC // PEER_NKEYS, a_body, 0)
    acc_scr[...] += jnp.dot(vt_ref[...], g_scr[...], preferred_element_type=F32)

    @pl.when(c == pl.num_programs(1) - 1)
    def _():
        o_ref[...] = h_ref[...] + g2_ref[0] * acc_scr[...].T


def _peer_experts(x_bf, u_bf, vt_bf, cnt, e1, rk, e2, h2d, seq_len, g2):
    n, d = h2d.shape
    t = 512
    n_exp = u_bf.shape[0]
    ea = _PEER_EC // PEER_NKEYS
    tok = pl.BlockSpec((PEER_HEADS, PEER_NKEYS, t), lambda i, c: (0, 0, i))
    first = pl.BlockSpec((PEER_HEADS, ea, 1, t), lambda i, c: (0, c, 0, i))
    per_first = lambda a: a.reshape(PEER_HEADS, PEER_NKEYS, 1, n)
    bm = _batch_map(t, seq_len, g2.shape[0])
    return pl.pallas_call(
        _peer_expert_kernel,
        grid=(n // t, n_exp // _PEER_EC),
        in_specs=[
            pl.BlockSpec((t, d), lambda i, c: (i, 0)),
            pl.BlockSpec((_PEER_EC, d), lambda i, c: (c, 0)),
            pl.BlockSpec((d, _PEER_EC), lambda i, c: (0, c)),
            first, first, tok, tok,
            pl.BlockSpec((t, d), lambda i, c: (i, 0)),
            pl.BlockSpec((1, 1, d), lambda i, c: bm(i)),
        ],
        out_specs=pl.BlockSpec((t, d), lambda i, c: (i, 0)),
        out_shape=jax.ShapeDtypeStruct((n, d), F32),
        scratch_shapes=[pltpu.VMEM((_PEER_EC, t), F32), pltpu.VMEM((_PEER_EC, t), BF16), pltpu.VMEM((d, t), F32)],
        compiler_params=_params("parallel", "arbitrary"),
        name="peer_experts",
    )(x_bf, u_bf, vt_bf, per_first(cnt), per_first(e1), rk, e2, h2d, g2)


def _peer_ffn_residual(h2d, seq_len, norm_g, shift, scale, gate, wq_heads, keys_bf, u_bf, vt_bf):
    x_bf, cnt, e1, rk, e2 = _peer_scores(h2d, seq_len, norm_g, shift, scale, wq_heads, keys_bf)
    return _peer_experts(x_bf, u_bf, vt_bf, cnt, e1, rk, e2, h2d, seq_len, gate)


def _final_norm_kernel(h_ref, g_ref, o_ref):
    x = h_ref[...]
    ms = jnp.mean(x * x, axis=-1, keepdims=True)
    o_ref[...] = (x * lax.rsqrt(ms + RMS_EPS)) * g_ref[...]


def _final_norm(h2d, g):
    n, d = h2d.shape
    t = 512
    return pl.pallas_call(
        _final_norm_kernel,
        grid=(n // t,),
        in_specs=[pl.BlockSpec((t, d), lambda i: (i, 0)), pl.BlockSpec((1, d), lambda i: (0, 0))],
        out_specs=pl.BlockSpec((t, d), lambda i: (i, 0)),
        out_shape=jax.ShapeDtypeStruct((n, d), F32),
        compiler_params=_params("parallel"),
        name="final_norm",
    )(h2d, g.reshape(1, d))


def _mixer_inputs(h2d, batch, seq_len, norm_g, shift, scale, w_perm, conv_w, conv_b):
    z, xbc, pool_u, q, k, v, dt = _project(h2d, seq_len, norm_g, shift, scale, w_perm)
    xbc_act = _conv_silu(xbc.reshape(batch, seq_len, SSD_XBC), conv_w, conv_b)
    dt3 = dt.reshape(batch, seq_len, LANES)
    dtt3 = jnp.swapaxes(dt3[:, :, :2 * SSD_HEADS], 1, 2)
    r3 = lambda a: a.reshape(batch, seq_len, a.shape[-1])
    return z, xbc_act, dt3, dtt3, r3(pool_u), r3(q), r3(k), r3(v)


def kernel(x, c, ctx, c_ctx, ada_w, ada_b, norm1_g, w_in, conv_w, conv_b, a_log, dt_bias, d_skip, ssd_norm_g, pool_w, pool_scale, na_rpb, w_out, norm2_g, peer_wq, peer_keys, peer_u, peer_v, final_g):
    batch, seq, d = x.shape
    ctx_len = ctx.shape[1]
    n, nc = batch * seq, batch * ctx_len
    h = x.reshape(n, d)
    hc = ctx.reshape(nc, d)

    c8 = jnp.concatenate([c, c_ctx[None], jnp.zeros((8 - batch - 1, d), F32)], axis=0)
    mod = _modulation(c8, ada_w, ada_b)

    for i in range(DEPTH):
        need_ctx_out = i < DEPTH - 1
        lat = [mod[i, :batch, j * d:(j + 1) * d].reshape(batch, 1, d) for j in range(6)]
        cx = [mod[i, batch:batch + 1, j * d:(j + 1) * d].reshape(1, 1, d) for j in range(6)]
        sh1, sc1, g1, sh2, sc2, g2 = lat
        csh1, csc1, cg1, csh2, csc2, cg2 = cx

        w_perm = _permute_w_in(w_in[i])
        w_out_bf = w_out[i].astype(BF16)
        n1 = norm1_g[i].reshape(1, d)
        n2 = norm2_g[i].reshape(1, d)
        zero_state = jnp.zeros((batch, SSD_HEADS, SSD_STATE, SSD_HEAD_DIM), F32)
        scan = functools.partial(_ssd_scan, dtb=dt_bias[i], alog=a_log[i])
        wq_heads = peer_wq[i].reshape(d, PEER_HEADS, 2 * PEER_SUB).transpose(1, 0, 2).astype(BF16)
        keys_bf = peer_keys[i].astype(BF16)
        u_bf = peer_u[i].astype(BF16)
        vt_bf = peer_v[i].T.astype(BF16)

        zc, xbc_c, dt3_c, dtt3_c, pool_c, qc, kc, vc = _mixer_inputs(
            hc, batch, ctx_len, n1, csh1, csc1, w_perm, conv_w[i], conv_b[i])
        yf_c, st_f = scan(xbc_c, dt3_c, dtt3_c, init=zero_state, reverse=False)
        yb_c, st_b = scan(xbc_c, dt3_c, dtt3_c, init=zero_state, reverse=True)
        if need_ctx_out:
            pool_yc = _pool_mixer(pool_c, pool_w[i], pool_scale[i])
            att_c = _context_attention(qc, kc, vc)
            hc = _mix_out(yf_c.reshape(nc, -1), yb_c.reshape(nc, -1), xbc_c.reshape(nc, -1), zc,
                          pool_yc.reshape(nc, -1), att_c.reshape(nc, -1), hc, ctx_len,
                          d_skip[i], ssd_norm_g[i], cg1, w_out_bf)
            hc = _peer_ffn_residual(hc, ctx_len, n2, csh2, csc2, cg2, wq_heads, keys_bf, u_bf, vt_bf)

        z, xbc_l, dt3_l, dtt3_l, pool_l, q, k, v = _mixer_inputs(
            h, batch, seq, n1, sh1, sc1, w_perm, conv_w[i], conv_b[i])
        yf, _ = scan(xbc_l, dt3_l, dtt3_l, init=st_f, reverse=False)
        yb, _ = scan(xbc_l, dt3_l, dtt3_l, init=st_b, reverse=True)
        pool_y = _pool_mixer(pool_l, pool_w[i], pool_scale[i])
        na = _neighbourhood_attention(q, k, v, kc, vc, _na_bias(na_rpb[i]))
        h = _mix_out(yf.reshape(n, -1), yb.reshape(n, -1), xbc_l.reshape(n, -1), z,
                     pool_y.reshape(n, -1), na.reshape(n, -1), h, seq,
                     d_skip[i], ssd_norm_g[i], g1, w_out_bf)
        h = _peer_ffn_residual(h, seq, n2, sh2, sc2, g2, wq_heads, keys_bf, u_bf, vt_bf)

    return _final_norm(h, final_g).reshape(batch, seq, d)
```

```python
import functools
import math

import jax
import jax.numpy as jnp
from jax import lax
from jax.experimental import pallas as pl
from jax.experimental.pallas import tpu as pltpu

F32 = jnp.float32
BF16 = jnp.bfloat16
HIGHEST = lax.Precision.HIGHEST

D_MODEL = 1024
DEPTH = 2
GRID_W = 64
RMS_EPS = 1e-6

SSD_HEAD_DIM = 64
SSD_HEADS = 16
SSD_GROUPS = 2
SSD_STATE = 128
SSD_CHUNK = 128
SSD_WIDTH = SSD_HEADS * SSD_HEAD_DIM
SSD_XBC = SSD_WIDTH + 2 * SSD_GROUPS * SSD_STATE

POOL_WINDOWS = (2, 4, 8, 16)
POOL_GROUP_DIM = 128
POOL_WIDTH = POOL_GROUP_DIM * len(POOL_WINDOWS)
POOL_PAD = 8

NA_HEAD_DIM = 64
NA_HEADS = 8
NA_WIDTH = NA_HEADS * NA_HEAD_DIM
NA_KH = 8
NA_KW = 16

PEER_HEADS = 8
PEER_NKEYS = 128
PEER_TOPK = 16
PEER_SUB = 128

LANES = 128
VMEM_LIMIT_BYTES = 56 * 1024 * 1024

_PROJ_SEGS = (SSD_WIDTH, SSD_XBC, POOL_WIDTH, NA_WIDTH, NA_WIDTH, NA_WIDTH, LANES)
_PROJ_DTYPES = (F32, F32, F32, BF16, BF16, BF16, F32)


def _params(*sem):
    return pltpu.CompilerParams(dimension_semantics=sem, vmem_limit_bytes=VMEM_LIMIT_BYTES)


def _rms_mod(x, g, scale, shift):
    ms = jnp.mean(x * x, axis=-1, keepdims=True)
    return (x * lax.rsqrt(ms + RMS_EPS)) * g * (1.0 + scale) + shift


def _silu(x):
    return x * jax.nn.sigmoid(x)


def _softplus(x):
    return jnp.maximum(x, 0.0) + jnp.log1p(jnp.exp(-jnp.abs(x)))


def _dot_nt(a, b):
    return lax.dot_general(a, b, (((1,), (1,)), ((), ())), preferred_element_type=F32)


def _batch_map(block_rows, seq_len, n_rows):
    if n_rows == 1:
        return lambda i, *_: (0, 0, 0)
    return lambda i, *_: ((i * block_rows) // seq_len, 0, 0)


def _mod_kernel(c_ref, w_ref, b_ref, o_ref):
    s = _silu(c_ref[...])
    o_ref[0] = jnp.dot(s, w_ref[0], precision=HIGHEST, preferred_element_type=F32) + b_ref[0]


def _modulation(c8, ada_w, ada_b):
    depth, d, six_d = ada_w.shape
    tn = 1024
    return pl.pallas_call(
        _mod_kernel,
        grid=(depth, six_d // tn),
        in_specs=[
            pl.BlockSpec((8, d), lambda l, j: (0, 0)),
            pl.BlockSpec((1, d, tn), lambda l, j: (l, 0, j)),
            pl.BlockSpec((1, 1, tn), lambda l, j: (l, 0, j)),
        ],
        out_specs=pl.BlockSpec((1, 8, tn), lambda l, j: (l, 0, j)),
        out_shape=jax.ShapeDtypeStruct((depth, 8, six_d), F32),
        compiler_params=_params("parallel", "parallel"),
        name="adaln_mod",
    )(c8, ada_w, ada_b.reshape(depth, 1, six_d))


def _proj_kernel(h_ref, g_ref, sh_ref, sc_ref, w_ref, *out_refs):
    u = _rms_mod(h_ref[...], g_ref[...], sc_ref[0], sh_ref[0]).astype(BF16)
    off = 0
    for o_ref, width in zip(out_refs, _PROJ_SEGS):
        o_ref[...] = jnp.dot(u, w_ref[:, off:off + width], preferred_element_type=F32).astype(o_ref.dtype)
        off += width


def _project(h2d, seq_len, g, shift, scale, w_perm):
    n, d = h2d.shape
    t = 256
    total = sum(_PROJ_SEGS)
    nb = shift.shape[0]
    row_map = lambda i: (i, 0)
    return pl.pallas_call(
        _proj_kernel,
        grid=(n // t,),
        in_specs=[
            pl.BlockSpec((t, d), row_map),
            pl.BlockSpec((1, d), lambda i: (0, 0)),
            pl.BlockSpec((1, 1, d), _batch_map(t, seq_len, nb)),
            pl.BlockSpec((1, 1, d), _batch_map(t, seq_len, nb)),
            pl.BlockSpec((d, total), lambda i: (0, 0)),
        ],
        out_specs=[pl.BlockSpec((t, w), row_map) for w in _PROJ_SEGS],
        out_shape=[jax.ShapeDtypeStruct((n, w), dt) for w, dt in zip(_PROJ_SEGS, _PROJ_DTYPES)],
        compiler_params=_params("parallel"),
        name="in_proj",
    )(h2d, g, shift, scale, w_perm)


def _permute_w_in(w_in):
    o = 0
    z = w_in[:, o:o + SSD_WIDTH]; o += SSD_WIDTH
    xbc = w_in[:, o:o + SSD_XBC]; o += SSD_XBC
    dt = w_in[:, o:o + 2 * SSD_HEADS]; o += 2 * SSD_HEADS
    rest = w_in[:, o:]
    dt = jnp.pad(dt, ((0, 0), (0, LANES - 2 * SSD_HEADS)))
    return jnp.concatenate([z, xbc, rest, dt], axis=1).astype(BF16)


def _conv_kernel(x_ref, w_ref, b_ref, o_ref):
    x = x_ref[0]
    n = x.shape[0]
    row = lax.broadcasted_iota(jnp.int32, x.shape, 0)
    prev = jnp.where(row == 0, 0.0, pltpu.roll(x, 1, 0))
    nxt = jnp.where(row == n - 1, 0.0, pltpu.roll(x, n - 1, 0))
    y = prev * w_ref[0:1, :] + x * w_ref[1:2, :] + nxt * w_ref[2:3, :] + b_ref[...]
    o_ref[0] = _silu(y)


def _conv_silu(xbc3, conv_w, conv_b):
    b, l, c = xbc3.shape
    tc = 256
    return pl.pallas_call(
        _conv_kernel,
        grid=(b, c // tc),
        in_specs=[
            pl.BlockSpec((1, l, tc), lambda i, j: (i, 0, j)),
            pl.BlockSpec((3, tc), lambda i, j: (0, j)),
            pl.BlockSpec((1, tc), lambda i, j: (0, j)),
        ],
        out_specs=pl.BlockSpec((1, l, tc), lambda i, j: (i, 0, j)),
        out_shape=jax.ShapeDtypeStruct((b, l, c), F32),
        compiler_params=_params("parallel", "parallel"),
        name="dwconv_silu",
    )(xbc3, conv_w, conv_b.reshape(1, c))


def _ssd_kernel(reverse, xbc_ref, dt_ref, dtt_ref, dtb_row, dtb_col, alog_row, alog_col, init_ref,
                y_ref, fin_ref, state_scr):
    c = pl.program_id(1)
    q = SSD_CHUNK

    @pl.when(c == 0)
    def _():
        state_scr[...] = init_ref[0]

    col0 = SSD_HEADS if reverse else 0
    dt_l = _softplus(dt_ref[0] + dtb_row[...])
    a_l = dt_l * (-jnp.exp(alog_row[...]))
    dt_t = _softplus(dtt_ref[0] + dtb_col[...])
    a_t = dt_t * (-jnp.exp(alog_col[...]))
    row = lax.broadcasted_iota(jnp.int32, (q, q), 0)
    col = lax.broadcasted_iota(jnp.int32, (q, q), 1)
    lower = (row >= col).astype(F32)
    upper = (row <= col).astype(F32)
    cs_l = jnp.dot(lower, a_l, precision=HIGHEST, preferred_element_type=F32)
    cs_t = jnp.dot(a_t, upper, precision=HIGHEST, preferred_element_type=F32)
    tot_l = cs_l[q - 1:q, :]
    if reverse:
        p_l, p_t = cs_l - a_l, cs_t - a_t
        tri = row <= col
    else:
        p_l, p_t = cs_l, cs_t
        tri = row >= col

    xbc = xbc_ref[0]
    for g in range(SSD_GROUPS):
        b_g = xbc[:, SSD_WIDTH + g * SSD_STATE:SSD_WIDTH + (g + 1) * SSD_STATE]
        c_g = xbc[:, SSD_WIDTH + (SSD_GROUPS + g) * SSD_STATE:SSD_WIDTH + (SSD_GROUPS + g + 1) * SSD_STATE]
        c_bf = c_g.astype(BF16)
        cb = _dot_nt(c_bf, b_g.astype(BF16))
        bt_bf = b_g.T.astype(BF16)
        for hh in range(SSD_HEADS // SSD_GROUPS):
            h = g * (SSD_HEADS // SSD_GROUPS) + hh
            k = col0 + h
            pcol = p_l[:, k:k + 1]
            prow = p_t[k:k + 1, :]
            tot = tot_l[:, k:k + 1]
            seg = (prow - pcol) if reverse else (pcol - prow)
            lmat = jnp.exp(jnp.where(tri, seg, -jnp.inf))
            xd = xbc[:, h * SSD_HEAD_DIM:(h + 1) * SSD_HEAD_DIM] * dt_l[:, k:k + 1]
            y_diag = jnp.dot((cb * lmat).astype(BF16), xd.astype(BF16), preferred_element_type=F32)
            s_prev = state_scr[h]
            if reverse:
                in_decay, to_end = jnp.exp(tot - pcol), jnp.exp(pcol)
            else:
                in_decay, to_end = jnp.exp(pcol), jnp.exp(tot - pcol)
            y_off = jnp.dot(c_bf, s_prev.astype(BF16), preferred_element_type=F32) * in_decay
            state_scr[h] = jnp.exp(tot) * s_prev + jnp.dot(bt_bf, (xd * to_end).astype(BF16),
                                                           preferred_element_type=F32)
            y_ref[0, :, h * SSD_HEAD_DIM:(h + 1) * SSD_HEAD_DIM] = y_diag + y_off

    @pl.when(c == pl.num_programs(1) - 1)
    def _():
        fin_ref[0] = state_scr[...]


def _ssd_scan(xbc_act, dt3, dtt3, dtb, alog, init, reverse):
    b, l, _ = xbc_act.shape
    nc = l // SSD_CHUNK
    cmap = (lambda c: nc - 1 - c) if reverse else (lambda c: c)
    dtb_row = jnp.pad(dtb.reshape(1, -1), ((0, 0), (0, LANES - 2 * SSD_HEADS)))
    alog_row = jnp.pad(alog.reshape(1, -1), ((0, 0), (0, LANES - 2 * SSD_HEADS)))
    small = lambda shape: pl.BlockSpec(shape, lambda i, c: (0, 0))
    st_shape = (b, SSD_HEADS, SSD_STATE, SSD_HEAD_DIM)
    st_spec = pl.BlockSpec((1,) + st_shape[1:], lambda i, c: (i, 0, 0, 0))
    return pl.pallas_call(
        functools.partial(_ssd_kernel, reverse),
        grid=(b, nc),
        in_specs=[
            pl.BlockSpec((1, SSD_CHUNK, SSD_XBC), lambda i, c: (i, cmap(c), 0)),
            pl.BlockSpec((1, SSD_CHUNK, LANES), lambda i, c: (i, cmap(c), 0)),
            pl.BlockSpec((1, 2 * SSD_HEADS, SSD_CHUNK), lambda i, c: (i, 0, cmap(c))),
            small((1, LANES)), small((2 * SSD_HEADS, 1)), small((1, LANES)), small((2 * SSD_HEADS, 1)),
            st_spec,
        ],
        out_specs=[pl.BlockSpec((1, SSD_CHUNK, SSD_WIDTH), lambda i, c: (i, cmap(c), 0)), st_spec],
        out_shape=[jax.ShapeDtypeStruct((b, l, SSD_WIDTH), F32), jax.ShapeDtypeStruct(st_shape, F32)],
        scratch_shapes=[pltpu.VMEM(st_shape[1:], F32)],
        compiler_params=_params("parallel", "arbitrary"),
        name="ssd_scan_bwd" if reverse else "ssd_scan_fwd",
    )(xbc_act, dt3, dtt3, dtb_row, dtb.reshape(-1, 1), alog_row, alog.reshape(-1, 1), init)


def _pool_kernel(x_ref, w_ref, sc_ref, o_ref, pad_scr):
    n = x_ref.shape[1]
    zeros = jnp.zeros((POOL_PAD, POOL_GROUP_DIM), F32)
    pad_scr[0:POOL_PAD, :] = zeros
    pad_scr[n + POOL_PAD:n + 2 * POOL_PAD, :] = zeros
    t = lax.broadcasted_iota(jnp.int32, (n, 1), 0)
    for gi, w in enumerate(POOL_WINDOWS):
        sl = slice(gi * POOL_GROUP_DIM, (gi + 1) * POOL_GROUP_DIM)
        x = x_ref[0, :, sl]
        pad_scr[POOL_PAD:n + POOL_PAD, :] = x
        acc = jnp.zeros_like(x)
        for o in range(-(w // 2), w - w // 2):
            acc = acc + pad_scr[POOL_PAD + o:POOL_PAD + o + n, :]
        lo = jnp.maximum(t - w // 2, 0)
        hi = jnp.minimum(t + (w - w // 2 - 1), n - 1)
        pooled = acc / (hi - lo + 1).astype(F32) - x
        y = jnp.dot(pooled.astype(BF16), w_ref[gi].astype(BF16), preferred_element_type=F32)
        o_ref[0, :, sl] = y * sc_ref[:, sl]


def _pool_mixer(u3, w_pool, scale):
    b, l, c = u3.shape
    return pl.pallas_call(
        _pool_kernel,
        grid=(b,),
        in_specs=[
            pl.BlockSpec((1, l, c), lambda i: (i, 0, 0)),
            pl.BlockSpec(w_pool.shape, lambda i: (0, 0, 0)),
            pl.BlockSpec((1, c), lambda i: (0, 0)),
        ],
        out_specs=pl.BlockSpec((1, l, c), lambda i: (i, 0, 0)),
        out_shape=jax.ShapeDtypeStruct((b, l, c), F32),
        scratch_shapes=[pltpu.VMEM((l + 2 * POOL_PAD, POOL_GROUP_DIM), F32)],
        compiler_params=_params("parallel"),
        name="pool_mixer",
    )(u3, w_pool, scale.reshape(1, c))


def _na_bias_kernel(rpb_ref, o_ref):
    h = pl.program_id(0)
    dr = pl.program_id(1)
    qi = lax.broadcasted_iota(jnp.int32, (GRID_W, LANES), 0)
    lane = lax.broadcasted_iota(jnp.int32, (GRID_W, LANES), 1)
    ki = lane % GRID_W
    second = lane >= GRID_W
    start = jnp.clip(qi - NA_KW // 2, 0, GRID_W - NA_KW)
    in_window = (ki >= start) & (ki < start + NA_KW)
    dc = jnp.clip(ki - qi, -(NA_KW - 1), NA_KW - 1) + NA_KW - 1
    val = jnp.zeros((GRID_W, LANES), F32)
    for j in range(2 * NA_KW - 1):
        pick = jnp.where(second, rpb_ref[h, dr + 1, j], rpb_ref[h, dr, j])
        val = jnp.where(dc == j, pick, val)
    o_ref[0, 0] = jnp.where(in_window, val, -jnp.inf)


def _na_bias(rpb):
    nh, ndr, ndc = rpb.shape
    return pl.pallas_call(
        _na_bias_kernel,
        grid=(nh, ndr - 1),
        in_specs=[pl.BlockSpec(memory_space=pltpu.SMEM)],
        out_specs=pl.BlockSpec((1, 1, GRID_W, LANES), lambda h, d: (h, d, 0, 0)),
        out_shape=jax.ShapeDtypeStruct((nh, ndr - 1, GRID_W, LANES), F32),
        compiler_params=_params("parallel", "parallel"),
        name="na_bias",
    )(rpb)


def _na_kernel(kh, q_ref, k_ref, v_ref, kc_ref, vc_ref, bias_ref, o_ref):
    r = pl.program_id(1)
    rows = pl.num_programs(1)
    r0 = jnp.clip(r - kh // 2, 0, rows - kh)
    start = pl.multiple_of(r0 * GRID_W, GRID_W)
    kblk = k_ref[0, pl.ds(start, kh * GRID_W), :]
    vblk = v_ref[0, pl.ds(start, kh * GRID_W), :]
    dr0 = r0 - r + NA_KH - 1
    scale = NA_HEAD_DIM ** -0.5
    for h in range(NA_HEADS):
        sl = slice(h * NA_HEAD_DIM, (h + 1) * NA_HEAD_DIM)
        qh = q_ref[0, :, sl] * scale
        s_loc = _dot_nt(qh, kblk[:, sl])
        bias = jnp.concatenate([bias_ref[h, dr0 + 2 * j] for j in range(kh // 2)], axis=1)
        s_loc = s_loc + bias
        s_ctx = _dot_nt(qh, kc_ref[0, :, sl])
        m = jnp.maximum(jnp.max(s_loc, axis=-1, keepdims=True), jnp.max(s_ctx, axis=-1, keepdims=True))
        p_loc = jnp.exp(s_loc - m)
        p_ctx = jnp.exp(s_ctx - m)
        den = jnp.sum(p_loc, axis=-1, keepdims=True) + jnp.sum(p_ctx, axis=-1, keepdims=True)
        acc = jnp.dot(p_loc.astype(BF16), vblk[:, sl], preferred_element_type=F32)
        acc = acc + jnp.dot(p_ctx.astype(BF16), vc_ref[0, :, sl], preferred_element_type=F32)
        o_ref[0, :, sl] = acc / den


def _neighbourhood_attention(q3, k3, v3, kc3, vc3, bias):
    b, s, c = q3.shape
    rows = s // GRID_W
    kh = min(NA_KH, rows)
    lc = kc3.shape[1]
    full = lambda n: pl.BlockSpec((1, n, c), lambda i, r: (i, 0, 0))
    return pl.pallas_call(
        functools.partial(_na_kernel, kh),
        grid=(b, rows),
        in_specs=[
            pl.BlockSpec((1, GRID_W, c), lambda i, r: (i, r, 0)),
            full(s), full(s), full(lc), full(lc),
            pl.BlockSpec(bias.shape, lambda i, r: (0, 0, 0, 0)),
        ],
        out_specs=pl.BlockSpec((1, GRID_W, c), lambda i, r: (i, r, 0)),
        out_shape=jax.ShapeDtypeStruct((b, s, c), F32),
        compiler_params=_params("parallel", "arbitrary"),
        name="na_attention",
    )(q3, k3, v3, kc3, vc3, bias)


def _ctx_attn_kernel(q_ref, k_ref, v_ref, o_ref):
    scale = NA_HEAD_DIM ** -0.5
    for h in range(NA_HEADS):
        sl = slice(h * NA_HEAD_DIM, (h + 1) * NA_HEAD_DIM)
        s = _dot_nt(q_ref[0, :, sl] * scale, k_ref[0, :, sl])
        p = jnp.exp(s - jnp.max(s, axis=-1, keepdims=True))
        den = jnp.sum(p, axis=-1, keepdims=True)
        o_ref[0, :, sl] = jnp.dot(p.astype(BF16), v_ref[0, :, sl], preferred_element_type=F32) / den


def _context_attention(q3, k3, v3):
    b, l, c = q3.shape
    spec = pl.BlockSpec((1, l, c), lambda i: (i, 0, 0))
    return pl.pallas_call(
        _ctx_attn_kernel,
        grid=(b,),
        in_specs=[spec, spec, spec],
        out_specs=spec,
        out_shape=jax.ShapeDtypeStruct((b, l, c), F32),
        compiler_params=_params("parallel"),
        name="ctx_attention",
    )(q3, k3, v3)


def _mix_out_kernel(yf_ref, yb_ref, xs_ref, z_ref, pool_ref, na_ref, h_ref, dsk_ref, ng_ref, g1_ref, w_ref, o_ref):
    y = yf_ref[...] + yb_ref[...] + dsk_ref[...] * xs_ref[...]
    y = y * _silu(z_ref[...])
    ms = jnp.mean(y * y, axis=-1, keepdims=True)
    yn = (y * lax.rsqrt(ms + RMS_EPS)) * ng_ref[...]
    mix = jnp.dot(yn.astype(BF16), w_ref[0:SSD_WIDTH, :], preferred_element_type=F32)
    mix = mix + jnp.dot(pool_ref[...].astype(BF16), w_ref[SSD_WIDTH:SSD_WIDTH + POOL_WIDTH, :],
                        preferred_element_type=F32)
    mix = mix + jnp.dot(na_ref[...].astype(BF16), w_ref[SSD_WIDTH + POOL_WIDTH:, :], preferred_element_type=F32)
    o_ref[...] = h_ref[...] + g1_ref[0] * mix


def _mix_out(yf, yb, xbc_act2d, z, pool_y, na_y, h2d, seq_len, d_skip, norm_g, g1, w_out_bf):
    n, d = h2d.shape
    t = 256
    row = lambda w: pl.BlockSpec((t, w), lambda i: (i, 0))
    vec = lambda w: pl.BlockSpec((1, w), lambda i: (0, 0))
    return pl.pallas_call(
        _mix_out_kernel,
        grid=(n // t,),
        in_specs=[
            row(SSD_WIDTH), row(SSD_WIDTH), row(SSD_WIDTH), row(SSD_WIDTH), row(POOL_WIDTH), row(NA_WIDTH), row(d),
            vec(SSD_WIDTH), vec(SSD_WIDTH),
            pl.BlockSpec((1, 1, d), _batch_map(t, seq_len, g1.shape[0])),
            pl.BlockSpec(w_out_bf.shape, lambda i: (0, 0)),
        ],
        out_specs=row(d),
        out_shape=jax.ShapeDtypeStruct((n, d), F32),
        compiler_params=_params("parallel"),
        name="mix_out",
    )(yf, yb, xbc_act2d, z, pool_y, na_y, h2d, jnp.repeat(d_skip, SSD_HEAD_DIM).reshape(1, -1),
      norm_g.reshape(1, -1), g1, w_out_bf)


_CAND_ROWS = 16 + 8 * 7 + 8


def _extract_top16_pair(s1, s2):
    n, t = s1.shape
    iota = lax.broadcasted_iota(jnp.int32, (n, t), 0).astype(F32)
    row16 = lax.broadcasted_iota(jnp.int32, (PEER_TOPK, t), 0)

    def pick(work, rank, vals, j):
        m = jnp.max(work, axis=0, keepdims=True)
        idx = jnp.min(jnp.where(work == m, iota, float(n)), axis=0, keepdims=True)
        sel = iota == idx
        return (jnp.where(sel, -jnp.inf, work), jnp.where(sel, j.astype(F32), rank),
                jnp.where(row16 == j, m, vals))

    def body(j, carry):
        a, b = carry
        return pick(*a, j), pick(*b, j)

    start = lambda s: (s, jnp.full((n, t), float(PEER_TOPK), F32), jnp.zeros((PEER_TOPK, t), F32))
    (_, rank1, v1), (_, rank2, v2) = lax.fori_loop(0, PEER_TOPK, body, (start(s1), start(s2)))
    return rank1, v1, rank2, v2


def _bf16_bits(x):
    return pltpu.bitcast(x.astype(BF16).astype(F32), jnp.uint32)


def _pack_row_pairs(x, scr):
    n, t = x.shape
    for j in range(t // LANES):
        scr[j] = x[:, j * LANES:(j + 1) * LANES]
    words = []
    for j in range(t // LANES):
        even = scr[j, pl.ds(0, n // 2, stride=2), :]
        odd = scr[j, pl.ds(1, n // 2, stride=2), :]
        words.append((_bf16_bits(even) >> 16) | _bf16_bits(odd))
    return jnp.concatenate(words, axis=1)


def _pack_same(x):
    w = _bf16_bits(x)
    return w | (w >> 16)


def _peer_score_kernel(h_ref, g_ref, sh_ref, sc_ref, wq_ref, keys_ref,
                       x_out, cnt_out, e1_out, rk_out, e2_out, u_scr, pair_scr):
    hd = pl.program_id(1)

    @pl.when(hd == 0)
    def _():
        u = _rms_mod(h_ref[...], g_ref[...], sc_ref[0], sh_ref[0]).astype(BF16)
        u_scr[...] = u
        x_out[...] = pltpu.bitcast(u, jnp.uint32)

    q = jnp.dot(u_scr[...], wq_ref[0], preferred_element_type=F32).astype(BF16)
    s1 = _dot_nt(keys_ref[0, 0], q[:, :PEER_SUB])
    s2 = _dot_nt(keys_ref[0, 1], q[:, PEER_SUB:])
    rank1, v1, rank2, v2 = _extract_top16_pair(s1, s2)

    blocks = [v1[0:1] + v2]
    for j in range(1, 8):
        blocks.append(v1[j:j + 1] + v2[0:8])
    blocks.append(v1[8:16] + v2[0:1])
    cand = jnp.concatenate(blocks, axis=0)
    iota = lax.broadcasted_iota(jnp.int32, cand.shape, 0).astype(F32)

    def body(_, carry):
        work, sel_acc = carry
        m = jnp.max(work, axis=0, keepdims=True)
        idx = jnp.min(jnp.where(work == m, iota, float(_CAND_ROWS)), axis=0, keepdims=True)
        sel = iota == idx
        return jnp.where(sel, -jnp.inf, work), jnp.where(sel, 1.0, sel_acc)

    _, sel = lax.fori_loop(0, PEER_TOPK, body, (cand, jnp.zeros(cand.shape, F32)))
    z = jnp.sum(sel * jnp.exp(cand - cand[0:1]), axis=0, keepdims=True)

    cnt = jnp.zeros(rank1.shape, F32)
    for j in range(8):
        lo = 0 if j == 0 else 16 + 8 * (j - 1)
        n_j = jnp.sum(sel[lo:lo + (16 if j == 0 else 8)], axis=0, keepdims=True)
        cnt = cnt + jnp.where(rank1 == float(j), n_j, 0.0)
    for j in range(8, 16):
        cnt = cnt + jnp.where(rank1 == float(j), sel[64 + j:65 + j], 0.0)
    cnt_out[0] = _pack_same(cnt)
    e1_out[0] = _pack_same(jnp.exp(s1 - v1[0:1]))
    rk_out[0] = _pack_row_pairs(rank2, pair_scr)
    e2_out[0] = _pack_row_pairs(jnp.exp(s2 - v2[0:1]) / z, pair_scr)


def _peer_scores(h2d, seq_len, g, shift, scale, wq_heads, keys_bf):
    n, d = h2d.shape
    t = 256
    nb = shift.shape[0]
    bm = _batch_map(t, seq_len, nb)
    mod_spec = pl.BlockSpec((1, 1, d), lambda i, hd: bm(i))
    first_out = pl.BlockSpec((1, PEER_NKEYS, t), lambda i, hd: (hd, 0, i))
    first_shape = jax.ShapeDtypeStruct((PEER_HEADS, PEER_NKEYS, n), jnp.uint32)
    second_out = pl.BlockSpec((1, PEER_NKEYS // 2, t), lambda i, hd: (hd, 0, i))
    second_shape = jax.ShapeDtypeStruct((PEER_HEADS, PEER_NKEYS // 2, n), jnp.uint32)
    return pl.pallas_call(
        _peer_score_kernel,
        grid=(n // t, PEER_HEADS),
        in_specs=[
            pl.BlockSpec((t, d), lambda i, hd: (i, 0)),
            pl.BlockSpec((1, d), lambda i, hd: (0, 0)),
            mod_spec, mod_spec,
            pl.BlockSpec((1, d, 2 * PEER_SUB), lambda i, hd: (hd, 0, 0)),
            pl.BlockSpec((1, 2, PEER_NKEYS, PEER_SUB), lambda i, hd: (hd, 0, 0, 0)),
        ],
        out_specs=[pl.BlockSpec((t // 2, d), lambda i, hd: (i, 0)), first_out, first_out, second_out, second_out],
        out_shape=[jax.ShapeDtypeStruct((n // 2, d), jnp.uint32), first_shape, first_shape, second_shape,
                   second_shape],
        scratch_shapes=[pltpu.VMEM((t, d), BF16), pltpu.VMEM((t // LANES, PEER_NKEYS, LANES), F32)],
        compiler_params=_params("parallel", "arbitrary"),
        name="peer_scores",
    )(h2d, g, shift, scale, wq_heads, keys_bf)


_PEER_EC = 1024
_INV_SQRT2 = 1.0 / math.sqrt(2.0)


def _as_bf16_rows(words):
    return pltpu.bitcast(words, BF16)


_PEER_A_GROUP = 4


def _peer_expert_kernel(n_chunks, x_ref, u_ref, vt_ref, cnt_ref, e1_ref, rk_ref, e2_ref, h_ref, g2_ref, o_ref,
                        ht0_scr, ht1_scr, g0_scr, g1_scr, acc_scr):
    s = pl.program_id(0)
    t = h_ref.shape[0]
    p2 = s - 2
    c2 = lax.rem(jnp.maximum(p2, 0), n_chunks)
    pack = 2 * 8
    n_a = _PEER_EC // PEER_NKEYS

    @pl.when(s == 0)
    def _():
        ht1_scr[...] = jnp.zeros_like(ht1_scr)
        g0_scr[...] = jnp.zeros_like(g0_scr)
        g1_scr[...] = jnp.zeros_like(g1_scr)

    @pl.when((p2 <= 0) | (c2 == 0))
    def _():
        acc_scr[...] = jnp.zeros_like(acc_scr)

    def step(ht_new, ht_old, g_new, g_old):
        ht_new[...] = _dot_nt(_as_bf16_rows(u_ref[...]), _as_bf16_rows(x_ref[...]))
        for tc in range(t // LANES):
            ls = slice(tc * LANES, (tc + 1) * LANES)
            for a0 in range(0, n_a, _PEER_A_GROUP):
                w = [[None] * (PEER_NKEYS // pack) for _ in range(_PEER_A_GROUP)]
                for hd in range(PEER_HEADS):
                    rows1 = [(_as_bf16_rows(jnp.broadcast_to(cnt_ref[hd, a0 + i, :, ls], (8, LANES))),
                              _as_bf16_rows(jnp.broadcast_to(e1_ref[hd, a0 + i, :, ls], (8, LANES))))
                             for i in range(_PEER_A_GROUP)]
                    for r in range(PEER_NKEYS // pack):
                        rk = _as_bf16_rows(rk_ref[hd, r * 8:(r + 1) * 8, ls])
                        e2 = _as_bf16_rows(e2_ref[hd, r * 8:(r + 1) * 8, ls])
                        for i, (cn, e1) in enumerate(rows1):
                            term = jnp.where(rk < cn, e2, 0.0) * e1
                            w[i][r] = term if w[i][r] is None else w[i][r] + term
                for i in range(_PEER_A_GROUP):
                    for r in range(PEER_NKEYS // pack):
                        row0 = (a0 + i) * PEER_NKEYS + r * pack
                        hs = ht_old[row0:row0 + pack, ls]
                        act = 0.5 * hs * (1.0 + lax.erf(hs * _INV_SQRT2))
                        g_new[row0 // 2:(row0 + pack) // 2, ls] = pltpu.bitcast(w[i][r] * act.astype(BF16), jnp.uint32)
        acc_scr[...] += jnp.dot(_as_bf16_rows(vt_ref[...]), _as_bf16_rows(g_old[...]),
                                preferred_element_type=F32)

    @pl.when(s % 2 == 0)
    def _():
        step(ht0_scr, ht1_scr, g1_scr, g0_scr)

    @pl.when(s % 2 == 1)
    def _():
        step(ht1_scr, ht0_scr, g0_scr, g1_scr)

    @pl.when((p2 >= 0) & (c2 == n_chunks - 1))
    def _():
        o_ref[...] = h_ref[...] + g2_ref[0] * acc_scr[...].T


def _peer_experts(x_pk, u_pk, vt_pk, cnt, e1, rk, e2, h2d, seq_len, g2):
    n, d = h2d.shape
    t = 512
    n_chunks = 2 * u_pk.shape[0] // _PEER_EC
    total = (n // t) * n_chunks
    ea = _PEER_EC // PEER_NKEYS

    def pair(p):
        p = jnp.clip(p, 0, total - 1)
        return p // n_chunks, lax.rem(p, n_chunks)

    blk = lambda lag: (lambda s: pair(s - lag)[0])
    chk = lambda lag: (lambda s: pair(s - lag)[1])
    tok = pl.BlockSpec((PEER_HEADS, PEER_NKEYS // 2, t), lambda s: (0, 0, blk(1)(s)))
    first = pl.BlockSpec((PEER_HEADS, ea, 1, t), lambda s: (0, chk(1)(s), 0, blk(1)(s)))
    per_first = lambda a: a.reshape(PEER_HEADS, PEER_NKEYS, 1, n)
    bm = _batch_map(t, seq_len, g2.shape[0])
    return pl.pallas_call(
        functools.partial(_peer_expert_kernel, n_chunks),
        grid=(total + 2,),
        in_specs=[
            pl.BlockSpec((t // 2, d), lambda s: (blk(0)(s), 0)),
            pl.BlockSpec((_PEER_EC // 2, d), lambda s: (chk(0)(s), 0)),
            pl.BlockSpec((d // 2, _PEER_EC), lambda s: (0, chk(2)(s))),
            first, first, tok, tok,
            pl.BlockSpec((t, d), lambda s: (blk(2)(s), 0)),
            pl.BlockSpec((1, 1, d), lambda s: bm(blk(2)(s))),
        ],
        out_specs=pl.BlockSpec((t, d), lambda s: (blk(2)(s), 0)),
        out_shape=jax.ShapeDtypeStruct((n, d), F32),
        scratch_shapes=[pltpu.VMEM((_PEER_EC, t), F32), pltpu.VMEM((_PEER_EC, t), F32),
                        pltpu.VMEM((_PEER_EC // 2, t), jnp.uint32), pltpu.VMEM((_PEER_EC // 2, t), jnp.uint32),
                        pltpu.VMEM((d, t), F32)],
        compiler_params=_params("arbitrary"),
        name="peer_experts",
    )(x_pk, u_pk, vt_pk, per_first(cnt), per_first(e1), rk, e2, h2d, g2)


def _pack_bf16_rows(a):
    r, c = a.shape
    pairs = a.astype(BF16).reshape(r // 2, 2, c).transpose(0, 2, 1)
    return lax.bitcast_convert_type(pairs, jnp.uint32)


def _peer_ffn_residual(h2d, seq_len, norm_g, shift, scale, gate, wq_heads, keys_bf, u_pk, vt_pk):
    x_pk, cnt, e1, rk, e2 = _peer_scores(h2d, seq_len, norm_g, shift, scale, wq_heads, keys_bf)
    return _peer_experts(x_pk, u_pk, vt_pk, cnt, e1, rk, e2, h2d, seq_len, gate)


def _final_norm_kernel(h_ref, g_ref, o_ref):
    x = h_ref[...]
    ms = jnp.mean(x * x, axis=-1, keepdims=True)
    o_ref[...] = (x * lax.rsqrt(ms + RMS_EPS)) * g_ref[...]


def _final_norm(h2d, g):
    n, d = h2d.shape
    t = 512
    return pl.pallas_call(
        _final_norm_kernel,
        grid=(n // t,),
        in_specs=[pl.BlockSpec((t, d), lambda i: (i, 0)), pl.BlockSpec((1, d), lambda i: (0, 0))],
        out_specs=pl.BlockSpec((t, d), lambda i: (i, 0)),
        out_shape=jax.ShapeDtypeStruct((n, d), F32),
        compiler_params=_params("parallel"),
        name="final_norm",
    )(h2d, g.reshape(1, d))


def _mixer_inputs(h2d, batch, seq_len, norm_g, shift, scale, w_perm, conv_w, conv_b):
    z, xbc, pool_u, q, k, v, dt = _project(h2d, seq_len, norm_g, shift, scale, w_perm)
    xbc_act = _conv_silu(xbc.reshape(batch, seq_len, SSD_XBC), conv_w, conv_b)
    dt3 = dt.reshape(batch, seq_len, LANES)
    dtt3 = jnp.swapaxes(dt3[:, :, :2 * SSD_HEADS], 1, 2)
    r3 = lambda a: a.reshape(batch, seq_len, a.shape[-1])
    return z, xbc_act, dt3, dtt3, r3(pool_u), r3(q), r3(k), r3(v)


def kernel(x, c, ctx, c_ctx, ada_w, ada_b, norm1_g, w_in, conv_w, conv_b, a_log, dt_bias, d_skip, ssd_norm_g, pool_w, pool_scale, na_rpb, w_out, norm2_g, peer_wq, peer_keys, peer_u, peer_v, final_g):
    batch, seq, d = x.shape
    ctx_len = ctx.shape[1]
    n, nc = batch * seq, batch * ctx_len
    h = x.reshape(n, d)
    hc = ctx.reshape(nc, d)

    c8 = jnp.concatenate([c, c_ctx[None], jnp.zeros((8 - batch - 1, d), F32)], axis=0)
    mod = _modulation(c8, ada_w, ada_b)

    for i in range(DEPTH):
        need_ctx_out = i < DEPTH - 1
        lat = [mod[i, :batch, j * d:(j + 1) * d].reshape(batch, 1, d) for j in range(6)]
        cx = [mod[i, batch:batch + 1, j * d:(j + 1) * d].reshape(1, 1, d) for j in range(6)]
        sh1, sc1, g1, sh2, sc2, g2 = lat
        csh1, csc1, cg1, csh2, csc2, cg2 = cx

        w_perm = _permute_w_in(w_in[i])
        w_out_bf = w_out[i].astype(BF16)
        n1 = norm1_g[i].reshape(1, d)
        n2 = norm2_g[i].reshape(1, d)
        zero_state = jnp.zeros((batch, SSD_HEADS, SSD_STATE, SSD_HEAD_DIM), F32)
        scan = functools.partial(_ssd_scan, dtb=dt_bias[i], alog=a_log[i])
        wq_heads = peer_wq[i].reshape(d, PEER_HEADS, 2 * PEER_SUB).transpose(1, 0, 2).astype(BF16)
        keys_bf = peer_keys[i].astype(BF16)
        u_pk = _pack_bf16_rows(peer_u[i])
        vt_pk = _pack_bf16_rows(peer_v[i].T)

        zc, xbc_c, dt3_c, dtt3_c, pool_c, qc, kc, vc = _mixer_inputs(
            hc, batch, ctx_len, n1, csh1, csc1, w_perm, conv_w[i], conv_b[i])
        yf_c, st_f = scan(xbc_c, dt3_c, dtt3_c, init=zero_state, reverse=False)
        yb_c, st_b = scan(xbc_c, dt3_c, dtt3_c, init=zero_state, reverse=True)
        if need_ctx_out:
            pool_yc = _pool_mixer(pool_c, pool_w[i], pool_scale[i])
            att_c = _context_attention(qc, kc, vc)
            hc = _mix_out(yf_c.reshape(nc, -1), yb_c.reshape(nc, -1), xbc_c.reshape(nc, -1), zc,
                          pool_yc.reshape(nc, -1), att_c.reshape(nc, -1), hc, ctx_len,
                          d_skip[i], ssd_norm_g[i], cg1, w_out_bf)
            hc = _peer_ffn_residual(hc, ctx_len, n2, csh2, csc2, cg2, wq_heads, keys_bf, u_pk, vt_pk)

        z, xbc_l, dt3_l, dtt3_l, pool_l, q, k, v = _mixer_inputs(
            h, batch, seq, n1, sh1, sc1, w_perm, conv_w[i], conv_b[i])
        yf, _ = scan(xbc_l, dt3_l, dtt3_l, init=st_f, reverse=False)
        yb, _ = scan(xbc_l, dt3_l, dtt3_l, init=st_b, reverse=True)
        pool_y = _pool_mixer(pool_l, pool_w[i], pool_scale[i])
        na = _neighbourhood_attention(q, k, v, kc, vc, _na_bias(na_rpb[i]))
        h = _mix_out(yf.reshape(n, -1), yb.reshape(n, -1), xbc_l.reshape(n, -1), z,
                     pool_y.reshape(n, -1), na.reshape(n, -1), h, seq,
                     d_skip[i], ssd_norm_g[i], g1, w_out_bf)
        h = _peer_ffn_residual(h, seq, n2, sh2, sc2, g2, wq_heads, keys_bf, u_pk, vt_pk)

    return _final_norm(h, final_g).reshape(batch, seq, d)
```

```python
import functools
import math

import jax
import jax.numpy as jnp
from jax import lax
from jax.experimental import pallas as pl
from jax.experimental.pallas import tpu as pltpu

F32 = jnp.float32
BF16 = jnp.bfloat16
HIGHEST = lax.Precision.HIGHEST

D_MODEL = 1024
DEPTH = 2
GRID_W = 64
RMS_EPS = 1e-6

SSD_HEAD_DIM = 64
SSD_HEADS = 16
SSD_GROUPS = 2
SSD_STATE = 128
SSD_CHUNK = 128
SSD_WIDTH = SSD_HEADS * SSD_HEAD_DIM
SSD_XBC = SSD_WIDTH + 2 * SSD_GROUPS * SSD_STATE

POOL_WINDOWS = (2, 4, 8, 16)
POOL_GROUP_DIM = 128
POOL_WIDTH = POOL_GROUP_DIM * len(POOL_WINDOWS)
POOL_PAD = 8

NA_HEAD_DIM = 64
NA_HEADS = 8
NA_WIDTH = NA_HEADS * NA_HEAD_DIM
NA_KH = 8
NA_KW = 16

PEER_HEADS = 8
PEER_NKEYS = 128
PEER_TOPK = 16
PEER_SUB = 128

LANES = 128
VMEM_LIMIT_BYTES = 56 * 1024 * 1024

_PROJ_SEGS = (SSD_WIDTH, SSD_XBC, POOL_WIDTH, NA_WIDTH, NA_WIDTH, NA_WIDTH, LANES)
_PROJ_DTYPES = (F32, F32, F32, BF16, BF16, BF16, F32)


def _params(*sem):
    return pltpu.CompilerParams(dimension_semantics=sem, vmem_limit_bytes=VMEM_LIMIT_BYTES)


def _rms_mod(x, g, scale, shift):
    ms = jnp.mean(x * x, axis=-1, keepdims=True)
    return (x * lax.rsqrt(ms + RMS_EPS)) * g * (1.0 + scale) + shift


def _silu(x):
    return x * jax.nn.sigmoid(x)


def _softplus(x):
    return jnp.maximum(x, 0.0) + jnp.log1p(jnp.exp(-jnp.abs(x)))


def _dot_nt(a, b):
    return lax.dot_general(a, b, (((1,), (1,)), ((), ())), preferred_element_type=F32)


def _batch_map(block_rows, seq_len, n_rows):
    if n_rows == 1:
        return lambda i, *_: (0, 0, 0)
    return lambda i, *_: ((i * block_rows) // seq_len, 0, 0)


def _mod_kernel(c_ref, w_ref, b_ref, o_ref):
    s = _silu(c_ref[...])
    o_ref[0] = jnp.dot(s, w_ref[0], precision=HIGHEST, preferred_element_type=F32) + b_ref[0]


def _modulation(c8, ada_w, ada_b):
    depth, d, six_d = ada_w.shape
    tn = 1024
    return pl.pallas_call(
        _mod_kernel,
        grid=(depth, six_d // tn),
        in_specs=[
            pl.BlockSpec((8, d), lambda l, j: (0, 0)),
            pl.BlockSpec((1, d, tn), lambda l, j: (l, 0, j)),
            pl.BlockSpec((1, 1, tn), lambda l, j: (l, 0, j)),
        ],
        out_specs=pl.BlockSpec((1, 8, tn), lambda l, j: (l, 0, j)),
        out_shape=jax.ShapeDtypeStruct((depth, 8, six_d), F32),
        compiler_params=_params("parallel", "parallel"),
        name="adaln_mod",
    )(c8, ada_w, ada_b.reshape(depth, 1, six_d))


def _proj_kernel(h_ref, g_ref, sh_ref, sc_ref, w_ref, *out_refs):
    u = _rms_mod(h_ref[...], g_ref[...], sc_ref[0], sh_ref[0]).astype(BF16)
    off = 0
    for o_ref, width in zip(out_refs, _PROJ_SEGS):
        o_ref[...] = jnp.dot(u, w_ref[:, off:off + width], preferred_element_type=F32).astype(o_ref.dtype)
        off += width


def _project(h2d, seq_len, g, shift, scale, w_perm):
    n, d = h2d.shape
    t = 256
    total = sum(_PROJ_SEGS)
    nb = shift.shape[0]
    row_map = lambda i: (i, 0)
    return pl.pallas_call(
        _proj_kernel,
        grid=(n // t,),
        in_specs=[
            pl.BlockSpec((t, d), row_map),
            pl.BlockSpec((1, d), lambda i: (0, 0)),
            pl.BlockSpec((1, 1, d), _batch_map(t, seq_len, nb)),
            pl.BlockSpec((1, 1, d), _batch_map(t, seq_len, nb)),
            pl.BlockSpec((d, total), lambda i: (0, 0)),
        ],
        out_specs=[pl.BlockSpec((t, w), row_map) for w in _PROJ_SEGS],
        out_shape=[jax.ShapeDtypeStruct((n, w), dt) for w, dt in zip(_PROJ_SEGS, _PROJ_DTYPES)],
        compiler_params=_params("parallel"),
        name="in_proj",
    )(h2d, g, shift, scale, w_perm)


def _permute_w_in(w_in):
    o = 0
    z = w_in[:, o:o + SSD_WIDTH]; o += SSD_WIDTH
    xbc = w_in[:, o:o + SSD_XBC]; o += SSD_XBC
    dt = w_in[:, o:o + 2 * SSD_HEADS]; o += 2 * SSD_HEADS
    rest = w_in[:, o:]
    dt = jnp.pad(dt, ((0, 0), (0, LANES - 2 * SSD_HEADS)))
    return jnp.concatenate([z, xbc, rest, dt], axis=1).astype(BF16)


def _conv_kernel(x_ref, w_ref, b_ref, o_ref):
    x = x_ref[0]
    n = x.shape[0]
    row = lax.broadcasted_iota(jnp.int32, x.shape, 0)
    prev = jnp.where(row == 0, 0.0, pltpu.roll(x, 1, 0))
    nxt = jnp.where(row == n - 1, 0.0, pltpu.roll(x, n - 1, 0))
    y = prev * w_ref[0:1, :] + x * w_ref[1:2, :] + nxt * w_ref[2:3, :] + b_ref[...]
    o_ref[0] = _silu(y)


def _conv_silu(xbc3, conv_w, conv_b):
    b, l, c = xbc3.shape
    tc = 256
    return pl.pallas_call(
        _conv_kernel,
        grid=(b, c // tc),
        in_specs=[
            pl.BlockSpec((1, l, tc), lambda i, j: (i, 0, j)),
            pl.BlockSpec((3, tc), lambda i, j: (0, j)),
            pl.BlockSpec((1, tc), lambda i, j: (0, j)),
        ],
        out_specs=pl.BlockSpec((1, l, tc), lambda i, j: (i, 0, j)),
        out_shape=jax.ShapeDtypeStruct((b, l, c), F32),
        compiler_params=_params("parallel", "parallel"),
        name="dwconv_silu",
    )(xbc3, conv_w, conv_b.reshape(1, c))


def _ssd_kernel(reverse, xbc_ref, dt_ref, dtt_ref, dtb_row, dtb_col, alog_row, alog_col, init_ref,
                y_ref, fin_ref, state_scr):
    c = pl.program_id(1)
    q = SSD_CHUNK

    @pl.when(c == 0)
    def _():
        state_scr[...] = init_ref[0]

    col0 = SSD_HEADS if reverse else 0
    dt_l = _softplus(dt_ref[0] + dtb_row[...])
    a_l = dt_l * (-jnp.exp(alog_row[...]))
    dt_t = _softplus(dtt_ref[0] + dtb_col[...])
    a_t = dt_t * (-jnp.exp(alog_col[...]))
    row = lax.broadcasted_iota(jnp.int32, (q, q), 0)
    col = lax.broadcasted_iota(jnp.int32, (q, q), 1)
    lower = (row >= col).astype(F32)
    upper = (row <= col).astype(F32)
    cs_l = jnp.dot(lower, a_l, precision=HIGHEST, preferred_element_type=F32)
    cs_t = jnp.dot(a_t, upper, precision=HIGHEST, preferred_element_type=F32)
    tot_l = cs_l[q - 1:q, :]
    if reverse:
        p_l, p_t = cs_l - a_l, cs_t - a_t
        tri = row <= col
    else:
        p_l, p_t = cs_l, cs_t
        tri = row >= col

    tot_t = cs_t[:, q - 1:q]
    to_end_t = jnp.exp(p_t) if reverse else jnp.exp(tot_t - p_t)
    w_t = dt_t * to_end_t
    left = lax.broadcasted_iota(jnp.int32, (q, 2 * SSD_HEAD_DIM), 1) < SSD_HEAD_DIM

    xbc = xbc_ref[0]
    heads_per_group = SSD_HEADS // SSD_GROUPS
    for g in range(SSD_GROUPS):
        b_g = xbc[:, SSD_WIDTH + g * SSD_STATE:SSD_WIDTH + (g + 1) * SSD_STATE]
        c_g = xbc[:, SSD_WIDTH + (SSD_GROUPS + g) * SSD_STATE:SSD_WIDTH + (SSD_GROUPS + g + 1) * SSD_STATE]
        c_bf = c_g.astype(BF16)
        cb = _dot_nt(c_bf, b_g.astype(BF16))
        bt = b_g.T
        for pair in range(g * heads_per_group // 2, (g + 1) * heads_per_group // 2):
            lanes = slice(pair * 2 * SSD_HEAD_DIM, (pair + 1) * 2 * SSD_HEAD_DIM)
            xp = xbc[:, lanes]
            x_bd = jnp.concatenate([jnp.where(left, xp, 0.0), jnp.where(left, 0.0, xp)], axis=0).astype(BF16)
            decay_tiles, state_tiles, pcol_tiles = [], [], []
            for k in (col0 + 2 * pair, col0 + 2 * pair + 1):
                pcol_b = jnp.broadcast_to(p_l[:, k:k + 1], (q, q))
                prow = p_t[k:k + 1, :]
                seg = (prow - pcol_b) if reverse else (pcol_b - prow)
                lmat = jnp.exp(jnp.where(tri, seg, -jnp.inf)) * dt_t[k:k + 1, :]
                decay_tiles.append((cb * lmat).astype(BF16))
                state_tiles.append((bt * w_t[k:k + 1, :]).astype(BF16))
                pcol_tiles.append(pcol_b)
            k0 = col0 + 2 * pair
            tot_pair = jnp.where(left[0:1], tot_l[:, k0:k0 + 1], tot_l[:, k0 + 1:k0 + 2])
            pcol_pair = jnp.where(left, pcol_tiles[0], pcol_tiles[1])
            in_decay = jnp.exp(tot_pair - pcol_pair) if reverse else jnp.exp(pcol_pair)
            s_prev = state_scr[pair]
            y_diag = jnp.dot(jnp.concatenate(decay_tiles, axis=1), x_bd, preferred_element_type=F32)
            y_off = jnp.dot(c_bf, s_prev.astype(BF16), preferred_element_type=F32) * in_decay
            state_scr[pair] = jnp.exp(tot_pair) * s_prev + jnp.dot(jnp.concatenate(state_tiles, axis=1), x_bd,
                                                                   preferred_element_type=F32)
            y_ref[0, :, lanes] = y_diag + y_off

    @pl.when(c == pl.num_programs(1) - 1)
    def _():
        fin_ref[0] = state_scr[...]


def _ssd_scan(xbc_act, dt3, dtt3, dtb, alog, init, reverse):
    b, l, _ = xbc_act.shape
    nc = l // SSD_CHUNK
    cmap = (lambda c: nc - 1 - c) if reverse else (lambda c: c)
    dtb_row = jnp.pad(dtb.reshape(1, -1), ((0, 0), (0, LANES - 2 * SSD_HEADS)))
    alog_row = jnp.pad(alog.reshape(1, -1), ((0, 0), (0, LANES - 2 * SSD_HEADS)))
    small = lambda shape: pl.BlockSpec(shape, lambda i, c: (0, 0))
    st_shape = (b, SSD_HEADS // 2, SSD_STATE, 2 * SSD_HEAD_DIM)
    st_spec = pl.BlockSpec((1,) + st_shape[1:], lambda i, c: (i, 0, 0, 0))
    return pl.pallas_call(
        functools.partial(_ssd_kernel, reverse),
        grid=(b, nc),
        in_specs=[
            pl.BlockSpec((1, SSD_CHUNK, SSD_XBC), lambda i, c: (i, cmap(c), 0)),
            pl.BlockSpec((1, SSD_CHUNK, LANES), lambda i, c: (i, cmap(c), 0)),
            pl.BlockSpec((1, 2 * SSD_HEADS, SSD_CHUNK), lambda i, c: (i, 0, cmap(c))),
            small((1, LANES)), small((2 * SSD_HEADS, 1)), small((1, LANES)), small((2 * SSD_HEADS, 1)),
            st_spec,
        ],
        out_specs=[pl.BlockSpec((1, SSD_CHUNK, SSD_WIDTH), lambda i, c: (i, cmap(c), 0)), st_spec],
        out_shape=[jax.ShapeDtypeStruct((b, l, SSD_WIDTH), F32), jax.ShapeDtypeStruct(st_shape, F32)],
        scratch_shapes=[pltpu.VMEM(st_shape[1:], F32)],
        compiler_params=_params("parallel", "arbitrary"),
        name="ssd_scan_bwd" if reverse else "ssd_scan_fwd",
    )(xbc_act, dt3, dtt3, dtb_row, dtb.reshape(-1, 1), alog_row, alog.reshape(-1, 1), init)


def _pool_kernel(x_ref, w_ref, sc_ref, o_ref, pad_scr):
    n = x_ref.shape[1]
    zeros = jnp.zeros((POOL_PAD, POOL_GROUP_DIM), F32)
    pad_scr[0:POOL_PAD, :] = zeros
    pad_scr[n + POOL_PAD:n + 2 * POOL_PAD, :] = zeros
    t = lax.broadcasted_iota(jnp.int32, (n, 1), 0)
    for gi, w in enumerate(POOL_WINDOWS):
        sl = slice(gi * POOL_GROUP_DIM, (gi + 1) * POOL_GROUP_DIM)
        x = x_ref[0, :, sl]
        pad_scr[POOL_PAD:n + POOL_PAD, :] = x
        acc = jnp.zeros_like(x)
        for o in range(-(w // 2), w - w // 2):
            acc = acc + pad_scr[POOL_PAD + o:POOL_PAD + o + n, :]
        lo = jnp.maximum(t - w // 2, 0)
        hi = jnp.minimum(t + (w - w // 2 - 1), n - 1)
        pooled = acc / (hi - lo + 1).astype(F32) - x
        y = jnp.dot(pooled.astype(BF16), w_ref[gi].astype(BF16), preferred_element_type=F32)
        o_ref[0, :, sl] = y * sc_ref[:, sl]


def _pool_mixer(u3, w_pool, scale):
    b, l, c = u3.shape
    return pl.pallas_call(
        _pool_kernel,
        grid=(b,),
        in_specs=[
            pl.BlockSpec((1, l, c), lambda i: (i, 0, 0)),
            pl.BlockSpec(w_pool.shape, lambda i: (0, 0, 0)),
            pl.BlockSpec((1, c), lambda i: (0, 0)),
        ],
        out_specs=pl.BlockSpec((1, l, c), lambda i: (i, 0, 0)),
        out_shape=jax.ShapeDtypeStruct((b, l, c), F32),
        scratch_shapes=[pltpu.VMEM((l + 2 * POOL_PAD, POOL_GROUP_DIM), F32)],
        compiler_params=_params("parallel"),
        name="pool_mixer",
    )(u3, w_pool, scale.reshape(1, c))


def _na_bias_kernel(rpb_ref, o_ref):
    h = pl.program_id(0)
    dr = pl.program_id(1)
    qi = lax.broadcasted_iota(jnp.int32, (GRID_W, LANES), 0)
    lane = lax.broadcasted_iota(jnp.int32, (GRID_W, LANES), 1)
    ki = lane % GRID_W
    second = lane >= GRID_W
    start = jnp.clip(qi - NA_KW // 2, 0, GRID_W - NA_KW)
    in_window = (ki >= start) & (ki < start + NA_KW)
    dc = jnp.clip(ki - qi, -(NA_KW - 1), NA_KW - 1) + NA_KW - 1
    val = jnp.zeros((GRID_W, LANES), F32)
    for j in range(2 * NA_KW - 1):
        pick = jnp.where(second, rpb_ref[h, dr + 1, j], rpb_ref[h, dr, j])
        val = jnp.where(dc == j, pick, val)
    o_ref[0, 0] = jnp.where(in_window, val, -jnp.inf)


def _na_bias(rpb):
    nh, ndr, ndc = rpb.shape
    return pl.pallas_call(
        _na_bias_kernel,
        grid=(nh, ndr - 1),
        in_specs=[pl.BlockSpec(memory_space=pltpu.SMEM)],
        out_specs=pl.BlockSpec((1, 1, GRID_W, LANES), lambda h, d: (h, d, 0, 0)),
        out_shape=jax.ShapeDtypeStruct((nh, ndr - 1, GRID_W, LANES), F32),
        compiler_params=_params("parallel", "parallel"),
        name="na_bias",
    )(rpb)


def _na_kernel(kh, q_ref, k_ref, v_ref, kc_ref, vc_ref, bias_ref, o_ref):
    r = pl.program_id(1)
    rows = pl.num_programs(1)
    r0 = jnp.clip(r - kh // 2, 0, rows - kh)
    start = pl.multiple_of(r0 * GRID_W, GRID_W)
    kblk = k_ref[0, pl.ds(start, kh * GRID_W), :]
    vblk = v_ref[0, pl.ds(start, kh * GRID_W), :]
    dr0 = r0 - r + NA_KH - 1
    scale = NA_HEAD_DIM ** -0.5
    for h in range(NA_HEADS):
        sl = slice(h * NA_HEAD_DIM, (h + 1) * NA_HEAD_DIM)
        qh = q_ref[0, :, sl] * scale
        s_loc = _dot_nt(qh, kblk[:, sl])
        bias = jnp.concatenate([bias_ref[h, dr0 + 2 * j] for j in range(kh // 2)], axis=1)
        s_loc = s_loc + bias
        s_ctx = _dot_nt(qh, kc_ref[0, :, sl])
        m = jnp.maximum(jnp.max(s_loc, axis=-1, keepdims=True), jnp.max(s_ctx, axis=-1, keepdims=True))
        p_loc = jnp.exp(s_loc - m)
        p_ctx = jnp.exp(s_ctx - m)
        den = jnp.sum(p_loc, axis=-1, keepdims=True) + jnp.sum(p_ctx, axis=-1, keepdims=True)
        acc = jnp.dot(p_loc.astype(BF16), vblk[:, sl], preferred_element_type=F32)
        acc = acc + jnp.dot(p_ctx.astype(BF16), vc_ref[0, :, sl], preferred_element_type=F32)
        o_ref[0, :, sl] = acc / den


def _neighbourhood_attention(q3, k3, v3, kc3, vc3, bias):
    b, s, c = q3.shape
    rows = s // GRID_W
    kh = min(NA_KH, rows)
    lc = kc3.shape[1]
    full = lambda n: pl.BlockSpec((1, n, c), lambda i, r: (i, 0, 0))
    return pl.pallas_call(
        functools.partial(_na_kernel, kh),
        grid=(b, rows),
        in_specs=[
            pl.BlockSpec((1, GRID_W, c), lambda i, r: (i, r, 0)),
            full(s), full(s), full(lc), full(lc),
            pl.BlockSpec(bias.shape, lambda i, r: (0, 0, 0, 0)),
        ],
        out_specs=pl.BlockSpec((1, GRID_W, c), lambda i, r: (i, r, 0)),
        out_shape=jax.ShapeDtypeStruct((b, s, c), F32),
        compiler_params=_params("parallel", "arbitrary"),
        name="na_attention",
    )(q3, k3, v3, kc3, vc3, bias)


def _ctx_attn_kernel(q_ref, k_ref, v_ref, o_ref):
    scale = NA_HEAD_DIM ** -0.5
    for h in range(NA_HEADS):
        sl = slice(h * NA_HEAD_DIM, (h + 1) * NA_HEAD_DIM)
        s = _dot_nt(q_ref[0, :, sl] * scale, k_ref[0, :, sl])
        p = jnp.exp(s - jnp.max(s, axis=-1, keepdims=True))
        den = jnp.sum(p, axis=-1, keepdims=True)
        o_ref[0, :, sl] = jnp.dot(p.astype(BF16), v_ref[0, :, sl], preferred_element_type=F32) / den


def _context_attention(q3, k3, v3):
    b, l, c = q3.shape
    spec = pl.BlockSpec((1, l, c), lambda i: (i, 0, 0))
    return pl.pallas_call(
        _ctx_attn_kernel,
        grid=(b,),
        in_specs=[spec, spec, spec],
        out_specs=spec,
        out_shape=jax.ShapeDtypeStruct((b, l, c), F32),
        compiler_params=_params("parallel"),
        name="ctx_attention",
    )(q3, k3, v3)


def _mix_out_kernel(yf_ref, yb_ref, xs_ref, z_ref, pool_ref, na_ref, h_ref, dsk_ref, ng_ref, g1_ref, w_ref, o_ref):
    y = yf_ref[...] + yb_ref[...] + dsk_ref[...] * xs_ref[...]
    y = y * _silu(z_ref[...])
    ms = jnp.mean(y * y, axis=-1, keepdims=True)
    yn = (y * lax.rsqrt(ms + RMS_EPS)) * ng_ref[...]
    mix = jnp.dot(yn.astype(BF16), w_ref[0:SSD_WIDTH, :], preferred_element_type=F32)
    mix = mix + jnp.dot(pool_ref[...].astype(BF16), w_ref[SSD_WIDTH:SSD_WIDTH + POOL_WIDTH, :],
                        preferred_element_type=F32)
    mix = mix + jnp.dot(na_ref[...].astype(BF16), w_ref[SSD_WIDTH + POOL_WIDTH:, :], preferred_element_type=F32)
    o_ref[...] = h_ref[...] + g1_ref[0] * mix


def _mix_out(yf, yb, xbc_act2d, z, pool_y, na_y, h2d, seq_len, d_skip, norm_g, g1, w_out_bf):
    n, d = h2d.shape
    t = 256
    row = lambda w: pl.BlockSpec((t, w), lambda i: (i, 0))
    vec = lambda w: pl.BlockSpec((1, w), lambda i: (0, 0))
    return pl.pallas_call(
        _mix_out_kernel,
        grid=(n // t,),
        in_specs=[
            row(SSD_WIDTH), row(SSD_WIDTH), row(SSD_WIDTH), row(SSD_WIDTH), row(POOL_WIDTH), row(NA_WIDTH), row(d),
            vec(SSD_WIDTH), vec(SSD_WIDTH),
            pl.BlockSpec((1, 1, d), _batch_map(t, seq_len, g1.shape[0])),
            pl.BlockSpec(w_out_bf.shape, lambda i: (0, 0)),
        ],
        out_specs=row(d),
        out_shape=jax.ShapeDtypeStruct((n, d), F32),
        compiler_params=_params("parallel"),
        name="mix_out",
    )(yf, yb, xbc_act2d, z, pool_y, na_y, h2d, jnp.repeat(d_skip, SSD_HEAD_DIM).reshape(1, -1),
      norm_g.reshape(1, -1), g1, w_out_bf)


_CAND_ROWS = 16 + 8 * 7 + 8


def _top16_pair_fast(s1, s2):
    t = s1.shape[1]
    row16 = lax.broadcasted_iota(jnp.int32, (PEER_TOPK, t), 0)

    def body(j, carry):
        p1, v1, p2, v2, below = carry
        m1 = jnp.max(jnp.where(s1 < p1, s1, -jnp.inf), axis=0, keepdims=True)
        lower = s2 < p2
        m2 = jnp.max(jnp.where(lower, s2, -jnp.inf), axis=0, keepdims=True)
        return (m1, jnp.where(row16 == j, m1, v1), m2, jnp.where(row16 == j, m2, v2),
                below + jnp.where(lower, 1.0, 0.0))

    top = jnp.full((1, t), jnp.inf, F32)
    zeros = jnp.zeros((PEER_TOPK, t), F32)
    m1, v1, m2, v2, below = lax.fori_loop(0, PEER_TOPK, body, (top, zeros, top, zeros, jnp.full(s2.shape, -1.0, F32)))
    rank2 = below + jnp.where(s2 < m2, 1.0, 0.0)
    count = lambda s, m: jnp.sum(jnp.where(s >= m, 1.0, 0.0), axis=0, keepdims=True)
    return v1, count(s1, m1), v2, rank2, count(s2, m2)


def _first_counts_fast(s1, v1, sel):
    inf = jnp.inf
    rows_v1 = jnp.concatenate([jnp.broadcast_to(v1[0:1], (PEER_TOPK, v1.shape[1]))]
                              + [jnp.broadcast_to(v1[j:j + 1], (8, v1.shape[1])) for j in range(1, 8)]
                              + [v1[8:16]], axis=0)
    bound = jnp.where(sel > 0.0, rows_v1, inf)
    low = bound[0:8]
    for j in range(1, 8):
        low = jnp.minimum(low, bound[16 + 8 * (j - 1):16 + 8 * j])
    tail = jnp.min(bound[64 + 8:64 + 16], axis=0, keepdims=True)
    cnt = jnp.where(s1 >= jnp.minimum(low[0:1], tail), 1.0, 0.0)
    for k in range(1, 8):
        cnt = cnt + jnp.where(s1 >= low[k:k + 1], 1.0, 0.0)
    n_high = jnp.sum(sel[8:16], axis=0, keepdims=True)
    return cnt + jnp.where(s1 >= v1[0:1], n_high, 0.0)


def _first_counts_exact(rank1, sel):
    cnt = jnp.zeros(rank1.shape, F32)
    for j in range(8):
        lo = 0 if j == 0 else 16 + 8 * (j - 1)
        n_j = jnp.sum(sel[lo:lo + (16 if j == 0 else 8)], axis=0, keepdims=True)
        cnt = cnt + jnp.where(rank1 == float(j), n_j, 0.0)
    for j in range(8, 16):
        cnt = cnt + jnp.where(rank1 == float(j), sel[64 + j:65 + j], 0.0)
    return cnt


def _top16_pair_exact(s1, s2):
    n, t = s1.shape
    iota = lax.broadcasted_iota(jnp.int32, (n, t), 0).astype(F32)
    row16 = lax.broadcasted_iota(jnp.int32, (PEER_TOPK, t), 0)

    def pick(work, rank, vals, j):
        m = jnp.max(work, axis=0, keepdims=True)
        idx = jnp.min(jnp.where(work == m, iota, float(n)), axis=0, keepdims=True)
        sel = iota == idx
        return (jnp.where(sel, -jnp.inf, work), jnp.where(sel, lax.convert_element_type(j, F32), rank),
                jnp.where(row16 == j, m, vals))

    def body(j, carry):
        a, b = carry
        return pick(*a, j), pick(*b, j)

    start = lambda s: (s, jnp.full((n, t), float(PEER_TOPK), F32), jnp.zeros((PEER_TOPK, t), F32))
    (_, rank1, v1), (_, rank2, v2) = lax.fori_loop(0, PEER_TOPK, body, (start(s1), start(s2)))
    return rank1, v1, rank2, v2


def _select16_fast(cand):
    def body(_, prev):
        return jnp.max(jnp.where(cand < prev, cand, -jnp.inf), axis=0, keepdims=True)

    m = lax.fori_loop(0, PEER_TOPK, body, jnp.full((1, cand.shape[1]), jnp.inf, F32))
    taken = jnp.where(cand >= m, 1.0, 0.0)
    return taken, jnp.sum(taken, axis=0, keepdims=True)


def _select16_exact(cand):
    iota = lax.broadcasted_iota(jnp.int32, cand.shape, 0).astype(F32)

    def body(_, carry):
        work, sel_acc = carry
        m = jnp.max(work, axis=0, keepdims=True)
        idx = jnp.min(jnp.where(work == m, iota, float(_CAND_ROWS)), axis=0, keepdims=True)
        sel = iota == idx
        return jnp.where(sel, -jnp.inf, work), jnp.where(sel, 1.0, sel_acc)

    return lax.fori_loop(0, PEER_TOPK, body, (cand, jnp.zeros(cand.shape, F32)))[1]


def _candidate_sums(v1, v2):
    blocks = [v1[0:1] + v2]
    for j in range(1, 8):
        blocks.append(v1[j:j + 1] + v2[0:8])
    blocks.append(v1[8:16] + v2[0:1])
    return jnp.concatenate(blocks, axis=0)


def _any_not_16(*counts):
    return jnp.max(sum(jnp.abs(c - float(PEER_TOPK)) for c in counts)) > 0.0


def _bf16_bits(x):
    return pltpu.bitcast(x.astype(BF16).astype(F32), jnp.uint32)


def _pack_row_pairs(x, scr):
    n, t = x.shape
    for j in range(t // LANES):
        scr[j] = x[:, j * LANES:(j + 1) * LANES]
    words = []
    for j in range(t // LANES):
        even = scr[j, pl.ds(0, n // 2, stride=2), :]
        odd = scr[j, pl.ds(1, n // 2, stride=2), :]
        words.append((_bf16_bits(even) >> 16) | _bf16_bits(odd))
    return jnp.concatenate(words, axis=1)


def _pack_same(x):
    w = _bf16_bits(x)
    return w | (w >> 16)


def _peer_score_kernel(h_ref, g_ref, sh_ref, sc_ref, wq_ref, keys_ref,
                       x_out, cnt_out, e1_out, rk_out, e2_out,
                       u_scr, pair_scr, cnt_scr, rank2_scr, top_scr):
    hd = pl.program_id(1)

    @pl.when(hd == 0)
    def _():
        u = _rms_mod(h_ref[...], g_ref[...], sc_ref[0], sh_ref[0]).astype(BF16)
        u_scr[...] = u
        x_out[...] = pltpu.bitcast(u, jnp.uint32)

    q = jnp.dot(u_scr[...], wq_ref[0], preferred_element_type=F32).astype(BF16)
    s1 = _dot_nt(keys_ref[0, 0], q[:, :PEER_SUB])
    s2 = _dot_nt(keys_ref[0, 1], q[:, PEER_SUB:])
    v1, n1, v2, rank2, n2 = _top16_pair_fast(s1, s2)
    cand = _candidate_sums(v1, v2)
    sel, n_sel = _select16_fast(cand)
    cnt_scr[...] = _first_counts_fast(s1, v1, sel)
    rank2_scr[...] = rank2
    top_scr[0:1] = v1[0:1]
    top_scr[1:2] = v2[0:1]
    top_scr[2:3] = jnp.sum(sel * jnp.exp(cand - cand[0:1]), axis=0, keepdims=True)

    @pl.when(_any_not_16(n1, n2, n_sel))
    def _():
        rank1_x, v1_x, rank2_x, v2_x = _top16_pair_exact(s1, s2)
        cand_x = _candidate_sums(v1_x, v2_x)
        sel_x = _select16_exact(cand_x)
        cnt_scr[...] = _first_counts_exact(rank1_x, sel_x)
        rank2_scr[...] = rank2_x
        top_scr[0:1] = v1_x[0:1]
        top_scr[1:2] = v2_x[0:1]
        top_scr[2:3] = jnp.sum(sel_x * jnp.exp(cand_x - cand_x[0:1]), axis=0, keepdims=True)

    cnt_out[0] = _pack_same(cnt_scr[...])
    e1_out[0] = _pack_same(jnp.exp(s1 - top_scr[0:1]))
    rk_out[0] = _pack_row_pairs(rank2_scr[...], pair_scr)
    e2_out[0] = _pack_row_pairs(jnp.exp(s2 - top_scr[1:2]) / top_scr[2:3], pair_scr)


def _peer_scores(h2d, seq_len, g, shift, scale, wq_heads, keys_bf):
    n, d = h2d.shape
    nb = shift.shape[0]
    t = 512
    assert n % t == 0 and (nb == 1 or seq_len % t == 0)
    bm = _batch_map(t, seq_len, nb)
    mod_spec = pl.BlockSpec((1, 1, d), lambda i, hd: bm(i))
    first_out = pl.BlockSpec((1, PEER_NKEYS, t), lambda i, hd: (hd, 0, i))
    first_shape = jax.ShapeDtypeStruct((PEER_HEADS, PEER_NKEYS, n), jnp.uint32)
    second_out = pl.BlockSpec((1, PEER_NKEYS // 2, t), lambda i, hd: (hd, 0, i))
    second_shape = jax.ShapeDtypeStruct((PEER_HEADS, PEER_NKEYS // 2, n), jnp.uint32)
    return pl.pallas_call(
        _peer_score_kernel,
        grid=(n // t, PEER_HEADS),
        in_specs=[
            pl.BlockSpec((t, d), lambda i, hd: (i, 0)),
            pl.BlockSpec((1, d), lambda i, hd: (0, 0)),
            mod_spec, mod_spec,
            pl.BlockSpec((1, d, 2 * PEER_SUB), lambda i, hd: (hd, 0, 0)),
            pl.BlockSpec((1, 2, PEER_NKEYS, PEER_SUB), lambda i, hd: (hd, 0, 0, 0)),
        ],
        out_specs=[pl.BlockSpec((t // 2, d), lambda i, hd: (i, 0)), first_out, first_out, second_out, second_out],
        out_shape=[jax.ShapeDtypeStruct((n // 2, d), jnp.uint32), first_shape, first_shape, second_shape,
                   second_shape],
        scratch_shapes=[pltpu.VMEM((t, d), BF16), pltpu.VMEM((t // LANES, PEER_NKEYS, LANES), F32),
                        pltpu.VMEM((PEER_NKEYS, t), F32), pltpu.VMEM((PEER_NKEYS, t), F32),
                        pltpu.VMEM((8, t), F32)],
        compiler_params=_params("parallel", "arbitrary"),
        name="peer_scores",
    )(h2d, g, shift, scale, wq_heads, keys_bf)


_PEER_EC = 1024
_INV_SQRT2 = 1.0 / math.sqrt(2.0)


def _as_bf16_rows(words):
    return pltpu.bitcast(words, BF16)


_PEER_A_GROUP = 4


def _peer_expert_kernel(n_chunks, x_ref, u_ref, vt_ref, cnt_ref, e1_ref, rk_ref, e2_ref, h_ref, g2_ref, o_ref,
                        ht0_scr, ht1_scr, g0_scr, g1_scr, acc_scr):
    s = pl.program_id(0)
    t = h_ref.shape[0]
    p2 = s - 2
    c2 = lax.rem(jnp.maximum(p2, 0), n_chunks)
    pack = 2 * 8
    n_a = _PEER_EC // PEER_NKEYS

    @pl.when(s == 0)
    def _():
        ht1_scr[...] = jnp.zeros_like(ht1_scr)
        g0_scr[...] = jnp.zeros_like(g0_scr)
        g1_scr[...] = jnp.zeros_like(g1_scr)

    @pl.when((p2 <= 0) | (c2 == 0))
    def _():
        acc_scr[...] = jnp.zeros_like(acc_scr)

    def step(ht_new, ht_old, g_new, g_old):
        ht_new[...] = _dot_nt(_as_bf16_rows(u_ref[...]), _as_bf16_rows(x_ref[...]))
        for tc in range(t // LANES):
            ls = slice(tc * LANES, (tc + 1) * LANES)
            for a0 in range(0, n_a, _PEER_A_GROUP):
                w = [[None] * (PEER_NKEYS // pack) for _ in range(_PEER_A_GROUP)]
                for hd in range(PEER_HEADS):
                    rows1 = [(_as_bf16_rows(jnp.broadcast_to(cnt_ref[hd, a0 + i, :, ls], (8, LANES))),
                              _as_bf16_rows(jnp.broadcast_to(e1_ref[hd, a0 + i, :, ls], (8, LANES))))
                             for i in range(_PEER_A_GROUP)]
                    for r in range(PEER_NKEYS // pack):
                        rk = _as_bf16_rows(rk_ref[hd, r * 8:(r + 1) * 8, ls])
                        e2 = _as_bf16_rows(e2_ref[hd, r * 8:(r + 1) * 8, ls])
                        for i, (cn, e1) in enumerate(rows1):
                            term = jnp.where(rk < cn, e2, 0.0) * e1
                            w[i][r] = term if w[i][r] is None else w[i][r] + term
                for i in range(_PEER_A_GROUP):
                    for r in range(PEER_NKEYS // pack):
                        row0 = (a0 + i) * PEER_NKEYS + r * pack
                        hs = ht_old[row0:row0 + pack, ls]
                        act = 0.5 * hs * (1.0 + lax.erf(hs * _INV_SQRT2))
                        g_new[row0 // 2:(row0 + pack) // 2, ls] = pltpu.bitcast(w[i][r] * act.astype(BF16), jnp.uint32)
        acc_scr[...] += jnp.dot(_as_bf16_rows(vt_ref[...]), _as_bf16_rows(g_old[...]),
                                preferred_element_type=F32)

    @pl.when(s % 2 == 0)
    def _():
        step(ht0_scr, ht1_scr, g1_scr, g0_scr)

    @pl.when(s % 2 == 1)
    def _():
        step(ht1_scr, ht0_scr, g0_scr, g1_scr)

    @pl.when((p2 >= 0) & (c2 == n_chunks - 1))
    def _():
        o_ref[...] = h_ref[...] + g2_ref[0] * acc_scr[...].T


def _peer_experts(x_pk, u_pk, vt_pk, cnt, e1, rk, e2, h2d, seq_len, g2):
    n, d = h2d.shape
    t = 512
    n_chunks = 2 * u_pk.shape[0] // _PEER_EC
    total = (n // t) * n_chunks
    ea = _PEER_EC // PEER_NKEYS

    def pair(p):
        p = jnp.clip(p, 0, total - 1)
        return p // n_chunks, lax.rem(p, n_chunks)

    blk = lambda lag: (lambda s: pair(s - lag)[0])
    chk = lambda lag: (lambda s: pair(s - lag)[1])
    tok = pl.BlockSpec((PEER_HEADS, PEER_NKEYS // 2, t), lambda s: (0, 0, blk(1)(s)))
    first = pl.BlockSpec((PEER_HEADS, ea, 1, t), lambda s: (0, chk(1)(s), 0, blk(1)(s)))
    per_first = lambda a: a.reshape(PEER_HEADS, PEER_NKEYS, 1, n)
    bm = _batch_map(t, seq_len, g2.shape[0])
    return pl.pallas_call(
        functools.partial(_peer_expert_kernel, n_chunks),
        grid=(total + 2,),
        in_specs=[
            pl.BlockSpec((t // 2, d), lambda s: (blk(0)(s), 0)),
            pl.BlockSpec((_PEER_EC // 2, d), lambda s: (chk(0)(s), 0)),
            pl.BlockSpec((d // 2, _PEER_EC), lambda s: (0, chk(2)(s))),
            first, first, tok, tok,
            pl.BlockSpec((t, d), lambda s: (blk(2)(s), 0)),
            pl.BlockSpec((1, 1, d), lambda s: bm(blk(2)(s))),
        ],
        out_specs=pl.BlockSpec((t, d), lambda s: (blk(2)(s), 0)),
        out_shape=jax.ShapeDtypeStruct((n, d), F32),
        scratch_shapes=[pltpu.VMEM((_PEER_EC, t), F32), pltpu.VMEM((_PEER_EC, t), F32),
                        pltpu.VMEM((_PEER_EC // 2, t), jnp.uint32), pltpu.VMEM((_PEER_EC // 2, t), jnp.uint32),
                        pltpu.VMEM((d, t), F32)],
        compiler_params=_params("arbitrary"),
        name="peer_experts",
    )(x_pk, u_pk, vt_pk, per_first(cnt), per_first(e1), rk, e2, h2d, g2)


def _pack_bf16_rows(a):
    r, c = a.shape
    pairs = a.astype(BF16).reshape(r // 2, 2, c).transpose(0, 2, 1)
    return lax.bitcast_convert_type(pairs, jnp.uint32)


def _peer_ffn_residual(h2d, seq_len, norm_g, shift, scale, gate, wq_heads, keys_bf, u_pk, vt_pk):
    x_pk, cnt, e1, rk, e2 = _peer_scores(h2d, seq_len, norm_g, shift, scale, wq_heads, keys_bf)
    return _peer_experts(x_pk, u_pk, vt_pk, cnt, e1, rk, e2, h2d, seq_len, gate)


def _final_norm_kernel(h_ref, g_ref, o_ref):
    x = h_ref[...]
    ms = jnp.mean(x * x, axis=-1, keepdims=True)
    o_ref[...] = (x * lax.rsqrt(ms + RMS_EPS)) * g_ref[...]


def _final_norm(h2d, g):
    n, d = h2d.shape
    t = 512
    return pl.pallas_call(
        _final_norm_kernel,
        grid=(n // t,),
        in_specs=[pl.BlockSpec((t, d), lambda i: (i, 0)), pl.BlockSpec((1, d), lambda i: (0, 0))],
        out_specs=pl.BlockSpec((t, d), lambda i: (i, 0)),
        out_shape=jax.ShapeDtypeStruct((n, d), F32),
        compiler_params=_params("parallel"),
        name="final_norm",
    )(h2d, g.reshape(1, d))


def _mixer_inputs(h2d, batch, seq_len, norm_g, shift, scale, w_perm, conv_w, conv_b):
    z, xbc, pool_u, q, k, v, dt = _project(h2d, seq_len, norm_g, shift, scale, w_perm)
    xbc_act = _conv_silu(xbc.reshape(batch, seq_len, SSD_XBC), conv_w, conv_b)
    dt3 = dt.reshape(batch, seq_len, LANES)
    dtt3 = jnp.swapaxes(dt3[:, :, :2 * SSD_HEADS], 1, 2)
    r3 = lambda a: a.reshape(batch, seq_len, a.shape[-1])
    return z, xbc_act, dt3, dtt3, r3(pool_u), r3(q), r3(k), r3(v)


def kernel(x, c, ctx, c_ctx, ada_w, ada_b, norm1_g, w_in, conv_w, conv_b, a_log, dt_bias, d_skip, ssd_norm_g, pool_w, pool_scale, na_rpb, w_out, norm2_g, peer_wq, peer_keys, peer_u, peer_v, final_g):
    batch, seq, d = x.shape
    ctx_len = ctx.shape[1]
    n, nc = batch * seq, batch * ctx_len
    h = x.reshape(n, d)
    hc = ctx.reshape(nc, d)

    c8 = jnp.concatenate([c, c_ctx[None], jnp.zeros((8 - batch - 1, d), F32)], axis=0)
    mod = _modulation(c8, ada_w, ada_b)

    for i in range(DEPTH):
        need_ctx_out = i < DEPTH - 1
        lat = [mod[i, :batch, j * d:(j + 1) * d].reshape(batch, 1, d) for j in range(6)]
        cx = [mod[i, batch:batch + 1, j * d:(j + 1) * d].reshape(1, 1, d) for j in range(6)]
        sh1, sc1, g1, sh2, sc2, g2 = lat
        csh1, csc1, cg1, csh2, csc2, cg2 = cx

        w_perm = _permute_w_in(w_in[i])
        w_out_bf = w_out[i].astype(BF16)
        n1 = norm1_g[i].reshape(1, d)
        n2 = norm2_g[i].reshape(1, d)
        zero_state = jnp.zeros((batch, SSD_HEADS // 2, SSD_STATE, 2 * SSD_HEAD_DIM), F32)
        scan = functools.partial(_ssd_scan, dtb=dt_bias[i], alog=a_log[i])
        wq_heads = peer_wq[i].reshape(d, PEER_HEADS, 2 * PEER_SUB).transpose(1, 0, 2).astype(BF16)
        keys_bf = peer_keys[i].astype(BF16)
        u_pk = _pack_bf16_rows(peer_u[i])
        vt_pk = _pack_bf16_rows(peer_v[i].T)

        zc, xbc_c, dt3_c, dtt3_c, pool_c, qc, kc, vc = _mixer_inputs(
            hc, batch, ctx_len, n1, csh1, csc1, w_perm, conv_w[i], conv_b[i])
        yf_c, st_f = scan(xbc_c, dt3_c, dtt3_c, init=zero_state, reverse=False)
        yb_c, st_b = scan(xbc_c, dt3_c, dtt3_c, init=zero_state, reverse=True)
        if need_ctx_out:
            pool_yc = _pool_mixer(pool_c, pool_w[i], pool_scale[i])
            att_c = _context_attention(qc, kc, vc)
            hc = _mix_out(yf_c.reshape(nc, -1), yb_c.reshape(nc, -1), xbc_c.reshape(nc, -1), zc,
                          pool_yc.reshape(nc, -1), att_c.reshape(nc, -1), hc, ctx_len,
                          d_skip[i], ssd_norm_g[i], cg1, w_out_bf)
            hc = _peer_ffn_residual(hc, ctx_len, n2, csh2, csc2, cg2, wq_heads, keys_bf, u_pk, vt_pk)

        z, xbc_l, dt3_l, dtt3_l, pool_l, q, k, v = _mixer_inputs(
            h, batch, seq, n1, sh1, sc1, w_perm, conv_w[i], conv_b[i])
        yf, _ = scan(xbc_l, dt3_l, dtt3_l, init=st_f, reverse=False)
        yb, _ = scan(xbc_l, dt3_l, dtt3_l, init=st_b, reverse=True)
        pool_y = _pool_mixer(pool_l, pool_w[i], pool_scale[i])
        na = _neighbourhood_attention(q, k, v, kc, vc, _na_bias(na_rpb[i]))
        h = _mix_out(yf.reshape(n, -1), yb.reshape(n, -1), xbc_l.reshape(n, -1), z,
                     pool_y.reshape(n, -1), na.reshape(n, -1), h, seq,
                     d_skip[i], ssd_norm_g[i], g1, w_out_bf)
        h = _peer_ffn_residual(h, seq, n2, sh2, sc2, g2, wq_heads, keys_bf, u_pk, vt_pk)

    return _final_norm(h, final_g).reshape(batch, seq, d)
```

```python
import functools
import math

import jax
import jax.numpy as jnp
from jax import lax
from jax.experimental import pallas as pl
from jax.experimental.pallas import tpu as pltpu

F32 = jnp.float32
BF16 = jnp.bfloat16
HIGHEST = lax.Precision.HIGHEST

D_MODEL = 1024
DEPTH = 2
GRID_W = 64
RMS_EPS = 1e-6

SSD_HEAD_DIM = 64
SSD_HEADS = 16
SSD_GROUPS = 2
SSD_STATE = 128
SSD_CHUNK = 128
SSD_WIDTH = SSD_HEADS * SSD_HEAD_DIM
SSD_XBC = SSD_WIDTH + 2 * SSD_GROUPS * SSD_STATE

POOL_WINDOWS = (2, 4, 8, 16)
POOL_GROUP_DIM = 128
POOL_WIDTH = POOL_GROUP_DIM * len(POOL_WINDOWS)
POOL_PAD = 8

NA_HEAD_DIM = 64
NA_HEADS = 8
NA_WIDTH = NA_HEADS * NA_HEAD_DIM
NA_KH = 8
NA_KW = 16

PEER_HEADS = 8
PEER_NKEYS = 128
PEER_TOPK = 16
PEER_SUB = 128

LANES = 128
VMEM_LIMIT_BYTES = 56 * 1024 * 1024

_PROJ_SEGS = (SSD_WIDTH, SSD_XBC, POOL_WIDTH, NA_WIDTH, NA_WIDTH, NA_WIDTH, LANES)
_PROJ_DTYPES = (F32, F32, F32, BF16, BF16, BF16, F32)


def _params(*sem):
    return pltpu.CompilerParams(dimension_semantics=sem, vmem_limit_bytes=VMEM_LIMIT_BYTES)


def _rms_mod(x, g, scale, shift):
    ms = jnp.mean(x * x, axis=-1, keepdims=True)
    return (x * lax.rsqrt(ms + RMS_EPS)) * g * (1.0 + scale) + shift


def _silu(x):
    return x * jax.nn.sigmoid(x)


def _softplus(x):
    return jnp.maximum(x, 0.0) + jnp.log1p(jnp.exp(-jnp.abs(x)))


def _dot_nt(a, b):
    return lax.dot_general(a, b, (((1,), (1,)), ((), ())), preferred_element_type=F32)


def _batch_map(block_rows, seq_len, n_rows):
    if n_rows == 1:
        return lambda i, *_: (0, 0, 0)
    return lambda i, *_: ((i * block_rows) // seq_len, 0, 0)


def _mod_kernel(c_ref, w_ref, b_ref, o_ref):
    s = _silu(c_ref[...])
    o_ref[0] = jnp.dot(s, w_ref[0], precision=HIGHEST, preferred_element_type=F32) + b_ref[0]


def _modulation(c8, ada_w, ada_b):
    depth, d, six_d = ada_w.shape
    tn = 1024
    return pl.pallas_call(
        _mod_kernel,
        grid=(depth, six_d // tn),
        in_specs=[
            pl.BlockSpec((8, d), lambda l, j: (0, 0)),
            pl.BlockSpec((1, d, tn), lambda l, j: (l, 0, j)),
            pl.BlockSpec((1, 1, tn), lambda l, j: (l, 0, j)),
        ],
        out_specs=pl.BlockSpec((1, 8, tn), lambda l, j: (l, 0, j)),
        out_shape=jax.ShapeDtypeStruct((depth, 8, six_d), F32),
        compiler_params=_params("parallel", "parallel"),
        name="adaln_mod",
    )(c8, ada_w, ada_b.reshape(depth, 1, six_d))


def _proj_kernel(h_ref, g_ref, sh_ref, sc_ref, w_ref, *out_refs):
    u = _rms_mod(h_ref[...], g_ref[...], sc_ref[0], sh_ref[0]).astype(BF16)
    off = 0
    for o_ref, width in zip(out_refs, _PROJ_SEGS):
        o_ref[...] = jnp.dot(u, w_ref[:, off:off + width], preferred_element_type=F32).astype(o_ref.dtype)
        off += width


def _project(h2d, seq_len, g, shift, scale, w_perm):
    n, d = h2d.shape
    t = 256
    total = sum(_PROJ_SEGS)
    nb = shift.shape[0]
    row_map = lambda i: (i, 0)
    return pl.pallas_call(
        _proj_kernel,
        grid=(n // t,),
        in_specs=[
            pl.BlockSpec((t, d), row_map),
            pl.BlockSpec((1, d), lambda i: (0, 0)),
            pl.BlockSpec((1, 1, d), _batch_map(t, seq_len, nb)),
            pl.BlockSpec((1, 1, d), _batch_map(t, seq_len, nb)),
            pl.BlockSpec((d, total), lambda i: (0, 0)),
        ],
        out_specs=[pl.BlockSpec((t, w), row_map) for w in _PROJ_SEGS],
        out_shape=[jax.ShapeDtypeStruct((n, w), dt) for w, dt in zip(_PROJ_SEGS, _PROJ_DTYPES)],
        compiler_params=_params("parallel"),
        name="in_proj",
    )(h2d, g, shift, scale, w_perm)


def _permute_w_in(w_in):
    o = 0
    z = w_in[:, o:o + SSD_WIDTH]; o += SSD_WIDTH
    xbc = w_in[:, o:o + SSD_XBC]; o += SSD_XBC
    dt = w_in[:, o:o + 2 * SSD_HEADS]; o += 2 * SSD_HEADS
    rest = w_in[:, o:]
    dt = jnp.pad(dt, ((0, 0), (0, LANES - 2 * SSD_HEADS)))
    return jnp.concatenate([z, xbc, rest, dt], axis=1).astype(BF16)


def _conv_kernel(x_ref, w_ref, b_ref, o_ref):
    x = x_ref[0]
    n = x.shape[0]
    row = lax.broadcasted_iota(jnp.int32, x.shape, 0)
    prev = jnp.where(row == 0, 0.0, pltpu.roll(x, 1, 0))
    nxt = jnp.where(row == n - 1, 0.0, pltpu.roll(x, n - 1, 0))
    y = prev * w_ref[0:1, :] + x * w_ref[1:2, :] + nxt * w_ref[2:3, :] + b_ref[...]
    o_ref[0] = _silu(y)


def _conv_silu(xbc3, conv_w, conv_b):
    b, l, c = xbc3.shape
    tc = 256
    return pl.pallas_call(
        _conv_kernel,
        grid=(b, c // tc),
        in_specs=[
            pl.BlockSpec((1, l, tc), lambda i, j: (i, 0, j)),
            pl.BlockSpec((3, tc), lambda i, j: (0, j)),
            pl.BlockSpec((1, tc), lambda i, j: (0, j)),
        ],
        out_specs=pl.BlockSpec((1, l, tc), lambda i, j: (i, 0, j)),
        out_shape=jax.ShapeDtypeStruct((b, l, c), F32),
        compiler_params=_params("parallel", "parallel"),
        name="dwconv_silu",
    )(xbc3, conv_w, conv_b.reshape(1, c))


def _ssd_kernel(reverse, xbc_ref, dt_ref, dtt_ref, dtb_row, dtb_col, alog_row, alog_col, init_ref,
                y_ref, fin_ref, state_scr):
    c = pl.program_id(1)
    q = SSD_CHUNK

    @pl.when(c == 0)
    def _():
        state_scr[...] = init_ref[0]

    col0 = SSD_HEADS if reverse else 0
    dt_l = _softplus(dt_ref[0] + dtb_row[...])
    a_l = dt_l * (-jnp.exp(alog_row[...]))
    dt_t = _softplus(dtt_ref[0] + dtb_col[...])
    a_t = dt_t * (-jnp.exp(alog_col[...]))
    row = lax.broadcasted_iota(jnp.int32, (q, q), 0)
    col = lax.broadcasted_iota(jnp.int32, (q, q), 1)
    lower = (row >= col).astype(F32)
    upper = (row <= col).astype(F32)
    cs_l = jnp.dot(lower, a_l, precision=HIGHEST, preferred_element_type=F32)
    cs_t = jnp.dot(a_t, upper, precision=HIGHEST, preferred_element_type=F32)
    tot_l = cs_l[q - 1:q, :]
    if reverse:
        p_l, p_t = cs_l - a_l, cs_t - a_t
        tri = row <= col
    else:
        p_l, p_t = cs_l, cs_t
        tri = row >= col

    tot_t = cs_t[:, q - 1:q]
    to_end_t = jnp.exp(p_t) if reverse else jnp.exp(tot_t - p_t)
    w_t = dt_t * to_end_t
    left = lax.broadcasted_iota(jnp.int32, (q, 2 * SSD_HEAD_DIM), 1) < SSD_HEAD_DIM

    xbc = xbc_ref[0]
    heads_per_group = SSD_HEADS // SSD_GROUPS
    for g in range(SSD_GROUPS):
        b_g = xbc[:, SSD_WIDTH + g * SSD_STATE:SSD_WIDTH + (g + 1) * SSD_STATE]
        c_g = xbc[:, SSD_WIDTH + (SSD_GROUPS + g) * SSD_STATE:SSD_WIDTH + (SSD_GROUPS + g + 1) * SSD_STATE]
        c_bf = c_g.astype(BF16)
        cb = _dot_nt(c_bf, b_g.astype(BF16))
        bt = b_g.T
        for pair in range(g * heads_per_group // 2, (g + 1) * heads_per_group // 2):
            lanes = slice(pair * 2 * SSD_HEAD_DIM, (pair + 1) * 2 * SSD_HEAD_DIM)
            xp = xbc[:, lanes]
            x_bd = jnp.concatenate([jnp.where(left, xp, 0.0), jnp.where(left, 0.0, xp)], axis=0).astype(BF16)
            decay_tiles, state_tiles, pcol_tiles = [], [], []
            for k in (col0 + 2 * pair, col0 + 2 * pair + 1):
                pcol_b = jnp.broadcast_to(p_l[:, k:k + 1], (q, q))
                prow = p_t[k:k + 1, :]
                seg = (prow - pcol_b) if reverse else (pcol_b - prow)
                lmat = jnp.exp(jnp.where(tri, seg, -jnp.inf)) * dt_t[k:k + 1, :]
                decay_tiles.append((cb * lmat).astype(BF16))
                state_tiles.append((bt * w_t[k:k + 1, :]).astype(BF16))
                pcol_tiles.append(pcol_b)
            k0 = col0 + 2 * pair
            tot_pair = jnp.where(left[0:1], tot_l[:, k0:k0 + 1], tot_l[:, k0 + 1:k0 + 2])
            pcol_pair = jnp.where(left, pcol_tiles[0], pcol_tiles[1])
            in_decay = jnp.exp(tot_pair - pcol_pair) if reverse else jnp.exp(pcol_pair)
            s_prev = state_scr[pair]
            y_diag = jnp.dot(jnp.concatenate(decay_tiles, axis=1), x_bd, preferred_element_type=F32)
            y_off = jnp.dot(c_bf, s_prev.astype(BF16), preferred_element_type=F32) * in_decay
            state_scr[pair] = jnp.exp(tot_pair) * s_prev + jnp.dot(jnp.concatenate(state_tiles, axis=1), x_bd,
                                                                   preferred_element_type=F32)
            y_ref[0, :, lanes] = y_diag + y_off

    @pl.when(c == pl.num_programs(1) - 1)
    def _():
        fin_ref[0] = state_scr[...]


def _ssd_scan(xbc_act, dt3, dtt3, dtb, alog, init, reverse):
    b, l, _ = xbc_act.shape
    nc = l // SSD_CHUNK
    cmap = (lambda c: nc - 1 - c) if reverse else (lambda c: c)
    dtb_row = jnp.pad(dtb.reshape(1, -1), ((0, 0), (0, LANES - 2 * SSD_HEADS)))
    alog_row = jnp.pad(alog.reshape(1, -1), ((0, 0), (0, LANES - 2 * SSD_HEADS)))
    small = lambda shape: pl.BlockSpec(shape, lambda i, c: (0, 0))
    st_shape = (b, SSD_HEADS // 2, SSD_STATE, 2 * SSD_HEAD_DIM)
    st_spec = pl.BlockSpec((1,) + st_shape[1:], lambda i, c: (i, 0, 0, 0))
    return pl.pallas_call(
        functools.partial(_ssd_kernel, reverse),
        grid=(b, nc),
        in_specs=[
            pl.BlockSpec((1, SSD_CHUNK, SSD_XBC), lambda i, c: (i, cmap(c), 0)),
            pl.BlockSpec((1, SSD_CHUNK, LANES), lambda i, c: (i, cmap(c), 0)),
            pl.BlockSpec((1, 2 * SSD_HEADS, SSD_CHUNK), lambda i, c: (i, 0, cmap(c))),
            small((1, LANES)), small((2 * SSD_HEADS, 1)), small((1, LANES)), small((2 * SSD_HEADS, 1)),
            st_spec,
        ],
        out_specs=[pl.BlockSpec((1, SSD_CHUNK, SSD_WIDTH), lambda i, c: (i, cmap(c), 0)), st_spec],
        out_shape=[jax.ShapeDtypeStruct((b, l, SSD_WIDTH), F32), jax.ShapeDtypeStruct(st_shape, F32)],
        scratch_shapes=[pltpu.VMEM(st_shape[1:], F32)],
        compiler_params=_params("parallel", "arbitrary"),
        name="ssd_scan_bwd" if reverse else "ssd_scan_fwd",
    )(xbc_act, dt3, dtt3, dtb_row, dtb.reshape(-1, 1), alog_row, alog.reshape(-1, 1), init)


def _pool_kernel(x_ref, w_ref, sc_ref, o_ref, pad_scr):
    n = x_ref.shape[1]
    zeros = jnp.zeros((POOL_PAD, POOL_GROUP_DIM), F32)
    pad_scr[0:POOL_PAD, :] = zeros
    pad_scr[n + POOL_PAD:n + 2 * POOL_PAD, :] = zeros
    t = lax.broadcasted_iota(jnp.int32, (n, 1), 0)
    for gi, w in enumerate(POOL_WINDOWS):
        sl = slice(gi * POOL_GROUP_DIM, (gi + 1) * POOL_GROUP_DIM)
        x = x_ref[0, :, sl]
        pad_scr[POOL_PAD:n + POOL_PAD, :] = x
        acc = jnp.zeros_like(x)
        for o in range(-(w // 2), w - w // 2):
            acc = acc + pad_scr[POOL_PAD + o:POOL_PAD + o + n, :]
        lo = jnp.maximum(t - w // 2, 0)
        hi = jnp.minimum(t + (w - w // 2 - 1), n - 1)
        pooled = acc / (hi - lo + 1).astype(F32) - x
        y = jnp.dot(pooled.astype(BF16), w_ref[gi].astype(BF16), preferred_element_type=F32)
        o_ref[0, :, sl] = y * sc_ref[:, sl]


def _pool_mixer(u3, w_pool, scale):
    b, l, c = u3.shape
    return pl.pallas_call(
        _pool_kernel,
        grid=(b,),
        in_specs=[
            pl.BlockSpec((1, l, c), lambda i: (i, 0, 0)),
            pl.BlockSpec(w_pool.shape, lambda i: (0, 0, 0)),
            pl.BlockSpec((1, c), lambda i: (0, 0)),
        ],
        out_specs=pl.BlockSpec((1, l, c), lambda i: (i, 0, 0)),
        out_shape=jax.ShapeDtypeStruct((b, l, c), F32),
        scratch_shapes=[pltpu.VMEM((l + 2 * POOL_PAD, POOL_GROUP_DIM), F32)],
        compiler_params=_params("parallel"),
        name="pool_mixer",
    )(u3, w_pool, scale.reshape(1, c))


def _na_bias_kernel(rpb_ref, o_ref):
    h = pl.program_id(0)
    qi = lax.broadcasted_iota(jnp.int32, (GRID_W, LANES), 0)
    lane = lax.broadcasted_iota(jnp.int32, (GRID_W, LANES), 1)
    ki = lane % GRID_W
    second = lane >= GRID_W
    start = jnp.clip(qi - NA_KW // 2, 0, GRID_W - NA_KW)
    in_window = (ki >= start) & (ki < start + NA_KW)
    dc = jnp.clip(ki - qi, -(NA_KW - 1), NA_KW - 1) + NA_KW - 1
    for dr in range(o_ref.shape[1]):
        val = jnp.zeros((GRID_W, LANES), F32)
        for j in range(2 * NA_KW - 1):
            pick = jnp.where(second, rpb_ref[h, dr + 1, j], rpb_ref[h, dr, j])
            val = jnp.where(dc == j, pick, val)
        o_ref[0, dr] = jnp.where(in_window, val, -jnp.inf)


def _na_bias(rpb):
    nh, ndr, ndc = rpb.shape
    return pl.pallas_call(
        _na_bias_kernel,
        grid=(nh,),
        in_specs=[pl.BlockSpec(memory_space=pltpu.SMEM)],
        out_specs=pl.BlockSpec((1, ndr - 1, GRID_W, LANES), lambda h: (h, 0, 0, 0)),
        out_shape=jax.ShapeDtypeStruct((nh, ndr - 1, GRID_W, LANES), F32),
        compiler_params=_params("parallel"),
        name="na_bias",
    )(rpb)


def _na_kernel(kh, q_ref, k_ref, v_ref, kc_ref, vc_ref, bias_ref, o_ref):
    r = pl.program_id(1)
    rows = pl.num_programs(1)
    r0 = jnp.clip(r - kh // 2, 0, rows - kh)
    start = pl.multiple_of(r0 * GRID_W, GRID_W)
    kblk = k_ref[0, pl.ds(start, kh * GRID_W), :]
    vblk = v_ref[0, pl.ds(start, kh * GRID_W), :]
    dr0 = r0 - r + NA_KH - 1
    scale = NA_HEAD_DIM ** -0.5
    pair_w = 2 * NA_HEAD_DIM
    halves = []
    for pair in range(NA_HEADS // 2):
        lanes = slice(pair * pair_w, (pair + 1) * pair_w)
        left_q = lax.broadcasted_iota(jnp.int32, (GRID_W, pair_w), 1) < NA_HEAD_DIM
        q_pair = q_ref[0, :, lanes] * scale
        k_pair, kc_pair = kblk[:, lanes], kc_ref[0, :, lanes]
        for side in range(2):
            keep = left_q if side == 0 else jnp.logical_not(left_q)
            q_h = jnp.where(keep, q_pair, 0.0)
            h = 2 * pair + side
            bias = jnp.concatenate([bias_ref[h, dr0 + 2 * j] for j in range(kh // 2)], axis=1)
            halves.append((_dot_nt(q_h, k_pair) + bias, _dot_nt(q_h, kc_pair)))
    probs = []
    for s_loc, s_ctx in halves:
        m = jnp.maximum(jnp.max(s_loc, axis=-1, keepdims=True), jnp.max(s_ctx, axis=-1, keepdims=True))
        p_loc = jnp.exp(s_loc - m)
        p_ctx = jnp.exp(s_ctx - m)
        inv = 1.0 / (jnp.sum(p_loc, axis=-1, keepdims=True) + jnp.sum(p_ctx, axis=-1, keepdims=True))
        probs.append((p_loc.astype(BF16), p_ctx.astype(BF16), inv))
    for pair in range(NA_HEADS // 2):
        lanes = slice(pair * pair_w, (pair + 1) * pair_w)
        v_pair, vc_pair = vblk[:, lanes], vc_ref[0, :, lanes]
        left_v = lax.broadcasted_iota(jnp.int32, v_pair.shape, 1) < NA_HEAD_DIM
        left_c = lax.broadcasted_iota(jnp.int32, vc_pair.shape, 1) < NA_HEAD_DIM
        out = None
        for side in range(2):
            p_loc, p_ctx, inv = probs[2 * pair + side]
            keep_v = left_v if side == 0 else jnp.logical_not(left_v)
            keep_c = left_c if side == 0 else jnp.logical_not(left_c)
            acc = jnp.dot(p_loc, jnp.where(keep_v, v_pair, 0.0), preferred_element_type=F32)
            acc = acc + jnp.dot(p_ctx, jnp.where(keep_c, vc_pair, 0.0), preferred_element_type=F32)
            out = acc * inv if out is None else out + acc * inv
        o_ref[0, :, lanes] = out


def _neighbourhood_attention(q3, k3, v3, kc3, vc3, bias):
    b, s, c = q3.shape
    rows = s // GRID_W
    kh = min(NA_KH, rows)
    lc = kc3.shape[1]
    full = lambda n: pl.BlockSpec((1, n, c), lambda i, r: (i, 0, 0))
    return pl.pallas_call(
        functools.partial(_na_kernel, kh),
        grid=(b, rows),
        in_specs=[
            pl.BlockSpec((1, GRID_W, c), lambda i, r: (i, r, 0)),
            full(s), full(s), full(lc), full(lc),
            pl.BlockSpec(bias.shape, lambda i, r: (0, 0, 0, 0)),
        ],
        out_specs=pl.BlockSpec((1, GRID_W, c), lambda i, r: (i, r, 0)),
        out_shape=jax.ShapeDtypeStruct((b, s, c), F32),
        compiler_params=_params("parallel", "arbitrary"),
        name="na_attention",
    )(q3, k3, v3, kc3, vc3, bias)


def _ctx_attn_kernel(q_ref, k_ref, v_ref, o_ref):
    scale = NA_HEAD_DIM ** -0.5
    for h in range(NA_HEADS):
        sl = slice(h * NA_HEAD_DIM, (h + 1) * NA_HEAD_DIM)
        s = _dot_nt(q_ref[0, :, sl] * scale, k_ref[0, :, sl])
        p = jnp.exp(s - jnp.max(s, axis=-1, keepdims=True))
        den = jnp.sum(p, axis=-1, keepdims=True)
        o_ref[0, :, sl] = jnp.dot(p.astype(BF16), v_ref[0, :, sl], preferred_element_type=F32) / den


def _context_attention(q3, k3, v3):
    b, l, c = q3.shape
    spec = pl.BlockSpec((1, l, c), lambda i: (i, 0, 0))
    return pl.pallas_call(
        _ctx_attn_kernel,
        grid=(b,),
        in_specs=[spec, spec, spec],
        out_specs=spec,
        out_shape=jax.ShapeDtypeStruct((b, l, c), F32),
        compiler_params=_params("parallel"),
        name="ctx_attention",
    )(q3, k3, v3)


def _mix_out_kernel(yf_ref, yb_ref, xs_ref, z_ref, pool_ref, na_ref, h_ref, dsk_ref, ng_ref, g1_ref, w_ref, o_ref):
    y = yf_ref[...] + yb_ref[...] + dsk_ref[...] * xs_ref[...]
    y = y * _silu(z_ref[...])
    ms = jnp.mean(y * y, axis=-1, keepdims=True)
    yn = (y * lax.rsqrt(ms + RMS_EPS)) * ng_ref[...]
    mix = jnp.dot(yn.astype(BF16), w_ref[0:SSD_WIDTH, :], preferred_element_type=F32)
    mix = mix + jnp.dot(pool_ref[...].astype(BF16), w_ref[SSD_WIDTH:SSD_WIDTH + POOL_WIDTH, :],
                        preferred_element_type=F32)
    mix = mix + jnp.dot(na_ref[...].astype(BF16), w_ref[SSD_WIDTH + POOL_WIDTH:, :], preferred_element_type=F32)
    o_ref[...] = h_ref[...] + g1_ref[0] * mix


def _mix_out(yf, yb, xbc_act2d, z, pool_y, na_y, h2d, seq_len, d_skip, norm_g, g1, w_out_bf):
    n, d = h2d.shape
    t = 256
    row = lambda w: pl.BlockSpec((t, w), lambda i: (i, 0))
    vec = lambda w: pl.BlockSpec((1, w), lambda i: (0, 0))
    return pl.pallas_call(
        _mix_out_kernel,
        grid=(n // t,),
        in_specs=[
            row(SSD_WIDTH), row(SSD_WIDTH), row(SSD_WIDTH), row(SSD_WIDTH), row(POOL_WIDTH), row(NA_WIDTH), row(d),
            vec(SSD_WIDTH), vec(SSD_WIDTH),
            pl.BlockSpec((1, 1, d), _batch_map(t, seq_len, g1.shape[0])),
            pl.BlockSpec(w_out_bf.shape, lambda i: (0, 0)),
        ],
        out_specs=row(d),
        out_shape=jax.ShapeDtypeStruct((n, d), F32),
        compiler_params=_params("parallel"),
        name="mix_out",
    )(yf, yb, xbc_act2d, z, pool_y, na_y, h2d, jnp.repeat(d_skip, SSD_HEAD_DIM).reshape(1, -1),
      norm_g.reshape(1, -1), g1, w_out_bf)


_CAND_ROWS = 16 + 8 * 7 + 8


def _top16_pair_fast(s1, s2):
    t = s1.shape[1]
    row16 = lax.broadcasted_iota(jnp.int32, (PEER_TOPK, t), 0)

    def body(j, carry):
        p1, v1, p2, v2, below = carry
        m1 = jnp.max(jnp.where(s1 < p1, s1, -jnp.inf), axis=0, keepdims=True)
        lower = s2 < p2
        m2 = jnp.max(jnp.where(lower, s2, -jnp.inf), axis=0, keepdims=True)
        return (m1, jnp.where(row16 == j, m1, v1), m2, jnp.where(row16 == j, m2, v2),
                below + jnp.where(lower, 1.0, 0.0))

    top = jnp.full((1, t), jnp.inf, F32)
    zeros = jnp.zeros((PEER_TOPK, t), F32)
    m1, v1, m2, v2, below = lax.fori_loop(0, PEER_TOPK, body, (top, zeros, top, zeros, jnp.full(s2.shape, -1.0, F32)))
    rank2 = below + jnp.where(s2 < m2, 1.0, 0.0)
    count = lambda s, m: jnp.sum(jnp.where(s >= m, 1.0, 0.0), axis=0, keepdims=True)
    return v1, count(s1, m1), v2, rank2, count(s2, m2)


def _first_counts_fast(s1, v1, sel):
    inf = jnp.inf
    rows_v1 = jnp.concatenate([jnp.broadcast_to(v1[0:1], (PEER_TOPK, v1.shape[1]))]
                              + [jnp.broadcast_to(v1[j:j + 1], (8, v1.shape[1])) for j in range(1, 8)]
                              + [v1[8:16]], axis=0)
    bound = jnp.where(sel > 0.0, rows_v1, inf)
    low = bound[0:8]
    for j in range(1, 8):
        low = jnp.minimum(low, bound[16 + 8 * (j - 1):16 + 8 * j])
    tail = jnp.min(bound[64 + 8:64 + 16], axis=0, keepdims=True)
    cnt = jnp.where(s1 >= jnp.minimum(low[0:1], tail), 1.0, 0.0)
    for k in range(1, 8):
        cnt = cnt + jnp.where(s1 >= low[k:k + 1], 1.0, 0.0)
    n_high = jnp.sum(sel[8:16], axis=0, keepdims=True)
    return cnt + jnp.where(s1 >= v1[0:1], n_high, 0.0)


def _first_counts_exact(rank1, sel):
    cnt = jnp.zeros(rank1.shape, F32)
    for j in range(8):
        lo = 0 if j == 0 else 16 + 8 * (j - 1)
        n_j = jnp.sum(sel[lo:lo + (16 if j == 0 else 8)], axis=0, keepdims=True)
        cnt = cnt + jnp.where(rank1 == float(j), n_j, 0.0)
    for j in range(8, 16):
        cnt = cnt + jnp.where(rank1 == float(j), sel[64 + j:65 + j], 0.0)
    return cnt


def _top16_pair_exact(s1, s2):
    n, t = s1.shape
    iota = lax.broadcasted_iota(jnp.int32, (n, t), 0).astype(F32)
    row16 = lax.broadcasted_iota(jnp.int32, (PEER_TOPK, t), 0)

    def pick(work, rank, vals, j):
        m = jnp.max(work, axis=0, keepdims=True)
        idx = jnp.min(jnp.where(work == m, iota, float(n)), axis=0, keepdims=True)
        sel = iota == idx
        return (jnp.where(sel, -jnp.inf, work), jnp.where(sel, lax.convert_element_type(j, F32), rank),
                jnp.where(row16 == j, m, vals))

    def body(j, carry):
        a, b = carry
        return pick(*a, j), pick(*b, j)

    start = lambda s: (s, jnp.full((n, t), float(PEER_TOPK), F32), jnp.zeros((PEER_TOPK, t), F32))
    (_, rank1, v1), (_, rank2, v2) = lax.fori_loop(0, PEER_TOPK, body, (start(s1), start(s2)))
    return rank1, v1, rank2, v2


def _select16_fast(cand):
    def body(_, prev):
        return jnp.max(jnp.where(cand < prev, cand, -jnp.inf), axis=0, keepdims=True)

    m = lax.fori_loop(0, PEER_TOPK, body, jnp.full((1, cand.shape[1]), jnp.inf, F32))
    taken = jnp.where(cand >= m, 1.0, 0.0)
    return taken, jnp.sum(taken, axis=0, keepdims=True)


def _select16_exact(cand):
    iota = lax.broadcasted_iota(jnp.int32, cand.shape, 0).astype(F32)

    def body(_, carry):
        work, sel_acc = carry
        m = jnp.max(work, axis=0, keepdims=True)
        idx = jnp.min(jnp.where(work == m, iota, float(_CAND_ROWS)), axis=0, keepdims=True)
        sel = iota == idx
        return jnp.where(sel, -jnp.inf, work), jnp.where(sel, 1.0, sel_acc)

    return lax.fori_loop(0, PEER_TOPK, body, (cand, jnp.zeros(cand.shape, F32)))[1]


def _candidate_sums(v1, v2):
    blocks = [v1[0:1] + v2]
    for j in range(1, 8):
        blocks.append(v1[j:j + 1] + v2[0:8])
    blocks.append(v1[8:16] + v2[0:1])
    return jnp.concatenate(blocks, axis=0)


def _any_not_16(*counts):
    return jnp.max(sum(jnp.abs(c - float(PEER_TOPK)) for c in counts)) > 0.0


def _bf16_bits(x):
    return pltpu.bitcast(x.astype(BF16).astype(F32), jnp.uint32)


def _pack_row_pairs(x, scr):
    n, t = x.shape
    for j in range(t // LANES):
        scr[j] = x[:, j * LANES:(j + 1) * LANES]
    words = []
    for j in range(t // LANES):
        even = scr[j, pl.ds(0, n // 2, stride=2), :]
        odd = scr[j, pl.ds(1, n // 2, stride=2), :]
        words.append((_bf16_bits(even) >> 16) | _bf16_bits(odd))
    return jnp.concatenate(words, axis=1)


def _pack_same(x):
    w = _bf16_bits(x)
    return w | (w >> 16)


def _peer_score_kernel(h_ref, g_ref, sh_ref, sc_ref, wq_ref, keys_ref,
                       x_out, cnt_out, e1_out, rk_out, e2_out,
                       u_scr, pair_scr, cnt_scr, rank2_scr, top_scr):
    hd = pl.program_id(1)

    @pl.when(hd == 0)
    def _():
        u = _rms_mod(h_ref[...], g_ref[...], sc_ref[0], sh_ref[0]).astype(BF16)
        u_scr[...] = u
        x_out[...] = pltpu.bitcast(u, jnp.uint32)

    q = jnp.dot(u_scr[...], wq_ref[0], preferred_element_type=F32).astype(BF16)
    s1 = _dot_nt(keys_ref[0, 0], q[:, :PEER_SUB])
    s2 = _dot_nt(keys_ref[0, 1], q[:, PEER_SUB:])
    v1, n1, v2, rank2, n2 = _top16_pair_fast(s1, s2)
    cand = _candidate_sums(v1, v2)
    sel, n_sel = _select16_fast(cand)
    cnt_scr[...] = _first_counts_fast(s1, v1, sel)
    rank2_scr[...] = rank2
    top_scr[0:1] = v1[0:1]
    top_scr[1:2] = v2[0:1]
    top_scr[2:3] = jnp.sum(sel * jnp.exp(cand - cand[0:1]), axis=0, keepdims=True)

    @pl.when(_any_not_16(n1, n2, n_sel))
    def _():
        rank1_x, v1_x, rank2_x, v2_x = _top16_pair_exact(s1, s2)
        cand_x = _candidate_sums(v1_x, v2_x)
        sel_x = _select16_exact(cand_x)
        cnt_scr[...] = _first_counts_exact(rank1_x, sel_x)
        rank2_scr[...] = rank2_x
        top_scr[0:1] = v1_x[0:1]
        top_scr[1:2] = v2_x[0:1]
        top_scr[2:3] = jnp.sum(sel_x * jnp.exp(cand_x - cand_x[0:1]), axis=0, keepdims=True)

    cnt_out[0] = _pack_same(cnt_scr[...])
    e1_out[0] = _pack_same(jnp.exp(s1 - top_scr[0:1]))
    rk_out[0] = _pack_row_pairs(rank2_scr[...], pair_scr)
    e2_out[0] = _pack_row_pairs(jnp.exp(s2 - top_scr[1:2]) / top_scr[2:3], pair_scr)


def _peer_scores(h2d, seq_len, g, shift, scale, wq_heads, keys_bf):
    n, d = h2d.shape
    nb = shift.shape[0]
    t = 512
    assert n % t == 0 and (nb == 1 or seq_len % t == 0)
    bm = _batch_map(t, seq_len, nb)
    mod_spec = pl.BlockSpec((1, 1, d), lambda i, hd: bm(i))
    first_out = pl.BlockSpec((1, PEER_NKEYS, t), lambda i, hd: (hd, 0, i))
    first_shape = jax.ShapeDtypeStruct((PEER_HEADS, PEER_NKEYS, n), jnp.uint32)
    second_out = pl.BlockSpec((1, PEER_NKEYS // 2, t), lambda i, hd: (hd, 0, i))
    second_shape = jax.ShapeDtypeStruct((PEER_HEADS, PEER_NKEYS // 2, n), jnp.uint32)
    return pl.pallas_call(
        _peer_score_kernel,
        grid=(n // t, PEER_HEADS),
        in_specs=[
            pl.BlockSpec((t, d), lambda i, hd: (i, 0)),
            pl.BlockSpec((1, d), lambda i, hd: (0, 0)),
            mod_spec, mod_spec,
            pl.BlockSpec((1, d, 2 * PEER_SUB), lambda i, hd: (hd, 0, 0)),
            pl.BlockSpec((1, 2, PEER_NKEYS, PEER_SUB), lambda i, hd: (hd, 0, 0, 0)),
        ],
        out_specs=[pl.BlockSpec((t // 2, d), lambda i, hd: (i, 0)), first_out, first_out, second_out, second_out],
        out_shape=[jax.ShapeDtypeStruct((n // 2, d), jnp.uint32), first_shape, first_shape, second_shape,
                   second_shape],
        scratch_shapes=[pltpu.VMEM((t, d), BF16), pltpu.VMEM((t // LANES, PEER_NKEYS, LANES), F32),
                        pltpu.VMEM((PEER_NKEYS, t), F32), pltpu.VMEM((PEER_NKEYS, t), F32),
                        pltpu.VMEM((8, t), F32)],
        compiler_params=_params("parallel", "arbitrary"),
        name="peer_scores",
    )(h2d, g, shift, scale, wq_heads, keys_bf)


_PEER_EC = 1024
_INV_SQRT2 = 1.0 / math.sqrt(2.0)


def _as_bf16_rows(words):
    return pltpu.bitcast(words, BF16)


_PEER_A_GROUP = 4


def _peer_expert_kernel(n_chunks, x_ref, u_ref, vt_ref, cnt_ref, e1_ref, rk_ref, e2_ref, h_ref, g2_ref, o_ref,
                        ht0_scr, ht1_scr, g0_scr, g1_scr, acc_scr):
    s = pl.program_id(0)
    t = h_ref.shape[0]
    p2 = s - 2
    c2 = lax.rem(jnp.maximum(p2, 0), n_chunks)
    pack = 2 * 8
    n_a = _PEER_EC // PEER_NKEYS

    @pl.when(s == 0)
    def _():
        ht1_scr[...] = jnp.zeros_like(ht1_scr)
        g0_scr[...] = jnp.zeros_like(g0_scr)
        g1_scr[...] = jnp.zeros_like(g1_scr)

    @pl.when((p2 <= 0) | (c2 == 0))
    def _():
        acc_scr[...] = jnp.zeros_like(acc_scr)

    def step(ht_new, ht_old, g_new, g_old):
        ht_new[...] = _dot_nt(_as_bf16_rows(u_ref[...]), _as_bf16_rows(x_ref[...]))
        for tc in range(t // LANES):
            ls = slice(tc * LANES, (tc + 1) * LANES)
            for a0 in range(0, n_a, _PEER_A_GROUP):
                w = [[None] * (PEER_NKEYS // pack) for _ in range(_PEER_A_GROUP)]
                for hd in range(PEER_HEADS):
                    rows1 = [(_as_bf16_rows(jnp.broadcast_to(cnt_ref[hd, a0 + i:a0 + i + 1, ls], (8, LANES))),
                              _as_bf16_rows(jnp.broadcast_to(e1_ref[hd, a0 + i:a0 + i + 1, ls], (8, LANES))))
                             for i in range(_PEER_A_GROUP)]
                    for r in range(PEER_NKEYS // pack):
                        rk = _as_bf16_rows(rk_ref[hd, r * 8:(r + 1) * 8, ls])
                        e2 = _as_bf16_rows(e2_ref[hd, r * 8:(r + 1) * 8, ls])
                        for i, (cn, e1) in enumerate(rows1):
                            term = jnp.where(rk < cn, e2, 0.0) * e1
                            w[i][r] = term if w[i][r] is None else w[i][r] + term
                for i in range(_PEER_A_GROUP):
                    for r in range(PEER_NKEYS // pack):
                        row0 = (a0 + i) * PEER_NKEYS + r * pack
                        hs = ht_old[row0:row0 + pack, ls]
                        act = 0.5 * hs * (1.0 + lax.erf(hs * _INV_SQRT2))
                        g_new[row0 // 2:(row0 + pack) // 2, ls] = pltpu.bitcast(w[i][r] * act.astype(BF16), jnp.uint32)
        acc_scr[...] += jnp.dot(_as_bf16_rows(vt_ref[...]), _as_bf16_rows(g_old[...]),
                                preferred_element_type=F32)

    @pl.when(s % 2 == 0)
    def _():
        step(ht0_scr, ht1_scr, g1_scr, g0_scr)

    @pl.when(s % 2 == 1)
    def _():
        step(ht1_scr, ht0_scr, g0_scr, g1_scr)

    @pl.when((p2 >= 0) & (c2 == n_chunks - 1))
    def _():
        o_ref[...] = h_ref[...] + g2_ref[0] * acc_scr[...].T


def _peer_experts(x_pk, u_pk, vt_pk, cnt, e1, rk, e2, h2d, seq_len, g2):
    n, d = h2d.shape
    t = 512
    n_chunks = 2 * u_pk.shape[0] // _PEER_EC
    total = (n // t) * n_chunks
    ea = _PEER_EC // PEER_NKEYS

    def pair(p):
        p = jnp.clip(p, 0, total - 1)
        return p // n_chunks, lax.rem(p, n_chunks)

    blk = lambda lag: (lambda s: pair(s - lag)[0])
    chk = lambda lag: (lambda s: pair(s - lag)[1])
    tok = pl.BlockSpec((PEER_HEADS, PEER_NKEYS // 2, t), lambda s: (0, 0, blk(1)(s)))
    first = pl.BlockSpec((PEER_HEADS, ea, t), lambda s: (0, chk(1)(s), blk(1)(s)))
    bm = _batch_map(t, seq_len, g2.shape[0])
    return pl.pallas_call(
        functools.partial(_peer_expert_kernel, n_chunks),
        grid=(total + 2,),
        in_specs=[
            pl.BlockSpec((t // 2, d), lambda s: (blk(0)(s), 0)),
            pl.BlockSpec((_PEER_EC // 2, d), lambda s: (chk(0)(s), 0)),
            pl.BlockSpec((d // 2, _PEER_EC), lambda s: (0, chk(2)(s))),
            first, first, tok, tok,
            pl.BlockSpec((t, d), lambda s: (blk(2)(s), 0)),
            pl.BlockSpec((1, 1, d), lambda s: bm(blk(2)(s))),
        ],
        out_specs=pl.BlockSpec((t, d), lambda s: (blk(2)(s), 0)),
        out_shape=jax.ShapeDtypeStruct((n, d), F32),
        scratch_shapes=[pltpu.VMEM((_PEER_EC, t), F32), pltpu.VMEM((_PEER_EC, t), F32),
                        pltpu.VMEM((_PEER_EC // 2, t), jnp.uint32), pltpu.VMEM((_PEER_EC // 2, t), jnp.uint32),
                        pltpu.VMEM((d, t), F32)],
        compiler_params=_params("arbitrary"),
        name="peer_experts",
    )(x_pk, u_pk, vt_pk, cnt, e1, rk, e2, h2d, g2)


def _pack_rows_kernel(transpose, x_ref, o_ref):
    x = x_ref[...].T if transpose else x_ref[...]
    o_ref[...] = pltpu.bitcast(x.astype(BF16), jnp.uint32)


def _pack_bf16_rows(a, transpose=False):
    tile = 512
    if transpose:
        c, r = a.shape
        in_spec = pl.BlockSpec((tile, r), lambda i: (i, 0))
        out_spec = pl.BlockSpec((r // 2, tile), lambda i: (0, i))
        steps = c // tile
    else:
        r, c = a.shape
        in_spec = pl.BlockSpec((tile, c), lambda i: (i, 0))
        out_spec = pl.BlockSpec((tile // 2, c), lambda i: (i, 0))
        steps = r // tile
    return pl.pallas_call(
        functools.partial(_pack_rows_kernel, transpose),
        grid=(steps,),
        in_specs=[in_spec],
        out_specs=out_spec,
        out_shape=jax.ShapeDtypeStruct((r // 2, c), jnp.uint32),
        compiler_params=_params("parallel"),
        name="pack_bf16_rows_t" if transpose else "pack_bf16_rows",
    )(a)


def _peer_ffn_residual(h2d, seq_len, norm_g, shift, scale, gate, wq_heads, keys_bf, u_pk, vt_pk):
    x_pk, cnt, e1, rk, e2 = _peer_scores(h2d, seq_len, norm_g, shift, scale, wq_heads, keys_bf)
    return _peer_experts(x_pk, u_pk, vt_pk, cnt, e1, rk, e2, h2d, seq_len, gate)


def _final_norm_kernel(h_ref, g_ref, o_ref):
    x = h_ref[...]
    ms = jnp.mean(x * x, axis=-1, keepdims=True)
    o_ref[...] = (x * lax.rsqrt(ms + RMS_EPS)) * g_ref[...]


def _final_norm(h2d, g):
    n, d = h2d.shape
    t = 512
    return pl.pallas_call(
        _final_norm_kernel,
        grid=(n // t,),
        in_specs=[pl.BlockSpec((t, d), lambda i: (i, 0)), pl.BlockSpec((1, d), lambda i: (0, 0))],
        out_specs=pl.BlockSpec((t, d), lambda i: (i, 0)),
        out_shape=jax.ShapeDtypeStruct((n, d), F32),
        compiler_params=_params("parallel"),
        name="final_norm",
    )(h2d, g.reshape(1, d))


def _mixer_inputs(h2d, batch, seq_len, norm_g, shift, scale, w_perm, conv_w, conv_b):
    z, xbc, pool_u, q, k, v, dt = _project(h2d, seq_len, norm_g, shift, scale, w_perm)
    xbc_act = _conv_silu(xbc.reshape(batch, seq_len, SSD_XBC), conv_w, conv_b)
    dt3 = dt.reshape(batch, seq_len, LANES)
    dtt3 = jnp.swapaxes(dt3[:, :, :2 * SSD_HEADS], 1, 2)
    r3 = lambda a: a.reshape(batch, seq_len, a.shape[-1])
    return z, xbc_act, dt3, dtt3, r3(pool_u), r3(q), r3(k), r3(v)


def kernel(x, c, ctx, c_ctx, ada_w, ada_b, norm1_g, w_in, conv_w, conv_b, a_log, dt_bias, d_skip, ssd_norm_g, pool_w, pool_scale, na_rpb, w_out, norm2_g, peer_wq, peer_keys, peer_u, peer_v, final_g):
    batch, seq, d = x.shape
    ctx_len = ctx.shape[1]
    n, nc = batch * seq, batch * ctx_len
    h = x.reshape(n, d)
    hc = ctx.reshape(nc, d)

    c8 = jnp.concatenate([c, c_ctx[None], jnp.zeros((8 - batch - 1, d), F32)], axis=0)
    mod = _modulation(c8, ada_w, ada_b)

    for i in range(DEPTH):
        need_ctx_out = i < DEPTH - 1
        lat = [mod[i, :batch, j * d:(j + 1) * d].reshape(batch, 1, d) for j in range(6)]
        cx = [mod[i, batch:batch + 1, j * d:(j + 1) * d].reshape(1, 1, d) for j in range(6)]
        sh1, sc1, g1, sh2, sc2, g2 = lat
        csh1, csc1, cg1, csh2, csc2, cg2 = cx

        w_perm = _permute_w_in(w_in[i])
        w_out_bf = w_out[i].astype(BF16)
        n1 = norm1_g[i].reshape(1, d)
        n2 = norm2_g[i].reshape(1, d)
        zero_state = jnp.zeros((batch, SSD_HEADS // 2, SSD_STATE, 2 * SSD_HEAD_DIM), F32)
        scan = functools.partial(_ssd_scan, dtb=dt_bias[i], alog=a_log[i])
        wq_heads = peer_wq[i].reshape(d, PEER_HEADS, 2 * PEER_SUB).transpose(1, 0, 2).astype(BF16)
        keys_bf = peer_keys[i].astype(BF16)
        u_pk = _pack_bf16_rows(peer_u[i])
        vt_pk = _pack_bf16_rows(peer_v[i], transpose=True)

        zc, xbc_c, dt3_c, dtt3_c, pool_c, qc, kc, vc = _mixer_inputs(
            hc, batch, ctx_len, n1, csh1, csc1, w_perm, conv_w[i], conv_b[i])
        yf_c, st_f = scan(xbc_c, dt3_c, dtt3_c, init=zero_state, reverse=False)
        yb_c, st_b = scan(xbc_c, dt3_c, dtt3_c, init=zero_state, reverse=True)
        if need_ctx_out:
            pool_yc = _pool_mixer(pool_c, pool_w[i], pool_scale[i])
            att_c = _context_attention(qc, kc, vc)
            hc = _mix_out(yf_c.reshape(nc, -1), yb_c.reshape(nc, -1), xbc_c.reshape(nc, -1), zc,
                          pool_yc.reshape(nc, -1), att_c.reshape(nc, -1), hc, ctx_len,
                          d_skip[i], ssd_norm_g[i], cg1, w_out_bf)
            hc = _peer_ffn_residual(hc, ctx_len, n2, csh2, csc2, cg2, wq_heads, keys_bf, u_pk, vt_pk)

        z, xbc_l, dt3_l, dtt3_l, pool_l, q, k, v = _mixer_inputs(
            h, batch, seq, n1, sh1, sc1, w_perm, conv_w[i], conv_b[i])
        yf, _ = scan(xbc_l, dt3_l, dtt3_l, init=st_f, reverse=False)
        yb, _ = scan(xbc_l, dt3_l, dtt3_l, init=st_b, reverse=True)
        pool_y = _pool_mixer(pool_l, pool_w[i], pool_scale[i])
        na = _neighbourhood_attention(q, k, v, kc, vc, _na_bias(na_rpb[i]))
        h = _mix_out(yf.reshape(n, -1), yb.reshape(n, -1), xbc_l.reshape(n, -1), z,
                     pool_y.reshape(n, -1), na.reshape(n, -1), h, seq,
                     d_skip[i], ssd_norm_g[i], g1, w_out_bf)
        h = _peer_ffn_residual(h, seq, n2, sh2, sc2, g2, wq_heads, keys_bf, u_pk, vt_pk)

    return _final_norm(h, final_g).reshape(batch, seq, d)
```

```python
import functools
import math

import jax
import jax.numpy as jnp
from jax import lax
from jax.experimental import pallas as pl
from jax.experimental.pallas import tpu as pltpu

F32 = jnp.float32
BF16 = jnp.bfloat16
HIGHEST = lax.Precision.HIGHEST

D_MODEL = 1024
DEPTH = 2
GRID_W = 64
RMS_EPS = 1e-6

SSD_HEAD_DIM = 64
SSD_HEADS = 16
SSD_GROUPS = 2
SSD_STATE = 128
SSD_CHUNK = 128
SSD_WIDTH = SSD_HEADS * SSD_HEAD_DIM
SSD_XBC = SSD_WIDTH + 2 * SSD_GROUPS * SSD_STATE

POOL_WINDOWS = (2, 4, 8, 16)
POOL_GROUP_DIM = 128
POOL_WIDTH = POOL_GROUP_DIM * len(POOL_WINDOWS)
POOL_PAD = 8

NA_HEAD_DIM = 64
NA_HEADS = 8
NA_WIDTH = NA_HEADS * NA_HEAD_DIM
NA_KH = 8
NA_KW = 16

PEER_HEADS = 8
PEER_NKEYS = 128
PEER_TOPK = 16
PEER_SUB = 128

LANES = 128
VMEM_LIMIT_BYTES = 56 * 1024 * 1024

_PROJ_SEGS = (SSD_WIDTH, SSD_XBC, POOL_WIDTH, NA_WIDTH, NA_WIDTH, NA_WIDTH, LANES)
_PROJ_DTYPES = (F32, F32, F32, BF16, BF16, BF16, F32)


def _params(*sem):
    return pltpu.CompilerParams(dimension_semantics=sem, vmem_limit_bytes=VMEM_LIMIT_BYTES)


def _rms_mod(x, g, scale, shift):
    ms = jnp.mean(x * x, axis=-1, keepdims=True)
    return (x * lax.rsqrt(ms + RMS_EPS)) * g * (1.0 + scale) + shift


def _silu(x):
    return x * jax.nn.sigmoid(x)


def _softplus(x):
    return jnp.maximum(x, 0.0) + jnp.log1p(jnp.exp(-jnp.abs(x)))


def _dot_nt(a, b):
    return lax.dot_general(a, b, (((1,), (1,)), ((), ())), preferred_element_type=F32)


def _batch_map(block_rows, seq_len, n_rows):
    if n_rows == 1:
        return lambda i, *_: (0, 0, 0)
    return lambda i, *_: ((i * block_rows) // seq_len, 0, 0)


def _mod_kernel(c_ref, w_ref, b_ref, o_ref):
    s = _silu(c_ref[...])
    o_ref[0] = jnp.dot(s, w_ref[0], precision=HIGHEST, preferred_element_type=F32) + b_ref[0]


def _modulation(c8, ada_w, ada_b):
    depth, d, six_d = ada_w.shape
    tn = 1024
    return pl.pallas_call(
        _mod_kernel,
        grid=(depth, six_d // tn),
        in_specs=[
            pl.BlockSpec((8, d), lambda l, j: (0, 0)),
            pl.BlockSpec((1, d, tn), lambda l, j: (l, 0, j)),
            pl.BlockSpec((1, 1, tn), lambda l, j: (l, 0, j)),
        ],
        out_specs=pl.BlockSpec((1, 8, tn), lambda l, j: (l, 0, j)),
        out_shape=jax.ShapeDtypeStruct((depth, 8, six_d), F32),
        compiler_params=_params("parallel", "parallel"),
        name="adaln_mod",
    )(c8, ada_w, ada_b.reshape(depth, 1, six_d))


def _proj_kernel(h_ref, g_ref, sh_ref, sc_ref, w_ref, *out_refs):
    u = _rms_mod(h_ref[...], g_ref[...], sc_ref[0], sh_ref[0]).astype(BF16)
    off = 0
    for o_ref, width in zip(out_refs, _PROJ_SEGS):
        o_ref[...] = jnp.dot(u, w_ref[:, off:off + width], preferred_element_type=F32).astype(o_ref.dtype)
        off += width


def _project(h2d, seq_len, g, shift, scale, w_perm):
    n, d = h2d.shape
    t = 256
    total = sum(_PROJ_SEGS)
    nb = shift.shape[0]
    row_map = lambda i: (i, 0)
    return pl.pallas_call(
        _proj_kernel,
        grid=(n // t,),
        in_specs=[
            pl.BlockSpec((t, d), row_map),
            pl.BlockSpec((1, d), lambda i: (0, 0)),
            pl.BlockSpec((1, 1, d), _batch_map(t, seq_len, nb)),
            pl.BlockSpec((1, 1, d), _batch_map(t, seq_len, nb)),
            pl.BlockSpec((d, total), lambda i: (0, 0)),
        ],
        out_specs=[pl.BlockSpec((t, w), row_map) for w in _PROJ_SEGS],
        out_shape=[jax.ShapeDtypeStruct((n, w), dt) for w, dt in zip(_PROJ_SEGS, _PROJ_DTYPES)],
        compiler_params=_params("parallel"),
        name="in_proj",
    )(h2d, g, shift, scale, w_perm)


def _permute_w_in(w_in):
    o = 0
    z = w_in[:, o:o + SSD_WIDTH]; o += SSD_WIDTH
    xbc = w_in[:, o:o + SSD_XBC]; o += SSD_XBC
    dt = w_in[:, o:o + 2 * SSD_HEADS]; o += 2 * SSD_HEADS
    rest = w_in[:, o:]
    dt = jnp.pad(dt, ((0, 0), (0, LANES - 2 * SSD_HEADS)))
    return jnp.concatenate([z, xbc, rest, dt], axis=1).astype(BF16)


def _conv_kernel(x_ref, w_ref, b_ref, o_ref):
    x = x_ref[0]
    n = x.shape[0]
    row = lax.broadcasted_iota(jnp.int32, x.shape, 0)
    prev = jnp.where(row == 0, 0.0, pltpu.roll(x, 1, 0))
    nxt = jnp.where(row == n - 1, 0.0, pltpu.roll(x, n - 1, 0))
    y = prev * w_ref[0:1, :] + x * w_ref[1:2, :] + nxt * w_ref[2:3, :] + b_ref[...]
    o_ref[0] = _silu(y)


def _conv_silu(xbc3, conv_w, conv_b):
    b, l, c = xbc3.shape
    tc = 256
    return pl.pallas_call(
        _conv_kernel,
        grid=(b, c // tc),
        in_specs=[
            pl.BlockSpec((1, l, tc), lambda i, j: (i, 0, j)),
            pl.BlockSpec((3, tc), lambda i, j: (0, j)),
            pl.BlockSpec((1, tc), lambda i, j: (0, j)),
        ],
        out_specs=pl.BlockSpec((1, l, tc), lambda i, j: (i, 0, j)),
        out_shape=jax.ShapeDtypeStruct((b, l, c), F32),
        compiler_params=_params("parallel", "parallel"),
        name="dwconv_silu",
    )(xbc3, conv_w, conv_b.reshape(1, c))


def _ssd_chunk(reverse, xbc_ref, dt_ref, dtt_ref, dtb_row, dtb_col, alog_row, alog_col, y_ref, state_scr):
    q = SSD_CHUNK
    col0 = SSD_HEADS if reverse else 0
    dt_l = _softplus(dt_ref[0] + dtb_row[...])
    a_l = dt_l * (-jnp.exp(alog_row[...]))
    dt_t = _softplus(dtt_ref[0] + dtb_col[...])
    a_t = dt_t * (-jnp.exp(alog_col[...]))
    row = lax.broadcasted_iota(jnp.int32, (q, q), 0)
    col = lax.broadcasted_iota(jnp.int32, (q, q), 1)
    lower = (row >= col).astype(F32)
    upper = (row <= col).astype(F32)
    cs_l = jnp.dot(lower, a_l, precision=HIGHEST, preferred_element_type=F32)
    cs_t = jnp.dot(a_t, upper, precision=HIGHEST, preferred_element_type=F32)
    tot_l = cs_l[q - 1:q, :]
    if reverse:
        p_l, p_t = cs_l - a_l, cs_t - a_t
        tri = row <= col
    else:
        p_l, p_t = cs_l, cs_t
        tri = row >= col

    tot_t = cs_t[:, q - 1:q]
    to_end_t = jnp.exp(p_t) if reverse else jnp.exp(tot_t - p_t)
    w_t = dt_t * to_end_t
    left = lax.broadcasted_iota(jnp.int32, (q, 2 * SSD_HEAD_DIM), 1) < SSD_HEAD_DIM

    xbc = xbc_ref[0]
    heads_per_group = SSD_HEADS // SSD_GROUPS
    for g in range(SSD_GROUPS):
        b_g = xbc[:, SSD_WIDTH + g * SSD_STATE:SSD_WIDTH + (g + 1) * SSD_STATE]
        c_g = xbc[:, SSD_WIDTH + (SSD_GROUPS + g) * SSD_STATE:SSD_WIDTH + (SSD_GROUPS + g + 1) * SSD_STATE]
        c_bf = c_g.astype(BF16)
        cb = _dot_nt(c_bf, b_g.astype(BF16))
        bt = b_g.T
        for pair in range(g * heads_per_group // 2, (g + 1) * heads_per_group // 2):
            lanes = slice(pair * 2 * SSD_HEAD_DIM, (pair + 1) * 2 * SSD_HEAD_DIM)
            xp = xbc[:, lanes]
            x_bd = jnp.concatenate([jnp.where(left, xp, 0.0), jnp.where(left, 0.0, xp)], axis=0).astype(BF16)
            decay_tiles, state_tiles, pcol_tiles = [], [], []
            for k in (col0 + 2 * pair, col0 + 2 * pair + 1):
                pcol_b = jnp.broadcast_to(p_l[:, k:k + 1], (q, q))
                prow = p_t[k:k + 1, :]
                seg = (prow - pcol_b) if reverse else (pcol_b - prow)
                lmat = jnp.exp(jnp.where(tri, seg, -jnp.inf)) * dt_t[k:k + 1, :]
                decay_tiles.append((cb * lmat).astype(BF16))
                state_tiles.append((bt * w_t[k:k + 1, :]).astype(BF16))
                pcol_tiles.append(pcol_b)
            k0 = col0 + 2 * pair
            tot_pair = jnp.where(left[0:1], tot_l[:, k0:k0 + 1], tot_l[:, k0 + 1:k0 + 2])
            pcol_pair = jnp.where(left, pcol_tiles[0], pcol_tiles[1])
            in_decay = jnp.exp(tot_pair - pcol_pair) if reverse else jnp.exp(pcol_pair)
            s_prev = state_scr[pair]
            y_diag = jnp.dot(jnp.concatenate(decay_tiles, axis=1), x_bd, preferred_element_type=F32)
            y_off = jnp.dot(c_bf, s_prev.astype(BF16), preferred_element_type=F32) * in_decay
            state_scr[pair] = jnp.exp(tot_pair) * s_prev + jnp.dot(jnp.concatenate(state_tiles, axis=1), x_bd,
                                                                   preferred_element_type=F32)
            y_ref[0, :, lanes] = y_diag + y_off


def _ssd_kernel(xf_ref, dtf_ref, dttf_ref, xb_ref, dtb_ref, dttb_ref, dtb_row, dtb_col, alog_row, alog_col,
                initf_ref, initb_ref, yf_ref, yb_ref, finf_ref, finb_ref, statef_scr, stateb_scr):
    c = pl.program_id(1)

    @pl.when(c == 0)
    def _():
        statef_scr[...] = initf_ref[0]
        stateb_scr[...] = initb_ref[0]

    params = (dtb_row, dtb_col, alog_row, alog_col)
    _ssd_chunk(False, xf_ref, dtf_ref, dttf_ref, *params, yf_ref, statef_scr)
    _ssd_chunk(True, xb_ref, dtb_ref, dttb_ref, *params, yb_ref, stateb_scr)

    @pl.when(c == pl.num_programs(1) - 1)
    def _():
        finf_ref[0] = statef_scr[...]
        finb_ref[0] = stateb_scr[...]


def _ssd_scan(xbc_act, dt3, dtt3, dtb, alog, init_f, init_b):
    b, l, _ = xbc_act.shape
    nc = l // SSD_CHUNK
    dtb_row = jnp.pad(dtb.reshape(1, -1), ((0, 0), (0, LANES - 2 * SSD_HEADS)))
    alog_row = jnp.pad(alog.reshape(1, -1), ((0, 0), (0, LANES - 2 * SSD_HEADS)))
    small = lambda shape: pl.BlockSpec(shape, lambda i, c: (0, 0))
    st_shape = (b, SSD_HEADS // 2, SSD_STATE, 2 * SSD_HEAD_DIM)
    st_spec = pl.BlockSpec((1,) + st_shape[1:], lambda i, c: (i, 0, 0, 0))
    fwd, bwd = (lambda c: c), (lambda c: nc - 1 - c)
    chunk_specs = lambda cm: [
        pl.BlockSpec((1, SSD_CHUNK, SSD_XBC), lambda i, c: (i, cm(c), 0)),
        pl.BlockSpec((1, SSD_CHUNK, LANES), lambda i, c: (i, cm(c), 0)),
        pl.BlockSpec((1, 2 * SSD_HEADS, SSD_CHUNK), lambda i, c: (i, 0, cm(c))),
    ]
    y_spec = lambda cm: pl.BlockSpec((1, SSD_CHUNK, SSD_WIDTH), lambda i, c: (i, cm(c), 0))
    y_shape = jax.ShapeDtypeStruct((b, l, SSD_WIDTH), F32)
    return pl.pallas_call(
        _ssd_kernel,
        grid=(b, nc),
        in_specs=chunk_specs(fwd) + chunk_specs(bwd)
        + [small((1, LANES)), small((2 * SSD_HEADS, 1)), small((1, LANES)), small((2 * SSD_HEADS, 1)), st_spec, st_spec],
        out_specs=[y_spec(fwd), y_spec(bwd), st_spec, st_spec],
        out_shape=[y_shape, y_shape, jax.ShapeDtypeStruct(st_shape, F32), jax.ShapeDtypeStruct(st_shape, F32)],
        scratch_shapes=[pltpu.VMEM(st_shape[1:], F32), pltpu.VMEM(st_shape[1:], F32)],
        compiler_params=_params("parallel", "arbitrary"),
        name="ssd_scan",
    )(xbc_act, dt3, dtt3, xbc_act, dt3, dtt3, dtb_row, dtb.reshape(-1, 1), alog_row, alog.reshape(-1, 1),
      init_f, init_b)


def _pool_kernel(x_ref, w_ref, sc_ref, o_ref, pad_scr):
    n = x_ref.shape[1]
    zeros = jnp.zeros((POOL_PAD, POOL_GROUP_DIM), F32)
    pad_scr[0:POOL_PAD, :] = zeros
    pad_scr[n + POOL_PAD:n + 2 * POOL_PAD, :] = zeros
    t = lax.broadcasted_iota(jnp.int32, (n, 1), 0)
    for gi, w in enumerate(POOL_WINDOWS):
        sl = slice(gi * POOL_GROUP_DIM, (gi + 1) * POOL_GROUP_DIM)
        x = x_ref[0, :, sl]
        pad_scr[POOL_PAD:n + POOL_PAD, :] = x
        acc = jnp.zeros_like(x)
        for o in range(-(w // 2), w - w // 2):
            acc = acc + pad_scr[POOL_PAD + o:POOL_PAD + o + n, :]
        lo = jnp.maximum(t - w // 2, 0)
        hi = jnp.minimum(t + (w - w // 2 - 1), n - 1)
        pooled = acc / (hi - lo + 1).astype(F32) - x
        y = jnp.dot(pooled.astype(BF16), w_ref[gi].astype(BF16), preferred_element_type=F32)
        o_ref[0, :, sl] = y * sc_ref[:, sl]


def _pool_mixer(u3, w_pool, scale):
    b, l, c = u3.shape
    return pl.pallas_call(
        _pool_kernel,
        grid=(b,),
        in_specs=[
            pl.BlockSpec((1, l, c), lambda i: (i, 0, 0)),
            pl.BlockSpec(w_pool.shape, lambda i: (0, 0, 0)),
            pl.BlockSpec((1, c), lambda i: (0, 0)),
        ],
        out_specs=pl.BlockSpec((1, l, c), lambda i: (i, 0, 0)),
        out_shape=jax.ShapeDtypeStruct((b, l, c), F32),
        scratch_shapes=[pltpu.VMEM((l + 2 * POOL_PAD, POOL_GROUP_DIM), F32)],
        compiler_params=_params("parallel"),
        name="pool_mixer",
    )(u3, w_pool, scale.reshape(1, c))


def _na_bias_kernel(rpb_ref, o_ref):
    h = pl.program_id(0)
    qi = lax.broadcasted_iota(jnp.int32, (GRID_W, LANES), 0)
    lane = lax.broadcasted_iota(jnp.int32, (GRID_W, LANES), 1)
    ki = lane % GRID_W
    second = lane >= GRID_W
    start = jnp.clip(qi - NA_KW // 2, 0, GRID_W - NA_KW)
    in_window = (ki >= start) & (ki < start + NA_KW)
    dc = jnp.clip(ki - qi, -(NA_KW - 1), NA_KW - 1) + NA_KW - 1
    for dr in range(o_ref.shape[1]):
        val = jnp.zeros((GRID_W, LANES), F32)
        for j in range(2 * NA_KW - 1):
            pick = jnp.where(second, rpb_ref[h, dr + 1, j], rpb_ref[h, dr, j])
            val = jnp.where(dc == j, pick, val)
        o_ref[0, dr] = jnp.where(in_window, val, -jnp.inf)


def _na_bias(rpb):
    nh, ndr, ndc = rpb.shape
    return pl.pallas_call(
        _na_bias_kernel,
        grid=(nh,),
        in_specs=[pl.BlockSpec(memory_space=pltpu.SMEM)],
        out_specs=pl.BlockSpec((1, ndr - 1, GRID_W, LANES), lambda h: (h, 0, 0, 0)),
        out_shape=jax.ShapeDtypeStruct((nh, ndr - 1, GRID_W, LANES), F32),
        compiler_params=_params("parallel"),
        name="na_bias",
    )(rpb)


def _na_kernel(kh, q_ref, k_ref, v_ref, kc_ref, vc_ref, bias_ref, o_ref):
    r = pl.program_id(1)
    rows = pl.num_programs(1)
    r0 = jnp.clip(r - kh // 2, 0, rows - kh)
    start = pl.multiple_of(r0 * GRID_W, GRID_W)
    kblk = k_ref[0, pl.ds(start, kh * GRID_W), :]
    vblk = v_ref[0, pl.ds(start, kh * GRID_W), :]
    dr0 = r0 - r + NA_KH - 1
    scale = NA_HEAD_DIM ** -0.5
    pair_w = 2 * NA_HEAD_DIM
    halves = []
    for pair in range(NA_HEADS // 2):
        lanes = slice(pair * pair_w, (pair + 1) * pair_w)
        left_q = lax.broadcasted_iota(jnp.int32, (GRID_W, pair_w), 1) < NA_HEAD_DIM
        q_pair = q_ref[0, :, lanes] * scale
        k_pair, kc_pair = kblk[:, lanes], kc_ref[0, :, lanes]
        for side in range(2):
            keep = left_q if side == 0 else jnp.logical_not(left_q)
            q_h = jnp.where(keep, q_pair, 0.0)
            h = 2 * pair + side
            bias = jnp.concatenate([bias_ref[h, dr0 + 2 * j] for j in range(kh // 2)], axis=1)
            halves.append((_dot_nt(q_h, k_pair) + bias, _dot_nt(q_h, kc_pair)))
    probs = []
    for s_loc, s_ctx in halves:
        m = jnp.maximum(jnp.max(s_loc, axis=-1, keepdims=True), jnp.max(s_ctx, axis=-1, keepdims=True))
        p_loc = jnp.exp(s_loc - m)
        p_ctx = jnp.exp(s_ctx - m)
        inv = 1.0 / (jnp.sum(p_loc, axis=-1, keepdims=True) + jnp.sum(p_ctx, axis=-1, keepdims=True))
        probs.append((p_loc.astype(BF16), p_ctx.astype(BF16), inv))
    for pair in range(NA_HEADS // 2):
        lanes = slice(pair * pair_w, (pair + 1) * pair_w)
        v_pair, vc_pair = vblk[:, lanes], vc_ref[0, :, lanes]
        left_v = lax.broadcasted_iota(jnp.int32, v_pair.shape, 1) < NA_HEAD_DIM
        left_c = lax.broadcasted_iota(jnp.int32, vc_pair.shape, 1) < NA_HEAD_DIM
        out = None
        for side in range(2):
            p_loc, p_ctx, inv = probs[2 * pair + side]
            keep_v = left_v if side == 0 else jnp.logical_not(left_v)
            keep_c = left_c if side == 0 else jnp.logical_not(left_c)
            acc = jnp.dot(p_loc, jnp.where(keep_v, v_pair, 0.0), preferred_element_type=F32)
            acc = acc + jnp.dot(p_ctx, jnp.where(keep_c, vc_pair, 0.0), preferred_element_type=F32)
            out = acc * inv if out is None else out + acc * inv
        o_ref[0, :, lanes] = out


def _neighbourhood_attention(q3, k3, v3, kc3, vc3, bias):
    b, s, c = q3.shape
    rows = s // GRID_W
    kh = min(NA_KH, rows)
    lc = kc3.shape[1]
    full = lambda n: pl.BlockSpec((1, n, c), lambda i, r: (i, 0, 0))
    return pl.pallas_call(
        functools.partial(_na_kernel, kh),
        grid=(b, rows),
        in_specs=[
            pl.BlockSpec((1, GRID_W, c), lambda i, r: (i, r, 0)),
            full(s), full(s), full(lc), full(lc),
            pl.BlockSpec(bias.shape, lambda i, r: (0, 0, 0, 0)),
        ],
        out_specs=pl.BlockSpec((1, GRID_W, c), lambda i, r: (i, r, 0)),
        out_shape=jax.ShapeDtypeStruct((b, s, c), F32),
        compiler_params=_params("parallel", "arbitrary"),
        name="na_attention",
    )(q3, k3, v3, kc3, vc3, bias)


def _ctx_attn_kernel(q_ref, k_ref, v_ref, o_ref):
    scale = NA_HEAD_DIM ** -0.5
    for h in range(NA_HEADS):
        sl = slice(h * NA_HEAD_DIM, (h + 1) * NA_HEAD_DIM)
        s = _dot_nt(q_ref[0, :, sl] * scale, k_ref[0, :, sl])
        p = jnp.exp(s - jnp.max(s, axis=-1, keepdims=True))
        den = jnp.sum(p, axis=-1, keepdims=True)
        o_ref[0, :, sl] = jnp.dot(p.astype(BF16), v_ref[0, :, sl], preferred_element_type=F32) / den


def _context_attention(q3, k3, v3):
    b, l, c = q3.shape
    spec = pl.BlockSpec((1, l, c), lambda i: (i, 0, 0))
    return pl.pallas_call(
        _ctx_attn_kernel,
        grid=(b,),
        in_specs=[spec, spec, spec],
        out_specs=spec,
        out_shape=jax.ShapeDtypeStruct((b, l, c), F32),
        compiler_params=_params("parallel"),
        name="ctx_attention",
    )(q3, k3, v3)


def _mix_out_kernel(yf_ref, yb_ref, xs_ref, z_ref, pool_ref, na_ref, h_ref, dsk_ref, ng_ref, g1_ref, w_ref, o_ref):
    y = yf_ref[...] + yb_ref[...] + dsk_ref[...] * xs_ref[...]
    y = y * _silu(z_ref[...])
    ms = jnp.mean(y * y, axis=-1, keepdims=True)
    yn = (y * lax.rsqrt(ms + RMS_EPS)) * ng_ref[...]
    mix = jnp.dot(yn.astype(BF16), w_ref[0:SSD_WIDTH, :], preferred_element_type=F32)
    mix = mix + jnp.dot(pool_ref[...].astype(BF16), w_ref[SSD_WIDTH:SSD_WIDTH + POOL_WIDTH, :],
                        preferred_element_type=F32)
    mix = mix + jnp.dot(na_ref[...].astype(BF16), w_ref[SSD_WIDTH + POOL_WIDTH:, :], preferred_element_type=F32)
    o_ref[...] = h_ref[...] + g1_ref[0] * mix


def _mix_out(yf, yb, xbc_act2d, z, pool_y, na_y, h2d, seq_len, d_skip, norm_g, g1, w_out_bf):
    n, d = h2d.shape
    t = 256
    row = lambda w: pl.BlockSpec((t, w), lambda i: (i, 0))
    vec = lambda w: pl.BlockSpec((1, w), lambda i: (0, 0))
    return pl.pallas_call(
        _mix_out_kernel,
        grid=(n // t,),
        in_specs=[
            row(SSD_WIDTH), row(SSD_WIDTH), row(SSD_WIDTH), row(SSD_WIDTH), row(POOL_WIDTH), row(NA_WIDTH), row(d),
            vec(SSD_WIDTH), vec(SSD_WIDTH),
            pl.BlockSpec((1, 1, d), _batch_map(t, seq_len, g1.shape[0])),
            pl.BlockSpec(w_out_bf.shape, lambda i: (0, 0)),
        ],
        out_specs=row(d),
        out_shape=jax.ShapeDtypeStruct((n, d), F32),
        compiler_params=_params("parallel"),
        name="mix_out",
    )(yf, yb, xbc_act2d, z, pool_y, na_y, h2d, jnp.repeat(d_skip, SSD_HEAD_DIM).reshape(1, -1),
      norm_g.reshape(1, -1), g1, w_out_bf)


_CAND_ROWS = 16 + 8 * 7 + 8


def _top16_pair_fast(s1, s2):
    t = s1.shape[1]
    row16 = lax.broadcasted_iota(jnp.int32, (PEER_TOPK, t), 0)

    def body(j, carry):
        p1, v1, p2, v2, below = carry
        m1 = jnp.max(jnp.where(s1 < p1, s1, -jnp.inf), axis=0, keepdims=True)
        lower = s2 < p2
        m2 = jnp.max(jnp.where(lower, s2, -jnp.inf), axis=0, keepdims=True)
        return (m1, jnp.where(row16 == j, m1, v1), m2, jnp.where(row16 == j, m2, v2),
                below + jnp.where(lower, 1.0, 0.0))

    top = jnp.full((1, t), jnp.inf, F32)
    zeros = jnp.zeros((PEER_TOPK, t), F32)
    m1, v1, m2, v2, below = lax.fori_loop(0, PEER_TOPK, body, (top, zeros, top, zeros, jnp.full(s2.shape, -1.0, F32)))
    rank2 = below + jnp.where(s2 < m2, 1.0, 0.0)
    count = lambda s, m: jnp.sum(jnp.where(s >= m, 1.0, 0.0), axis=0, keepdims=True)
    return v1, count(s1, m1), v2, rank2, count(s2, m2)


def _first_counts_fast(s1, v1, sel):
    inf = jnp.inf
    rows_v1 = jnp.concatenate([jnp.broadcast_to(v1[0:1], (PEER_TOPK, v1.shape[1]))]
                              + [jnp.broadcast_to(v1[j:j + 1], (8, v1.shape[1])) for j in range(1, 8)]
                              + [v1[8:16]], axis=0)
    bound = jnp.where(sel > 0.0, rows_v1, inf)
    low = bound[0:8]
    for j in range(1, 8):
        low = jnp.minimum(low, bound[16 + 8 * (j - 1):16 + 8 * j])
    tail = jnp.min(bound[64 + 8:64 + 16], axis=0, keepdims=True)
    cnt = jnp.where(s1 >= jnp.minimum(low[0:1], tail), 1.0, 0.0)
    for k in range(1, 8):
        cnt = cnt + jnp.where(s1 >= low[k:k + 1], 1.0, 0.0)
    n_high = jnp.sum(sel[8:16], axis=0, keepdims=True)
    return cnt + jnp.where(s1 >= v1[0:1], n_high, 0.0)


def _first_counts_exact(rank1, sel):
    cnt = jnp.zeros(rank1.shape, F32)
    for j in range(8):
        lo = 0 if j == 0 else 16 + 8 * (j - 1)
        n_j = jnp.sum(sel[lo:lo + (16 if j == 0 else 8)], axis=0, keepdims=True)
        cnt = cnt + jnp.where(rank1 == float(j), n_j, 0.0)
    for j in range(8, 16):
        cnt = cnt + jnp.where(rank1 == float(j), sel[64 + j:65 + j], 0.0)
    return cnt


def _top16_pair_exact(s1, s2):
    n, t = s1.shape
    iota = lax.broadcasted_iota(jnp.int32, (n, t), 0).astype(F32)
    row16 = lax.broadcasted_iota(jnp.int32, (PEER_TOPK, t), 0)

    def pick(work, rank, vals, j):
        m = jnp.max(work, axis=0, keepdims=True)
        idx = jnp.min(jnp.where(work == m, iota, float(n)), axis=0, keepdims=True)
        sel = iota == idx
        return (jnp.where(sel, -jnp.inf, work), jnp.where(sel, lax.convert_element_type(j, F32), rank),
                jnp.where(row16 == j, m, vals))

    def body(j, carry):
        a, b = carry
        return pick(*a, j), pick(*b, j)

    start = lambda s: (s, jnp.full((n, t), float(PEER_TOPK), F32), jnp.zeros((PEER_TOPK, t), F32))
    (_, rank1, v1), (_, rank2, v2) = lax.fori_loop(0, PEER_TOPK, body, (start(s1), start(s2)))
    return rank1, v1, rank2, v2


def _select16_fast(cand):
    def body(_, prev):
        return jnp.max(jnp.where(cand < prev, cand, -jnp.inf), axis=0, keepdims=True)

    m = lax.fori_loop(0, PEER_TOPK, body, jnp.full((1, cand.shape[1]), jnp.inf, F32))
    taken = jnp.where(cand >= m, 1.0, 0.0)
    return taken, jnp.sum(taken, axis=0, keepdims=True)


def _select16_exact(cand):
    iota = lax.broadcasted_iota(jnp.int32, cand.shape, 0).astype(F32)

    def body(_, carry):
        work, sel_acc = carry
        m = jnp.max(work, axis=0, keepdims=True)
        idx = jnp.min(jnp.where(work == m, iota, float(_CAND_ROWS)), axis=0, keepdims=True)
        sel = iota == idx
        return jnp.where(sel, -jnp.inf, work), jnp.where(sel, 1.0, sel_acc)

    return lax.fori_loop(0, PEER_TOPK, body, (cand, jnp.zeros(cand.shape, F32)))[1]


def _candidate_sums(v1, v2):
    blocks = [v1[0:1] + v2]
    for j in range(1, 8):
        blocks.append(v1[j:j + 1] + v2[0:8])
    blocks.append(v1[8:16] + v2[0:1])
    return jnp.concatenate(blocks, axis=0)


def _any_not_16(*counts):
    return jnp.max(sum(jnp.abs(c - float(PEER_TOPK)) for c in counts)) > 0.0


def _bf16_bits(x):
    return pltpu.bitcast(x.astype(BF16).astype(F32), jnp.uint32)


def _pack_row_pairs(x, scr):
    n, t = x.shape
    for j in range(t // LANES):
        scr[j] = x[:, j * LANES:(j + 1) * LANES]
    words = []
    for j in range(t // LANES):
        even = scr[j, pl.ds(0, n // 2, stride=2), :]
        odd = scr[j, pl.ds(1, n // 2, stride=2), :]
        words.append((_bf16_bits(even) >> 16) | _bf16_bits(odd))
    return jnp.concatenate(words, axis=1)


def _pack_same(x):
    w = _bf16_bits(x)
    return w | (w >> 16)


def _peer_score_kernel(h_ref, g_ref, sh_ref, sc_ref, wq_ref, keys_ref,
                       x_out, cnt_out, e1_out, rk_out, e2_out,
                       u_scr, pair_scr, cnt_scr, rank2_scr, top_scr):
    hd = pl.program_id(1)

    @pl.when(hd == 0)
    def _():
        u = _rms_mod(h_ref[...], g_ref[...], sc_ref[0], sh_ref[0]).astype(BF16)
        u_scr[...] = u
        x_out[...] = pltpu.bitcast(u, jnp.uint32)

    q = jnp.dot(u_scr[...], wq_ref[0], preferred_element_type=F32).astype(BF16)
    s1 = _dot_nt(keys_ref[0, 0], q[:, :PEER_SUB])
    s2 = _dot_nt(keys_ref[0, 1], q[:, PEER_SUB:])
    v1, n1, v2, rank2, n2 = _top16_pair_fast(s1, s2)
    cand = _candidate_sums(v1, v2)
    sel, n_sel = _select16_fast(cand)
    cnt_scr[...] = _first_counts_fast(s1, v1, sel)
    rank2_scr[...] = rank2
    top_scr[0:1] = v1[0:1]
    top_scr[1:2] = v2[0:1]
    top_scr[2:3] = jnp.sum(sel * jnp.exp(cand - cand[0:1]), axis=0, keepdims=True)

    @pl.when(_any_not_16(n1, n2, n_sel))
    def _():
        rank1_x, v1_x, rank2_x, v2_x = _top16_pair_exact(s1, s2)
        cand_x = _candidate_sums(v1_x, v2_x)
        sel_x = _select16_exact(cand_x)
        cnt_scr[...] = _first_counts_exact(rank1_x, sel_x)
        rank2_scr[...] = rank2_x
        top_scr[0:1] = v1_x[0:1]
        top_scr[1:2] = v2_x[0:1]
        top_scr[2:3] = jnp.sum(sel_x * jnp.exp(cand_x - cand_x[0:1]), axis=0, keepdims=True)

    cnt_out[0] = _pack_same(cnt_scr[...])
    e1_out[0] = _pack_same(jnp.exp(s1 - top_scr[0:1]))
    rk_out[0] = _pack_row_pairs(rank2_scr[...], pair_scr)
    e2_out[0] = _pack_row_pairs(jnp.exp(s2 - top_scr[1:2]) / top_scr[2:3], pair_scr)


def _peer_scores(h2d, seq_len, g, shift, scale, wq_heads, keys_bf):
    n, d = h2d.shape
    nb = shift.shape[0]
    t = 512
    assert n % t == 0 and (nb == 1 or seq_len % t == 0)
    bm = _batch_map(t, seq_len, nb)
    mod_spec = pl.BlockSpec((1, 1, d), lambda i, hd: bm(i))
    first_out = pl.BlockSpec((1, PEER_NKEYS, t), lambda i, hd: (hd, 0, i))
    first_shape = jax.ShapeDtypeStruct((PEER_HEADS, PEER_NKEYS, n), jnp.uint32)
    second_out = pl.BlockSpec((1, PEER_NKEYS // 2, t), lambda i, hd: (hd, 0, i))
    second_shape = jax.ShapeDtypeStruct((PEER_HEADS, PEER_NKEYS // 2, n), jnp.uint32)
    return pl.pallas_call(
        _peer_score_kernel,
        grid=(n // t, PEER_HEADS),
        in_specs=[
            pl.BlockSpec((t, d), lambda i, hd: (i, 0)),
            pl.BlockSpec((1, d), lambda i, hd: (0, 0)),
            mod_spec, mod_spec,
            pl.BlockSpec((1, d, 2 * PEER_SUB), lambda i, hd: (hd, 0, 0)),
            pl.BlockSpec((1, 2, PEER_NKEYS, PEER_SUB), lambda i, hd: (hd, 0, 0, 0)),
        ],
        out_specs=[pl.BlockSpec((t // 2, d), lambda i, hd: (i, 0)), first_out, first_out, second_out, second_out],
        out_shape=[jax.ShapeDtypeStruct((n // 2, d), jnp.uint32), first_shape, first_shape, second_shape,
                   second_shape],
        scratch_shapes=[pltpu.VMEM((t, d), BF16), pltpu.VMEM((t // LANES, PEER_NKEYS, LANES), F32),
                        pltpu.VMEM((PEER_NKEYS, t), F32), pltpu.VMEM((PEER_NKEYS, t), F32),
                        pltpu.VMEM((8, t), F32)],
        compiler_params=_params("parallel", "arbitrary"),
        name="peer_scores",
    )(h2d, g, shift, scale, wq_heads, keys_bf)


_PEER_EC = 1024
_INV_SQRT2 = 1.0 / math.sqrt(2.0)


def _as_bf16_rows(words):
    return pltpu.bitcast(words, BF16)


_PEER_A_GROUP = 4


def _peer_expert_kernel(n_chunks, x_ref, u_ref, vt_ref, cnt_ref, e1_ref, rk_ref, e2_ref, h_ref, g2_ref, o_ref,
                        ht0_scr, ht1_scr, g0_scr, g1_scr, acc_scr):
    s = pl.program_id(0)
    t = h_ref.shape[0]
    p2 = s - 2
    c2 = lax.rem(jnp.maximum(p2, 0), n_chunks)
    pack = 2 * 8
    n_a = _PEER_EC // PEER_NKEYS

    @pl.when(s == 0)
    def _():
        ht1_scr[...] = jnp.zeros_like(ht1_scr)
        g0_scr[...] = jnp.zeros_like(g0_scr)
        g1_scr[...] = jnp.zeros_like(g1_scr)

    @pl.when((p2 <= 0) | (c2 == 0))
    def _():
        acc_scr[...] = jnp.zeros_like(acc_scr)

    def step(ht_new, ht_old, g_new, g_old):
        ht_new[...] = _dot_nt(_as_bf16_rows(u_ref[...]), _as_bf16_rows(x_ref[...]))
        for tc in range(t // LANES):
            ls = slice(tc * LANES, (tc + 1) * LANES)
            for a0 in range(0, n_a, _PEER_A_GROUP):
                w = [[None] * (PEER_NKEYS // pack) for _ in range(_PEER_A_GROUP)]
                for hd in range(PEER_HEADS):
                    rows1 = [(_as_bf16_rows(jnp.broadcast_to(cnt_ref[hd, a0 + i:a0 + i + 1, ls], (8, LANES))),
                              _as_bf16_rows(jnp.broadcast_to(e1_ref[hd, a0 + i:a0 + i + 1, ls], (8, LANES))))
                             for i in range(_PEER_A_GROUP)]
                    for r in range(PEER_NKEYS // pack):
                        rk = _as_bf16_rows(rk_ref[hd, r * 8:(r + 1) * 8, ls])
                        e2 = _as_bf16_rows(e2_ref[hd, r * 8:(r + 1) * 8, ls])
                        for i, (cn, e1) in enumerate(rows1):
                            term = jnp.where(rk < cn, e2, 0.0) * e1
                            w[i][r] = term if w[i][r] is None else w[i][r] + term
                for i in range(_PEER_A_GROUP):
                    for r in range(PEER_NKEYS // pack):
                        row0 = (a0 + i) * PEER_NKEYS + r * pack
                        hs = ht_old[row0:row0 + pack, ls]
                        act = 0.5 * hs * (1.0 + lax.erf(hs * _INV_SQRT2))
                        g_new[row0 // 2:(row0 + pack) // 2, ls] = pltpu.bitcast(w[i][r] * act.astype(BF16), jnp.uint32)
        acc_scr[...] += jnp.dot(_as_bf16_rows(vt_ref[...]), _as_bf16_rows(g_old[...]),
                                preferred_element_type=F32)

    @pl.when(s % 2 == 0)
    def _():
        step(ht0_scr, ht1_scr, g1_scr, g0_scr)

    @pl.when(s % 2 == 1)
    def _():
        step(ht1_scr, ht0_scr, g0_scr, g1_scr)

    @pl.when((p2 >= 0) & (c2 == n_chunks - 1))
    def _():
        o_ref[...] = h_ref[...] + g2_ref[0] * acc_scr[...].T


def _peer_experts(x_pk, u_pk, vt_pk, cnt, e1, rk, e2, h2d, seq_len, g2):
    n, d = h2d.shape
    t = 512
    n_chunks = 2 * u_pk.shape[0] // _PEER_EC
    total = (n // t) * n_chunks
    ea = _PEER_EC // PEER_NKEYS

    def pair(p):
        p = jnp.clip(p, 0, total - 1)
        return p // n_chunks, lax.rem(p, n_chunks)

    blk = lambda lag: (lambda s: pair(s - lag)[0])
    chk = lambda lag: (lambda s: pair(s - lag)[1])
    tok = pl.BlockSpec((PEER_HEADS, PEER_NKEYS // 2, t), lambda s: (0, 0, blk(1)(s)))
    first = pl.BlockSpec((PEER_HEADS, ea, t), lambda s: (0, chk(1)(s), blk(1)(s)))
    bm = _batch_map(t, seq_len, g2.shape[0])
    return pl.pallas_call(
        functools.partial(_peer_expert_kernel, n_chunks),
        grid=(total + 2,),
        in_specs=[
            pl.BlockSpec((t // 2, d), lambda s: (blk(0)(s), 0)),
            pl.BlockSpec((_PEER_EC // 2, d), lambda s: (chk(0)(s), 0)),
            pl.BlockSpec((d // 2, _PEER_EC), lambda s: (0, chk(2)(s))),
            first, first, tok, tok,
            pl.BlockSpec((t, d), lambda s: (blk(2)(s), 0)),
            pl.BlockSpec((1, 1, d), lambda s: bm(blk(2)(s))),
        ],
        out_specs=pl.BlockSpec((t, d), lambda s: (blk(2)(s), 0)),
        out_shape=jax.ShapeDtypeStruct((n, d), F32),
        scratch_shapes=[pltpu.VMEM((_PEER_EC, t), F32), pltpu.VMEM((_PEER_EC, t), F32),
                        pltpu.VMEM((_PEER_EC // 2, t), jnp.uint32), pltpu.VMEM((_PEER_EC // 2, t), jnp.uint32),
                        pltpu.VMEM((d, t), F32)],
        compiler_params=_params("arbitrary"),
        name="peer_experts",
    )(x_pk, u_pk, vt_pk, cnt, e1, rk, e2, h2d, g2)


def _pack_rows_kernel(transpose, x_ref, o_ref):
    x = x_ref[0].T if transpose else x_ref[0]
    o_ref[...] = pltpu.bitcast(x.astype(BF16), jnp.uint32)


def _pack_bf16_rows(stack, layer, transpose=False):
    tile = 512
    if transpose:
        _, c, r = stack.shape
        in_spec = pl.BlockSpec((1, tile, r), lambda i: (layer, i, 0))
        out_spec = pl.BlockSpec((r // 2, tile), lambda i: (0, i))
        steps = c // tile
    else:
        _, r, c = stack.shape
        in_spec = pl.BlockSpec((1, tile, c), lambda i: (layer, i, 0))
        out_spec = pl.BlockSpec((tile // 2, c), lambda i: (i, 0))
        steps = r // tile
    return pl.pallas_call(
        functools.partial(_pack_rows_kernel, transpose),
        grid=(steps,),
        in_specs=[in_spec],
        out_specs=out_spec,
        out_shape=jax.ShapeDtypeStruct((r // 2, c), jnp.uint32),
        compiler_params=_params("parallel"),
        name="pack_bf16_rows_t" if transpose else "pack_bf16_rows",
    )(stack)


def _peer_ffn_residual(h2d, seq_len, norm_g, shift, scale, gate, wq_heads, keys_bf, u_pk, vt_pk):
    x_pk, cnt, e1, rk, e2 = _peer_scores(h2d, seq_len, norm_g, shift, scale, wq_heads, keys_bf)
    return _peer_experts(x_pk, u_pk, vt_pk, cnt, e1, rk, e2, h2d, seq_len, gate)


def _final_norm_kernel(h_ref, g_ref, o_ref):
    x = h_ref[...]
    ms = jnp.mean(x * x, axis=-1, keepdims=True)
    o_ref[...] = (x * lax.rsqrt(ms + RMS_EPS)) * g_ref[...]


def _final_norm(h2d, g):
    n, d = h2d.shape
    t = 512
    return pl.pallas_call(
        _final_norm_kernel,
        grid=(n // t,),
        in_specs=[pl.BlockSpec((t, d), lambda i: (i, 0)), pl.BlockSpec((1, d), lambda i: (0, 0))],
        out_specs=pl.BlockSpec((t, d), lambda i: (i, 0)),
        out_shape=jax.ShapeDtypeStruct((n, d), F32),
        compiler_params=_params("parallel"),
        name="final_norm",
    )(h2d, g.reshape(1, d))


def _mixer_inputs(h2d, batch, seq_len, norm_g, shift, scale, w_perm, conv_w, conv_b):
    z, xbc, pool_u, q, k, v, dt = _project(h2d, seq_len, norm_g, shift, scale, w_perm)
    xbc_act = _conv_silu(xbc.reshape(batch, seq_len, SSD_XBC), conv_w, conv_b)
    dt3 = dt.reshape(batch, seq_len, LANES)
    dtt3 = jnp.swapaxes(dt3[:, :, :2 * SSD_HEADS], 1, 2)
    r3 = lambda a: a.reshape(batch, seq_len, a.shape[-1])
    return z, xbc_act, dt3, dtt3, r3(pool_u), r3(q), r3(k), r3(v)


def kernel(x, c, ctx, c_ctx, ada_w, ada_b, norm1_g, w_in, conv_w, conv_b, a_log, dt_bias, d_skip, ssd_norm_g, pool_w, pool_scale, na_rpb, w_out, norm2_g, peer_wq, peer_keys, peer_u, peer_v, final_g):
    batch, seq, d = x.shape
    ctx_len = ctx.shape[1]
    n, nc = batch * seq, batch * ctx_len
    h = x.reshape(n, d)
    hc = ctx.reshape(nc, d)

    c8 = jnp.concatenate([c, c_ctx[None], jnp.zeros((8 - batch - 1, d), F32)], axis=0)
    mod = _modulation(c8, ada_w, ada_b)

    for i in range(DEPTH):
        need_ctx_out = i < DEPTH - 1
        lat = [mod[i, :batch, j * d:(j + 1) * d].reshape(batch, 1, d) for j in range(6)]
        cx = [mod[i, batch:batch + 1, j * d:(j + 1) * d].reshape(1, 1, d) for j in range(6)]
        sh1, sc1, g1, sh2, sc2, g2 = lat
        csh1, csc1, cg1, csh2, csc2, cg2 = cx

        w_perm = _permute_w_in(w_in[i])
        w_out_bf = w_out[i].astype(BF16)
        n1 = norm1_g[i].reshape(1, d)
        n2 = norm2_g[i].reshape(1, d)
        zero_state = jnp.zeros((batch, SSD_HEADS // 2, SSD_STATE, 2 * SSD_HEAD_DIM), F32)
        scan = functools.partial(_ssd_scan, dtb=dt_bias[i], alog=a_log[i])
        wq_heads = peer_wq[i].reshape(d, PEER_HEADS, 2 * PEER_SUB).transpose(1, 0, 2).astype(BF16)
        keys_bf = peer_keys[i].astype(BF16)
        u_pk = _pack_bf16_rows(peer_u, i)
        vt_pk = _pack_bf16_rows(peer_v, i, transpose=True)

        zc, xbc_c, dt3_c, dtt3_c, pool_c, qc, kc, vc = _mixer_inputs(
            hc, batch, ctx_len, n1, csh1, csc1, w_perm, conv_w[i], conv_b[i])
        yf_c, yb_c, st_f, st_b = scan(xbc_c, dt3_c, dtt3_c, init_f=zero_state, init_b=zero_state)
        if need_ctx_out:
            pool_yc = _pool_mixer(pool_c, pool_w[i], pool_scale[i])
            att_c = _context_attention(qc, kc, vc)
            hc = _mix_out(yf_c.reshape(nc, -1), yb_c.reshape(nc, -1), xbc_c.reshape(nc, -1), zc,
                          pool_yc.reshape(nc, -1), att_c.reshape(nc, -1), hc, ctx_len,
                          d_skip[i], ssd_norm_g[i], cg1, w_out_bf)
            hc = _peer_ffn_residual(hc, ctx_len, n2, csh2, csc2, cg2, wq_heads, keys_bf, u_pk, vt_pk)

        z, xbc_l, dt3_l, dtt3_l, pool_l, q, k, v = _mixer_inputs(
            h, batch, seq, n1, sh1, sc1, w_perm, conv_w[i], conv_b[i])
        yf, yb, _, _ = scan(xbc_l, dt3_l, dtt3_l, init_f=st_f, init_b=st_b)
        pool_y = _pool_mixer(pool_l, pool_w[i], pool_scale[i])
        na = _neighbourhood_attention(q, k, v, kc, vc, _na_bias(na_rpb[i]))
        h = _mix_out(yf.reshape(n, -1), yb.reshape(n, -1), xbc_l.reshape(n, -1), z,
                     pool_y.reshape(n, -1), na.reshape(n, -1), h, seq,
                     d_skip[i], ssd_norm_g[i], g1, w_out_bf)
        h = _peer_ffn_residual(h, seq, n2, sh2, sc2, g2, wq_heads, keys_bf, u_pk, vt_pk)

    return _final_norm(h, final_g).reshape(batch, seq, d)
```

```python
import functools
import math

import jax
import jax.numpy as jnp
from jax import lax
from jax.experimental import pallas as pl
from jax.experimental.pallas import tpu as pltpu

F32 = jnp.float32
BF16 = jnp.bfloat16
F8 = jnp.float8_e4m3fn
HIGHEST = lax.Precision.HIGHEST

D_MODEL = 1024
DEPTH = 2
GRID_W = 64
RMS_EPS = 1e-6

SSD_HEAD_DIM = 64
SSD_HEADS = 16
SSD_GROUPS = 2
SSD_STATE = 128
SSD_CHUNK = 128
SSD_WIDTH = SSD_HEADS * SSD_HEAD_DIM
SSD_XBC = SSD_WIDTH + 2 * SSD_GROUPS * SSD_STATE

POOL_WINDOWS = (2, 4, 8, 16)
POOL_GROUP_DIM = 128
POOL_WIDTH = POOL_GROUP_DIM * len(POOL_WINDOWS)
POOL_PAD = 8

NA_HEAD_DIM = 64
NA_HEADS = 8
NA_WIDTH = NA_HEADS * NA_HEAD_DIM
NA_KH = 8
NA_KW = 16

PEER_HEADS = 8
PEER_NKEYS = 128
PEER_TOPK = 16
PEER_SUB = 128

LANES = 128
VMEM_LIMIT_BYTES = 56 * 1024 * 1024

_PROJ_SEGS = (SSD_WIDTH, SSD_XBC, POOL_WIDTH, NA_WIDTH, NA_WIDTH, NA_WIDTH, LANES)
_PROJ_DTYPES = (F32, F32, F32, BF16, BF16, BF16, F32)


def _params(*sem):
    return pltpu.CompilerParams(dimension_semantics=sem, vmem_limit_bytes=VMEM_LIMIT_BYTES)


def _rms_mod(x, g, scale, shift):
    ms = jnp.mean(x * x, axis=-1, keepdims=True)
    return (x * lax.rsqrt(ms + RMS_EPS)) * g * (1.0 + scale) + shift


def _silu(x):
    return x * jax.nn.sigmoid(x)


def _softplus(x):
    return jnp.maximum(x, 0.0) + jnp.log1p(jnp.exp(-jnp.abs(x)))


def _dot_nt(a, b):
    return lax.dot_general(a, b, (((1,), (1,)), ((), ())), preferred_element_type=F32)


def _batch_map(block_rows, seq_len, n_rows):
    if n_rows == 1:
        return lambda i, *_: (0, 0, 0)
    return lambda i, *_: ((i * block_rows) // seq_len, 0, 0)


def _mod_kernel(c_ref, w_ref, b_ref, o_ref):
    s = _silu(c_ref[...])
    o_ref[0] = jnp.dot(s, w_ref[0], precision=HIGHEST, preferred_element_type=F32) + b_ref[0]


def _modulation(c8, ada_w, ada_b):
    depth, d, six_d = ada_w.shape
    tn = 1024
    return pl.pallas_call(
        _mod_kernel,
        grid=(depth, six_d // tn),
        in_specs=[
            pl.BlockSpec((8, d), lambda l, j: (0, 0)),
            pl.BlockSpec((1, d, tn), lambda l, j: (l, 0, j)),
            pl.BlockSpec((1, 1, tn), lambda l, j: (l, 0, j)),
        ],
        out_specs=pl.BlockSpec((1, 8, tn), lambda l, j: (l, 0, j)),
        out_shape=jax.ShapeDtypeStruct((depth, 8, six_d), F32),
        compiler_params=_params("parallel", "parallel"),
        name="adaln_mod",
    )(c8, ada_w, ada_b.reshape(depth, 1, six_d))


def _proj_kernel(h_ref, g_ref, sh_ref, sc_ref, w_ref, *out_refs):
    u = _rms_mod(h_ref[...], g_ref[...], sc_ref[0], sh_ref[0]).astype(BF16)
    off = 0
    for o_ref, width in zip(out_refs, _PROJ_SEGS):
        o_ref[...] = jnp.dot(u, w_ref[:, off:off + width], preferred_element_type=F32).astype(o_ref.dtype)
        off += width


def _project(h2d, seq_len, g, shift, scale, w_perm):
    n, d = h2d.shape
    t = 256
    total = sum(_PROJ_SEGS)
    nb = shift.shape[0]
    row_map = lambda i: (i, 0)
    return pl.pallas_call(
        _proj_kernel,
        grid=(n // t,),
        in_specs=[
            pl.BlockSpec((t, d), row_map),
            pl.BlockSpec((1, d), lambda i: (0, 0)),
            pl.BlockSpec((1, 1, d), _batch_map(t, seq_len, nb)),
            pl.BlockSpec((1, 1, d), _batch_map(t, seq_len, nb)),
            pl.BlockSpec((d, total), lambda i: (0, 0)),
        ],
        out_specs=[pl.BlockSpec((t, w), row_map) for w in _PROJ_SEGS],
        out_shape=[jax.ShapeDtypeStruct((n, w), dt) for w, dt in zip(_PROJ_SEGS, _PROJ_DTYPES)],
        compiler_params=_params("parallel"),
        name="in_proj",
    )(h2d, g, shift, scale, w_perm)


def _permute_w_in(w_in):
    o = 0
    z = w_in[:, o:o + SSD_WIDTH]; o += SSD_WIDTH
    xbc = w_in[:, o:o + SSD_XBC]; o += SSD_XBC
    dt = w_in[:, o:o + 2 * SSD_HEADS]; o += 2 * SSD_HEADS
    rest = w_in[:, o:]
    dt = jnp.pad(dt, ((0, 0), (0, LANES - 2 * SSD_HEADS)))
    return jnp.concatenate([z, xbc, rest, dt], axis=1).astype(BF16)


def _conv_kernel(x_ref, w_ref, b_ref, o_ref):
    x = x_ref[0]
    n = x.shape[0]
    row = lax.broadcasted_iota(jnp.int32, x.shape, 0)
    prev = jnp.where(row == 0, 0.0, pltpu.roll(x, 1, 0))
    nxt = jnp.where(row == n - 1, 0.0, pltpu.roll(x, n - 1, 0))
    y = prev * w_ref[0:1, :] + x * w_ref[1:2, :] + nxt * w_ref[2:3, :] + b_ref[...]
    o_ref[0] = _silu(y)


def _conv_silu(xbc3, conv_w, conv_b):
    b, l, c = xbc3.shape
    tc = 256
    return pl.pallas_call(
        _conv_kernel,
        grid=(b, c // tc),
        in_specs=[
            pl.BlockSpec((1, l, tc), lambda i, j: (i, 0, j)),
            pl.BlockSpec((3, tc), lambda i, j: (0, j)),
            pl.BlockSpec((1, tc), lambda i, j: (0, j)),
        ],
        out_specs=pl.BlockSpec((1, l, tc), lambda i, j: (i, 0, j)),
        out_shape=jax.ShapeDtypeStruct((b, l, c), F32),
        compiler_params=_params("parallel", "parallel"),
        name="dwconv_silu",
    )(xbc3, conv_w, conv_b.reshape(1, c))


def _ssd_chunk(reverse, xbc_ref, dt_ref, dtt_ref, dtb_row, dtb_col, alog_row, alog_col, y_ref, state_scr):
    q = SSD_CHUNK
    col0 = SSD_HEADS if reverse else 0
    dt_l = _softplus(dt_ref[0] + dtb_row[...])
    a_l = dt_l * (-jnp.exp(alog_row[...]))
    dt_t = _softplus(dtt_ref[0] + dtb_col[...])
    a_t = dt_t * (-jnp.exp(alog_col[...]))
    row = lax.broadcasted_iota(jnp.int32, (q, q), 0)
    col = lax.broadcasted_iota(jnp.int32, (q, q), 1)
    lower = (row >= col).astype(F32)
    upper = (row <= col).astype(F32)
    cs_l = jnp.dot(lower, a_l, precision=HIGHEST, preferred_element_type=F32)
    cs_t = jnp.dot(a_t, upper, precision=HIGHEST, preferred_element_type=F32)
    tot_l = cs_l[q - 1:q, :]
    if reverse:
        p_l, p_t = cs_l - a_l, cs_t - a_t
        tri = row <= col
    else:
        p_l, p_t = cs_l, cs_t
        tri = row >= col

    tot_t = cs_t[:, q - 1:q]
    to_end_t = jnp.exp(p_t) if reverse else jnp.exp(tot_t - p_t)
    w_t = dt_t * to_end_t
    left = lax.broadcasted_iota(jnp.int32, (q, 2 * SSD_HEAD_DIM), 1) < SSD_HEAD_DIM

    xbc = xbc_ref[0]
    heads_per_group = SSD_HEADS // SSD_GROUPS
    for g in range(SSD_GROUPS):
        b_g = xbc[:, SSD_WIDTH + g * SSD_STATE:SSD_WIDTH + (g + 1) * SSD_STATE]
        c_g = xbc[:, SSD_WIDTH + (SSD_GROUPS + g) * SSD_STATE:SSD_WIDTH + (SSD_GROUPS + g + 1) * SSD_STATE]
        c_bf = c_g.astype(BF16)
        cb = _dot_nt(c_bf, b_g.astype(BF16))
        bt = b_g.T
        for pair in range(g * heads_per_group // 2, (g + 1) * heads_per_group // 2):
            lanes = slice(pair * 2 * SSD_HEAD_DIM, (pair + 1) * 2 * SSD_HEAD_DIM)
            xp = xbc[:, lanes]
            x_bd = jnp.concatenate([jnp.where(left, xp, 0.0), jnp.where(left, 0.0, xp)], axis=0).astype(BF16)
            decay_tiles, state_tiles, pcol_tiles = [], [], []
            for k in (col0 + 2 * pair, col0 + 2 * pair + 1):
                pcol_b = jnp.broadcast_to(p_l[:, k:k + 1], (q, q))
                prow = p_t[k:k + 1, :]
                seg = (prow - pcol_b) if reverse else (pcol_b - prow)
                lmat = jnp.exp(jnp.where(tri, seg, -jnp.inf)) * dt_t[k:k + 1, :]
                decay_tiles.append((cb * lmat).astype(BF16))
                state_tiles.append((bt * w_t[k:k + 1, :]).astype(BF16))
                pcol_tiles.append(pcol_b)
            k0 = col0 + 2 * pair
            tot_pair = jnp.where(left[0:1], tot_l[:, k0:k0 + 1], tot_l[:, k0 + 1:k0 + 2])
            pcol_pair = jnp.where(left, pcol_tiles[0], pcol_tiles[1])
            in_decay = jnp.exp(tot_pair - pcol_pair) if reverse else jnp.exp(pcol_pair)
            s_prev = state_scr[pair]
            y_diag = jnp.dot(jnp.concatenate(decay_tiles, axis=1), x_bd, preferred_element_type=F32)
            y_off = jnp.dot(c_bf, s_prev.astype(BF16), preferred_element_type=F32) * in_decay
            state_scr[pair] = jnp.exp(tot_pair) * s_prev + jnp.dot(jnp.concatenate(state_tiles, axis=1), x_bd,
                                                                   preferred_element_type=F32)
            y_ref[0, :, lanes] = y_diag + y_off


def _ssd_kernel(xf_ref, dtf_ref, dttf_ref, xb_ref, dtb_ref, dttb_ref, dtb_row, dtb_col, alog_row, alog_col,
                initf_ref, initb_ref, yf_ref, yb_ref, finf_ref, finb_ref, statef_scr, stateb_scr):
    c = pl.program_id(1)

    @pl.when(c == 0)
    def _():
        statef_scr[...] = initf_ref[0]
        stateb_scr[...] = initb_ref[0]

    params = (dtb_row, dtb_col, alog_row, alog_col)
    _ssd_chunk(False, xf_ref, dtf_ref, dttf_ref, *params, yf_ref, statef_scr)
    _ssd_chunk(True, xb_ref, dtb_ref, dttb_ref, *params, yb_ref, stateb_scr)

    @pl.when(c == pl.num_programs(1) - 1)
    def _():
        finf_ref[0] = statef_scr[...]
        finb_ref[0] = stateb_scr[...]


def _ssd_scan(xbc_act, dt3, dtt3, dtb, alog, init_f, init_b):
    b, l, _ = xbc_act.shape
    nc = l // SSD_CHUNK
    dtb_row = jnp.pad(dtb.reshape(1, -1), ((0, 0), (0, LANES - 2 * SSD_HEADS)))
    alog_row = jnp.pad(alog.reshape(1, -1), ((0, 0), (0, LANES - 2 * SSD_HEADS)))
    small = lambda shape: pl.BlockSpec(shape, lambda i, c: (0, 0))
    st_shape = (b, SSD_HEADS // 2, SSD_STATE, 2 * SSD_HEAD_DIM)
    st_spec = pl.BlockSpec((1,) + st_shape[1:], lambda i, c: (i, 0, 0, 0))
    fwd, bwd = (lambda c: c), (lambda c: nc - 1 - c)
    chunk_specs = lambda cm: [
        pl.BlockSpec((1, SSD_CHUNK, SSD_XBC), lambda i, c: (i, cm(c), 0)),
        pl.BlockSpec((1, SSD_CHUNK, LANES), lambda i, c: (i, cm(c), 0)),
        pl.BlockSpec((1, 2 * SSD_HEADS, SSD_CHUNK), lambda i, c: (i, 0, cm(c))),
    ]
    y_spec = lambda cm: pl.BlockSpec((1, SSD_CHUNK, SSD_WIDTH), lambda i, c: (i, cm(c), 0))
    y_shape = jax.ShapeDtypeStruct((b, l, SSD_WIDTH), F32)
    return pl.pallas_call(
        _ssd_kernel,
        grid=(b, nc),
        in_specs=chunk_specs(fwd) + chunk_specs(bwd)
        + [small((1, LANES)), small((2 * SSD_HEADS, 1)), small((1, LANES)), small((2 * SSD_HEADS, 1)), st_spec, st_spec],
        out_specs=[y_spec(fwd), y_spec(bwd), st_spec, st_spec],
        out_shape=[y_shape, y_shape, jax.ShapeDtypeStruct(st_shape, F32), jax.ShapeDtypeStruct(st_shape, F32)],
        scratch_shapes=[pltpu.VMEM(st_shape[1:], F32), pltpu.VMEM(st_shape[1:], F32)],
        compiler_params=_params("parallel", "arbitrary"),
        name="ssd_scan",
    )(xbc_act, dt3, dtt3, xbc_act, dt3, dtt3, dtb_row, dtb.reshape(-1, 1), alog_row, alog.reshape(-1, 1),
      init_f, init_b)


def _pool_kernel(x_ref, w_ref, sc_ref, o_ref, pad_scr):
    n = x_ref.shape[1]
    zeros = jnp.zeros((POOL_PAD, POOL_GROUP_DIM), F32)
    pad_scr[0:POOL_PAD, :] = zeros
    pad_scr[n + POOL_PAD:n + 2 * POOL_PAD, :] = zeros
    t = lax.broadcasted_iota(jnp.int32, (n, 1), 0)
    for gi, w in enumerate(POOL_WINDOWS):
        sl = slice(gi * POOL_GROUP_DIM, (gi + 1) * POOL_GROUP_DIM)
        x = x_ref[0, :, sl]
        pad_scr[POOL_PAD:n + POOL_PAD, :] = x
        acc = jnp.zeros_like(x)
        for o in range(-(w // 2), w - w // 2):
            acc = acc + pad_scr[POOL_PAD + o:POOL_PAD + o + n, :]
        lo = jnp.maximum(t - w // 2, 0)
        hi = jnp.minimum(t + (w - w // 2 - 1), n - 1)
        pooled = acc / (hi - lo + 1).astype(F32) - x
        y = jnp.dot(pooled.astype(BF16), w_ref[gi].astype(BF16), preferred_element_type=F32)
        o_ref[0, :, sl] = y * sc_ref[:, sl]


def _pool_mixer(u3, w_pool, scale):
    b, l, c = u3.shape
    return pl.pallas_call(
        _pool_kernel,
        grid=(b,),
        in_specs=[
            pl.BlockSpec((1, l, c), lambda i: (i, 0, 0)),
            pl.BlockSpec(w_pool.shape, lambda i: (0, 0, 0)),
            pl.BlockSpec((1, c), lambda i: (0, 0)),
        ],
        out_specs=pl.BlockSpec((1, l, c), lambda i: (i, 0, 0)),
        out_shape=jax.ShapeDtypeStruct((b, l, c), F32),
        scratch_shapes=[pltpu.VMEM((l + 2 * POOL_PAD, POOL_GROUP_DIM), F32)],
        compiler_params=_params("parallel"),
        name="pool_mixer",
    )(u3, w_pool, scale.reshape(1, c))


def _na_bias_kernel(rpb_ref, o_ref):
    h = pl.program_id(0)
    qi = lax.broadcasted_iota(jnp.int32, (GRID_W, LANES), 0)
    lane = lax.broadcasted_iota(jnp.int32, (GRID_W, LANES), 1)
    ki = lane % GRID_W
    second = lane >= GRID_W
    start = jnp.clip(qi - NA_KW // 2, 0, GRID_W - NA_KW)
    in_window = (ki >= start) & (ki < start + NA_KW)
    dc = jnp.clip(ki - qi, -(NA_KW - 1), NA_KW - 1) + NA_KW - 1
    for dr in range(o_ref.shape[1]):
        val = jnp.zeros((GRID_W, LANES), F32)
        for j in range(2 * NA_KW - 1):
            pick = jnp.where(second, rpb_ref[h, dr + 1, j], rpb_ref[h, dr, j])
            val = jnp.where(dc == j, pick, val)
        o_ref[0, dr] = jnp.where(in_window, val, -jnp.inf)


def _na_bias(rpb):
    nh, ndr, ndc = rpb.shape
    return pl.pallas_call(
        _na_bias_kernel,
        grid=(nh,),
        in_specs=[pl.BlockSpec(memory_space=pltpu.SMEM)],
        out_specs=pl.BlockSpec((1, ndr - 1, GRID_W, LANES), lambda h: (h, 0, 0, 0)),
        out_shape=jax.ShapeDtypeStruct((nh, ndr - 1, GRID_W, LANES), F32),
        compiler_params=_params("parallel"),
        name="na_bias",
    )(rpb)


def _na_kernel(kh, q_ref, k_ref, v_ref, kc_ref, vc_ref, bias_ref, o_ref):
    r = pl.program_id(1)
    rows = pl.num_programs(1)
    r0 = jnp.clip(r - kh // 2, 0, rows - kh)
    start = pl.multiple_of(r0 * GRID_W, GRID_W)
    kblk = k_ref[0, pl.ds(start, kh * GRID_W), :]
    vblk = v_ref[0, pl.ds(start, kh * GRID_W), :]
    dr0 = r0 - r + NA_KH - 1
    scale = NA_HEAD_DIM ** -0.5
    pair_w = 2 * NA_HEAD_DIM
    halves = []
    for pair in range(NA_HEADS // 2):
        lanes = slice(pair * pair_w, (pair + 1) * pair_w)
        left_q = lax.broadcasted_iota(jnp.int32, (GRID_W, pair_w), 1) < NA_HEAD_DIM
        q_pair = q_ref[0, :, lanes] * scale
        k_pair, kc_pair = kblk[:, lanes], kc_ref[0, :, lanes]
        for side in range(2):
            keep = left_q if side == 0 else jnp.logical_not(left_q)
            q_h = jnp.where(keep, q_pair, 0.0)
            h = 2 * pair + side
            bias = jnp.concatenate([bias_ref[h, dr0 + 2 * j] for j in range(kh // 2)], axis=1)
            halves.append((_dot_nt(q_h, k_pair) + bias, _dot_nt(q_h, kc_pair)))
    probs = []
    for s_loc, s_ctx in halves:
        m = jnp.maximum(jnp.max(s_loc, axis=-1, keepdims=True), jnp.max(s_ctx, axis=-1, keepdims=True))
        p_loc = jnp.exp(s_loc - m)
        p_ctx = jnp.exp(s_ctx - m)
        inv = 1.0 / (jnp.sum(p_loc, axis=-1, keepdims=True) + jnp.sum(p_ctx, axis=-1, keepdims=True))
        probs.append((p_loc.astype(BF16), p_ctx.astype(BF16), inv))
    for pair in range(NA_HEADS // 2):
        lanes = slice(pair * pair_w, (pair + 1) * pair_w)
        v_pair, vc_pair = vblk[:, lanes], vc_ref[0, :, lanes]
        left_v = lax.broadcasted_iota(jnp.int32, v_pair.shape, 1) < NA_HEAD_DIM
        left_c = lax.broadcasted_iota(jnp.int32, vc_pair.shape, 1) < NA_HEAD_DIM
        out = None
        for side in range(2):
            p_loc, p_ctx, inv = probs[2 * pair + side]
            keep_v = left_v if side == 0 else jnp.logical_not(left_v)
            keep_c = left_c if side == 0 else jnp.logical_not(left_c)
            acc = jnp.dot(p_loc, jnp.where(keep_v, v_pair, 0.0), preferred_element_type=F32)
            acc = acc + jnp.dot(p_ctx, jnp.where(keep_c, vc_pair, 0.0), preferred_element_type=F32)
            out = acc * inv if out is None else out + acc * inv
        o_ref[0, :, lanes] = out


def _neighbourhood_attention(q3, k3, v3, kc3, vc3, bias):
    b, s, c = q3.shape
    rows = s // GRID_W
    kh = min(NA_KH, rows)
    lc = kc3.shape[1]
    full = lambda n: pl.BlockSpec((1, n, c), lambda i, r: (i, 0, 0))
    return pl.pallas_call(
        functools.partial(_na_kernel, kh),
        grid=(b, rows),
        in_specs=[
            pl.BlockSpec((1, GRID_W, c), lambda i, r: (i, r, 0)),
            full(s), full(s), full(lc), full(lc),
            pl.BlockSpec(bias.shape, lambda i, r: (0, 0, 0, 0)),
        ],
        out_specs=pl.BlockSpec((1, GRID_W, c), lambda i, r: (i, r, 0)),
        out_shape=jax.ShapeDtypeStruct((b, s, c), F32),
        compiler_params=_params("parallel", "arbitrary"),
        name="na_attention",
    )(q3, k3, v3, kc3, vc3, bias)


def _ctx_attn_kernel(q_ref, k_ref, v_ref, o_ref):
    scale = NA_HEAD_DIM ** -0.5
    for h in range(NA_HEADS):
        sl = slice(h * NA_HEAD_DIM, (h + 1) * NA_HEAD_DIM)
        s = _dot_nt(q_ref[0, :, sl] * scale, k_ref[0, :, sl])
        p = jnp.exp(s - jnp.max(s, axis=-1, keepdims=True))
        den = jnp.sum(p, axis=-1, keepdims=True)
        o_ref[0, :, sl] = jnp.dot(p.astype(BF16), v_ref[0, :, sl], preferred_element_type=F32) / den


def _context_attention(q3, k3, v3):
    b, l, c = q3.shape
    spec = pl.BlockSpec((1, l, c), lambda i: (i, 0, 0))
    return pl.pallas_call(
        _ctx_attn_kernel,
        grid=(b,),
        in_specs=[spec, spec, spec],
        out_specs=spec,
        out_shape=jax.ShapeDtypeStruct((b, l, c), F32),
        compiler_params=_params("parallel"),
        name="ctx_attention",
    )(q3, k3, v3)


def _mix_out_kernel(yf_ref, yb_ref, xs_ref, z_ref, pool_ref, na_ref, h_ref, dsk_ref, ng_ref, g1_ref, w_ref, o_ref):
    y = yf_ref[...] + yb_ref[...] + dsk_ref[...] * xs_ref[...]
    y = y * _silu(z_ref[...])
    ms = jnp.mean(y * y, axis=-1, keepdims=True)
    yn = (y * lax.rsqrt(ms + RMS_EPS)) * ng_ref[...]
    mix = jnp.dot(yn.astype(BF16), w_ref[0:SSD_WIDTH, :], preferred_element_type=F32)
    mix = mix + jnp.dot(pool_ref[...].astype(BF16), w_ref[SSD_WIDTH:SSD_WIDTH + POOL_WIDTH, :],
                        preferred_element_type=F32)
    mix = mix + jnp.dot(na_ref[...].astype(BF16), w_ref[SSD_WIDTH + POOL_WIDTH:, :], preferred_element_type=F32)
    o_ref[...] = h_ref[...] + g1_ref[0] * mix


def _mix_out(yf, yb, xbc_act2d, z, pool_y, na_y, h2d, seq_len, d_skip, norm_g, g1, w_out_bf):
    n, d = h2d.shape
    t = 256
    row = lambda w: pl.BlockSpec((t, w), lambda i: (i, 0))
    vec = lambda w: pl.BlockSpec((1, w), lambda i: (0, 0))
    return pl.pallas_call(
        _mix_out_kernel,
        grid=(n // t,),
        in_specs=[
            row(SSD_WIDTH), row(SSD_WIDTH), row(SSD_WIDTH), row(SSD_WIDTH), row(POOL_WIDTH), row(NA_WIDTH), row(d),
            vec(SSD_WIDTH), vec(SSD_WIDTH),
            pl.BlockSpec((1, 1, d), _batch_map(t, seq_len, g1.shape[0])),
            pl.BlockSpec(w_out_bf.shape, lambda i: (0, 0)),
        ],
        out_specs=row(d),
        out_shape=jax.ShapeDtypeStruct((n, d), F32),
        compiler_params=_params("parallel"),
        name="mix_out",
    )(yf, yb, xbc_act2d, z, pool_y, na_y, h2d, jnp.repeat(d_skip, SSD_HEAD_DIM).reshape(1, -1),
      norm_g.reshape(1, -1), g1, w_out_bf)


_CAND_ROWS = 16 + 8 * 7 + 8


def _top16_pair_fast(s1, s2):
    t = s1.shape[1]
    row16 = lax.broadcasted_iota(jnp.int32, (PEER_TOPK, t), 0)

    def body(j, carry):
        p1, v1, p2, v2, below = carry
        m1 = jnp.max(jnp.where(s1 < p1, s1, -jnp.inf), axis=0, keepdims=True)
        lower = s2 < p2
        m2 = jnp.max(jnp.where(lower, s2, -jnp.inf), axis=0, keepdims=True)
        return (m1, jnp.where(row16 == j, m1, v1), m2, jnp.where(row16 == j, m2, v2),
                below + jnp.where(lower, 1.0, 0.0))

    top = jnp.full((1, t), jnp.inf, F32)
    zeros = jnp.zeros((PEER_TOPK, t), F32)
    m1, v1, m2, v2, below = lax.fori_loop(0, PEER_TOPK, body, (top, zeros, top, zeros, jnp.full(s2.shape, -1.0, F32)))
    rank2 = below + jnp.where(s2 < m2, 1.0, 0.0)
    count = lambda s, m: jnp.sum(jnp.where(s >= m, 1.0, 0.0), axis=0, keepdims=True)
    return v1, count(s1, m1), v2, rank2, count(s2, m2)


def _first_counts_fast(s1, v1, sel):
    inf = jnp.inf
    rows_v1 = jnp.concatenate([jnp.broadcast_to(v1[0:1], (PEER_TOPK, v1.shape[1]))]
                              + [jnp.broadcast_to(v1[j:j + 1], (8, v1.shape[1])) for j in range(1, 8)]
                              + [v1[8:16]], axis=0)
    bound = jnp.where(sel > 0.0, rows_v1, inf)
    low = bound[0:8]
    for j in range(1, 8):
        low = jnp.minimum(low, bound[16 + 8 * (j - 1):16 + 8 * j])
    tail = jnp.min(bound[64 + 8:64 + 16], axis=0, keepdims=True)
    cnt = jnp.where(s1 >= jnp.minimum(low[0:1], tail), 1.0, 0.0)
    for k in range(1, 8):
        cnt = cnt + jnp.where(s1 >= low[k:k + 1], 1.0, 0.0)
    n_high = jnp.sum(sel[8:16], axis=0, keepdims=True)
    return cnt + jnp.where(s1 >= v1[0:1], n_high, 0.0)


def _first_counts_exact(rank1, sel):
    cnt = jnp.zeros(rank1.shape, F32)
    for j in range(8):
        lo = 0 if j == 0 else 16 + 8 * (j - 1)
        n_j = jnp.sum(sel[lo:lo + (16 if j == 0 else 8)], axis=0, keepdims=True)
        cnt = cnt + jnp.where(rank1 == float(j), n_j, 0.0)
    for j in range(8, 16):
        cnt = cnt + jnp.where(rank1 == float(j), sel[64 + j:65 + j], 0.0)
    return cnt


def _top16_pair_exact(s1, s2):
    n, t = s1.shape
    iota = lax.broadcasted_iota(jnp.int32, (n, t), 0).astype(F32)
    row16 = lax.broadcasted_iota(jnp.int32, (PEER_TOPK, t), 0)

    def pick(work, rank, vals, j):
        m = jnp.max(work, axis=0, keepdims=True)
        idx = jnp.min(jnp.where(work == m, iota, float(n)), axis=0, keepdims=True)
        sel = iota == idx
        return (jnp.where(sel, -jnp.inf, work), jnp.where(sel, lax.convert_element_type(j, F32), rank),
                jnp.where(row16 == j, m, vals))

    def body(j, carry):
        a, b = carry
        return pick(*a, j), pick(*b, j)

    start = lambda s: (s, jnp.full((n, t), float(PEER_TOPK), F32), jnp.zeros((PEER_TOPK, t), F32))
    (_, rank1, v1), (_, rank2, v2) = lax.fori_loop(0, PEER_TOPK, body, (start(s1), start(s2)))
    return rank1, v1, rank2, v2


def _select16_fast(cand):
    def body(_, prev):
        return jnp.max(jnp.where(cand < prev, cand, -jnp.inf), axis=0, keepdims=True)

    m = lax.fori_loop(0, PEER_TOPK, body, jnp.full((1, cand.shape[1]), jnp.inf, F32))
    taken = jnp.where(cand >= m, 1.0, 0.0)
    return taken, jnp.sum(taken, axis=0, keepdims=True)


def _select16_exact(cand):
    iota = lax.broadcasted_iota(jnp.int32, cand.shape, 0).astype(F32)

    def body(_, carry):
        work, sel_acc = carry
        m = jnp.max(work, axis=0, keepdims=True)
        idx = jnp.min(jnp.where(work == m, iota, float(_CAND_ROWS)), axis=0, keepdims=True)
        sel = iota == idx
        return jnp.where(sel, -jnp.inf, work), jnp.where(sel, 1.0, sel_acc)

    return lax.fori_loop(0, PEER_TOPK, body, (cand, jnp.zeros(cand.shape, F32)))[1]


def _candidate_sums(v1, v2):
    blocks = [v1[0:1] + v2]
    for j in range(1, 8):
        blocks.append(v1[j:j + 1] + v2[0:8])
    blocks.append(v1[8:16] + v2[0:1])
    return jnp.concatenate(blocks, axis=0)


def _any_not_16(*counts):
    return jnp.max(sum(jnp.abs(c - float(PEER_TOPK)) for c in counts)) > 0.0


def _bf16_bits(x):
    return pltpu.bitcast(x.astype(BF16).astype(F32), jnp.uint32)


def _pack_row_pairs(x, scr):
    n, t = x.shape
    for j in range(t // LANES):
        scr[j] = x[:, j * LANES:(j + 1) * LANES]
    words = []
    for j in range(t // LANES):
        even = scr[j, pl.ds(0, n // 2, stride=2), :]
        odd = scr[j, pl.ds(1, n // 2, stride=2), :]
        words.append((_bf16_bits(even) >> 16) | _bf16_bits(odd))
    return jnp.concatenate(words, axis=1)


def _pack_same(x):
    w = _bf16_bits(x)
    return w | (w >> 16)


_F8_TARGET_EXP = 6


def _pow2_scale(amax):
    bits = pltpu.bitcast(jnp.maximum(amax, 2.0 ** -100), jnp.int32)
    exponent = (bits >> 23) - 127
    return pltpu.bitcast((_F8_TARGET_EXP - exponent + 127) << 23, F32)


def _peer_score_kernel(h_ref, g_ref, sh_ref, sc_ref, wq_ref, keys_ref,
                       x_out, xinv_out, cnt_out, e1_out, rk_out, e2_out,
                       u_scr, pair_scr, cnt_scr, rank2_scr, top_scr):
    hd = pl.program_id(1)

    @pl.when(hd == 0)
    def _():
        u = _rms_mod(h_ref[...], g_ref[...], sc_ref[0], sh_ref[0])
        u_scr[...] = u.astype(BF16)
        scale = _pow2_scale(jnp.max(jnp.abs(u), axis=-1, keepdims=True))
        x_out[...] = (u * scale).astype(F8)
        xinv_out[...] = 1.0 / scale

    q = jnp.dot(u_scr[...], wq_ref[0], preferred_element_type=F32).astype(BF16)
    s1 = _dot_nt(keys_ref[0, 0], q[:, :PEER_SUB])
    s2 = _dot_nt(keys_ref[0, 1], q[:, PEER_SUB:])
    v1, n1, v2, rank2, n2 = _top16_pair_fast(s1, s2)
    cand = _candidate_sums(v1, v2)
    sel, n_sel = _select16_fast(cand)
    cnt_scr[...] = _first_counts_fast(s1, v1, sel)
    rank2_scr[...] = rank2
    top_scr[0:1] = v1[0:1]
    top_scr[1:2] = v2[0:1]
    top_scr[2:3] = jnp.sum(sel * jnp.exp(cand - cand[0:1]), axis=0, keepdims=True)

    @pl.when(_any_not_16(n1, n2, n_sel))
    def _():
        rank1_x, v1_x, rank2_x, v2_x = _top16_pair_exact(s1, s2)
        cand_x = _candidate_sums(v1_x, v2_x)
        sel_x = _select16_exact(cand_x)
        cnt_scr[...] = _first_counts_exact(rank1_x, sel_x)
        rank2_scr[...] = rank2_x
        top_scr[0:1] = v1_x[0:1]
        top_scr[1:2] = v2_x[0:1]
        top_scr[2:3] = jnp.sum(sel_x * jnp.exp(cand_x - cand_x[0:1]), axis=0, keepdims=True)

    cnt_out[0] = _pack_same(cnt_scr[...])
    e1_out[0] = _pack_same(jnp.exp(s1 - top_scr[0:1]))
    rk_out[0] = _pack_row_pairs(rank2_scr[...], pair_scr)
    e2_out[0] = _pack_row_pairs(jnp.exp(s2 - top_scr[1:2]) / top_scr[2:3], pair_scr)


def _peer_scores(h2d, seq_len, g, shift, scale, wq_heads, keys_bf):
    n, d = h2d.shape
    nb = shift.shape[0]
    t = 512
    assert n % t == 0 and (nb == 1 or seq_len % t == 0)
    bm = _batch_map(t, seq_len, nb)
    mod_spec = pl.BlockSpec((1, 1, d), lambda i, hd: bm(i))
    first_out = pl.BlockSpec((1, PEER_NKEYS, t), lambda i, hd: (hd, 0, i))
    first_shape = jax.ShapeDtypeStruct((PEER_HEADS, PEER_NKEYS, n), jnp.uint32)
    second_out = pl.BlockSpec((1, PEER_NKEYS // 2, t), lambda i, hd: (hd, 0, i))
    second_shape = jax.ShapeDtypeStruct((PEER_HEADS, PEER_NKEYS // 2, n), jnp.uint32)
    return pl.pallas_call(
        _peer_score_kernel,
        grid=(n // t, PEER_HEADS),
        in_specs=[
            pl.BlockSpec((t, d), lambda i, hd: (i, 0)),
            pl.BlockSpec((1, d), lambda i, hd: (0, 0)),
            mod_spec, mod_spec,
            pl.BlockSpec((1, d, 2 * PEER_SUB), lambda i, hd: (hd, 0, 0)),
            pl.BlockSpec((1, 2, PEER_NKEYS, PEER_SUB), lambda i, hd: (hd, 0, 0, 0)),
        ],
        out_specs=[pl.BlockSpec((t, d), lambda i, hd: (i, 0)), pl.BlockSpec((t, 1), lambda i, hd: (i, 0)),
                   first_out, first_out, second_out, second_out],
        out_shape=[jax.ShapeDtypeStruct((n, d), F8), jax.ShapeDtypeStruct((n, 1), F32),
                   first_shape, first_shape, second_shape, second_shape],
        scratch_shapes=[pltpu.VMEM((t, d), BF16), pltpu.VMEM((t // LANES, PEER_NKEYS, LANES), F32),
                        pltpu.VMEM((PEER_NKEYS, t), F32), pltpu.VMEM((PEER_NKEYS, t), F32),
                        pltpu.VMEM((8, t), F32)],
        compiler_params=_params("parallel", "arbitrary"),
        name="peer_scores",
    )(h2d, g, shift, scale, wq_heads, keys_bf)


_PEER_EC = 1024
_INV_SQRT2 = 1.0 / math.sqrt(2.0)


def _as_bf16_rows(words):
    return pltpu.bitcast(words, BF16)


_PEER_A_GROUP = 4


def _peer_expert_kernel(n_chunks, x_ref, u_ref, inv_ref, vt_ref, cnt_ref, e1_ref, rk_ref, e2_ref, h_ref, g2_ref, o_ref,
                        ht0_scr, ht1_scr, g0_scr, g1_scr, acc_scr):
    s = pl.program_id(0)
    t = h_ref.shape[0]
    p2 = s - 2
    c2 = lax.rem(jnp.maximum(p2, 0), n_chunks)
    pack = 2 * 8
    n_a = _PEER_EC // PEER_NKEYS

    @pl.when(s == 0)
    def _():
        ht1_scr[...] = jnp.zeros_like(ht1_scr)
        g0_scr[...] = jnp.zeros_like(g0_scr)
        g1_scr[...] = jnp.zeros_like(g1_scr)

    @pl.when((p2 <= 0) | (c2 == 0))
    def _():
        acc_scr[...] = jnp.zeros_like(acc_scr)

    def step(ht_new, ht_old, g_new, g_old):
        ht_new[...] = _dot_nt(u_ref[...], x_ref[...])
        for tc in range(t // LANES):
            ls = slice(tc * LANES, (tc + 1) * LANES)
            for a0 in range(0, n_a, _PEER_A_GROUP):
                w = [[None] * (PEER_NKEYS // pack) for _ in range(_PEER_A_GROUP)]
                for hd in range(PEER_HEADS):
                    rows1 = [(_as_bf16_rows(jnp.broadcast_to(cnt_ref[hd, a0 + i:a0 + i + 1, ls], (8, LANES))),
                              _as_bf16_rows(jnp.broadcast_to(e1_ref[hd, a0 + i:a0 + i + 1, ls], (8, LANES))))
                             for i in range(_PEER_A_GROUP)]
                    for r in range(PEER_NKEYS // pack):
                        rk = _as_bf16_rows(rk_ref[hd, r * 8:(r + 1) * 8, ls])
                        e2 = _as_bf16_rows(e2_ref[hd, r * 8:(r + 1) * 8, ls])
                        for i, (cn, e1) in enumerate(rows1):
                            term = jnp.where(rk < cn, e2, 0.0) * e1
                            w[i][r] = term if w[i][r] is None else w[i][r] + term
                for i in range(_PEER_A_GROUP):
                    for r in range(PEER_NKEYS // pack):
                        row0 = (a0 + i) * PEER_NKEYS + r * pack
                        hs = ht_old[row0:row0 + pack, ls] * inv_ref[:, ls]
                        act = 0.5 * hs * (1.0 + lax.erf(hs * _INV_SQRT2))
                        g_new[row0 // 2:(row0 + pack) // 2, ls] = pltpu.bitcast(w[i][r] * act.astype(BF16), jnp.uint32)
        acc_scr[...] += jnp.dot(_as_bf16_rows(vt_ref[...]), _as_bf16_rows(g_old[...]),
                                preferred_element_type=F32)

    @pl.when(s % 2 == 0)
    def _():
        step(ht0_scr, ht1_scr, g1_scr, g0_scr)

    @pl.when(s % 2 == 1)
    def _():
        step(ht1_scr, ht0_scr, g0_scr, g1_scr)

    @pl.when((p2 >= 0) & (c2 == n_chunks - 1))
    def _():
        o_ref[...] = h_ref[...] + g2_ref[0] * acc_scr[...].T


def _peer_experts(x8, u8, inv_row, vt_pk, cnt, e1, rk, e2, h2d, seq_len, g2):
    n, d = h2d.shape
    t = 512
    n_chunks = u8.shape[0] // _PEER_EC
    total = (n // t) * n_chunks
    ea = _PEER_EC // PEER_NKEYS

    def pair(p):
        p = jnp.clip(p, 0, total - 1)
        return p // n_chunks, lax.rem(p, n_chunks)

    blk = lambda lag: (lambda s: pair(s - lag)[0])
    chk = lambda lag: (lambda s: pair(s - lag)[1])
    tok = pl.BlockSpec((PEER_HEADS, PEER_NKEYS // 2, t), lambda s: (0, 0, blk(1)(s)))
    first = pl.BlockSpec((PEER_HEADS, ea, t), lambda s: (0, chk(1)(s), blk(1)(s)))
    bm = _batch_map(t, seq_len, g2.shape[0])
    return pl.pallas_call(
        functools.partial(_peer_expert_kernel, n_chunks),
        grid=(total + 2,),
        in_specs=[
            pl.BlockSpec((t, d), lambda s: (blk(0)(s), 0)),
            pl.BlockSpec((_PEER_EC, d), lambda s: (chk(0)(s), 0)),
            pl.BlockSpec((1, t), lambda s: (0, blk(1)(s))),
            pl.BlockSpec((d // 2, _PEER_EC), lambda s: (0, chk(2)(s))),
            first, first, tok, tok,
            pl.BlockSpec((t, d), lambda s: (blk(2)(s), 0)),
            pl.BlockSpec((1, 1, d), lambda s: bm(blk(2)(s))),
        ],
        out_specs=pl.BlockSpec((t, d), lambda s: (blk(2)(s), 0)),
        out_shape=jax.ShapeDtypeStruct((n, d), F32),
        scratch_shapes=[pltpu.VMEM((_PEER_EC, t), F32), pltpu.VMEM((_PEER_EC, t), F32),
                        pltpu.VMEM((_PEER_EC // 2, t), jnp.uint32), pltpu.VMEM((_PEER_EC // 2, t), jnp.uint32),
                        pltpu.VMEM((d, t), F32)],
        compiler_params=_params("arbitrary"),
        name="peer_experts",
    )(x8, u8, inv_row, vt_pk, cnt, e1, rk, e2, h2d, g2)


def _pack_rows_kernel(transpose, x_ref, o_ref):
    x = x_ref[0].T if transpose else x_ref[0]
    o_ref[...] = pltpu.bitcast(x.astype(BF16), jnp.uint32)


def _pack_bf16_rows(stack, layer, transpose=False):
    tile = 512
    if transpose:
        _, c, r = stack.shape
        in_spec = pl.BlockSpec((1, tile, r), lambda i: (layer, i, 0))
        out_spec = pl.BlockSpec((r // 2, tile), lambda i: (0, i))
        steps = c // tile
    else:
        _, r, c = stack.shape
        in_spec = pl.BlockSpec((1, tile, c), lambda i: (layer, i, 0))
        out_spec = pl.BlockSpec((tile // 2, c), lambda i: (i, 0))
        steps = r // tile
    return pl.pallas_call(
        functools.partial(_pack_rows_kernel, transpose),
        grid=(steps,),
        in_specs=[in_spec],
        out_specs=out_spec,
        out_shape=jax.ShapeDtypeStruct((r // 2, c), jnp.uint32),
        compiler_params=_params("parallel"),
        name="pack_bf16_rows_t" if transpose else "pack_bf16_rows",
    )(stack)


def _table_absmax_kernel(x_ref, o_ref):
    @pl.when(pl.program_id(0) == 0)
    def _():
        o_ref[...] = jnp.zeros_like(o_ref)

    m = jnp.max(jnp.max(jnp.abs(x_ref[0]), axis=0, keepdims=True), axis=1, keepdims=True)
    o_ref[...] = jnp.maximum(o_ref[...], m)


def _quantize_kernel(s_ref, x_ref, o_ref):
    o_ref[...] = (x_ref[0] * s_ref[0]).astype(F8)


def _quantize_table(stack, layer):
    _, r, c = stack.shape
    tile = 512
    in_spec = pl.BlockSpec((1, tile, c), lambda i: (layer, i, 0))
    amax = pl.pallas_call(
        _table_absmax_kernel,
        grid=(r // tile,),
        in_specs=[in_spec],
        out_specs=pl.BlockSpec((8, LANES), lambda i: (0, 0)),
        out_shape=jax.ShapeDtypeStruct((8, LANES), F32),
        compiler_params=_params("arbitrary"),
        name="table_absmax",
    )(stack)[0, 0]
    exponent = jnp.floor(jnp.log2(jnp.maximum(amax, 2.0 ** -100)))
    scale = jnp.exp2(_F8_TARGET_EXP - exponent)
    q = pl.pallas_call(
        _quantize_kernel,
        grid=(r // tile,),
        in_specs=[pl.BlockSpec(memory_space=pltpu.SMEM), in_spec],
        out_specs=pl.BlockSpec((tile, c), lambda i: (i, 0)),
        out_shape=jax.ShapeDtypeStruct((r, c), F8),
        compiler_params=_params("parallel"),
        name="quantize_table",
    )(scale.reshape(1), stack)
    return q, 1.0 / scale


def _peer_ffn_residual(h2d, seq_len, norm_g, shift, scale, gate, wq_heads, keys_bf, u8, u_inv, vt_pk):
    x8, x_inv, cnt, e1, rk, e2 = _peer_scores(h2d, seq_len, norm_g, shift, scale, wq_heads, keys_bf)
    inv_row = x_inv.reshape(1, -1) * u_inv
    return _peer_experts(x8, u8, inv_row, vt_pk, cnt, e1, rk, e2, h2d, seq_len, gate)


def _final_norm_kernel(h_ref, g_ref, o_ref):
    x = h_ref[...]
    ms = jnp.mean(x * x, axis=-1, keepdims=True)
    o_ref[...] = (x * lax.rsqrt(ms + RMS_EPS)) * g_ref[...]


def _final_norm(h2d, g):
    n, d = h2d.shape
    t = 512
    return pl.pallas_call(
        _final_norm_kernel,
        grid=(n // t,),
        in_specs=[pl.BlockSpec((t, d), lambda i: (i, 0)), pl.BlockSpec((1, d), lambda i: (0, 0))],
        out_specs=pl.BlockSpec((t, d), lambda i: (i, 0)),
        out_shape=jax.ShapeDtypeStruct((n, d), F32),
        compiler_params=_params("parallel"),
        name="final_norm",
    )(h2d, g.reshape(1, d))


def _mixer_inputs(h2d, batch, seq_len, norm_g, shift, scale, w_perm, conv_w, conv_b):
    z, xbc, pool_u, q, k, v, dt = _project(h2d, seq_len, norm_g, shift, scale, w_perm)
    xbc_act = _conv_silu(xbc.reshape(batch, seq_len, SSD_XBC), conv_w, conv_b)
    dt3 = dt.reshape(batch, seq_len, LANES)
    dtt3 = jnp.swapaxes(dt3[:, :, :2 * SSD_HEADS], 1, 2)
    r3 = lambda a: a.reshape(batch, seq_len, a.shape[-1])
    return z, xbc_act, dt3, dtt3, r3(pool_u), r3(q), r3(k), r3(v)


def kernel(x, c, ctx, c_ctx, ada_w, ada_b, norm1_g, w_in, conv_w, conv_b, a_log, dt_bias, d_skip, ssd_norm_g, pool_w, pool_scale, na_rpb, w_out, norm2_g, peer_wq, peer_keys, peer_u, peer_v, final_g):
    batch, seq, d = x.shape
    ctx_len = ctx.shape[1]
    n, nc = batch * seq, batch * ctx_len
    h = x.reshape(n, d)
    hc = ctx.reshape(nc, d)

    c8 = jnp.concatenate([c, c_ctx[None], jnp.zeros((8 - batch - 1, d), F32)], axis=0)
    mod = _modulation(c8, ada_w, ada_b)

    for i in range(DEPTH):
        need_ctx_out = i < DEPTH - 1
        lat = [mod[i, :batch, j * d:(j + 1) * d].reshape(batch, 1, d) for j in range(6)]
        cx = [mod[i, batch:batch + 1, j * d:(j + 1) * d].reshape(1, 1, d) for j in range(6)]
        sh1, sc1, g1, sh2, sc2, g2 = lat
        csh1, csc1, cg1, csh2, csc2, cg2 = cx

        w_perm = _permute_w_in(w_in[i])
        w_out_bf = w_out[i].astype(BF16)
        n1 = norm1_g[i].reshape(1, d)
        n2 = norm2_g[i].reshape(1, d)
        zero_state = jnp.zeros((batch, SSD_HEADS // 2, SSD_STATE, 2 * SSD_HEAD_DIM), F32)
        scan = functools.partial(_ssd_scan, dtb=dt_bias[i], alog=a_log[i])
        wq_heads = peer_wq[i].reshape(d, PEER_HEADS, 2 * PEER_SUB).transpose(1, 0, 2).astype(BF16)
        keys_bf = peer_keys[i].astype(BF16)
        u8, u_inv = _quantize_table(peer_u, i)
        vt_pk = _pack_bf16_rows(peer_v, i, transpose=True)

        zc, xbc_c, dt3_c, dtt3_c, pool_c, qc, kc, vc = _mixer_inputs(
            hc, batch, ctx_len, n1, csh1, csc1, w_perm, conv_w[i], conv_b[i])
        yf_c, yb_c, st_f, st_b = scan(xbc_c, dt3_c, dtt3_c, init_f=zero_state, init_b=zero_state)
        if need_ctx_out:
            pool_yc = _pool_mixer(pool_c, pool_w[i], pool_scale[i])
            att_c = _context_attention(qc, kc, vc)
            hc = _mix_out(yf_c.reshape(nc, -1), yb_c.reshape(nc, -1), xbc_c.reshape(nc, -1), zc,
                          pool_yc.reshape(nc, -1), att_c.reshape(nc, -1), hc, ctx_len,
                          d_skip[i], ssd_norm_g[i], cg1, w_out_bf)
            hc = _peer_ffn_residual(hc, ctx_len, n2, csh2, csc2, cg2, wq_heads, keys_bf, u8, u_inv, vt_pk)

        z, xbc_l, dt3_l, dtt3_l, pool_l, q, k, v = _mixer_inputs(
            h, batch, seq, n1, sh1, sc1, w_perm, conv_w[i], conv_b[i])
        yf, yb, _, _ = scan(xbc_l, dt3_l, dtt3_l, init_f=st_f, init_b=st_b)
        pool_y = _pool_mixer(pool_l, pool_w[i], pool_scale[i])
        na = _neighbourhood_attention(q, k, v, kc, vc, _na_bias(na_rpb[i]))
        h = _mix_out(yf.reshape(n, -1), yb.reshape(n, -1), xbc_l.reshape(n, -1), z,
                     pool_y.reshape(n, -1), na.reshape(n, -1), h, seq,
                     d_skip[i], ssd_norm_g[i], g1, w_out_bf)
        h = _peer_ffn_residual(h, seq, n2, sh2, sc2, g2, wq_heads, keys_bf, u8, u_inv, vt_pk)

    return _final_norm(h, final_g).reshape(batch, seq, d)
```

```python
import functools
import math

import jax
import jax.numpy as jnp
from jax import lax
from jax.experimental import pallas as pl
from jax.experimental.pallas import tpu as pltpu

F32 = jnp.float32
BF16 = jnp.bfloat16
F8 = jnp.float8_e4m3fn
HIGHEST = lax.Precision.HIGHEST

D_MODEL = 1024
DEPTH = 2
GRID_W = 64
RMS_EPS = 1e-6

SSD_HEAD_DIM = 64
SSD_HEADS = 16
SSD_GROUPS = 2
SSD_STATE = 128
SSD_CHUNK = 128
SSD_WIDTH = SSD_HEADS * SSD_HEAD_DIM
SSD_XBC = SSD_WIDTH + 2 * SSD_GROUPS * SSD_STATE

POOL_WINDOWS = (2, 4, 8, 16)
POOL_GROUP_DIM = 128
POOL_WIDTH = POOL_GROUP_DIM * len(POOL_WINDOWS)
POOL_PAD = 8

NA_HEAD_DIM = 64
NA_HEADS = 8
NA_WIDTH = NA_HEADS * NA_HEAD_DIM
NA_KH = 8
NA_KW = 16

PEER_HEADS = 8
PEER_NKEYS = 128
PEER_TOPK = 16
PEER_SUB = 128

LANES = 128
VMEM_LIMIT_BYTES = 56 * 1024 * 1024

_PROJ_SEGS = (SSD_WIDTH, SSD_XBC, POOL_WIDTH, NA_WIDTH, NA_WIDTH, NA_WIDTH, LANES)
_PROJ_DTYPES = (F32, F32, F32, BF16, BF16, BF16, F32)


def _params(*sem):
    return pltpu.CompilerParams(dimension_semantics=sem, vmem_limit_bytes=VMEM_LIMIT_BYTES)


def _rms_mod(x, g, scale, shift):
    ms = jnp.mean(x * x, axis=-1, keepdims=True)
    return (x * lax.rsqrt(ms + RMS_EPS)) * g * (1.0 + scale) + shift


def _silu(x):
    return x * jax.nn.sigmoid(x)


def _softplus(x):
    return jnp.maximum(x, 0.0) + jnp.log1p(jnp.exp(-jnp.abs(x)))


def _dot_nt(a, b):
    return lax.dot_general(a, b, (((1,), (1,)), ((), ())), preferred_element_type=F32)


def _batch_map(block_rows, seq_len, n_rows):
    if n_rows == 1:
        return lambda i, *_: (0, 0, 0)
    return lambda i, *_: ((i * block_rows) // seq_len, 0, 0)


def _mod_kernel(c_ref, w_ref, b_ref, o_ref):
    s = _silu(c_ref[...])
    o_ref[0] = jnp.dot(s, w_ref[0], precision=HIGHEST, preferred_element_type=F32) + b_ref[0]


def _modulation(c8, ada_w, ada_b):
    depth, d, six_d = ada_w.shape
    tn = 1024
    return pl.pallas_call(
        _mod_kernel,
        grid=(depth, six_d // tn),
        in_specs=[
            pl.BlockSpec((8, d), lambda l, j: (0, 0)),
            pl.BlockSpec((1, d, tn), lambda l, j: (l, 0, j)),
            pl.BlockSpec((1, 1, tn), lambda l, j: (l, 0, j)),
        ],
        out_specs=pl.BlockSpec((1, 8, tn), lambda l, j: (l, 0, j)),
        out_shape=jax.ShapeDtypeStruct((depth, 8, six_d), F32),
        compiler_params=_params("parallel", "parallel"),
        name="adaln_mod",
    )(c8, ada_w, ada_b.reshape(depth, 1, six_d))


def _proj_kernel(h_ref, g_ref, sh_ref, sc_ref, w_ref, *out_refs):
    u = _rms_mod(h_ref[...], g_ref[...], sc_ref[0], sh_ref[0]).astype(BF16)
    off = 0
    for o_ref, width in zip(out_refs, _PROJ_SEGS):
        o_ref[...] = jnp.dot(u, w_ref[:, off:off + width], preferred_element_type=F32).astype(o_ref.dtype)
        off += width


def _project(h2d, seq_len, g, shift, scale, w_perm):
    n, d = h2d.shape
    t = 256
    total = sum(_PROJ_SEGS)
    nb = shift.shape[0]
    row_map = lambda i: (i, 0)
    return pl.pallas_call(
        _proj_kernel,
        grid=(n // t,),
        in_specs=[
            pl.BlockSpec((t, d), row_map),
            pl.BlockSpec((1, d), lambda i: (0, 0)),
            pl.BlockSpec((1, 1, d), _batch_map(t, seq_len, nb)),
            pl.BlockSpec((1, 1, d), _batch_map(t, seq_len, nb)),
            pl.BlockSpec((d, total), lambda i: (0, 0)),
        ],
        out_specs=[pl.BlockSpec((t, w), row_map) for w in _PROJ_SEGS],
        out_shape=[jax.ShapeDtypeStruct((n, w), dt) for w, dt in zip(_PROJ_SEGS, _PROJ_DTYPES)],
        compiler_params=_params("parallel"),
        name="in_proj",
    )(h2d, g, shift, scale, w_perm)


def _permute_w_in(w_in):
    o = 0
    z = w_in[:, o:o + SSD_WIDTH]; o += SSD_WIDTH
    xbc = w_in[:, o:o + SSD_XBC]; o += SSD_XBC
    dt = w_in[:, o:o + 2 * SSD_HEADS]; o += 2 * SSD_HEADS
    rest = w_in[:, o:]
    dt = jnp.pad(dt, ((0, 0), (0, LANES - 2 * SSD_HEADS)))
    return jnp.concatenate([z, xbc, rest, dt], axis=1).astype(BF16)


def _conv_kernel(x_ref, w_ref, b_ref, o_ref):
    x = x_ref[0]
    n = x.shape[0]
    row = lax.broadcasted_iota(jnp.int32, x.shape, 0)
    prev = jnp.where(row == 0, 0.0, pltpu.roll(x, 1, 0))
    nxt = jnp.where(row == n - 1, 0.0, pltpu.roll(x, n - 1, 0))
    y = prev * w_ref[0:1, :] + x * w_ref[1:2, :] + nxt * w_ref[2:3, :] + b_ref[...]
    o_ref[0] = _silu(y)


def _conv_silu(xbc3, conv_w, conv_b):
    b, l, c = xbc3.shape
    tc = 256
    return pl.pallas_call(
        _conv_kernel,
        grid=(b, c // tc),
        in_specs=[
            pl.BlockSpec((1, l, tc), lambda i, j: (i, 0, j)),
            pl.BlockSpec((3, tc), lambda i, j: (0, j)),
            pl.BlockSpec((1, tc), lambda i, j: (0, j)),
        ],
        out_specs=pl.BlockSpec((1, l, tc), lambda i, j: (i, 0, j)),
        out_shape=jax.ShapeDtypeStruct((b, l, c), F32),
        compiler_params=_params("parallel", "parallel"),
        name="dwconv_silu",
    )(xbc3, conv_w, conv_b.reshape(1, c))


def _ssd_chunk(reverse, xbc_ref, dt_ref, dtt_ref, dtb_row, dtb_col, alog_row, alog_col, y_ref, state_scr):
    q = SSD_CHUNK
    col0 = SSD_HEADS if reverse else 0
    dt_l = _softplus(dt_ref[0] + dtb_row[...])
    a_l = dt_l * (-jnp.exp(alog_row[...]))
    dt_t = _softplus(dtt_ref[0] + dtb_col[...])
    a_t = dt_t * (-jnp.exp(alog_col[...]))
    row = lax.broadcasted_iota(jnp.int32, (q, q), 0)
    col = lax.broadcasted_iota(jnp.int32, (q, q), 1)
    lower = (row >= col).astype(F32)
    upper = (row <= col).astype(F32)
    cs_l = jnp.dot(lower, a_l, precision=HIGHEST, preferred_element_type=F32)
    cs_t = jnp.dot(a_t, upper, precision=HIGHEST, preferred_element_type=F32)
    tot_l = cs_l[q - 1:q, :]
    if reverse:
        p_l, p_t = cs_l - a_l, cs_t - a_t
        tri = row <= col
    else:
        p_l, p_t = cs_l, cs_t
        tri = row >= col

    tot_t = cs_t[:, q - 1:q]
    to_end_t = jnp.exp(p_t) if reverse else jnp.exp(tot_t - p_t)
    w_t = dt_t * to_end_t
    left = lax.broadcasted_iota(jnp.int32, (q, 2 * SSD_HEAD_DIM), 1) < SSD_HEAD_DIM

    xbc = xbc_ref[0]
    heads_per_group = SSD_HEADS // SSD_GROUPS
    for g in range(SSD_GROUPS):
        b_g = xbc[:, SSD_WIDTH + g * SSD_STATE:SSD_WIDTH + (g + 1) * SSD_STATE]
        c_g = xbc[:, SSD_WIDTH + (SSD_GROUPS + g) * SSD_STATE:SSD_WIDTH + (SSD_GROUPS + g + 1) * SSD_STATE]
        c_bf = c_g.astype(BF16)
        cb = _dot_nt(c_bf, b_g.astype(BF16))
        bt = b_g.T
        for pair in range(g * heads_per_group // 2, (g + 1) * heads_per_group // 2):
            lanes = slice(pair * 2 * SSD_HEAD_DIM, (pair + 1) * 2 * SSD_HEAD_DIM)
            xp = xbc[:, lanes]
            x_bd = jnp.concatenate([jnp.where(left, xp, 0.0), jnp.where(left, 0.0, xp)], axis=0).astype(BF16)
            decay_tiles, state_tiles, pcol_tiles = [], [], []
            for k in (col0 + 2 * pair, col0 + 2 * pair + 1):
                pcol_b = jnp.broadcast_to(p_l[:, k:k + 1], (q, q))
                prow = p_t[k:k + 1, :]
                seg = (prow - pcol_b) if reverse else (pcol_b - prow)
                lmat = jnp.exp(jnp.where(tri, seg, -jnp.inf)) * dt_t[k:k + 1, :]
                decay_tiles.append((cb * lmat).astype(BF16))
                state_tiles.append((bt * w_t[k:k + 1, :]).astype(BF16))
                pcol_tiles.append(pcol_b)
            k0 = col0 + 2 * pair
            tot_pair = jnp.where(left[0:1], tot_l[:, k0:k0 + 1], tot_l[:, k0 + 1:k0 + 2])
            pcol_pair = jnp.where(left, pcol_tiles[0], pcol_tiles[1])
            in_decay = jnp.exp(tot_pair - pcol_pair) if reverse else jnp.exp(pcol_pair)
            s_prev = state_scr[pair]
            y_diag = jnp.dot(jnp.concatenate(decay_tiles, axis=1), x_bd, preferred_element_type=F32)
            y_off = jnp.dot(c_bf, s_prev.astype(BF16), preferred_element_type=F32) * in_decay
            state_scr[pair] = jnp.exp(tot_pair) * s_prev + jnp.dot(jnp.concatenate(state_tiles, axis=1), x_bd,
                                                                   preferred_element_type=F32)
            y_ref[0, :, lanes] = y_diag + y_off


def _ssd_kernel(xf_ref, dtf_ref, dttf_ref, xb_ref, dtb_ref, dttb_ref, dtb_row, dtb_col, alog_row, alog_col,
                initf_ref, initb_ref, yf_ref, yb_ref, finf_ref, finb_ref, statef_scr, stateb_scr):
    c = pl.program_id(1)

    @pl.when(c == 0)
    def _():
        statef_scr[...] = initf_ref[0]
        stateb_scr[...] = initb_ref[0]

    params = (dtb_row, dtb_col, alog_row, alog_col)
    _ssd_chunk(False, xf_ref, dtf_ref, dttf_ref, *params, yf_ref, statef_scr)
    _ssd_chunk(True, xb_ref, dtb_ref, dttb_ref, *params, yb_ref, stateb_scr)

    @pl.when(c == pl.num_programs(1) - 1)
    def _():
        finf_ref[0] = statef_scr[...]
        finb_ref[0] = stateb_scr[...]


def _ssd_scan(xbc_act, dt3, dtt3, dtb, alog, init_f, init_b):
    b, l, _ = xbc_act.shape
    nc = l // SSD_CHUNK
    dtb_row = jnp.pad(dtb.reshape(1, -1), ((0, 0), (0, LANES - 2 * SSD_HEADS)))
    alog_row = jnp.pad(alog.reshape(1, -1), ((0, 0), (0, LANES - 2 * SSD_HEADS)))
    small = lambda shape: pl.BlockSpec(shape, lambda i, c: (0, 0))
    st_shape = (b, SSD_HEADS // 2, SSD_STATE, 2 * SSD_HEAD_DIM)
    st_spec = pl.BlockSpec((1,) + st_shape[1:], lambda i, c: (i, 0, 0, 0))
    fwd, bwd = (lambda c: c), (lambda c: nc - 1 - c)
    chunk_specs = lambda cm: [
        pl.BlockSpec((1, SSD_CHUNK, SSD_XBC), lambda i, c: (i, cm(c), 0)),
        pl.BlockSpec((1, SSD_CHUNK, LANES), lambda i, c: (i, cm(c), 0)),
        pl.BlockSpec((1, 2 * SSD_HEADS, SSD_CHUNK), lambda i, c: (i, 0, cm(c))),
    ]
    y_spec = lambda cm: pl.BlockSpec((1, SSD_CHUNK, SSD_WIDTH), lambda i, c: (i, cm(c), 0))
    y_shape = jax.ShapeDtypeStruct((b, l, SSD_WIDTH), F32)
    return pl.pallas_call(
        _ssd_kernel,
        grid=(b, nc),
        in_specs=chunk_specs(fwd) + chunk_specs(bwd)
        + [small((1, LANES)), small((2 * SSD_HEADS, 1)), small((1, LANES)), small((2 * SSD_HEADS, 1)), st_spec, st_spec],
        out_specs=[y_spec(fwd), y_spec(bwd), st_spec, st_spec],
        out_shape=[y_shape, y_shape, jax.ShapeDtypeStruct(st_shape, F32), jax.ShapeDtypeStruct(st_shape, F32)],
        scratch_shapes=[pltpu.VMEM(st_shape[1:], F32), pltpu.VMEM(st_shape[1:], F32)],
        compiler_params=_params("parallel", "arbitrary"),
        name="ssd_scan",
    )(xbc_act, dt3, dtt3, xbc_act, dt3, dtt3, dtb_row, dtb.reshape(-1, 1), alog_row, alog.reshape(-1, 1),
      init_f, init_b)


def _pool_kernel(x_ref, w_ref, sc_ref, o_ref, pad_scr):
    n = x_ref.shape[1]
    zeros = jnp.zeros((POOL_PAD, POOL_GROUP_DIM), F32)
    pad_scr[0:POOL_PAD, :] = zeros
    pad_scr[n + POOL_PAD:n + 2 * POOL_PAD, :] = zeros
    t = lax.broadcasted_iota(jnp.int32, (n, 1), 0)
    for gi, w in enumerate(POOL_WINDOWS):
        sl = slice(gi * POOL_GROUP_DIM, (gi + 1) * POOL_GROUP_DIM)
        x = x_ref[0, :, sl]
        pad_scr[POOL_PAD:n + POOL_PAD, :] = x
        acc = jnp.zeros_like(x)
        for o in range(-(w // 2), w - w // 2):
            acc = acc + pad_scr[POOL_PAD + o:POOL_PAD + o + n, :]
        lo = jnp.maximum(t - w // 2, 0)
        hi = jnp.minimum(t + (w - w // 2 - 1), n - 1)
        pooled = acc / (hi - lo + 1).astype(F32) - x
        y = jnp.dot(pooled.astype(BF16), w_ref[gi].astype(BF16), preferred_element_type=F32)
        o_ref[0, :, sl] = y * sc_ref[:, sl]


def _pool_mixer(u3, w_pool, scale):
    b, l, c = u3.shape
    return pl.pallas_call(
        _pool_kernel,
        grid=(b,),
        in_specs=[
            pl.BlockSpec((1, l, c), lambda i: (i, 0, 0)),
            pl.BlockSpec(w_pool.shape, lambda i: (0, 0, 0)),
            pl.BlockSpec((1, c), lambda i: (0, 0)),
        ],
        out_specs=pl.BlockSpec((1, l, c), lambda i: (i, 0, 0)),
        out_shape=jax.ShapeDtypeStruct((b, l, c), F32),
        scratch_shapes=[pltpu.VMEM((l + 2 * POOL_PAD, POOL_GROUP_DIM), F32)],
        compiler_params=_params("parallel"),
        name="pool_mixer",
    )(u3, w_pool, scale.reshape(1, c))


def _na_bias_kernel(rpb_ref, o_ref):
    h = pl.program_id(0)
    qi = lax.broadcasted_iota(jnp.int32, (GRID_W, LANES), 0)
    lane = lax.broadcasted_iota(jnp.int32, (GRID_W, LANES), 1)
    ki = lane % GRID_W
    second = lane >= GRID_W
    start = jnp.clip(qi - NA_KW // 2, 0, GRID_W - NA_KW)
    in_window = (ki >= start) & (ki < start + NA_KW)
    dc = jnp.clip(ki - qi, -(NA_KW - 1), NA_KW - 1) + NA_KW - 1
    for dr in range(o_ref.shape[1]):
        val = jnp.zeros((GRID_W, LANES), F32)
        for j in range(2 * NA_KW - 1):
            pick = jnp.where(second, rpb_ref[h, dr + 1, j], rpb_ref[h, dr, j])
            val = jnp.where(dc == j, pick, val)
        o_ref[0, dr] = jnp.where(in_window, val, -jnp.inf)


def _na_bias(rpb):
    nh, ndr, ndc = rpb.shape
    return pl.pallas_call(
        _na_bias_kernel,
        grid=(nh,),
        in_specs=[pl.BlockSpec(memory_space=pltpu.SMEM)],
        out_specs=pl.BlockSpec((1, ndr - 1, GRID_W, LANES), lambda h: (h, 0, 0, 0)),
        out_shape=jax.ShapeDtypeStruct((nh, ndr - 1, GRID_W, LANES), F32),
        compiler_params=_params("parallel"),
        name="na_bias",
    )(rpb)


def _na_kernel(kh, q_ref, k_ref, v_ref, kc_ref, vc_ref, bias_ref, o_ref):
    r = pl.program_id(1)
    rows = pl.num_programs(1)
    r0 = jnp.clip(r - kh // 2, 0, rows - kh)
    start = pl.multiple_of(r0 * GRID_W, GRID_W)
    kblk = k_ref[0, pl.ds(start, kh * GRID_W), :]
    vblk = v_ref[0, pl.ds(start, kh * GRID_W), :]
    dr0 = r0 - r + NA_KH - 1
    scale = NA_HEAD_DIM ** -0.5
    pair_w = 2 * NA_HEAD_DIM
    halves = []
    for pair in range(NA_HEADS // 2):
        lanes = slice(pair * pair_w, (pair + 1) * pair_w)
        left_q = lax.broadcasted_iota(jnp.int32, (GRID_W, pair_w), 1) < NA_HEAD_DIM
        q_pair = q_ref[0, :, lanes] * scale
        k_pair, kc_pair = kblk[:, lanes], kc_ref[0, :, lanes]
        for side in range(2):
            keep = left_q if side == 0 else jnp.logical_not(left_q)
            q_h = jnp.where(keep, q_pair, 0.0)
            h = 2 * pair + side
            bias = jnp.concatenate([bias_ref[h, dr0 + 2 * j] for j in range(kh // 2)], axis=1)
            halves.append((_dot_nt(q_h, k_pair) + bias, _dot_nt(q_h, kc_pair)))
    probs = []
    for s_loc, s_ctx in halves:
        m = jnp.maximum(jnp.max(s_loc, axis=-1, keepdims=True), jnp.max(s_ctx, axis=-1, keepdims=True))
        p_loc = jnp.exp(s_loc - m)
        p_ctx = jnp.exp(s_ctx - m)
        inv = 1.0 / (jnp.sum(p_loc, axis=-1, keepdims=True) + jnp.sum(p_ctx, axis=-1, keepdims=True))
        probs.append((p_loc.astype(BF16), p_ctx.astype(BF16), inv))
    for pair in range(NA_HEADS // 2):
        lanes = slice(pair * pair_w, (pair + 1) * pair_w)
        v_pair, vc_pair = vblk[:, lanes], vc_ref[0, :, lanes]
        left_v = lax.broadcasted_iota(jnp.int32, v_pair.shape, 1) < NA_HEAD_DIM
        left_c = lax.broadcasted_iota(jnp.int32, vc_pair.shape, 1) < NA_HEAD_DIM
        out = None
        for side in range(2):
            p_loc, p_ctx, inv = probs[2 * pair + side]
            keep_v = left_v if side == 0 else jnp.logical_not(left_v)
            keep_c = left_c if side == 0 else jnp.logical_not(left_c)
            acc = jnp.dot(p_loc, jnp.where(keep_v, v_pair, 0.0), preferred_element_type=F32)
            acc = acc + jnp.dot(p_ctx, jnp.where(keep_c, vc_pair, 0.0), preferred_element_type=F32)
            out = acc * inv if out is None else out + acc * inv
        o_ref[0, :, lanes] = out


def _neighbourhood_attention(q3, k3, v3, kc3, vc3, bias):
    b, s, c = q3.shape
    rows = s // GRID_W
    kh = min(NA_KH, rows)
    lc = kc3.shape[1]
    full = lambda n: pl.BlockSpec((1, n, c), lambda i, r: (i, 0, 0))
    return pl.pallas_call(
        functools.partial(_na_kernel, kh),
        grid=(b, rows),
        in_specs=[
            pl.BlockSpec((1, GRID_W, c), lambda i, r: (i, r, 0)),
            full(s), full(s), full(lc), full(lc),
            pl.BlockSpec(bias.shape, lambda i, r: (0, 0, 0, 0)),
        ],
        out_specs=pl.BlockSpec((1, GRID_W, c), lambda i, r: (i, r, 0)),
        out_shape=jax.ShapeDtypeStruct((b, s, c), F32),
        compiler_params=_params("parallel", "arbitrary"),
        name="na_attention",
    )(q3, k3, v3, kc3, vc3, bias)


def _ctx_attn_kernel(q_ref, k_ref, v_ref, o_ref):
    scale = NA_HEAD_DIM ** -0.5
    for h in range(NA_HEADS):
        sl = slice(h * NA_HEAD_DIM, (h + 1) * NA_HEAD_DIM)
        s = _dot_nt(q_ref[0, :, sl] * scale, k_ref[0, :, sl])
        p = jnp.exp(s - jnp.max(s, axis=-1, keepdims=True))
        den = jnp.sum(p, axis=-1, keepdims=True)
        o_ref[0, :, sl] = jnp.dot(p.astype(BF16), v_ref[0, :, sl], preferred_element_type=F32) / den


def _context_attention(q3, k3, v3):
    b, l, c = q3.shape
    spec = pl.BlockSpec((1, l, c), lambda i: (i, 0, 0))
    return pl.pallas_call(
        _ctx_attn_kernel,
        grid=(b,),
        in_specs=[spec, spec, spec],
        out_specs=spec,
        out_shape=jax.ShapeDtypeStruct((b, l, c), F32),
        compiler_params=_params("parallel"),
        name="ctx_attention",
    )(q3, k3, v3)


def _mix_out_kernel(yf_ref, yb_ref, xs_ref, z_ref, pool_ref, na_ref, h_ref, dsk_ref, ng_ref, g1_ref, w_ref, o_ref):
    y = yf_ref[...] + yb_ref[...] + dsk_ref[...] * xs_ref[...]
    y = y * _silu(z_ref[...])
    ms = jnp.mean(y * y, axis=-1, keepdims=True)
    yn = (y * lax.rsqrt(ms + RMS_EPS)) * ng_ref[...]
    mix = jnp.dot(yn.astype(BF16), w_ref[0:SSD_WIDTH, :], preferred_element_type=F32)
    mix = mix + jnp.dot(pool_ref[...].astype(BF16), w_ref[SSD_WIDTH:SSD_WIDTH + POOL_WIDTH, :],
                        preferred_element_type=F32)
    mix = mix + jnp.dot(na_ref[...].astype(BF16), w_ref[SSD_WIDTH + POOL_WIDTH:, :], preferred_element_type=F32)
    o_ref[...] = h_ref[...] + g1_ref[0] * mix


def _mix_out(yf, yb, xbc_act2d, z, pool_y, na_y, h2d, seq_len, d_skip, norm_g, g1, w_out_bf):
    n, d = h2d.shape
    t = 256
    row = lambda w: pl.BlockSpec((t, w), lambda i: (i, 0))
    vec = lambda w: pl.BlockSpec((1, w), lambda i: (0, 0))
    return pl.pallas_call(
        _mix_out_kernel,
        grid=(n // t,),
        in_specs=[
            row(SSD_WIDTH), row(SSD_WIDTH), row(SSD_WIDTH), row(SSD_WIDTH), row(POOL_WIDTH), row(NA_WIDTH), row(d),
            vec(SSD_WIDTH), vec(SSD_WIDTH),
            pl.BlockSpec((1, 1, d), _batch_map(t, seq_len, g1.shape[0])),
            pl.BlockSpec(w_out_bf.shape, lambda i: (0, 0)),
        ],
        out_specs=row(d),
        out_shape=jax.ShapeDtypeStruct((n, d), F32),
        compiler_params=_params("parallel"),
        name="mix_out",
    )(yf, yb, xbc_act2d, z, pool_y, na_y, h2d, jnp.repeat(d_skip, SSD_HEAD_DIM).reshape(1, -1),
      norm_g.reshape(1, -1), g1, w_out_bf)


_CAND_ROWS = 16 + 8 * 7 + 8


def _batcher_pairs(n):
    pairs = []

    def merge(lo, m, r):
        step = 2 * r
        if step < m:
            merge(lo, m, step)
            merge(lo + r, m, step)
            pairs.extend((i, i + r) for i in range(lo + r, lo + m - r, step))
        else:
            pairs.append((lo, lo + r))

    def sort(lo, m):
        if m > 1:
            sort(lo, m // 2)
            sort(lo + m // 2, m // 2)
            merge(lo, m, 1)

    sort(0, n)
    return tuple(pairs)


_SORT16 = _batcher_pairs(PEER_TOPK)
_BITONIC16 = tuple((i, i + d) for d in (8, 4, 2, 1) for i in range(PEER_TOPK) if not i & d)


def _exchange(x, pairs):
    for i, j in pairs:
        x[i], x[j] = jnp.maximum(x[i], x[j]), jnp.minimum(x[i], x[j])


def _sorted_top16(s):
    tiles = []
    for l0 in range(0, s.shape[1], LANES):
        x = [s[8 * v:8 * v + 8, l0:l0 + LANES] for v in range(PEER_TOPK)]
        _exchange(x, _SORT16)
        for shift in (4, 2, 1):
            x = [jnp.maximum(x[i], pltpu.roll(x[PEER_TOPK - 1 - i], shift, 0)) for i in range(PEER_TOPK)]
            _exchange(x, _BITONIC16)
        tiles.append(jnp.concatenate([xi[0:1] for xi in x], axis=0))
    return jnp.concatenate(tiles, axis=1)


def _count_above(s, v):
    r = lambda j: v[j:j + 1]
    pick = jnp.where
    a = s < r(7)
    b = s < pick(a, r(11), r(3))
    c = s < pick(a, pick(b, r(13), r(9)), pick(b, r(5), r(1)))
    d = s < pick(a, pick(b, pick(c, r(14), r(12)), pick(c, r(10), r(8))),
                 pick(b, pick(c, r(6), r(4)), pick(c, r(2), r(0))))
    return (pick(a, 8.0, 0.0) + pick(b, 4.0, 0.0) + pick(c, 2.0, 0.0) + pick(d, 1.0, 0.0)
            + pick(s < r(15), 1.0, 0.0))


def _top16_pair_fast(s1, s2):
    v1, v2 = _sorted_top16(s1), _sorted_top16(s2)
    rank2 = _count_above(s2, v2)

    def check(s, v):
        n = jnp.sum(jnp.where(s >= v[PEER_TOPK - 1:PEER_TOPK], 1.0, 0.0), axis=0, keepdims=True)
        strict = jnp.min(v[:-1] - v[1:], axis=0, keepdims=True) > 0.0
        return jnp.where(strict, n, 0.0)

    return v1, check(s1, v1), v2, rank2, check(s2, v2)


def _first_counts_fast(s1, v1, sel):
    inf = jnp.inf
    rows_v1 = jnp.concatenate([jnp.broadcast_to(v1[0:1], (PEER_TOPK, v1.shape[1]))]
                              + [jnp.broadcast_to(v1[j:j + 1], (8, v1.shape[1])) for j in range(1, 8)]
                              + [v1[8:16]], axis=0)
    bound = jnp.where(sel > 0.0, rows_v1, inf)
    low = bound[0:8]
    for j in range(1, 8):
        low = jnp.minimum(low, bound[16 + 8 * (j - 1):16 + 8 * j])
    tail = jnp.min(bound[64 + 8:64 + 16], axis=0, keepdims=True)
    t = [jnp.minimum(low[0:1], tail)] + [low[k:k + 1] for k in range(1, 8)]
    pick = jnp.where
    a = s1 >= t[3]
    b = s1 >= pick(a, t[5], t[1])
    c = s1 >= pick(a, pick(b, t[6], t[4]), pick(b, t[2], t[0]))
    cnt = pick(a, 4.0, 0.0) + pick(b, 2.0, 0.0) + pick(c, 1.0, 0.0) + pick(s1 >= t[7], 1.0, 0.0)
    n_high = jnp.sum(sel[8:16], axis=0, keepdims=True)
    return cnt + jnp.where(s1 >= v1[0:1], n_high, 0.0)


def _first_counts_exact(rank1, sel):
    cnt = jnp.zeros(rank1.shape, F32)
    for j in range(8):
        lo = 0 if j == 0 else 16 + 8 * (j - 1)
        n_j = jnp.sum(sel[lo:lo + (16 if j == 0 else 8)], axis=0, keepdims=True)
        cnt = cnt + jnp.where(rank1 == float(j), n_j, 0.0)
    for j in range(8, 16):
        cnt = cnt + jnp.where(rank1 == float(j), sel[64 + j:65 + j], 0.0)
    return cnt


def _top16_pair_exact(s1, s2):
    n, t = s1.shape
    iota = lax.broadcasted_iota(jnp.int32, (n, t), 0).astype(F32)
    row16 = lax.broadcasted_iota(jnp.int32, (PEER_TOPK, t), 0)

    def pick(work, rank, vals, j):
        m = jnp.max(work, axis=0, keepdims=True)
        idx = jnp.min(jnp.where(work == m, iota, float(n)), axis=0, keepdims=True)
        sel = iota == idx
        return (jnp.where(sel, -jnp.inf, work), jnp.where(sel, lax.convert_element_type(j, F32), rank),
                jnp.where(row16 == j, m, vals))

    def body(j, carry):
        a, b = carry
        return pick(*a, j), pick(*b, j)

    start = lambda s: (s, jnp.full((n, t), float(PEER_TOPK), F32), jnp.zeros((PEER_TOPK, t), F32))
    (_, rank1, v1), (_, rank2, v2) = lax.fori_loop(0, PEER_TOPK, body, (start(s1), start(s2)))
    return rank1, v1, rank2, v2


def _select16_fast(cand):
    def body(_, prev):
        return jnp.max(jnp.where(cand < prev, cand, -jnp.inf), axis=0, keepdims=True)

    m = lax.fori_loop(0, PEER_TOPK, body, jnp.full((1, cand.shape[1]), jnp.inf, F32))
    taken = jnp.where(cand >= m, 1.0, 0.0)
    return taken, jnp.sum(taken, axis=0, keepdims=True)


def _select16_exact(cand):
    iota = lax.broadcasted_iota(jnp.int32, cand.shape, 0).astype(F32)

    def body(_, carry):
        work, sel_acc = carry
        m = jnp.max(work, axis=0, keepdims=True)
        idx = jnp.min(jnp.where(work == m, iota, float(_CAND_ROWS)), axis=0, keepdims=True)
        sel = iota == idx
        return jnp.where(sel, -jnp.inf, work), jnp.where(sel, 1.0, sel_acc)

    return lax.fori_loop(0, PEER_TOPK, body, (cand, jnp.zeros(cand.shape, F32)))[1]


def _candidate_sums(v1, v2):
    blocks = [v1[0:1] + v2]
    for j in range(1, 8):
        blocks.append(v1[j:j + 1] + v2[0:8])
    blocks.append(v1[8:16] + v2[0:1])
    return jnp.concatenate(blocks, axis=0)


def _any_not_16(*counts):
    return jnp.max(sum(jnp.abs(c - float(PEER_TOPK)) for c in counts)) > 0.0


def _bf16_bits(x):
    return pltpu.bitcast(x.astype(BF16).astype(F32), jnp.uint32)


def _pack_row_pairs(x, scr):
    n, t = x.shape
    for j in range(t // LANES):
        scr[j] = x[:, j * LANES:(j + 1) * LANES]
    words = []
    for j in range(t // LANES):
        even = scr[j, pl.ds(0, n // 2, stride=2), :]
        odd = scr[j, pl.ds(1, n // 2, stride=2), :]
        words.append((_bf16_bits(even) >> 16) | _bf16_bits(odd))
    return jnp.concatenate(words, axis=1)


def _pack_same(x):
    w = _bf16_bits(x)
    return w | (w >> 16)


_F8_TARGET_EXP = 6


def _pow2_scale(amax):
    bits = pltpu.bitcast(jnp.maximum(amax, 2.0 ** -100), jnp.int32)
    exponent = (bits >> 23) - 127
    return pltpu.bitcast((_F8_TARGET_EXP - exponent + 127) << 23, F32)


def _peer_score_kernel(h_ref, g_ref, sh_ref, sc_ref, wq_ref, keys_ref,
                       x_out, xinv_out, cnt_out, e1_out, rk_out, e2_out,
                       u_scr, pair_scr, cnt_scr, rank2_scr, top_scr):
    hd = pl.program_id(1)

    @pl.when(hd == 0)
    def _():
        u = _rms_mod(h_ref[...], g_ref[...], sc_ref[0], sh_ref[0])
        u_scr[...] = u.astype(BF16)
        scale = _pow2_scale(jnp.max(jnp.abs(u), axis=-1, keepdims=True))
        x_out[...] = (u * scale).astype(F8)
        xinv_out[...] = 1.0 / scale

    q = jnp.dot(u_scr[...], wq_ref[0], preferred_element_type=F32).astype(BF16)
    s1 = _dot_nt(keys_ref[0, 0], q[:, :PEER_SUB])
    s2 = _dot_nt(keys_ref[0, 1], q[:, PEER_SUB:])
    v1, n1, v2, rank2, n2 = _top16_pair_fast(s1, s2)
    cand = _candidate_sums(v1, v2)
    sel, n_sel = _select16_fast(cand)
    cnt_scr[...] = _first_counts_fast(s1, v1, sel)
    rank2_scr[...] = rank2
    top_scr[0:1] = v1[0:1]
    top_scr[1:2] = v2[0:1]
    top_scr[2:3] = jnp.sum(sel * jnp.exp(cand - cand[0:1]), axis=0, keepdims=True)

    @pl.when(_any_not_16(n1, n2, n_sel))
    def _():
        rank1_x, v1_x, rank2_x, v2_x = _top16_pair_exact(s1, s2)
        cand_x = _candidate_sums(v1_x, v2_x)
        sel_x = _select16_exact(cand_x)
        cnt_scr[...] = _first_counts_exact(rank1_x, sel_x)
        rank2_scr[...] = rank2_x
        top_scr[0:1] = v1_x[0:1]
        top_scr[1:2] = v2_x[0:1]
        top_scr[2:3] = jnp.sum(sel_x * jnp.exp(cand_x - cand_x[0:1]), axis=0, keepdims=True)

    cnt_out[0] = _pack_same(cnt_scr[...])
    e1_out[0] = _pack_same(jnp.exp(s1 - top_scr[0:1]))
    rk_out[0] = _pack_row_pairs(rank2_scr[...], pair_scr)
    e2_out[0] = _pack_row_pairs(jnp.exp(s2 - top_scr[1:2]) / top_scr[2:3], pair_scr)


def _peer_scores(h2d, seq_len, g, shift, scale, wq_heads, keys_bf):
    n, d = h2d.shape
    nb = shift.shape[0]
    t = 512
    assert n % t == 0 and (nb == 1 or seq_len % t == 0)
    bm = _batch_map(t, seq_len, nb)
    mod_spec = pl.BlockSpec((1, 1, d), lambda i, hd: bm(i))
    first_out = pl.BlockSpec((1, PEER_NKEYS, t), lambda i, hd: (hd, 0, i))
    first_shape = jax.ShapeDtypeStruct((PEER_HEADS, PEER_NKEYS, n), jnp.uint32)
    second_out = pl.BlockSpec((1, PEER_NKEYS // 2, t), lambda i, hd: (hd, 0, i))
    second_shape = jax.ShapeDtypeStruct((PEER_HEADS, PEER_NKEYS // 2, n), jnp.uint32)
    return pl.pallas_call(
        _peer_score_kernel,
        grid=(n // t, PEER_HEADS),
        in_specs=[
            pl.BlockSpec((t, d), lambda i, hd: (i, 0)),
            pl.BlockSpec((1, d), lambda i, hd: (0, 0)),
            mod_spec, mod_spec,
            pl.BlockSpec((1, d, 2 * PEER_SUB), lambda i, hd: (hd, 0, 0)),
            pl.BlockSpec((1, 2, PEER_NKEYS, PEER_SUB), lambda i, hd: (hd, 0, 0, 0)),
        ],
        out_specs=[pl.BlockSpec((t, d), lambda i, hd: (i, 0)), pl.BlockSpec((t, 1), lambda i, hd: (i, 0)),
                   first_out, first_out, second_out, second_out],
        out_shape=[jax.ShapeDtypeStruct((n, d), F8), jax.ShapeDtypeStruct((n, 1), F32),
                   first_shape, first_shape, second_shape, second_shape],
        scratch_shapes=[pltpu.VMEM((t, d), BF16), pltpu.VMEM((t // LANES, PEER_NKEYS, LANES), F32),
                        pltpu.VMEM((PEER_NKEYS, t), F32), pltpu.VMEM((PEER_NKEYS, t), F32),
                        pltpu.VMEM((8, t), F32)],
        compiler_params=_params("parallel", "arbitrary"),
        name="peer_scores",
    )(h2d, g, shift, scale, wq_heads, keys_bf)


_PEER_EC = 1024
_INV_SQRT2 = 1.0 / math.sqrt(2.0)


def _as_bf16_rows(words):
    return pltpu.bitcast(words, BF16)


_PEER_A_GROUP = 4


def _peer_expert_kernel(n_chunks, x_ref, u_ref, inv_ref, vt_ref, cnt_ref, e1_ref, rk_ref, e2_ref, h_ref, g2_ref, o_ref,
                        ht0_scr, ht1_scr, g0_scr, g1_scr, acc_scr):
    s = pl.program_id(0)
    t = h_ref.shape[0]
    p2 = s - 2
    c2 = lax.rem(jnp.maximum(p2, 0), n_chunks)
    pack = 2 * 8
    n_a = _PEER_EC // PEER_NKEYS

    @pl.when(s == 0)
    def _():
        ht1_scr[...] = jnp.zeros_like(ht1_scr)
        g0_scr[...] = jnp.zeros_like(g0_scr)
        g1_scr[...] = jnp.zeros_like(g1_scr)

    @pl.when((p2 <= 0) | (c2 == 0))
    def _():
        acc_scr[...] = jnp.zeros_like(acc_scr)

    def step(ht_new, ht_old, g_new, g_old):
        ht_new[...] = _dot_nt(u_ref[...], x_ref[...])
        for tc in range(t // LANES):
            ls = slice(tc * LANES, (tc + 1) * LANES)
            for a0 in range(0, n_a, _PEER_A_GROUP):
                w = [[None] * (PEER_NKEYS // pack) for _ in range(_PEER_A_GROUP)]
                for hd in range(PEER_HEADS):
                    rows1 = [(_as_bf16_rows(jnp.broadcast_to(cnt_ref[hd, a0 + i:a0 + i + 1, ls], (8, LANES))),
                              _as_bf16_rows(jnp.broadcast_to(e1_ref[hd, a0 + i:a0 + i + 1, ls], (8, LANES))))
                             for i in range(_PEER_A_GROUP)]
                    for r in range(PEER_NKEYS // pack):
                        rk = _as_bf16_rows(rk_ref[hd, r * 8:(r + 1) * 8, ls])
                        e2 = _as_bf16_rows(e2_ref[hd, r * 8:(r + 1) * 8, ls])
                        for i, (cn, e1) in enumerate(rows1):
                            term = jnp.where(rk < cn, e2, 0.0) * e1
                            w[i][r] = term if w[i][r] is None else w[i][r] + term
                for i in range(_PEER_A_GROUP):
                    for r in range(PEER_NKEYS // pack):
                        row0 = (a0 + i) * PEER_NKEYS + r * pack
                        hs = ht_old[row0:row0 + pack, ls] * inv_ref[:, ls]
                        act = 0.5 * hs * (1.0 + lax.erf(hs * _INV_SQRT2))
                        g_new[row0 // 2:(row0 + pack) // 2, ls] = pltpu.bitcast(w[i][r] * act.astype(BF16), jnp.uint32)
        acc_scr[...] += jnp.dot(_as_bf16_rows(vt_ref[...]), _as_bf16_rows(g_old[...]),
                                preferred_element_type=F32)

    @pl.when(s % 2 == 0)
    def _():
        step(ht0_scr, ht1_scr, g1_scr, g0_scr)

    @pl.when(s % 2 == 1)
    def _():
        step(ht1_scr, ht0_scr, g0_scr, g1_scr)

    @pl.when((p2 >= 0) & (c2 == n_chunks - 1))
    def _():
        o_ref[...] = h_ref[...] + g2_ref[0] * acc_scr[...].T


def _peer_experts(x8, u8, inv_row, vt_pk, cnt, e1, rk, e2, h2d, seq_len, g2):
    n, d = h2d.shape
    t = 512
    n_chunks = u8.shape[0] // _PEER_EC
    total = (n // t) * n_chunks
    ea = _PEER_EC // PEER_NKEYS

    def pair(p):
        p = jnp.clip(p, 0, total - 1)
        return p // n_chunks, lax.rem(p, n_chunks)

    blk = lambda lag: (lambda s: pair(s - lag)[0])
    chk = lambda lag: (lambda s: pair(s - lag)[1])
    tok = pl.BlockSpec((PEER_HEADS, PEER_NKEYS // 2, t), lambda s: (0, 0, blk(1)(s)))
    first = pl.BlockSpec((PEER_HEADS, ea, t), lambda s: (0, chk(1)(s), blk(1)(s)))
    bm = _batch_map(t, seq_len, g2.shape[0])
    return pl.pallas_call(
        functools.partial(_peer_expert_kernel, n_chunks),
        grid=(total + 2,),
        in_specs=[
            pl.BlockSpec((t, d), lambda s: (blk(0)(s), 0)),
            pl.BlockSpec((_PEER_EC, d), lambda s: (chk(0)(s), 0)),
            pl.BlockSpec((1, t), lambda s: (0, blk(1)(s))),
            pl.BlockSpec((d // 2, _PEER_EC), lambda s: (0, chk(2)(s))),
            first, first, tok, tok,
            pl.BlockSpec((t, d), lambda s: (blk(2)(s), 0)),
            pl.BlockSpec((1, 1, d), lambda s: bm(blk(2)(s))),
        ],
        out_specs=pl.BlockSpec((t, d), lambda s: (blk(2)(s), 0)),
        out_shape=jax.ShapeDtypeStruct((n, d), F32),
        scratch_shapes=[pltpu.VMEM((_PEER_EC, t), F32), pltpu.VMEM((_PEER_EC, t), F32),
                        pltpu.VMEM((_PEER_EC // 2, t), jnp.uint32), pltpu.VMEM((_PEER_EC // 2, t), jnp.uint32),
                        pltpu.VMEM((d, t), F32)],
        compiler_params=_params("arbitrary"),
        name="peer_experts",
    )(x8, u8, inv_row, vt_pk, cnt, e1, rk, e2, h2d, g2)


def _pack_rows_kernel(transpose, x_ref, o_ref):
    x = x_ref[0].T if transpose else x_ref[0]
    o_ref[...] = pltpu.bitcast(x.astype(BF16), jnp.uint32)


def _pack_bf16_rows(stack, layer, transpose=False):
    tile = 512
    if transpose:
        _, c, r = stack.shape
        in_spec = pl.BlockSpec((1, tile, r), lambda i: (layer, i, 0))
        out_spec = pl.BlockSpec((r // 2, tile), lambda i: (0, i))
        steps = c // tile
    else:
        _, r, c = stack.shape
        in_spec = pl.BlockSpec((1, tile, c), lambda i: (layer, i, 0))
        out_spec = pl.BlockSpec((tile // 2, c), lambda i: (i, 0))
        steps = r // tile
    return pl.pallas_call(
        functools.partial(_pack_rows_kernel, transpose),
        grid=(steps,),
        in_specs=[in_spec],
        out_specs=out_spec,
        out_shape=jax.ShapeDtypeStruct((r // 2, c), jnp.uint32),
        compiler_params=_params("parallel"),
        name="pack_bf16_rows_t" if transpose else "pack_bf16_rows",
    )(stack)


def _table_absmax_kernel(x_ref, o_ref):
    @pl.when(pl.program_id(0) == 0)
    def _():
        o_ref[...] = jnp.zeros_like(o_ref)

    m = jnp.max(jnp.max(jnp.abs(x_ref[0]), axis=0, keepdims=True), axis=1, keepdims=True)
    o_ref[...] = jnp.maximum(o_ref[...], m)


def _quantize_kernel(s_ref, x_ref, o_ref):
    o_ref[...] = (x_ref[0] * s_ref[0]).astype(F8)


def _quantize_table(stack, layer):
    _, r, c = stack.shape
    tile = 512
    in_spec = pl.BlockSpec((1, tile, c), lambda i: (layer, i, 0))
    amax = pl.pallas_call(
        _table_absmax_kernel,
        grid=(r // tile,),
        in_specs=[in_spec],
        out_specs=pl.BlockSpec((8, LANES), lambda i: (0, 0)),
        out_shape=jax.ShapeDtypeStruct((8, LANES), F32),
        compiler_params=_params("arbitrary"),
        name="table_absmax",
    )(stack)[0, 0]
    exponent = jnp.floor(jnp.log2(jnp.maximum(amax, 2.0 ** -100)))
    scale = jnp.exp2(_F8_TARGET_EXP - exponent)
    q = pl.pallas_call(
        _quantize_kernel,
        grid=(r // tile,),
        in_specs=[pl.BlockSpec(memory_space=pltpu.SMEM), in_spec],
        out_specs=pl.BlockSpec((tile, c), lambda i: (i, 0)),
        out_shape=jax.ShapeDtypeStruct((r, c), F8),
        compiler_params=_params("parallel"),
        name="quantize_table",
    )(scale.reshape(1), stack)
    return q, 1.0 / scale


def _peer_ffn_residual(h2d, seq_len, norm_g, shift, scale, gate, wq_heads, keys_bf, u8, u_inv, vt_pk):
    x8, x_inv, cnt, e1, rk, e2 = _peer_scores(h2d, seq_len, norm_g, shift, scale, wq_heads, keys_bf)
    inv_row = x_inv.reshape(1, -1) * u_inv
    return _peer_experts(x8, u8, inv_row, vt_pk, cnt, e1, rk, e2, h2d, seq_len, gate)


def _final_norm_kernel(h_ref, g_ref, o_ref):
    x = h_ref[...]
    ms = jnp.mean(x * x, axis=-1, keepdims=True)
    o_ref[...] = (x * lax.rsqrt(ms + RMS_EPS)) * g_ref[...]


def _final_norm(h2d, g):
    n, d = h2d.shape
    t = 512
    return pl.pallas_call(
        _final_norm_kernel,
        grid=(n // t,),
        in_specs=[pl.BlockSpec((t, d), lambda i: (i, 0)), pl.BlockSpec((1, d), lambda i: (0, 0))],
        out_specs=pl.BlockSpec((t, d), lambda i: (i, 0)),
        out_shape=jax.ShapeDtypeStruct((n, d), F32),
        compiler_params=_params("parallel"),
        name="final_norm",
    )(h2d, g.reshape(1, d))


def _mixer_inputs(h2d, batch, seq_len, norm_g, shift, scale, w_perm, conv_w, conv_b):
    z, xbc, pool_u, q, k, v, dt = _project(h2d, seq_len, norm_g, shift, scale, w_perm)
    xbc_act = _conv_silu(xbc.reshape(batch, seq_len, SSD_XBC), conv_w, conv_b)
    dt3 = dt.reshape(batch, seq_len, LANES)
    dtt3 = jnp.swapaxes(dt3[:, :, :2 * SSD_HEADS], 1, 2)
    r3 = lambda a: a.reshape(batch, seq_len, a.shape[-1])
    return z, xbc_act, dt3, dtt3, r3(pool_u), r3(q), r3(k), r3(v)


def kernel(x, c, ctx, c_ctx, ada_w, ada_b, norm1_g, w_in, conv_w, conv_b, a_log, dt_bias, d_skip, ssd_norm_g, pool_w, pool_scale, na_rpb, w_out, norm2_g, peer_wq, peer_keys, peer_u, peer_v, final_g):
    batch, seq, d = x.shape
    ctx_len = ctx.shape[1]
    n, nc = batch * seq, batch * ctx_len
    h = x.reshape(n, d)
    hc = ctx.reshape(nc, d)

    c8 = jnp.concatenate([c, c_ctx[None], jnp.zeros((8 - batch - 1, d), F32)], axis=0)
    mod = _modulation(c8, ada_w, ada_b)

    for i in range(DEPTH):
        need_ctx_out = i < DEPTH - 1
        lat = [mod[i, :batch, j * d:(j + 1) * d].reshape(batch, 1, d) for j in range(6)]
        cx = [mod[i, batch:batch + 1, j * d:(j + 1) * d].reshape(1, 1, d) for j in range(6)]
        sh1, sc1, g1, sh2, sc2, g2 = lat
        csh1, csc1, cg1, csh2, csc2, cg2 = cx

        w_perm = _permute_w_in(w_in[i])
        w_out_bf = w_out[i].astype(BF16)
        n1 = norm1_g[i].reshape(1, d)
        n2 = norm2_g[i].reshape(1, d)
        zero_state = jnp.zeros((batch, SSD_HEADS // 2, SSD_STATE, 2 * SSD_HEAD_DIM), F32)
        scan = functools.partial(_ssd_scan, dtb=dt_bias[i], alog=a_log[i])
        wq_heads = peer_wq[i].reshape(d, PEER_HEADS, 2 * PEER_SUB).transpose(1, 0, 2).astype(BF16)
        keys_bf = peer_keys[i].astype(BF16)
        u8, u_inv = _quantize_table(peer_u, i)
        vt_pk = _pack_bf16_rows(peer_v, i, transpose=True)

        zc, xbc_c, dt3_c, dtt3_c, pool_c, qc, kc, vc = _mixer_inputs(
            hc, batch, ctx_len, n1, csh1, csc1, w_perm, conv_w[i], conv_b[i])
        yf_c, yb_c, st_f, st_b = scan(xbc_c, dt3_c, dtt3_c, init_f=zero_state, init_b=zero_state)
        if need_ctx_out:
            pool_yc = _pool_mixer(pool_c, pool_w[i], pool_scale[i])
            att_c = _context_attention(qc, kc, vc)
            hc = _mix_out(yf_c.reshape(nc, -1), yb_c.reshape(nc, -1), xbc_c.reshape(nc, -1), zc,
                          pool_yc.reshape(nc, -1), att_c.reshape(nc, -1), hc, ctx_len,
                          d_skip[i], ssd_norm_g[i], cg1, w_out_bf)
            hc = _peer_ffn_residual(hc, ctx_len, n2, csh2, csc2, cg2, wq_heads, keys_bf, u8, u_inv, vt_pk)

        z, xbc_l, dt3_l, dtt3_l, pool_l, q, k, v = _mixer_inputs(
            h, batch, seq, n1, sh1, sc1, w_perm, conv_w[i], conv_b[i])
        yf, yb, _, _ = scan(xbc_l, dt3_l, dtt3_l, init_f=st_f, init_b=st_b)
        pool_y = _pool_mixer(pool_l, pool_w[i], pool_scale[i])
        na = _neighbourhood_attention(q, k, v, kc, vc, _na_bias(na_rpb[i]))
        h = _mix_out(yf.reshape(n, -1), yb.reshape(n, -1), xbc_l.reshape(n, -1), z,
                     pool_y.reshape(n, -1), na.reshape(n, -1), h, seq,
                     d_skip[i], ssd_norm_g[i], g1, w_out_bf)
        h = _peer_ffn_residual(h, seq, n2, sh2, sc2, g2, wq_heads, keys_bf, u8, u_inv, vt_pk)

    return _final_norm(h, final_g).reshape(batch, seq, d)
```

```python
import functools
import math

import jax
import jax.numpy as jnp
from jax import lax
from jax.experimental import pallas as pl
from jax.experimental.pallas import tpu as pltpu

F32 = jnp.float32
BF16 = jnp.bfloat16
F8 = jnp.float8_e4m3fn
HIGHEST = lax.Precision.HIGHEST

D_MODEL = 1024
DEPTH = 2
GRID_W = 64
RMS_EPS = 1e-6

SSD_HEAD_DIM = 64
SSD_HEADS = 16
SSD_GROUPS = 2
SSD_STATE = 128
SSD_CHUNK = 128
SSD_WIDTH = SSD_HEADS * SSD_HEAD_DIM
SSD_XBC = SSD_WIDTH + 2 * SSD_GROUPS * SSD_STATE

POOL_WINDOWS = (2, 4, 8, 16)
POOL_GROUP_DIM = 128
POOL_WIDTH = POOL_GROUP_DIM * len(POOL_WINDOWS)
POOL_PAD = 8

NA_HEAD_DIM = 64
NA_HEADS = 8
NA_WIDTH = NA_HEADS * NA_HEAD_DIM
NA_KH = 8
NA_KW = 16

PEER_HEADS = 8
PEER_NKEYS = 128
PEER_TOPK = 16
PEER_SUB = 128

LANES = 128
VMEM_LIMIT_BYTES = 56 * 1024 * 1024

_PROJ_SEGS = (SSD_WIDTH, SSD_XBC, POOL_WIDTH, NA_WIDTH, NA_WIDTH, NA_WIDTH, LANES)
_PROJ_DTYPES = (F32, F32, F32, BF16, BF16, BF16, F32)


def _params(*sem):
    return pltpu.CompilerParams(dimension_semantics=sem, vmem_limit_bytes=VMEM_LIMIT_BYTES)


def _rms_mod(x, g, scale, shift):
    ms = jnp.mean(x * x, axis=-1, keepdims=True)
    return (x * lax.rsqrt(ms + RMS_EPS)) * g * (1.0 + scale) + shift


def _silu(x):
    return x * jax.nn.sigmoid(x)


def _softplus(x):
    return jnp.maximum(x, 0.0) + jnp.log1p(jnp.exp(-jnp.abs(x)))


def _dot_nt(a, b):
    return lax.dot_general(a, b, (((1,), (1,)), ((), ())), preferred_element_type=F32)


def _batch_map(block_rows, seq_len, n_rows):
    if n_rows == 1:
        return lambda i, *_: (0, 0, 0)
    return lambda i, *_: ((i * block_rows) // seq_len, 0, 0)


def _mod_kernel(c_ref, w_ref, b_ref, o_ref):
    s = _silu(c_ref[...])
    o_ref[0] = jnp.dot(s, w_ref[0], precision=HIGHEST, preferred_element_type=F32) + b_ref[0]


def _modulation(c8, ada_w, ada_b):
    depth, d, six_d = ada_w.shape
    tn = 1024
    return pl.pallas_call(
        _mod_kernel,
        grid=(depth, six_d // tn),
        in_specs=[
            pl.BlockSpec((8, d), lambda l, j: (0, 0)),
            pl.BlockSpec((1, d, tn), lambda l, j: (l, 0, j)),
            pl.BlockSpec((1, 1, tn), lambda l, j: (l, 0, j)),
        ],
        out_specs=pl.BlockSpec((1, 8, tn), lambda l, j: (l, 0, j)),
        out_shape=jax.ShapeDtypeStruct((depth, 8, six_d), F32),
        compiler_params=_params("parallel", "parallel"),
        name="adaln_mod",
    )(c8, ada_w, ada_b.reshape(depth, 1, six_d))


def _proj_kernel(h_ref, g_ref, sh_ref, sc_ref, w_ref, *out_refs):
    u = _rms_mod(h_ref[...], g_ref[...], sc_ref[0], sh_ref[0]).astype(BF16)
    off = 0
    for o_ref, width in zip(out_refs, _PROJ_SEGS):
        o_ref[...] = jnp.dot(u, w_ref[:, off:off + width], preferred_element_type=F32).astype(o_ref.dtype)
        off += width


def _project(h2d, seq_len, g, shift, scale, w_perm):
    n, d = h2d.shape
    t = 256
    total = sum(_PROJ_SEGS)
    nb = shift.shape[0]
    row_map = lambda i: (i, 0)
    return pl.pallas_call(
        _proj_kernel,
        grid=(n // t,),
        in_specs=[
            pl.BlockSpec((t, d), row_map),
            pl.BlockSpec((1, d), lambda i: (0, 0)),
            pl.BlockSpec((1, 1, d), _batch_map(t, seq_len, nb)),
            pl.BlockSpec((1, 1, d), _batch_map(t, seq_len, nb)),
            pl.BlockSpec((d, total), lambda i: (0, 0)),
        ],
        out_specs=[pl.BlockSpec((t, w), row_map) for w in _PROJ_SEGS],
        out_shape=[jax.ShapeDtypeStruct((n, w), dt) for w, dt in zip(_PROJ_SEGS, _PROJ_DTYPES)],
        compiler_params=_params("parallel"),
        name="in_proj",
    )(h2d, g, shift, scale, w_perm)


def _permute_w_in(w_in):
    o = 0
    z = w_in[:, o:o + SSD_WIDTH]; o += SSD_WIDTH
    xbc = w_in[:, o:o + SSD_XBC]; o += SSD_XBC
    dt = w_in[:, o:o + 2 * SSD_HEADS]; o += 2 * SSD_HEADS
    rest = w_in[:, o:]
    dt = jnp.pad(dt, ((0, 0), (0, LANES - 2 * SSD_HEADS)))
    return jnp.concatenate([z, xbc, rest, dt], axis=1).astype(BF16)


def _conv_kernel(x_ref, w_ref, b_ref, o_ref):
    x = x_ref[0]
    n = x.shape[0]
    row = lax.broadcasted_iota(jnp.int32, x.shape, 0)
    prev = jnp.where(row == 0, 0.0, pltpu.roll(x, 1, 0))
    nxt = jnp.where(row == n - 1, 0.0, pltpu.roll(x, n - 1, 0))
    y = prev * w_ref[0:1, :] + x * w_ref[1:2, :] + nxt * w_ref[2:3, :] + b_ref[...]
    o_ref[0] = _silu(y)


def _conv_silu(xbc3, conv_w, conv_b):
    b, l, c = xbc3.shape
    tc = 256
    return pl.pallas_call(
        _conv_kernel,
        grid=(b, c // tc),
        in_specs=[
            pl.BlockSpec((1, l, tc), lambda i, j: (i, 0, j)),
            pl.BlockSpec((3, tc), lambda i, j: (0, j)),
            pl.BlockSpec((1, tc), lambda i, j: (0, j)),
        ],
        out_specs=pl.BlockSpec((1, l, tc), lambda i, j: (i, 0, j)),
        out_shape=jax.ShapeDtypeStruct((b, l, c), F32),
        compiler_params=_params("parallel", "parallel"),
        name="dwconv_silu",
    )(xbc3, conv_w, conv_b.reshape(1, c))


def _ssd_chunk(reverse, xbc_ref, dt_ref, dtt_ref, dtb_row, dtb_col, alog_row, alog_col, y_ref, state_scr):
    q = SSD_CHUNK
    col0 = SSD_HEADS if reverse else 0
    dt_l = _softplus(dt_ref[0] + dtb_row[...])
    a_l = dt_l * (-jnp.exp(alog_row[...]))
    dt_t = _softplus(dtt_ref[0] + dtb_col[...])
    a_t = dt_t * (-jnp.exp(alog_col[...]))
    row = lax.broadcasted_iota(jnp.int32, (q, q), 0)
    col = lax.broadcasted_iota(jnp.int32, (q, q), 1)
    lower = (row >= col).astype(F32)
    upper = (row <= col).astype(F32)
    cs_l = jnp.dot(lower, a_l, precision=HIGHEST, preferred_element_type=F32)
    cs_t = jnp.dot(a_t, upper, precision=HIGHEST, preferred_element_type=F32)
    tot_l = cs_l[q - 1:q, :]
    if reverse:
        p_l, p_t = cs_l - a_l, cs_t - a_t
        tri = row <= col
    else:
        p_l, p_t = cs_l, cs_t
        tri = row >= col

    tot_t = cs_t[:, q - 1:q]
    to_end_t = jnp.exp(p_t) if reverse else jnp.exp(tot_t - p_t)
    w_t = dt_t * to_end_t
    left = lax.broadcasted_iota(jnp.int32, (q, 2 * SSD_HEAD_DIM), 1) < SSD_HEAD_DIM

    xbc = xbc_ref[0]
    heads_per_group = SSD_HEADS // SSD_GROUPS
    for g in range(SSD_GROUPS):
        b_g = xbc[:, SSD_WIDTH + g * SSD_STATE:SSD_WIDTH + (g + 1) * SSD_STATE]
        c_g = xbc[:, SSD_WIDTH + (SSD_GROUPS + g) * SSD_STATE:SSD_WIDTH + (SSD_GROUPS + g + 1) * SSD_STATE]
        c_bf = c_g.astype(BF16)
        cb = _dot_nt(c_bf, b_g.astype(BF16))
        bt = b_g.T
        for pair in range(g * heads_per_group // 2, (g + 1) * heads_per_group // 2):
            lanes = slice(pair * 2 * SSD_HEAD_DIM, (pair + 1) * 2 * SSD_HEAD_DIM)
            xp = xbc[:, lanes]
            x_bd = jnp.concatenate([jnp.where(left, xp, 0.0), jnp.where(left, 0.0, xp)], axis=0).astype(BF16)
            decay_tiles, state_tiles, pcol_tiles = [], [], []
            for k in (col0 + 2 * pair, col0 + 2 * pair + 1):
                pcol_b = jnp.broadcast_to(p_l[:, k:k + 1], (q, q))
                prow = p_t[k:k + 1, :]
                seg = (prow - pcol_b) if reverse else (pcol_b - prow)
                lmat = jnp.exp(jnp.where(tri, seg, -jnp.inf)) * dt_t[k:k + 1, :]
                decay_tiles.append((cb * lmat).astype(BF16))
                state_tiles.append((bt * w_t[k:k + 1, :]).astype(BF16))
                pcol_tiles.append(pcol_b)
            k0 = col0 + 2 * pair
            tot_pair = jnp.where(left[0:1], tot_l[:, k0:k0 + 1], tot_l[:, k0 + 1:k0 + 2])
            pcol_pair = jnp.where(left, pcol_tiles[0], pcol_tiles[1])
            in_decay = jnp.exp(tot_pair - pcol_pair) if reverse else jnp.exp(pcol_pair)
            s_prev = state_scr[pair]
            y_diag = jnp.dot(jnp.concatenate(decay_tiles, axis=1), x_bd, preferred_element_type=F32)
            y_off = jnp.dot(c_bf, s_prev.astype(BF16), preferred_element_type=F32) * in_decay
            state_scr[pair] = jnp.exp(tot_pair) * s_prev + jnp.dot(jnp.concatenate(state_tiles, axis=1), x_bd,
                                                                   preferred_element_type=F32)
            y_ref[0, :, lanes] = y_diag + y_off


def _ssd_kernel(xf_ref, dtf_ref, dttf_ref, xb_ref, dtb_ref, dttb_ref, dtb_row, dtb_col, alog_row, alog_col,
                initf_ref, initb_ref, yf_ref, yb_ref, finf_ref, finb_ref, statef_scr, stateb_scr):
    c = pl.program_id(1)

    @pl.when(c == 0)
    def _():
        statef_scr[...] = initf_ref[0]
        stateb_scr[...] = initb_ref[0]

    params = (dtb_row, dtb_col, alog_row, alog_col)
    _ssd_chunk(False, xf_ref, dtf_ref, dttf_ref, *params, yf_ref, statef_scr)
    _ssd_chunk(True, xb_ref, dtb_ref, dttb_ref, *params, yb_ref, stateb_scr)

    @pl.when(c == pl.num_programs(1) - 1)
    def _():
        finf_ref[0] = statef_scr[...]
        finb_ref[0] = stateb_scr[...]


def _ssd_scan(xbc_act, dt3, dtt3, dtb, alog, init_f, init_b):
    b, l, _ = xbc_act.shape
    nc = l // SSD_CHUNK
    dtb_row = jnp.pad(dtb.reshape(1, -1), ((0, 0), (0, LANES - 2 * SSD_HEADS)))
    alog_row = jnp.pad(alog.reshape(1, -1), ((0, 0), (0, LANES - 2 * SSD_HEADS)))
    small = lambda shape: pl.BlockSpec(shape, lambda i, c: (0, 0))
    st_shape = (b, SSD_HEADS // 2, SSD_STATE, 2 * SSD_HEAD_DIM)
    st_spec = pl.BlockSpec((1,) + st_shape[1:], lambda i, c: (i, 0, 0, 0))
    fwd, bwd = (lambda c: c), (lambda c: nc - 1 - c)
    chunk_specs = lambda cm: [
        pl.BlockSpec((1, SSD_CHUNK, SSD_XBC), lambda i, c: (i, cm(c), 0)),
        pl.BlockSpec((1, SSD_CHUNK, LANES), lambda i, c: (i, cm(c), 0)),
        pl.BlockSpec((1, 2 * SSD_HEADS, SSD_CHUNK), lambda i, c: (i, 0, cm(c))),
    ]
    y_spec = lambda cm: pl.BlockSpec((1, SSD_CHUNK, SSD_WIDTH), lambda i, c: (i, cm(c), 0))
    y_shape = jax.ShapeDtypeStruct((b, l, SSD_WIDTH), F32)
    return pl.pallas_call(
        _ssd_kernel,
        grid=(b, nc),
        in_specs=chunk_specs(fwd) + chunk_specs(bwd)
        + [small((1, LANES)), small((2 * SSD_HEADS, 1)), small((1, LANES)), small((2 * SSD_HEADS, 1)), st_spec, st_spec],
        out_specs=[y_spec(fwd), y_spec(bwd), st_spec, st_spec],
        out_shape=[y_shape, y_shape, jax.ShapeDtypeStruct(st_shape, F32), jax.ShapeDtypeStruct(st_shape, F32)],
        scratch_shapes=[pltpu.VMEM(st_shape[1:], F32), pltpu.VMEM(st_shape[1:], F32)],
        compiler_params=_params("parallel", "arbitrary"),
        name="ssd_scan",
    )(xbc_act, dt3, dtt3, xbc_act, dt3, dtt3, dtb_row, dtb.reshape(-1, 1), alog_row, alog.reshape(-1, 1),
      init_f, init_b)


def _pool_kernel(x_ref, w_ref, sc_ref, o_ref, pad_scr):
    n = x_ref.shape[1]
    zeros = jnp.zeros((POOL_PAD, POOL_GROUP_DIM), F32)
    pad_scr[0:POOL_PAD, :] = zeros
    pad_scr[n + POOL_PAD:n + 2 * POOL_PAD, :] = zeros
    t = lax.broadcasted_iota(jnp.int32, (n, 1), 0)
    for gi, w in enumerate(POOL_WINDOWS):
        sl = slice(gi * POOL_GROUP_DIM, (gi + 1) * POOL_GROUP_DIM)
        x = x_ref[0, :, sl]
        pad_scr[POOL_PAD:n + POOL_PAD, :] = x
        acc = jnp.zeros_like(x)
        for o in range(-(w // 2), w - w // 2):
            acc = acc + pad_scr[POOL_PAD + o:POOL_PAD + o + n, :]
        lo = jnp.maximum(t - w // 2, 0)
        hi = jnp.minimum(t + (w - w // 2 - 1), n - 1)
        pooled = acc / (hi - lo + 1).astype(F32) - x
        y = jnp.dot(pooled.astype(BF16), w_ref[gi].astype(BF16), preferred_element_type=F32)
        o_ref[0, :, sl] = y * sc_ref[:, sl]


def _pool_mixer(u3, w_pool, scale):
    b, l, c = u3.shape
    return pl.pallas_call(
        _pool_kernel,
        grid=(b,),
        in_specs=[
            pl.BlockSpec((1, l, c), lambda i: (i, 0, 0)),
            pl.BlockSpec(w_pool.shape, lambda i: (0, 0, 0)),
            pl.BlockSpec((1, c), lambda i: (0, 0)),
        ],
        out_specs=pl.BlockSpec((1, l, c), lambda i: (i, 0, 0)),
        out_shape=jax.ShapeDtypeStruct((b, l, c), F32),
        scratch_shapes=[pltpu.VMEM((l + 2 * POOL_PAD, POOL_GROUP_DIM), F32)],
        compiler_params=_params("parallel"),
        name="pool_mixer",
    )(u3, w_pool, scale.reshape(1, c))


def _na_bias_kernel(rpb_ref, o_ref):
    h = pl.program_id(0)
    qi = lax.broadcasted_iota(jnp.int32, (GRID_W, LANES), 0)
    lane = lax.broadcasted_iota(jnp.int32, (GRID_W, LANES), 1)
    ki = lane % GRID_W
    second = lane >= GRID_W
    start = jnp.clip(qi - NA_KW // 2, 0, GRID_W - NA_KW)
    in_window = (ki >= start) & (ki < start + NA_KW)
    dc = jnp.clip(ki - qi, -(NA_KW - 1), NA_KW - 1) + NA_KW - 1
    for dr in range(o_ref.shape[1]):
        val = jnp.zeros((GRID_W, LANES), F32)
        for j in range(2 * NA_KW - 1):
            pick = jnp.where(second, rpb_ref[h, dr + 1, j], rpb_ref[h, dr, j])
            val = jnp.where(dc == j, pick, val)
        o_ref[0, dr] = jnp.where(in_window, val, -jnp.inf)


def _na_bias(rpb):
    nh, ndr, ndc = rpb.shape
    return pl.pallas_call(
        _na_bias_kernel,
        grid=(nh,),
        in_specs=[pl.BlockSpec(memory_space=pltpu.SMEM)],
        out_specs=pl.BlockSpec((1, ndr - 1, GRID_W, LANES), lambda h: (h, 0, 0, 0)),
        out_shape=jax.ShapeDtypeStruct((nh, ndr - 1, GRID_W, LANES), F32),
        compiler_params=_params("parallel"),
        name="na_bias",
    )(rpb)


def _na_kernel(kh, q_ref, k_ref, v_ref, kc_ref, vc_ref, bias_ref, o_ref):
    r = pl.program_id(1)
    rows = pl.num_programs(1)
    r0 = jnp.clip(r - kh // 2, 0, rows - kh)
    start = pl.multiple_of(r0 * GRID_W, GRID_W)
    kblk = k_ref[0, pl.ds(start, kh * GRID_W), :]
    vblk = v_ref[0, pl.ds(start, kh * GRID_W), :]
    dr0 = r0 - r + NA_KH - 1
    scale = NA_HEAD_DIM ** -0.5
    pair_w = 2 * NA_HEAD_DIM
    halves = []
    for pair in range(NA_HEADS // 2):
        lanes = slice(pair * pair_w, (pair + 1) * pair_w)
        left_q = lax.broadcasted_iota(jnp.int32, (GRID_W, pair_w), 1) < NA_HEAD_DIM
        q_pair = q_ref[0, :, lanes] * scale
        k_pair, kc_pair = kblk[:, lanes], kc_ref[0, :, lanes]
        for side in range(2):
            keep = left_q if side == 0 else jnp.logical_not(left_q)
            q_h = jnp.where(keep, q_pair, 0.0)
            h = 2 * pair + side
            bias = jnp.concatenate([bias_ref[h, dr0 + 2 * j] for j in range(kh // 2)], axis=1)
            halves.append((_dot_nt(q_h, k_pair) + bias, _dot_nt(q_h, kc_pair)))
    probs = []
    for s_loc, s_ctx in halves:
        m = jnp.maximum(jnp.max(s_loc, axis=-1, keepdims=True), jnp.max(s_ctx, axis=-1, keepdims=True))
        p_loc = jnp.exp(s_loc - m)
        p_ctx = jnp.exp(s_ctx - m)
        inv = 1.0 / (jnp.sum(p_loc, axis=-1, keepdims=True) + jnp.sum(p_ctx, axis=-1, keepdims=True))
        probs.append((p_loc.astype(BF16), p_ctx.astype(BF16), inv))
    for pair in range(NA_HEADS // 2):
        lanes = slice(pair * pair_w, (pair + 1) * pair_w)
        v_pair, vc_pair = vblk[:, lanes], vc_ref[0, :, lanes]
        left_v = lax.broadcasted_iota(jnp.int32, v_pair.shape, 1) < NA_HEAD_DIM
        left_c = lax.broadcasted_iota(jnp.int32, vc_pair.shape, 1) < NA_HEAD_DIM
        out = None
        for side in range(2):
            p_loc, p_ctx, inv = probs[2 * pair + side]
            keep_v = left_v if side == 0 else jnp.logical_not(left_v)
            keep_c = left_c if side == 0 else jnp.logical_not(left_c)
            acc = jnp.dot(p_loc, jnp.where(keep_v, v_pair, 0.0), preferred_element_type=F32)
            acc = acc + jnp.dot(p_ctx, jnp.where(keep_c, vc_pair, 0.0), preferred_element_type=F32)
            out = acc * inv if out is None else out + acc * inv
        o_ref[0, :, lanes] = out


def _neighbourhood_attention(q3, k3, v3, kc3, vc3, bias):
    b, s, c = q3.shape
    rows = s // GRID_W
    kh = min(NA_KH, rows)
    lc = kc3.shape[1]
    full = lambda n: pl.BlockSpec((1, n, c), lambda i, r: (i, 0, 0))
    return pl.pallas_call(
        functools.partial(_na_kernel, kh),
        grid=(b, rows),
        in_specs=[
            pl.BlockSpec((1, GRID_W, c), lambda i, r: (i, r, 0)),
            full(s), full(s), full(lc), full(lc),
            pl.BlockSpec(bias.shape, lambda i, r: (0, 0, 0, 0)),
        ],
        out_specs=pl.BlockSpec((1, GRID_W, c), lambda i, r: (i, r, 0)),
        out_shape=jax.ShapeDtypeStruct((b, s, c), F32),
        compiler_params=_params("parallel", "arbitrary"),
        name="na_attention",
    )(q3, k3, v3, kc3, vc3, bias)


def _ctx_attn_kernel(q_ref, k_ref, v_ref, o_ref):
    scale = NA_HEAD_DIM ** -0.5
    for h in range(NA_HEADS):
        sl = slice(h * NA_HEAD_DIM, (h + 1) * NA_HEAD_DIM)
        s = _dot_nt(q_ref[0, :, sl] * scale, k_ref[0, :, sl])
        p = jnp.exp(s - jnp.max(s, axis=-1, keepdims=True))
        den = jnp.sum(p, axis=-1, keepdims=True)
        o_ref[0, :, sl] = jnp.dot(p.astype(BF16), v_ref[0, :, sl], preferred_element_type=F32) / den


def _context_attention(q3, k3, v3):
    b, l, c = q3.shape
    spec = pl.BlockSpec((1, l, c), lambda i: (i, 0, 0))
    return pl.pallas_call(
        _ctx_attn_kernel,
        grid=(b,),
        in_specs=[spec, spec, spec],
        out_specs=spec,
        out_shape=jax.ShapeDtypeStruct((b, l, c), F32),
        compiler_params=_params("parallel"),
        name="ctx_attention",
    )(q3, k3, v3)


def _mix_out_kernel(yf_ref, yb_ref, xs_ref, z_ref, pool_ref, na_ref, h_ref, dsk_ref, ng_ref, g1_ref, w_ref, o_ref):
    y = yf_ref[...] + yb_ref[...] + dsk_ref[...] * xs_ref[...]
    y = y * _silu(z_ref[...])
    ms = jnp.mean(y * y, axis=-1, keepdims=True)
    yn = (y * lax.rsqrt(ms + RMS_EPS)) * ng_ref[...]
    mix = jnp.dot(yn.astype(BF16), w_ref[0:SSD_WIDTH, :], preferred_element_type=F32)
    mix = mix + jnp.dot(pool_ref[...].astype(BF16), w_ref[SSD_WIDTH:SSD_WIDTH + POOL_WIDTH, :],
                        preferred_element_type=F32)
    mix = mix + jnp.dot(na_ref[...].astype(BF16), w_ref[SSD_WIDTH + POOL_WIDTH:, :], preferred_element_type=F32)
    o_ref[...] = h_ref[...] + g1_ref[0] * mix


def _mix_out(yf, yb, xbc_act2d, z, pool_y, na_y, h2d, seq_len, d_skip, norm_g, g1, w_out_bf):
    n, d = h2d.shape
    t = 256
    row = lambda w: pl.BlockSpec((t, w), lambda i: (i, 0))
    vec = lambda w: pl.BlockSpec((1, w), lambda i: (0, 0))
    return pl.pallas_call(
        _mix_out_kernel,
        grid=(n // t,),
        in_specs=[
            row(SSD_WIDTH), row(SSD_WIDTH), row(SSD_WIDTH), row(SSD_WIDTH), row(POOL_WIDTH), row(NA_WIDTH), row(d),
            vec(SSD_WIDTH), vec(SSD_WIDTH),
            pl.BlockSpec((1, 1, d), _batch_map(t, seq_len, g1.shape[0])),
            pl.BlockSpec(w_out_bf.shape, lambda i: (0, 0)),
        ],
        out_specs=row(d),
        out_shape=jax.ShapeDtypeStruct((n, d), F32),
        compiler_params=_params("parallel"),
        name="mix_out",
    )(yf, yb, xbc_act2d, z, pool_y, na_y, h2d, jnp.repeat(d_skip, SSD_HEAD_DIM).reshape(1, -1),
      norm_g.reshape(1, -1), g1, w_out_bf)


_CAND_ROWS = 16 + 8 * 7 + 8


def _batcher_pairs(n):
    pairs = []

    def merge(lo, m, r):
        step = 2 * r
        if step < m:
            merge(lo, m, step)
            merge(lo + r, m, step)
            pairs.extend((i, i + r) for i in range(lo + r, lo + m - r, step))
        else:
            pairs.append((lo, lo + r))

    def sort(lo, m):
        if m > 1:
            sort(lo, m // 2)
            sort(lo + m // 2, m // 2)
            merge(lo, m, 1)

    sort(0, n)
    return tuple(pairs)


_SORT16 = _batcher_pairs(PEER_TOPK)
_BITONIC16 = tuple((i, i + d) for d in (8, 4, 2, 1) for i in range(PEER_TOPK) if not i & d)


def _exchange(x, pairs):
    for i, j in pairs:
        x[i], x[j] = jnp.maximum(x[i], x[j]), jnp.minimum(x[i], x[j])


def _sorted_top16(s):
    tiles = []
    for l0 in range(0, s.shape[1], LANES):
        x = [s[8 * v:8 * v + 8, l0:l0 + LANES] for v in range(PEER_TOPK)]
        _exchange(x, _SORT16)
        for shift in (4, 2, 1):
            x = [jnp.maximum(x[i], pltpu.roll(x[PEER_TOPK - 1 - i], shift, 0)) for i in range(PEER_TOPK)]
            _exchange(x, _BITONIC16)
        tiles.append(jnp.concatenate([xi[0:1] for xi in x], axis=0))
    return jnp.concatenate(tiles, axis=1)


def _count_above(s, v):
    r = lambda j: v[j:j + 1]
    pick = jnp.where
    a = s < r(7)
    b = s < pick(a, r(11), r(3))
    c = s < pick(a, pick(b, r(13), r(9)), pick(b, r(5), r(1)))
    d = s < pick(a, pick(b, pick(c, r(14), r(12)), pick(c, r(10), r(8))),
                 pick(b, pick(c, r(6), r(4)), pick(c, r(2), r(0))))
    return (pick(a, 8.0, 0.0) + pick(b, 4.0, 0.0) + pick(c, 2.0, 0.0) + pick(d, 1.0, 0.0)
            + pick(s < r(15), 1.0, 0.0))


def _top16_pair_fast(s1, s2):
    v1, v2 = _sorted_top16(s1), _sorted_top16(s2)
    rank2 = _count_above(s2, v2)

    def check(s, v):
        n = jnp.sum(jnp.where(s >= v[PEER_TOPK - 1:PEER_TOPK], 1.0, 0.0), axis=0, keepdims=True)
        strict = jnp.min(v[:-1] - v[1:], axis=0, keepdims=True) > 0.0
        return jnp.where(strict, n, 0.0)

    return v1, check(s1, v1), v2, rank2, check(s2, v2)


def _first_counts_fast(s1, v1, sel):
    inf = jnp.inf
    rows_v1 = jnp.concatenate([jnp.broadcast_to(v1[0:1], (PEER_TOPK, v1.shape[1]))]
                              + [jnp.broadcast_to(v1[j:j + 1], (8, v1.shape[1])) for j in range(1, 8)]
                              + [v1[8:16]], axis=0)
    bound = jnp.where(sel > 0.0, rows_v1, inf)
    low = bound[0:8]
    for j in range(1, 8):
        low = jnp.minimum(low, bound[16 + 8 * (j - 1):16 + 8 * j])
    tail = jnp.min(bound[64 + 8:64 + 16], axis=0, keepdims=True)
    t = [jnp.minimum(low[0:1], tail)] + [low[k:k + 1] for k in range(1, 8)]
    pick = jnp.where
    a = s1 >= t[3]
    b = s1 >= pick(a, t[5], t[1])
    c = s1 >= pick(a, pick(b, t[6], t[4]), pick(b, t[2], t[0]))
    cnt = pick(a, 4.0, 0.0) + pick(b, 2.0, 0.0) + pick(c, 1.0, 0.0) + pick(s1 >= t[7], 1.0, 0.0)
    n_high = jnp.sum(sel[8:16], axis=0, keepdims=True)
    return cnt + jnp.where(s1 >= v1[0:1], n_high, 0.0)


def _first_counts_exact(rank1, sel):
    cnt = jnp.zeros(rank1.shape, F32)
    for j in range(8):
        lo = 0 if j == 0 else 16 + 8 * (j - 1)
        n_j = jnp.sum(sel[lo:lo + (16 if j == 0 else 8)], axis=0, keepdims=True)
        cnt = cnt + jnp.where(rank1 == float(j), n_j, 0.0)
    for j in range(8, 16):
        cnt = cnt + jnp.where(rank1 == float(j), sel[64 + j:65 + j], 0.0)
    return cnt


def _top16_pair_exact(s1, s2):
    n, t = s1.shape
    iota = lax.broadcasted_iota(jnp.int32, (n, t), 0).astype(F32)
    row16 = lax.broadcasted_iota(jnp.int32, (PEER_TOPK, t), 0)

    def pick(work, rank, vals, j):
        m = jnp.max(work, axis=0, keepdims=True)
        idx = jnp.min(jnp.where(work == m, iota, float(n)), axis=0, keepdims=True)
        sel = iota == idx
        return (jnp.where(sel, -jnp.inf, work), jnp.where(sel, lax.convert_element_type(j, F32), rank),
                jnp.where(row16 == j, m, vals))

    def body(j, carry):
        a, b = carry
        return pick(*a, j), pick(*b, j)

    start = lambda s: (s, jnp.full((n, t), float(PEER_TOPK), F32), jnp.zeros((PEER_TOPK, t), F32))
    (_, rank1, v1), (_, rank2, v2) = lax.fori_loop(0, PEER_TOPK, body, (start(s1), start(s2)))
    return rank1, v1, rank2, v2


def _select16_fast(cand):
    def body(_, prev):
        return jnp.max(jnp.where(cand < prev, cand, -jnp.inf), axis=0, keepdims=True)

    m = lax.fori_loop(0, PEER_TOPK, body, jnp.full((1, cand.shape[1]), jnp.inf, F32))
    taken = jnp.where(cand >= m, 1.0, 0.0)
    return taken, jnp.sum(taken, axis=0, keepdims=True)


def _select16_exact(cand):
    iota = lax.broadcasted_iota(jnp.int32, cand.shape, 0).astype(F32)

    def body(_, carry):
        work, sel_acc = carry
        m = jnp.max(work, axis=0, keepdims=True)
        idx = jnp.min(jnp.where(work == m, iota, float(_CAND_ROWS)), axis=0, keepdims=True)
        sel = iota == idx
        return jnp.where(sel, -jnp.inf, work), jnp.where(sel, 1.0, sel_acc)

    return lax.fori_loop(0, PEER_TOPK, body, (cand, jnp.zeros(cand.shape, F32)))[1]


def _candidate_sums(v1, v2):
    blocks = [v1[0:1] + v2]
    for j in range(1, 8):
        blocks.append(v1[j:j + 1] + v2[0:8])
    blocks.append(v1[8:16] + v2[0:1])
    return jnp.concatenate(blocks, axis=0)


def _any_not_16(*counts):
    return jnp.max(sum(jnp.abs(c - float(PEER_TOPK)) for c in counts)) > 0.0


def _bf16_bits(x):
    return pltpu.bitcast(x.astype(BF16).astype(F32), jnp.uint32)


def _pack_row_pairs(x, scr):
    n, t = x.shape
    for j in range(t // LANES):
        scr[j] = x[:, j * LANES:(j + 1) * LANES]
    words = []
    for j in range(t // LANES):
        even = scr[j, pl.ds(0, n // 2, stride=2), :]
        odd = scr[j, pl.ds(1, n // 2, stride=2), :]
        words.append((_bf16_bits(even) >> 16) | _bf16_bits(odd))
    return jnp.concatenate(words, axis=1)


def _pack_same(x):
    w = _bf16_bits(x)
    return w | (w >> 16)


_F8_TARGET_EXP = 6


def _pow2_scale(amax):
    bits = pltpu.bitcast(jnp.maximum(amax, 2.0 ** -100), jnp.int32)
    exponent = (bits >> 23) - 127
    return pltpu.bitcast((_F8_TARGET_EXP - exponent + 127) << 23, F32)


def _peer_score_kernel(h_ref, g_ref, sh_ref, sc_ref, wq_ref, keys_ref,
                       x_out, xinv_out, cnt_out, e1_out, rk_out, e2_out,
                       u_scr, pair_scr, cnt_scr, rank2_scr, top_scr):
    hd = pl.program_id(1)

    @pl.when(hd == 0)
    def _():
        u = _rms_mod(h_ref[...], g_ref[...], sc_ref[0], sh_ref[0])
        u_scr[...] = u.astype(BF16)
        scale = _pow2_scale(jnp.max(jnp.abs(u), axis=-1, keepdims=True))
        x_out[...] = (u * scale).astype(F8)
        xinv_out[...] = 1.0 / scale

    q = jnp.dot(u_scr[...], wq_ref[0], preferred_element_type=F32).astype(BF16)
    s1 = _dot_nt(keys_ref[0, 0], q[:, :PEER_SUB])
    s2 = _dot_nt(keys_ref[0, 1], q[:, PEER_SUB:])
    v1, n1, v2, rank2, n2 = _top16_pair_fast(s1, s2)
    cand = _candidate_sums(v1, v2)
    sel, n_sel = _select16_fast(cand)
    cnt_scr[...] = _first_counts_fast(s1, v1, sel)
    rank2_scr[...] = rank2
    top_scr[0:1] = v1[0:1]
    top_scr[1:2] = v2[0:1]
    top_scr[2:3] = jnp.sum(sel * jnp.exp(cand - cand[0:1]), axis=0, keepdims=True)

    @pl.when(_any_not_16(n1, n2, n_sel))
    def _():
        rank1_x, v1_x, rank2_x, v2_x = _top16_pair_exact(s1, s2)
        cand_x = _candidate_sums(v1_x, v2_x)
        sel_x = _select16_exact(cand_x)
        cnt_scr[...] = _first_counts_exact(rank1_x, sel_x)
        rank2_scr[...] = rank2_x
        top_scr[0:1] = v1_x[0:1]
        top_scr[1:2] = v2_x[0:1]
        top_scr[2:3] = jnp.sum(sel_x * jnp.exp(cand_x - cand_x[0:1]), axis=0, keepdims=True)

    cnt_out[0] = _pack_same(cnt_scr[...])
    e1_out[0] = _pack_same(jnp.exp(s1 - top_scr[0:1]))
    rk_out[0] = _pack_row_pairs(rank2_scr[...], pair_scr)
    e2_out[0] = _pack_row_pairs(jnp.exp(s2 - top_scr[1:2]) / top_scr[2:3], pair_scr)


def _peer_scores(h2d, seq_len, g, shift, scale, wq_heads, keys_bf):
    n, d = h2d.shape
    nb = shift.shape[0]
    t = 512
    assert n % t == 0 and (nb == 1 or seq_len % t == 0)
    bm = _batch_map(t, seq_len, nb)
    mod_spec = pl.BlockSpec((1, 1, d), lambda i, hd: bm(i))
    first_out = pl.BlockSpec((1, PEER_NKEYS, t), lambda i, hd: (hd, 0, i))
    first_shape = jax.ShapeDtypeStruct((PEER_HEADS, PEER_NKEYS, n), jnp.uint32)
    second_out = pl.BlockSpec((1, PEER_NKEYS // 2, t), lambda i, hd: (hd, 0, i))
    second_shape = jax.ShapeDtypeStruct((PEER_HEADS, PEER_NKEYS // 2, n), jnp.uint32)
    return pl.pallas_call(
        _peer_score_kernel,
        grid=(n // t, PEER_HEADS),
        in_specs=[
            pl.BlockSpec((t, d), lambda i, hd: (i, 0)),
            pl.BlockSpec((1, d), lambda i, hd: (0, 0)),
            mod_spec, mod_spec,
            pl.BlockSpec((1, d, 2 * PEER_SUB), lambda i, hd: (hd, 0, 0)),
            pl.BlockSpec((1, 2, PEER_NKEYS, PEER_SUB), lambda i, hd: (hd, 0, 0, 0)),
        ],
        out_specs=[pl.BlockSpec((t, d), lambda i, hd: (i, 0)), pl.BlockSpec((t, 1), lambda i, hd: (i, 0)),
                   first_out, first_out, second_out, second_out],
        out_shape=[jax.ShapeDtypeStruct((n, d), F8), jax.ShapeDtypeStruct((n, 1), F32),
                   first_shape, first_shape, second_shape, second_shape],
        scratch_shapes=[pltpu.VMEM((t, d), BF16), pltpu.VMEM((t // LANES, PEER_NKEYS, LANES), F32),
                        pltpu.VMEM((PEER_NKEYS, t), F32), pltpu.VMEM((PEER_NKEYS, t), F32),
                        pltpu.VMEM((8, t), F32)],
        compiler_params=_params("parallel", "arbitrary"),
        name="peer_scores",
    )(h2d, g, shift, scale, wq_heads, keys_bf)


_PEER_EC = 1024
_INV_SQRT2 = 1.0 / math.sqrt(2.0)


def _as_bf16_rows(words):
    return pltpu.bitcast(words, BF16)


_PEER_A_GROUP = 4


def _peer_expert_kernel(n_chunks, x_ref, u_ref, inv_ref, vt_ref, cnt_ref, e1_ref, rk_ref, e2_ref, h_ref, g2_ref, o_ref,
                        ht0_scr, ht1_scr, g0_scr, g1_scr, acc_scr):
    s = pl.program_id(0)
    t = h_ref.shape[0]
    p2 = s - 2
    c2 = lax.rem(jnp.maximum(p2, 0), n_chunks)
    pack = 2 * 8
    n_a = _PEER_EC // PEER_NKEYS

    @pl.when(s == 0)
    def _():
        ht1_scr[...] = jnp.zeros_like(ht1_scr)
        g0_scr[...] = jnp.zeros_like(g0_scr)
        g1_scr[...] = jnp.zeros_like(g1_scr)

    @pl.when((p2 <= 0) | (c2 == 0))
    def _():
        acc_scr[...] = jnp.zeros_like(acc_scr)

    def step(ht_new, ht_old, g_new, g_old):
        for tc in range(t // LANES):
            ls = slice(tc * LANES, (tc + 1) * LANES)
            half_inv = 0.5 * inv_ref[:, ls]
            c_inv = _INV_SQRT2 * inv_ref[:, ls]
            for a0 in range(0, n_a, _PEER_A_GROUP):
                w = [[None] * (PEER_NKEYS // pack) for _ in range(_PEER_A_GROUP)]
                for hd in range(PEER_HEADS):
                    rows1 = [(_as_bf16_rows(jnp.broadcast_to(cnt_ref[hd, a0 + i:a0 + i + 1, ls], (8, LANES))),
                              _as_bf16_rows(jnp.broadcast_to(e1_ref[hd, a0 + i:a0 + i + 1, ls], (8, LANES))))
                             for i in range(_PEER_A_GROUP)]
                    for r in range(PEER_NKEYS // pack):
                        rk = _as_bf16_rows(rk_ref[hd, r * 8:(r + 1) * 8, ls])
                        e2 = _as_bf16_rows(e2_ref[hd, r * 8:(r + 1) * 8, ls])
                        for i, (cn, e1) in enumerate(rows1):
                            term = jnp.where(rk < cn, e2, 0.0) * e1
                            w[i][r] = term if w[i][r] is None else w[i][r] + term
                for i in range(_PEER_A_GROUP):
                    for r in range(PEER_NKEYS // pack):
                        row0 = (a0 + i) * PEER_NKEYS + r * pack
                        hs = ht_old[row0:row0 + pack, ls]
                        act = (hs * half_inv) * (1.0 + lax.erf(hs * c_inv))
                        g_new[row0 // 2:(row0 + pack) // 2, ls] = pltpu.bitcast(w[i][r] * act.astype(BF16), jnp.uint32)
        ht_new[...] = _dot_nt(u_ref[...], x_ref[...])
        acc_scr[...] += jnp.dot(_as_bf16_rows(vt_ref[...]), _as_bf16_rows(g_old[...]),
                                preferred_element_type=F32)

    @pl.when(s % 2 == 0)
    def _():
        step(ht0_scr, ht1_scr, g1_scr, g0_scr)

    @pl.when(s % 2 == 1)
    def _():
        step(ht1_scr, ht0_scr, g0_scr, g1_scr)

    @pl.when((p2 >= 0) & (c2 == n_chunks - 1))
    def _():
        o_ref[...] = h_ref[...] + g2_ref[0] * acc_scr[...].T


def _peer_experts(x8, u8, inv_row, vt_pk, cnt, e1, rk, e2, h2d, seq_len, g2):
    n, d = h2d.shape
    t = 512
    n_chunks = u8.shape[0] // _PEER_EC
    total = (n // t) * n_chunks
    ea = _PEER_EC // PEER_NKEYS

    def pair(p):
        p = jnp.clip(p, 0, total - 1)
        return p // n_chunks, lax.rem(p, n_chunks)

    blk = lambda lag: (lambda s: pair(s - lag)[0])
    chk = lambda lag: (lambda s: pair(s - lag)[1])
    tok = pl.BlockSpec((PEER_HEADS, PEER_NKEYS // 2, t), lambda s: (0, 0, blk(1)(s)))
    first = pl.BlockSpec((PEER_HEADS, ea, t), lambda s: (0, chk(1)(s), blk(1)(s)))
    bm = _batch_map(t, seq_len, g2.shape[0])
    return pl.pallas_call(
        functools.partial(_peer_expert_kernel, n_chunks),
        grid=(total + 2,),
        in_specs=[
            pl.BlockSpec((t, d), lambda s: (blk(0)(s), 0)),
            pl.BlockSpec((_PEER_EC, d), lambda s: (chk(0)(s), 0)),
            pl.BlockSpec((1, t), lambda s: (0, blk(1)(s))),
            pl.BlockSpec((d // 2, _PEER_EC), lambda s: (0, chk(2)(s))),
            first, first, tok, tok,
            pl.BlockSpec((t, d), lambda s: (blk(2)(s), 0)),
            pl.BlockSpec((1, 1, d), lambda s: bm(blk(2)(s))),
        ],
        out_specs=pl.BlockSpec((t, d), lambda s: (blk(2)(s), 0)),
        out_shape=jax.ShapeDtypeStruct((n, d), F32),
        scratch_shapes=[pltpu.VMEM((_PEER_EC, t), F32), pltpu.VMEM((_PEER_EC, t), F32),
                        pltpu.VMEM((_PEER_EC // 2, t), jnp.uint32), pltpu.VMEM((_PEER_EC // 2, t), jnp.uint32),
                        pltpu.VMEM((d, t), F32)],
        compiler_params=_params("arbitrary"),
        name="peer_experts",
    )(x8, u8, inv_row, vt_pk, cnt, e1, rk, e2, h2d, g2)


def _pack_rows_kernel(transpose, x_ref, o_ref):
    x = x_ref[0].T if transpose else x_ref[0]
    o_ref[...] = pltpu.bitcast(x.astype(BF16), jnp.uint32)


def _pack_bf16_rows(stack, layer, transpose=False):
    tile = 512
    if transpose:
        _, c, r = stack.shape
        in_spec = pl.BlockSpec((1, tile, r), lambda i: (layer, i, 0))
        out_spec = pl.BlockSpec((r // 2, tile), lambda i: (0, i))
        steps = c // tile
    else:
        _, r, c = stack.shape
        in_spec = pl.BlockSpec((1, tile, c), lambda i: (layer, i, 0))
        out_spec = pl.BlockSpec((tile // 2, c), lambda i: (i, 0))
        steps = r // tile
    return pl.pallas_call(
        functools.partial(_pack_rows_kernel, transpose),
        grid=(steps,),
        in_specs=[in_spec],
        out_specs=out_spec,
        out_shape=jax.ShapeDtypeStruct((r // 2, c), jnp.uint32),
        compiler_params=_params("parallel"),
        name="pack_bf16_rows_t" if transpose else "pack_bf16_rows",
    )(stack)


def _table_absmax_kernel(x_ref, o_ref):
    @pl.when(pl.program_id(0) == 0)
    def _():
        o_ref[...] = jnp.zeros_like(o_ref)

    m = jnp.max(jnp.max(jnp.abs(x_ref[0]), axis=0, keepdims=True), axis=1, keepdims=True)
    o_ref[...] = jnp.maximum(o_ref[...], m)


def _quantize_kernel(s_ref, x_ref, o_ref):
    o_ref[...] = (x_ref[0] * s_ref[0]).astype(F8)


def _quantize_table(stack, layer):
    _, r, c = stack.shape
    tile = 512
    in_spec = pl.BlockSpec((1, tile, c), lambda i: (layer, i, 0))
    amax = pl.pallas_call(
        _table_absmax_kernel,
        grid=(r // tile,),
        in_specs=[in_spec],
        out_specs=pl.BlockSpec((8, LANES), lambda i: (0, 0)),
        out_shape=jax.ShapeDtypeStruct((8, LANES), F32),
        compiler_params=_params("arbitrary"),
        name="table_absmax",
    )(stack)[0, 0]
    exponent = jnp.floor(jnp.log2(jnp.maximum(amax, 2.0 ** -100)))
    scale = jnp.exp2(_F8_TARGET_EXP - exponent)
    q = pl.pallas_call(
        _quantize_kernel,
        grid=(r // tile,),
        in_specs=[pl.BlockSpec(memory_space=pltpu.SMEM), in_spec],
        out_specs=pl.BlockSpec((tile, c), lambda i: (i, 0)),
        out_shape=jax.ShapeDtypeStruct((r, c), F8),
        compiler_params=_params("parallel"),
        name="quantize_table",
    )(scale.reshape(1), stack)
    return q, 1.0 / scale


def _peer_ffn_residual(h2d, seq_len, norm_g, shift, scale, gate, wq_heads, keys_bf, u8, u_inv, vt_pk):
    x8, x_inv, cnt, e1, rk, e2 = _peer_scores(h2d, seq_len, norm_g, shift, scale, wq_heads, keys_bf)
    inv_row = x_inv.reshape(1, -1) * u_inv
    return _peer_experts(x8, u8, inv_row, vt_pk, cnt, e1, rk, e2, h2d, seq_len, gate)


def _final_norm_kernel(h_ref, g_ref, o_ref):
    x = h_ref[...]
    ms = jnp.mean(x * x, axis=-1, keepdims=True)
    o_ref[...] = (x * lax.rsqrt(ms + RMS_EPS)) * g_ref[...]


def _final_norm(h2d, g):
    n, d = h2d.shape
    t = 512
    return pl.pallas_call(
        _final_norm_kernel,
        grid=(n // t,),
        in_specs=[pl.BlockSpec((t, d), lambda i: (i, 0)), pl.BlockSpec((1, d), lambda i: (0, 0))],
        out_specs=pl.BlockSpec((t, d), lambda i: (i, 0)),
        out_shape=jax.ShapeDtypeStruct((n, d), F32),
        compiler_params=_params("parallel"),
        name="final_norm",
    )(h2d, g.reshape(1, d))


def _mixer_inputs(h2d, batch, seq_len, norm_g, shift, scale, w_perm, conv_w, conv_b):
    z, xbc, pool_u, q, k, v, dt = _project(h2d, seq_len, norm_g, shift, scale, w_perm)
    xbc_act = _conv_silu(xbc.reshape(batch, seq_len, SSD_XBC), conv_w, conv_b)
    dt3 = dt.reshape(batch, seq_len, LANES)
    dtt3 = jnp.swapaxes(dt3[:, :, :2 * SSD_HEADS], 1, 2)
    r3 = lambda a: a.reshape(batch, seq_len, a.shape[-1])
    return z, xbc_act, dt3, dtt3, r3(pool_u), r3(q), r3(k), r3(v)


def kernel(x, c, ctx, c_ctx, ada_w, ada_b, norm1_g, w_in, conv_w, conv_b, a_log, dt_bias, d_skip, ssd_norm_g, pool_w, pool_scale, na_rpb, w_out, norm2_g, peer_wq, peer_keys, peer_u, peer_v, final_g):
    batch, seq, d = x.shape
    ctx_len = ctx.shape[1]
    n, nc = batch * seq, batch * ctx_len
    h = x.reshape(n, d)
    hc = ctx.reshape(nc, d)

    c8 = jnp.concatenate([c, c_ctx[None], jnp.zeros((8 - batch - 1, d), F32)], axis=0)
    mod = _modulation(c8, ada_w, ada_b)

    for i in range(DEPTH):
        need_ctx_out = i < DEPTH - 1
        lat = [mod[i, :batch, j * d:(j + 1) * d].reshape(batch, 1, d) for j in range(6)]
        cx = [mod[i, batch:batch + 1, j * d:(j + 1) * d].reshape(1, 1, d) for j in range(6)]
        sh1, sc1, g1, sh2, sc2, g2 = lat
        csh1, csc1, cg1, csh2, csc2, cg2 = cx

        w_perm = _permute_w_in(w_in[i])
        w_out_bf = w_out[i].astype(BF16)
        n1 = norm1_g[i].reshape(1, d)
        n2 = norm2_g[i].reshape(1, d)
        zero_state = jnp.zeros((batch, SSD_HEADS // 2, SSD_STATE, 2 * SSD_HEAD_DIM), F32)
        scan = functools.partial(_ssd_scan, dtb=dt_bias[i], alog=a_log[i])
        wq_heads = peer_wq[i].reshape(d, PEER_HEADS, 2 * PEER_SUB).transpose(1, 0, 2).astype(BF16)
        keys_bf = peer_keys[i].astype(BF16)
        u8, u_inv = _quantize_table(peer_u, i)
        vt_pk = _pack_bf16_rows(peer_v, i, transpose=True)

        zc, xbc_c, dt3_c, dtt3_c, pool_c, qc, kc, vc = _mixer_inputs(
            hc, batch, ctx_len, n1, csh1, csc1, w_perm, conv_w[i], conv_b[i])
        yf_c, yb_c, st_f, st_b = scan(xbc_c, dt3_c, dtt3_c, init_f=zero_state, init_b=zero_state)
        if need_ctx_out:
            pool_yc = _pool_mixer(pool_c, pool_w[i], pool_scale[i])
            att_c = _context_attention(qc, kc, vc)
            hc = _mix_out(yf_c.reshape(nc, -1), yb_c.reshape(nc, -1), xbc_c.reshape(nc, -1), zc,
                          pool_yc.reshape(nc, -1), att_c.reshape(nc, -1), hc, ctx_len,
                          d_skip[i], ssd_norm_g[i], cg1, w_out_bf)
            hc = _peer_ffn_residual(hc, ctx_len, n2, csh2, csc2, cg2, wq_heads, keys_bf, u8, u_inv, vt_pk)

        z, xbc_l, dt3_l, dtt3_l, pool_l, q, k, v = _mixer_inputs(
            h, batch, seq, n1, sh1, sc1, w_perm, conv_w[i], conv_b[i])
        yf, yb, _, _ = scan(xbc_l, dt3_l, dtt3_l, init_f=st_f, init_b=st_b)
        pool_y = _pool_mixer(pool_l, pool_w[i], pool_scale[i])
        na = _neighbourhood_attention(q, k, v, kc, vc, _na_bias(na_rpb[i]))
        h = _mix_out(yf.reshape(n, -1), yb.reshape(n, -1), xbc_l.reshape(n, -1), z,
                     pool_y.reshape(n, -1), na.reshape(n, -1), h, seq,
                     d_skip[i], ssd_norm_g[i], g1, w_out_bf)
        h = _peer_ffn_residual(h, seq, n2, sh2, sc2, g2, wq_heads, keys_bf, u8, u_inv, vt_pk)

    return _final_norm(h, final_g).reshape(batch, seq, d)
```

```python
import functools
import math

import jax
import jax.numpy as jnp
from jax import lax
from jax.experimental import pallas as pl
from jax.experimental.pallas import tpu as pltpu

F32 = jnp.float32
BF16 = jnp.bfloat16
F8 = jnp.float8_e4m3fn
HIGHEST = lax.Precision.HIGHEST

D_MODEL = 1024
DEPTH = 2
GRID_W = 64
RMS_EPS = 1e-6

SSD_HEAD_DIM = 64
SSD_HEADS = 16
SSD_GROUPS = 2
SSD_STATE = 128
SSD_CHUNK = 128
SSD_WIDTH = SSD_HEADS * SSD_HEAD_DIM
SSD_XBC = SSD_WIDTH + 2 * SSD_GROUPS * SSD_STATE

POOL_WINDOWS = (2, 4, 8, 16)
POOL_GROUP_DIM = 128
POOL_WIDTH = POOL_GROUP_DIM * len(POOL_WINDOWS)
POOL_PAD = 8

NA_HEAD_DIM = 64
NA_HEADS = 8
NA_WIDTH = NA_HEADS * NA_HEAD_DIM
NA_KH = 8
NA_KW = 16

PEER_HEADS = 8
PEER_NKEYS = 128
PEER_TOPK = 16
PEER_SUB = 128

LANES = 128
VMEM_LIMIT_BYTES = 56 * 1024 * 1024

_PROJ_SEGS = (SSD_WIDTH, SSD_XBC, POOL_WIDTH, NA_WIDTH, NA_WIDTH, NA_WIDTH, LANES)
_PROJ_DTYPES = (F32, F32, F32, BF16, BF16, BF16, F32)


def _params(*sem):
    return pltpu.CompilerParams(dimension_semantics=sem, vmem_limit_bytes=VMEM_LIMIT_BYTES)


def _rms_mod(x, g, scale, shift):
    ms = jnp.mean(x * x, axis=-1, keepdims=True)
    return (x * lax.rsqrt(ms + RMS_EPS)) * g * (1.0 + scale) + shift


def _silu(x):
    return x * jax.nn.sigmoid(x)


def _softplus(x):
    return jnp.maximum(x, 0.0) + jnp.log1p(jnp.exp(-jnp.abs(x)))


def _dot_nt(a, b):
    return lax.dot_general(a, b, (((1,), (1,)), ((), ())), preferred_element_type=F32)


def _batch_map(block_rows, seq_len, n_rows):
    if n_rows == 1:
        return lambda i, *_: (0, 0, 0)
    return lambda i, *_: ((i * block_rows) // seq_len, 0, 0)


def _mod_kernel(c_ref, w_ref, b_ref, o_ref):
    s = _silu(c_ref[...])
    o_ref[0] = jnp.dot(s, w_ref[0], precision=HIGHEST, preferred_element_type=F32) + b_ref[0]


def _modulation(c8, ada_w, ada_b):
    depth, d, six_d = ada_w.shape
    tn = 1024
    return pl.pallas_call(
        _mod_kernel,
        grid=(depth, six_d // tn),
        in_specs=[
            pl.BlockSpec((8, d), lambda l, j: (0, 0)),
            pl.BlockSpec((1, d, tn), lambda l, j: (l, 0, j)),
            pl.BlockSpec((1, 1, tn), lambda l, j: (l, 0, j)),
        ],
        out_specs=pl.BlockSpec((1, 8, tn), lambda l, j: (l, 0, j)),
        out_shape=jax.ShapeDtypeStruct((depth, 8, six_d), F32),
        compiler_params=_params("parallel", "parallel"),
        name="adaln_mod",
    )(c8, ada_w, ada_b.reshape(depth, 1, six_d))


def _proj_kernel(h_ref, g_ref, sh_ref, sc_ref, w_ref, *out_refs):
    u = _rms_mod(h_ref[...], g_ref[...], sc_ref[0], sh_ref[0]).astype(BF16)
    off = 0
    for o_ref, width in zip(out_refs, _PROJ_SEGS):
        o_ref[...] = jnp.dot(u, w_ref[:, off:off + width], preferred_element_type=F32).astype(o_ref.dtype)
        off += width


def _project(h2d, seq_len, g, shift, scale, w_perm):
    n, d = h2d.shape
    t = 256
    total = sum(_PROJ_SEGS)
    nb = shift.shape[0]
    row_map = lambda i: (i, 0)
    return pl.pallas_call(
        _proj_kernel,
        grid=(n // t,),
        in_specs=[
            pl.BlockSpec((t, d), row_map),
            pl.BlockSpec((1, d), lambda i: (0, 0)),
            pl.BlockSpec((1, 1, d), _batch_map(t, seq_len, nb)),
            pl.BlockSpec((1, 1, d), _batch_map(t, seq_len, nb)),
            pl.BlockSpec((d, total), lambda i: (0, 0)),
        ],
        out_specs=[pl.BlockSpec((t, w), row_map) for w in _PROJ_SEGS],
        out_shape=[jax.ShapeDtypeStruct((n, w), dt) for w, dt in zip(_PROJ_SEGS, _PROJ_DTYPES)],
        compiler_params=_params("parallel"),
        name="in_proj",
    )(h2d, g, shift, scale, w_perm)


def _permute_w_in(w_in):
    o = 0
    z = w_in[:, o:o + SSD_WIDTH]; o += SSD_WIDTH
    xbc = w_in[:, o:o + SSD_XBC]; o += SSD_XBC
    dt = w_in[:, o:o + 2 * SSD_HEADS]; o += 2 * SSD_HEADS
    rest = w_in[:, o:]
    dt = jnp.pad(dt, ((0, 0), (0, LANES - 2 * SSD_HEADS)))
    return jnp.concatenate([z, xbc, rest, dt], axis=1).astype(BF16)


def _conv_kernel(x_ref, w_ref, b_ref, o_ref):
    x = x_ref[0]
    n = x.shape[0]
    row = lax.broadcasted_iota(jnp.int32, x.shape, 0)
    prev = jnp.where(row == 0, 0.0, pltpu.roll(x, 1, 0))
    nxt = jnp.where(row == n - 1, 0.0, pltpu.roll(x, n - 1, 0))
    y = prev * w_ref[0:1, :] + x * w_ref[1:2, :] + nxt * w_ref[2:3, :] + b_ref[...]
    o_ref[0] = _silu(y)


def _conv_silu(xbc3, conv_w, conv_b):
    b, l, c = xbc3.shape
    tc = 256
    return pl.pallas_call(
        _conv_kernel,
        grid=(b, c // tc),
        in_specs=[
            pl.BlockSpec((1, l, tc), lambda i, j: (i, 0, j)),
            pl.BlockSpec((3, tc), lambda i, j: (0, j)),
            pl.BlockSpec((1, tc), lambda i, j: (0, j)),
        ],
        out_specs=pl.BlockSpec((1, l, tc), lambda i, j: (i, 0, j)),
        out_shape=jax.ShapeDtypeStruct((b, l, c), F32),
        compiler_params=_params("parallel", "parallel"),
        name="dwconv_silu",
    )(xbc3, conv_w, conv_b.reshape(1, c))


def _ssd_chunk(reverse, xbc_ref, dt_ref, dtt_ref, dtb_row, dtb_col, alog_row, alog_col, y_ref, state_scr):
    q = SSD_CHUNK
    col0 = SSD_HEADS if reverse else 0
    dt_l = _softplus(dt_ref[0] + dtb_row[...])
    a_l = dt_l * (-jnp.exp(alog_row[...]))
    dt_t = _softplus(dtt_ref[0] + dtb_col[...])
    a_t = dt_t * (-jnp.exp(alog_col[...]))
    row = lax.broadcasted_iota(jnp.int32, (q, q), 0)
    col = lax.broadcasted_iota(jnp.int32, (q, q), 1)
    lower = (row >= col).astype(F32)
    upper = (row <= col).astype(F32)
    cs_l = jnp.dot(lower, a_l, precision=HIGHEST, preferred_element_type=F32)
    cs_t = jnp.dot(a_t, upper, precision=HIGHEST, preferred_element_type=F32)
    tot_l = cs_l[q - 1:q, :]
    if reverse:
        p_l, p_t = cs_l - a_l, cs_t - a_t
        tri = row <= col
    else:
        p_l, p_t = cs_l, cs_t
        tri = row >= col

    tot_t = cs_t[:, q - 1:q]
    to_end_t = jnp.exp(p_t) if reverse else jnp.exp(tot_t - p_t)
    w_t = dt_t * to_end_t
    left = lax.broadcasted_iota(jnp.int32, (q, 2 * SSD_HEAD_DIM), 1) < SSD_HEAD_DIM

    xbc = xbc_ref[0]
    heads_per_group = SSD_HEADS // SSD_GROUPS
    for g in range(SSD_GROUPS):
        b_g = xbc[:, SSD_WIDTH + g * SSD_STATE:SSD_WIDTH + (g + 1) * SSD_STATE]
        c_g = xbc[:, SSD_WIDTH + (SSD_GROUPS + g) * SSD_STATE:SSD_WIDTH + (SSD_GROUPS + g + 1) * SSD_STATE]
        c_bf = c_g.astype(BF16)
        cb = _dot_nt(c_bf, b_g.astype(BF16))
        bt = b_g.T
        for pair in range(g * heads_per_group // 2, (g + 1) * heads_per_group // 2):
            lanes = slice(pair * 2 * SSD_HEAD_DIM, (pair + 1) * 2 * SSD_HEAD_DIM)
            xp = xbc[:, lanes]
            x_bd = jnp.concatenate([jnp.where(left, xp, 0.0), jnp.where(left, 0.0, xp)], axis=0).astype(BF16)
            decay_tiles, state_tiles, pcol_tiles = [], [], []
            for k in (col0 + 2 * pair, col0 + 2 * pair + 1):
                pcol_b = jnp.broadcast_to(p_l[:, k:k + 1], (q, q))
                prow = p_t[k:k + 1, :]
                seg = (prow - pcol_b) if reverse else (pcol_b - prow)
                lmat = jnp.exp(jnp.where(tri, seg, -jnp.inf)) * dt_t[k:k + 1, :]
                decay_tiles.append((cb * lmat).astype(BF16))
                state_tiles.append((bt * w_t[k:k + 1, :]).astype(BF16))
                pcol_tiles.append(pcol_b)
            k0 = col0 + 2 * pair
            tot_pair = jnp.where(left[0:1], tot_l[:, k0:k0 + 1], tot_l[:, k0 + 1:k0 + 2])
            pcol_pair = jnp.where(left, pcol_tiles[0], pcol_tiles[1])
            in_decay = jnp.exp(tot_pair - pcol_pair) if reverse else jnp.exp(pcol_pair)
            s_prev = state_scr[pair]
            y_diag = jnp.dot(jnp.concatenate(decay_tiles, axis=1), x_bd, preferred_element_type=F32)
            y_off = jnp.dot(c_bf, s_prev.astype(BF16), preferred_element_type=F32) * in_decay
            state_scr[pair] = jnp.exp(tot_pair) * s_prev + jnp.dot(jnp.concatenate(state_tiles, axis=1), x_bd,
                                                                   preferred_element_type=F32)
            y_ref[0, :, lanes] = y_diag + y_off


def _ssd_kernel(xf_ref, dtf_ref, dttf_ref, xb_ref, dtb_ref, dttb_ref, dtb_row, dtb_col, alog_row, alog_col,
                initf_ref, initb_ref, yf_ref, yb_ref, finf_ref, finb_ref, statef_scr, stateb_scr):
    c = pl.program_id(1)

    @pl.when(c == 0)
    def _():
        statef_scr[...] = initf_ref[0]
        stateb_scr[...] = initb_ref[0]

    params = (dtb_row, dtb_col, alog_row, alog_col)
    _ssd_chunk(False, xf_ref, dtf_ref, dttf_ref, *params, yf_ref, statef_scr)
    _ssd_chunk(True, xb_ref, dtb_ref, dttb_ref, *params, yb_ref, stateb_scr)

    @pl.when(c == pl.num_programs(1) - 1)
    def _():
        finf_ref[0] = statef_scr[...]
        finb_ref[0] = stateb_scr[...]


def _ssd_scan(xbc_act, dt3, dtt3, dtb, alog, init_f, init_b):
    b, l, _ = xbc_act.shape
    nc = l // SSD_CHUNK
    dtb_row = jnp.pad(dtb.reshape(1, -1), ((0, 0), (0, LANES - 2 * SSD_HEADS)))
    alog_row = jnp.pad(alog.reshape(1, -1), ((0, 0), (0, LANES - 2 * SSD_HEADS)))
    small = lambda shape: pl.BlockSpec(shape, lambda i, c: (0, 0))
    st_shape = (b, SSD_HEADS // 2, SSD_STATE, 2 * SSD_HEAD_DIM)
    st_spec = pl.BlockSpec((1,) + st_shape[1:], lambda i, c: (i, 0, 0, 0))
    fwd, bwd = (lambda c: c), (lambda c: nc - 1 - c)
    chunk_specs = lambda cm: [
        pl.BlockSpec((1, SSD_CHUNK, SSD_XBC), lambda i, c: (i, cm(c), 0)),
        pl.BlockSpec((1, SSD_CHUNK, LANES), lambda i, c: (i, cm(c), 0)),
        pl.BlockSpec((1, 2 * SSD_HEADS, SSD_CHUNK), lambda i, c: (i, 0, cm(c))),
    ]
    y_spec = lambda cm: pl.BlockSpec((1, SSD_CHUNK, SSD_WIDTH), lambda i, c: (i, cm(c), 0))
    y_shape = jax.ShapeDtypeStruct((b, l, SSD_WIDTH), F32)
    return pl.pallas_call(
        _ssd_kernel,
        grid=(b, nc),
        in_specs=chunk_specs(fwd) + chunk_specs(bwd)
        + [small((1, LANES)), small((2 * SSD_HEADS, 1)), small((1, LANES)), small((2 * SSD_HEADS, 1)), st_spec, st_spec],
        out_specs=[y_spec(fwd), y_spec(bwd), st_spec, st_spec],
        out_shape=[y_shape, y_shape, jax.ShapeDtypeStruct(st_shape, F32), jax.ShapeDtypeStruct(st_shape, F32)],
        scratch_shapes=[pltpu.VMEM(st_shape[1:], F32), pltpu.VMEM(st_shape[1:], F32)],
        compiler_params=_params("parallel", "arbitrary"),
        name="ssd_scan",
    )(xbc_act, dt3, dtt3, xbc_act, dt3, dtt3, dtb_row, dtb.reshape(-1, 1), alog_row, alog.reshape(-1, 1),
      init_f, init_b)


def _pool_kernel(x_ref, w_ref, sc_ref, o_ref, pad_scr):
    n = x_ref.shape[1]
    zeros = jnp.zeros((POOL_PAD, POOL_GROUP_DIM), F32)
    pad_scr[0:POOL_PAD, :] = zeros
    pad_scr[n + POOL_PAD:n + 2 * POOL_PAD, :] = zeros
    t = lax.broadcasted_iota(jnp.int32, (n, 1), 0)
    for gi, w in enumerate(POOL_WINDOWS):
        sl = slice(gi * POOL_GROUP_DIM, (gi + 1) * POOL_GROUP_DIM)
        x = x_ref[0, :, sl]
        pad_scr[POOL_PAD:n + POOL_PAD, :] = x
        acc = jnp.zeros_like(x)
        for o in range(-(w // 2), w - w // 2):
            acc = acc + pad_scr[POOL_PAD + o:POOL_PAD + o + n, :]
        lo = jnp.maximum(t - w // 2, 0)
        hi = jnp.minimum(t + (w - w // 2 - 1), n - 1)
        pooled = acc / (hi - lo + 1).astype(F32) - x
        y = jnp.dot(pooled.astype(BF16), w_ref[gi].astype(BF16), preferred_element_type=F32)
        o_ref[0, :, sl] = (y * sc_ref[:, sl]).astype(o_ref.dtype)


def _pool_mixer(u3, w_pool, scale):
    b, l, c = u3.shape
    return pl.pallas_call(
        _pool_kernel,
        grid=(b,),
        in_specs=[
            pl.BlockSpec((1, l, c), lambda i: (i, 0, 0)),
            pl.BlockSpec(w_pool.shape, lambda i: (0, 0, 0)),
            pl.BlockSpec((1, c), lambda i: (0, 0)),
        ],
        out_specs=pl.BlockSpec((1, l, c), lambda i: (i, 0, 0)),
        out_shape=jax.ShapeDtypeStruct((b, l, c), BF16),
        scratch_shapes=[pltpu.VMEM((l + 2 * POOL_PAD, POOL_GROUP_DIM), F32)],
        compiler_params=_params("parallel"),
        name="pool_mixer",
    )(u3, w_pool, scale.reshape(1, c))


def _na_bias_kernel(rpb_ref, o_ref):
    h = pl.program_id(0)
    qi = lax.broadcasted_iota(jnp.int32, (GRID_W, LANES), 0)
    lane = lax.broadcasted_iota(jnp.int32, (GRID_W, LANES), 1)
    ki = lane % GRID_W
    second = lane >= GRID_W
    start = jnp.clip(qi - NA_KW // 2, 0, GRID_W - NA_KW)
    in_window = (ki >= start) & (ki < start + NA_KW)
    dc = jnp.clip(ki - qi, -(NA_KW - 1), NA_KW - 1) + NA_KW - 1
    for dr in range(o_ref.shape[1]):
        val = jnp.zeros((GRID_W, LANES), F32)
        for j in range(2 * NA_KW - 1):
            pick = jnp.where(second, rpb_ref[h, dr + 1, j], rpb_ref[h, dr, j])
            val = jnp.where(dc == j, pick, val)
        o_ref[0, dr] = jnp.where(in_window, val, -jnp.inf)


def _na_bias(rpb):
    nh, ndr, ndc = rpb.shape
    return pl.pallas_call(
        _na_bias_kernel,
        grid=(nh,),
        in_specs=[pl.BlockSpec(memory_space=pltpu.SMEM)],
        out_specs=pl.BlockSpec((1, ndr - 1, GRID_W, LANES), lambda h: (h, 0, 0, 0)),
        out_shape=jax.ShapeDtypeStruct((nh, ndr - 1, GRID_W, LANES), F32),
        compiler_params=_params("parallel"),
        name="na_bias",
    )(rpb)


def _na_kernel(kh, q_ref, k_ref, v_ref, kc_ref, vc_ref, bias_ref, o_ref):
    r = pl.program_id(1)
    rows = pl.num_programs(1)
    r0 = jnp.clip(r - kh // 2, 0, rows - kh)
    start = pl.multiple_of(r0 * GRID_W, GRID_W)
    kblk = k_ref[0, pl.ds(start, kh * GRID_W), :]
    vblk = v_ref[0, pl.ds(start, kh * GRID_W), :]
    dr0 = r0 - r + NA_KH - 1
    scale = NA_HEAD_DIM ** -0.5
    pair_w = 2 * NA_HEAD_DIM
    halves = []
    for pair in range(NA_HEADS // 2):
        lanes = slice(pair * pair_w, (pair + 1) * pair_w)
        left_q = lax.broadcasted_iota(jnp.int32, (GRID_W, pair_w), 1) < NA_HEAD_DIM
        q_pair = q_ref[0, :, lanes] * scale
        k_pair, kc_pair = kblk[:, lanes], kc_ref[0, :, lanes]
        for side in range(2):
            keep = left_q if side == 0 else jnp.logical_not(left_q)
            q_h = jnp.where(keep, q_pair, 0.0)
            h = 2 * pair + side
            bias = jnp.concatenate([bias_ref[h, dr0 + 2 * j] for j in range(kh // 2)], axis=1)
            halves.append((_dot_nt(q_h, k_pair) + bias, _dot_nt(q_h, kc_pair)))
    probs = []
    for s_loc, s_ctx in halves:
        m = jnp.maximum(jnp.max(s_loc, axis=-1, keepdims=True), jnp.max(s_ctx, axis=-1, keepdims=True))
        p_loc = jnp.exp(s_loc - m)
        p_ctx = jnp.exp(s_ctx - m)
        inv = 1.0 / (jnp.sum(p_loc, axis=-1, keepdims=True) + jnp.sum(p_ctx, axis=-1, keepdims=True))
        probs.append((p_loc.astype(BF16), p_ctx.astype(BF16), inv))
    for pair in range(NA_HEADS // 2):
        lanes = slice(pair * pair_w, (pair + 1) * pair_w)
        v_pair, vc_pair = vblk[:, lanes], vc_ref[0, :, lanes]
        left_v = lax.broadcasted_iota(jnp.int32, v_pair.shape, 1) < NA_HEAD_DIM
        left_c = lax.broadcasted_iota(jnp.int32, vc_pair.shape, 1) < NA_HEAD_DIM
        out = None
        for side in range(2):
            p_loc, p_ctx, inv = probs[2 * pair + side]
            keep_v = left_v if side == 0 else jnp.logical_not(left_v)
            keep_c = left_c if side == 0 else jnp.logical_not(left_c)
            acc = jnp.dot(p_loc, jnp.where(keep_v, v_pair, 0.0), preferred_element_type=F32)
            acc = acc + jnp.dot(p_ctx, jnp.where(keep_c, vc_pair, 0.0), preferred_element_type=F32)
            out = acc * inv if out is None else out + acc * inv
        o_ref[0, :, lanes] = out.astype(o_ref.dtype)


def _neighbourhood_attention(q3, k3, v3, kc3, vc3, bias):
    b, s, c = q3.shape
    rows = s // GRID_W
    kh = min(NA_KH, rows)
    lc = kc3.shape[1]
    full = lambda n: pl.BlockSpec((1, n, c), lambda i, r: (i, 0, 0))
    return pl.pallas_call(
        functools.partial(_na_kernel, kh),
        grid=(b, rows),
        in_specs=[
            pl.BlockSpec((1, GRID_W, c), lambda i, r: (i, r, 0)),
            full(s), full(s), full(lc), full(lc),
            pl.BlockSpec(bias.shape, lambda i, r: (0, 0, 0, 0)),
        ],
        out_specs=pl.BlockSpec((1, GRID_W, c), lambda i, r: (i, r, 0)),
        out_shape=jax.ShapeDtypeStruct((b, s, c), BF16),
        compiler_params=_params("parallel", "arbitrary"),
        name="na_attention",
    )(q3, k3, v3, kc3, vc3, bias)


def _ctx_attn_kernel(q_ref, k_ref, v_ref, o_ref):
    scale = NA_HEAD_DIM ** -0.5
    for h in range(NA_HEADS):
        sl = slice(h * NA_HEAD_DIM, (h + 1) * NA_HEAD_DIM)
        s = _dot_nt(q_ref[0, :, sl] * scale, k_ref[0, :, sl])
        p = jnp.exp(s - jnp.max(s, axis=-1, keepdims=True))
        den = jnp.sum(p, axis=-1, keepdims=True)
        o_ref[0, :, sl] = (jnp.dot(p.astype(BF16), v_ref[0, :, sl], preferred_element_type=F32) / den).astype(o_ref.dtype)


def _context_attention(q3, k3, v3):
    b, l, c = q3.shape
    spec = pl.BlockSpec((1, l, c), lambda i: (i, 0, 0))
    return pl.pallas_call(
        _ctx_attn_kernel,
        grid=(b,),
        in_specs=[spec, spec, spec],
        out_specs=spec,
        out_shape=jax.ShapeDtypeStruct((b, l, c), BF16),
        compiler_params=_params("parallel"),
        name="ctx_attention",
    )(q3, k3, v3)


def _mix_out_kernel(yf_ref, yb_ref, xs_ref, z_ref, pool_ref, na_ref, h_ref, dsk_ref, ng_ref, g1_ref, w_ref, o_ref):
    y = yf_ref[...] + yb_ref[...] + dsk_ref[...] * xs_ref[...]
    y = y * _silu(z_ref[...])
    ms = jnp.mean(y * y, axis=-1, keepdims=True)
    yn = (y * lax.rsqrt(ms + RMS_EPS)) * ng_ref[...]
    mix = jnp.dot(yn.astype(BF16), w_ref[0:SSD_WIDTH, :], preferred_element_type=F32)
    mix = mix + jnp.dot(pool_ref[...].astype(BF16), w_ref[SSD_WIDTH:SSD_WIDTH + POOL_WIDTH, :],
                        preferred_element_type=F32)
    mix = mix + jnp.dot(na_ref[...].astype(BF16), w_ref[SSD_WIDTH + POOL_WIDTH:, :], preferred_element_type=F32)
    o_ref[...] = h_ref[...] + g1_ref[0] * mix


def _mix_out(yf, yb, xbc_act2d, z, pool_y, na_y, h2d, seq_len, d_skip, norm_g, g1, w_out_bf):
    n, d = h2d.shape
    t = 256
    row = lambda w: pl.BlockSpec((t, w), lambda i: (i, 0))
    vec = lambda w: pl.BlockSpec((1, w), lambda i: (0, 0))
    return pl.pallas_call(
        _mix_out_kernel,
        grid=(n // t,),
        in_specs=[
            row(SSD_WIDTH), row(SSD_WIDTH), row(SSD_WIDTH), row(SSD_WIDTH), row(POOL_WIDTH), row(NA_WIDTH), row(d),
            vec(SSD_WIDTH), vec(SSD_WIDTH),
            pl.BlockSpec((1, 1, d), _batch_map(t, seq_len, g1.shape[0])),
            pl.BlockSpec(w_out_bf.shape, lambda i: (0, 0)),
        ],
        out_specs=row(d),
        out_shape=jax.ShapeDtypeStruct((n, d), F32),
        compiler_params=_params("parallel"),
        name="mix_out",
    )(yf, yb, xbc_act2d, z, pool_y, na_y, h2d, jnp.repeat(d_skip, SSD_HEAD_DIM).reshape(1, -1),
      norm_g.reshape(1, -1), g1, w_out_bf)


_CAND_ROWS = 16 + 8 * 7 + 8


def _batcher_pairs(n):
    pairs = []

    def merge(lo, m, r):
        step = 2 * r
        if step < m:
            merge(lo, m, step)
            merge(lo + r, m, step)
            pairs.extend((i, i + r) for i in range(lo + r, lo + m - r, step))
        else:
            pairs.append((lo, lo + r))

    def sort(lo, m):
        if m > 1:
            sort(lo, m // 2)
            sort(lo + m // 2, m // 2)
            merge(lo, m, 1)

    sort(0, n)
    return tuple(pairs)


_SORT16 = _batcher_pairs(PEER_TOPK)
_BITONIC16 = tuple((i, i + d) for d in (8, 4, 2, 1) for i in range(PEER_TOPK) if not i & d)


def _exchange(x, pairs):
    for i, j in pairs:
        x[i], x[j] = jnp.maximum(x[i], x[j]), jnp.minimum(x[i], x[j])


def _sorted_top16(s):
    tiles = []
    for l0 in range(0, s.shape[1], LANES):
        x = [s[8 * v:8 * v + 8, l0:l0 + LANES] for v in range(PEER_TOPK)]
        _exchange(x, _SORT16)
        for shift in (4, 2, 1):
            x = [jnp.maximum(x[i], pltpu.roll(x[PEER_TOPK - 1 - i], shift, 0)) for i in range(PEER_TOPK)]
            _exchange(x, _BITONIC16)
        tiles.append(jnp.concatenate([xi[0:1] for xi in x], axis=0))
    return jnp.concatenate(tiles, axis=1)


def _count_above(s, v):
    r = lambda j: v[j:j + 1]
    pick = jnp.where
    a = s < r(7)
    b = s < pick(a, r(11), r(3))
    c = s < pick(a, pick(b, r(13), r(9)), pick(b, r(5), r(1)))
    d = s < pick(a, pick(b, pick(c, r(14), r(12)), pick(c, r(10), r(8))),
                 pick(b, pick(c, r(6), r(4)), pick(c, r(2), r(0))))
    return (pick(a, 8.0, 0.0) + pick(b, 4.0, 0.0) + pick(c, 2.0, 0.0) + pick(d, 1.0, 0.0)
            + pick(s < r(15), 1.0, 0.0))


def _top16_pair_fast(s1, s2):
    v1, v2 = _sorted_top16(s1), _sorted_top16(s2)
    rank2 = _count_above(s2, v2)

    def check(s, v):
        n = jnp.sum(jnp.where(s >= v[PEER_TOPK - 1:PEER_TOPK], 1.0, 0.0), axis=0, keepdims=True)
        strict = jnp.min(v[:-1] - v[1:], axis=0, keepdims=True) > 0.0
        return jnp.where(strict, n, 0.0)

    return v1, check(s1, v1), v2, rank2, check(s2, v2)


def _first_counts_fast(s1, v1, sel):
    inf = jnp.inf
    rows_v1 = jnp.concatenate([jnp.broadcast_to(v1[0:1], (PEER_TOPK, v1.shape[1]))]
                              + [jnp.broadcast_to(v1[j:j + 1], (8, v1.shape[1])) for j in range(1, 8)]
                              + [v1[8:16]], axis=0)
    bound = jnp.where(sel > 0.0, rows_v1, inf)
    low = bound[0:8]
    for j in range(1, 8):
        low = jnp.minimum(low, bound[16 + 8 * (j - 1):16 + 8 * j])
    tail = jnp.min(bound[64 + 8:64 + 16], axis=0, keepdims=True)
    t = [jnp.minimum(low[0:1], tail)] + [low[k:k + 1] for k in range(1, 8)]
    pick = jnp.where
    a = s1 >= t[3]
    b = s1 >= pick(a, t[5], t[1])
    c = s1 >= pick(a, pick(b, t[6], t[4]), pick(b, t[2], t[0]))
    cnt = pick(a, 4.0, 0.0) + pick(b, 2.0, 0.0) + pick(c, 1.0, 0.0) + pick(s1 >= t[7], 1.0, 0.0)
    n_high = jnp.sum(sel[8:16], axis=0, keepdims=True)
    return cnt + jnp.where(s1 >= v1[0:1], n_high, 0.0)


def _first_counts_exact(rank1, sel):
    cnt = jnp.zeros(rank1.shape, F32)
    for j in range(8):
        lo = 0 if j == 0 else 16 + 8 * (j - 1)
        n_j = jnp.sum(sel[lo:lo + (16 if j == 0 else 8)], axis=0, keepdims=True)
        cnt = cnt + jnp.where(rank1 == float(j), n_j, 0.0)
    for j in range(8, 16):
        cnt = cnt + jnp.where(rank1 == float(j), sel[64 + j:65 + j], 0.0)
    return cnt


def _top16_pair_exact(s1, s2):
    n, t = s1.shape
    iota = lax.broadcasted_iota(jnp.int32, (n, t), 0).astype(F32)
    row16 = lax.broadcasted_iota(jnp.int32, (PEER_TOPK, t), 0)

    def pick(work, rank, vals, j):
        m = jnp.max(work, axis=0, keepdims=True)
        idx = jnp.min(jnp.where(work == m, iota, float(n)), axis=0, keepdims=True)
        sel = iota == idx
        return (jnp.where(sel, -jnp.inf, work), jnp.where(sel, lax.convert_element_type(j, F32), rank),
                jnp.where(row16 == j, m, vals))

    def body(j, carry):
        a, b = carry
        return pick(*a, j), pick(*b, j)

    start = lambda s: (s, jnp.full((n, t), float(PEER_TOPK), F32), jnp.zeros((PEER_TOPK, t), F32))
    (_, rank1, v1), (_, rank2, v2) = lax.fori_loop(0, PEER_TOPK, body, (start(s1), start(s2)))
    return rank1, v1, rank2, v2


def _select16_fast(cand):
    rows, t = cand.shape
    padded = jnp.concatenate([cand, jnp.full((PEER_NKEYS - rows, t), -jnp.inf, F32)], axis=0)
    m = _sorted_top16(padded)[PEER_TOPK - 1:PEER_TOPK]
    taken = jnp.where(cand >= m, 1.0, 0.0)
    return taken, jnp.sum(taken, axis=0, keepdims=True)


def _select16_exact(cand):
    iota = lax.broadcasted_iota(jnp.int32, cand.shape, 0).astype(F32)

    def body(_, carry):
        work, sel_acc = carry
        m = jnp.max(work, axis=0, keepdims=True)
        idx = jnp.min(jnp.where(work == m, iota, float(_CAND_ROWS)), axis=0, keepdims=True)
        sel = iota == idx
        return jnp.where(sel, -jnp.inf, work), jnp.where(sel, 1.0, sel_acc)

    return lax.fori_loop(0, PEER_TOPK, body, (cand, jnp.zeros(cand.shape, F32)))[1]


def _candidate_sums(v1, v2):
    blocks = [v1[0:1] + v2]
    for j in range(1, 8):
        blocks.append(v1[j:j + 1] + v2[0:8])
    blocks.append(v1[8:16] + v2[0:1])
    return jnp.concatenate(blocks, axis=0)


def _any_not_16(*counts):
    return jnp.max(sum(jnp.abs(c - float(PEER_TOPK)) for c in counts)) > 0.0


def _bf16_bits(x):
    return pltpu.bitcast(x.astype(BF16).astype(F32), jnp.uint32)


def _pack_row_pairs(x, scr):
    n, t = x.shape
    for j in range(t // LANES):
        scr[j] = x[:, j * LANES:(j + 1) * LANES]
    words = []
    for j in range(t // LANES):
        even = scr[j, pl.ds(0, n // 2, stride=2), :]
        odd = scr[j, pl.ds(1, n // 2, stride=2), :]
        words.append((_bf16_bits(even) >> 16) | _bf16_bits(odd))
    return jnp.concatenate(words, axis=1)


def _pack_same(x):
    w = _bf16_bits(x)
    return w | (w >> 16)


_F8_TARGET_EXP = 6


def _pow2_scale(amax):
    bits = pltpu.bitcast(jnp.maximum(amax, 2.0 ** -100), jnp.int32)
    exponent = (bits >> 23) - 127
    return pltpu.bitcast((_F8_TARGET_EXP - exponent + 127) << 23, F32)


def _peer_score_kernel(h_ref, g_ref, sh_ref, sc_ref, wq_ref, keys_ref,
                       x_out, xinv_out, cnt_out, e1_out, rk_out, e2_out,
                       u_scr, pair_scr, cnt_scr, rank2_scr, top_scr):
    hd = pl.program_id(1)

    @pl.when(hd == 0)
    def _():
        u = _rms_mod(h_ref[...], g_ref[...], sc_ref[0], sh_ref[0])
        u_scr[...] = u.astype(BF16)
        scale = _pow2_scale(jnp.max(jnp.abs(u), axis=-1, keepdims=True))
        x_out[...] = (u * scale).astype(F8)
        xinv_out[...] = 1.0 / scale

    q = jnp.dot(u_scr[...], wq_ref[0], preferred_element_type=F32).astype(BF16)
    s1 = _dot_nt(keys_ref[0, 0], q[:, :PEER_SUB])
    s2 = _dot_nt(keys_ref[0, 1], q[:, PEER_SUB:])
    v1, n1, v2, rank2, n2 = _top16_pair_fast(s1, s2)
    cand = _candidate_sums(v1, v2)
    sel, n_sel = _select16_fast(cand)
    cnt_scr[...] = _first_counts_fast(s1, v1, sel)
    rank2_scr[...] = rank2
    top_scr[0:1] = v1[0:1]
    top_scr[1:2] = v2[0:1]
    top_scr[2:3] = jnp.sum(sel * jnp.exp(cand - cand[0:1]), axis=0, keepdims=True)

    @pl.when(_any_not_16(n1, n2, n_sel))
    def _():
        rank1_x, v1_x, rank2_x, v2_x = _top16_pair_exact(s1, s2)
        cand_x = _candidate_sums(v1_x, v2_x)
        sel_x = _select16_exact(cand_x)
        cnt_scr[...] = _first_counts_exact(rank1_x, sel_x)
        rank2_scr[...] = rank2_x
        top_scr[0:1] = v1_x[0:1]
        top_scr[1:2] = v2_x[0:1]
        top_scr[2:3] = jnp.sum(sel_x * jnp.exp(cand_x - cand_x[0:1]), axis=0, keepdims=True)

    cnt_out[0] = _pack_same(cnt_scr[...])
    e1_out[0] = _pack_same(jnp.exp(s1 - top_scr[0:1]))
    rk_out[0] = _pack_row_pairs(rank2_scr[...], pair_scr)
    e2_out[0] = _pack_row_pairs(jnp.exp(s2 - top_scr[1:2]) / top_scr[2:3], pair_scr)


def _peer_scores(h2d, seq_len, g, shift, scale, wq_heads, keys_bf):
    n, d = h2d.shape
    nb = shift.shape[0]
    t = 512
    assert n % t == 0 and (nb == 1 or seq_len % t == 0)
    bm = _batch_map(t, seq_len, nb)
    mod_spec = pl.BlockSpec((1, 1, d), lambda i, hd: bm(i))
    first_out = pl.BlockSpec((1, PEER_NKEYS, t), lambda i, hd: (hd, 0, i))
    first_shape = jax.ShapeDtypeStruct((PEER_HEADS, PEER_NKEYS, n), jnp.uint32)
    second_out = pl.BlockSpec((1, PEER_NKEYS // 2, t), lambda i, hd: (hd, 0, i))
    second_shape = jax.ShapeDtypeStruct((PEER_HEADS, PEER_NKEYS // 2, n), jnp.uint32)
    return pl.pallas_call(
        _peer_score_kernel,
        grid=(n // t, PEER_HEADS),
        in_specs=[
            pl.BlockSpec((t, d), lambda i, hd: (i, 0)),
            pl.BlockSpec((1, d), lambda i, hd: (0, 0)),
            mod_spec, mod_spec,
            pl.BlockSpec((1, d, 2 * PEER_SUB), lambda i, hd: (hd, 0, 0)),
            pl.BlockSpec((1, 2, PEER_NKEYS, PEER_SUB), lambda i, hd: (hd, 0, 0, 0)),
        ],
        out_specs=[pl.BlockSpec((t, d), lambda i, hd: (i, 0)), pl.BlockSpec((t, 1), lambda i, hd: (i, 0)),
                   first_out, first_out, second_out, second_out],
        out_shape=[jax.ShapeDtypeStruct((n, d), F8), jax.ShapeDtypeStruct((n, 1), F32),
                   first_shape, first_shape, second_shape, second_shape],
        scratch_shapes=[pltpu.VMEM((t, d), BF16), pltpu.VMEM((t // LANES, PEER_NKEYS, LANES), F32),
                        pltpu.VMEM((PEER_NKEYS, t), F32), pltpu.VMEM((PEER_NKEYS, t), F32),
                        pltpu.VMEM((8, t), F32)],
        compiler_params=_params("parallel", "arbitrary"),
        name="peer_scores",
    )(h2d, g, shift, scale, wq_heads, keys_bf)


_PEER_EC = 1024
_INV_SQRT2 = 1.0 / math.sqrt(2.0)


def _as_bf16_rows(words):
    return pltpu.bitcast(words, BF16)


_PEER_A_GROUP = 4


def _peer_expert_kernel(n_chunks, final_norm, x_ref, u_ref, inv_ref, vt_ref, cnt_ref, e1_ref, rk_ref, e2_ref, h_ref,
                        g2_ref, fg_ref, o_ref,
                        ht0_scr, ht1_scr, g0_scr, g1_scr, acc_scr):
    s = pl.program_id(0)
    t = h_ref.shape[0]
    p2 = s - 2
    c2 = lax.rem(jnp.maximum(p2, 0), n_chunks)
    pack = 2 * 8
    n_a = _PEER_EC // PEER_NKEYS

    @pl.when(s == 0)
    def _():
        ht1_scr[...] = jnp.zeros_like(ht1_scr)
        g0_scr[...] = jnp.zeros_like(g0_scr)
        g1_scr[...] = jnp.zeros_like(g1_scr)

    @pl.when((p2 <= 0) | (c2 == 0))
    def _():
        acc_scr[...] = jnp.zeros_like(acc_scr)

    def step(ht_new, ht_old, g_new, g_old):
        for tc in range(t // LANES):
            ls = slice(tc * LANES, (tc + 1) * LANES)
            half_inv = 0.5 * inv_ref[:, ls]
            c_inv = _INV_SQRT2 * inv_ref[:, ls]
            for a0 in range(0, n_a, _PEER_A_GROUP):
                w = [[None] * (PEER_NKEYS // pack) for _ in range(_PEER_A_GROUP)]
                for hd in range(PEER_HEADS):
                    rows1 = [(_as_bf16_rows(jnp.broadcast_to(cnt_ref[hd, a0 + i:a0 + i + 1, ls], (8, LANES))),
                              _as_bf16_rows(jnp.broadcast_to(e1_ref[hd, a0 + i:a0 + i + 1, ls], (8, LANES))))
                             for i in range(_PEER_A_GROUP)]
                    for r in range(PEER_NKEYS // pack):
                        rk = _as_bf16_rows(rk_ref[hd, r * 8:(r + 1) * 8, ls])
                        e2 = _as_bf16_rows(e2_ref[hd, r * 8:(r + 1) * 8, ls])
                        for i, (cn, e1) in enumerate(rows1):
                            term = jnp.where(rk < cn, e2, 0.0) * e1
                            w[i][r] = term if w[i][r] is None else w[i][r] + term
                for i in range(_PEER_A_GROUP):
                    for r in range(PEER_NKEYS // pack):
                        row0 = (a0 + i) * PEER_NKEYS + r * pack
                        hs = ht_old[row0:row0 + pack, ls]
                        act = (hs * half_inv) * (1.0 + lax.erf(hs * c_inv))
                        g_new[row0 // 2:(row0 + pack) // 2, ls] = pltpu.bitcast(w[i][r] * act.astype(BF16), jnp.uint32)
        ht_new[...] = _dot_nt(u_ref[...], x_ref[...])
        acc_scr[...] += jnp.dot(_as_bf16_rows(vt_ref[...]), _as_bf16_rows(g_old[...]),
                                preferred_element_type=F32)

    @pl.when(s % 2 == 0)
    def _():
        step(ht0_scr, ht1_scr, g1_scr, g0_scr)

    @pl.when(s % 2 == 1)
    def _():
        step(ht1_scr, ht0_scr, g0_scr, g1_scr)

    @pl.when((p2 >= 0) & (c2 == n_chunks - 1))
    def _():
        y = h_ref[...] + g2_ref[0] * acc_scr[...].T
        if final_norm:
            y = (y * lax.rsqrt(jnp.mean(y * y, axis=-1, keepdims=True) + RMS_EPS)) * fg_ref[...]
        o_ref[...] = y


def _peer_experts(x8, u8, inv_row, vt_pk, cnt, e1, rk, e2, h2d, seq_len, g2, final_g, final_norm):
    n, d = h2d.shape
    t = 512
    n_chunks = u8.shape[0] // _PEER_EC
    total = (n // t) * n_chunks
    ea = _PEER_EC // PEER_NKEYS

    def pair(p):
        p = jnp.clip(p, 0, total - 1)
        return p // n_chunks, lax.rem(p, n_chunks)

    blk = lambda lag: (lambda s: pair(s - lag)[0])
    chk = lambda lag: (lambda s: pair(s - lag)[1])
    tok = pl.BlockSpec((PEER_HEADS, PEER_NKEYS // 2, t), lambda s: (0, 0, blk(1)(s)))
    first = pl.BlockSpec((PEER_HEADS, ea, t), lambda s: (0, chk(1)(s), blk(1)(s)))
    bm = _batch_map(t, seq_len, g2.shape[0])
    return pl.pallas_call(
        functools.partial(_peer_expert_kernel, n_chunks, final_norm),
        grid=(total + 2,),
        in_specs=[
            pl.BlockSpec((t, d), lambda s: (blk(0)(s), 0)),
            pl.BlockSpec((_PEER_EC, d), lambda s: (chk(0)(s), 0)),
            pl.BlockSpec((1, t), lambda s: (0, blk(1)(s))),
            pl.BlockSpec((d // 2, _PEER_EC), lambda s: (0, chk(2)(s))),
            first, first, tok, tok,
            pl.BlockSpec((t, d), lambda s: (blk(2)(s), 0)),
            pl.BlockSpec((1, 1, d), lambda s: bm(blk(2)(s))),
            pl.BlockSpec((1, d), lambda s: (0, 0)),
        ],
        out_specs=pl.BlockSpec((t, d), lambda s: (blk(2)(s), 0)),
        out_shape=jax.ShapeDtypeStruct((n, d), F32),
        scratch_shapes=[pltpu.VMEM((_PEER_EC, t), F32), pltpu.VMEM((_PEER_EC, t), F32),
                        pltpu.VMEM((_PEER_EC // 2, t), jnp.uint32), pltpu.VMEM((_PEER_EC // 2, t), jnp.uint32),
                        pltpu.VMEM((d, t), F32)],
        compiler_params=_params("arbitrary"),
        name="peer_experts",
    )(x8, u8, inv_row, vt_pk, cnt, e1, rk, e2, h2d, g2, final_g.reshape(1, d))


def _pack_rows_kernel(transpose, x_ref, o_ref):
    x = x_ref[0].T if transpose else x_ref[0]
    o_ref[...] = pltpu.bitcast(x.astype(BF16), jnp.uint32)


def _pack_bf16_rows(stack, layer, transpose=False):
    tile = 512
    if transpose:
        _, c, r = stack.shape
        in_spec = pl.BlockSpec((1, tile, r), lambda i: (layer, i, 0))
        out_spec = pl.BlockSpec((r // 2, tile), lambda i: (0, i))
        steps = c // tile
    else:
        _, r, c = stack.shape
        in_spec = pl.BlockSpec((1, tile, c), lambda i: (layer, i, 0))
        out_spec = pl.BlockSpec((tile // 2, c), lambda i: (i, 0))
        steps = r // tile
    return pl.pallas_call(
        functools.partial(_pack_rows_kernel, transpose),
        grid=(steps,),
        in_specs=[in_spec],
        out_specs=out_spec,
        out_shape=jax.ShapeDtypeStruct((r // 2, c), jnp.uint32),
        compiler_params=_params("parallel"),
        name="pack_bf16_rows_t" if transpose else "pack_bf16_rows",
    )(stack)


def _table_absmax_kernel(x_ref, o_ref):
    @pl.when(pl.program_id(0) == 0)
    def _():
        o_ref[...] = jnp.zeros_like(o_ref)

    m = jnp.max(jnp.max(jnp.abs(x_ref[0]), axis=0, keepdims=True), axis=1, keepdims=True)
    o_ref[...] = jnp.maximum(o_ref[...], m)


def _quantize_kernel(s_ref, x_ref, o_ref):
    o_ref[...] = (x_ref[0] * s_ref[0]).astype(F8)


def _quantize_table(stack, layer):
    _, r, c = stack.shape
    tile = 512
    in_spec = pl.BlockSpec((1, tile, c), lambda i: (layer, i, 0))
    amax = pl.pallas_call(
        _table_absmax_kernel,
        grid=(r // tile,),
        in_specs=[in_spec],
        out_specs=pl.BlockSpec((8, LANES), lambda i: (0, 0)),
        out_shape=jax.ShapeDtypeStruct((8, LANES), F32),
        compiler_params=_params("arbitrary"),
        name="table_absmax",
    )(stack)[0, 0]
    exponent = jnp.floor(jnp.log2(jnp.maximum(amax, 2.0 ** -100)))
    scale = jnp.exp2(_F8_TARGET_EXP - exponent)
    q = pl.pallas_call(
        _quantize_kernel,
        grid=(r // tile,),
        in_specs=[pl.BlockSpec(memory_space=pltpu.SMEM), in_spec],
        out_specs=pl.BlockSpec((tile, c), lambda i: (i, 0)),
        out_shape=jax.ShapeDtypeStruct((r, c), F8),
        compiler_params=_params("parallel"),
        name="quantize_table",
    )(scale.reshape(1), stack)
    return q, 1.0 / scale


def _peer_ffn_residual(h2d, seq_len, norm_g, shift, scale, gate, wq_heads, keys_bf, u8, u_inv, vt_pk, final_g,
                       final_norm=False):
    x8, x_inv, cnt, e1, rk, e2 = _peer_scores(h2d, seq_len, norm_g, shift, scale, wq_heads, keys_bf)
    inv_row = x_inv.reshape(1, -1) * u_inv
    return _peer_experts(x8, u8, inv_row, vt_pk, cnt, e1, rk, e2, h2d, seq_len, gate, final_g, final_norm)


def _mixer_inputs(h2d, batch, seq_len, norm_g, shift, scale, w_perm, conv_w, conv_b):
    z, xbc, pool_u, q, k, v, dt = _project(h2d, seq_len, norm_g, shift, scale, w_perm)
    xbc_act = _conv_silu(xbc.reshape(batch, seq_len, SSD_XBC), conv_w, conv_b)
    dt3 = dt.reshape(batch, seq_len, LANES)
    dtt3 = jnp.swapaxes(dt3[:, :, :2 * SSD_HEADS], 1, 2)
    r3 = lambda a: a.reshape(batch, seq_len, a.shape[-1])
    return z, xbc_act, dt3, dtt3, r3(pool_u), r3(q), r3(k), r3(v)


def kernel(x, c, ctx, c_ctx, ada_w, ada_b, norm1_g, w_in, conv_w, conv_b, a_log, dt_bias, d_skip, ssd_norm_g, pool_w, pool_scale, na_rpb, w_out, norm2_g, peer_wq, peer_keys, peer_u, peer_v, final_g):
    batch, seq, d = x.shape
    ctx_len = ctx.shape[1]
    n, nc = batch * seq, batch * ctx_len
    h = x.reshape(n, d)
    hc = ctx.reshape(nc, d)

    c8 = jnp.concatenate([c, c_ctx[None], jnp.zeros((8 - batch - 1, d), F32)], axis=0)
    mod = _modulation(c8, ada_w, ada_b)

    for i in range(DEPTH):
        need_ctx_out = i < DEPTH - 1
        lat = [mod[i, :batch, j * d:(j + 1) * d].reshape(batch, 1, d) for j in range(6)]
        cx = [mod[i, batch:batch + 1, j * d:(j + 1) * d].reshape(1, 1, d) for j in range(6)]
        sh1, sc1, g1, sh2, sc2, g2 = lat
        csh1, csc1, cg1, csh2, csc2, cg2 = cx

        w_perm = _permute_w_in(w_in[i])
        w_out_bf = w_out[i].astype(BF16)
        n1 = norm1_g[i].reshape(1, d)
        n2 = norm2_g[i].reshape(1, d)
        zero_state = jnp.zeros((batch, SSD_HEADS // 2, SSD_STATE, 2 * SSD_HEAD_DIM), F32)
        scan = functools.partial(_ssd_scan, dtb=dt_bias[i], alog=a_log[i])
        wq_heads = peer_wq[i].reshape(d, PEER_HEADS, 2 * PEER_SUB).transpose(1, 0, 2).astype(BF16)
        keys_bf = peer_keys[i].astype(BF16)
        u8, u_inv = _quantize_table(peer_u, i)
        vt_pk = _pack_bf16_rows(peer_v, i, transpose=True)

        zc, xbc_c, dt3_c, dtt3_c, pool_c, qc, kc, vc = _mixer_inputs(
            hc, batch, ctx_len, n1, csh1, csc1, w_perm, conv_w[i], conv_b[i])
        yf_c, yb_c, st_f, st_b = scan(xbc_c, dt3_c, dtt3_c, init_f=zero_state, init_b=zero_state)
        if need_ctx_out:
            pool_yc = _pool_mixer(pool_c, pool_w[i], pool_scale[i])
            att_c = _context_attention(qc, kc, vc)
            hc = _mix_out(yf_c.reshape(nc, -1), yb_c.reshape(nc, -1), xbc_c.reshape(nc, -1), zc,
                          pool_yc.reshape(nc, -1), att_c.reshape(nc, -1), hc, ctx_len,
                          d_skip[i], ssd_norm_g[i], cg1, w_out_bf)
            hc = _peer_ffn_residual(hc, ctx_len, n2, csh2, csc2, cg2, wq_heads, keys_bf, u8, u_inv, vt_pk, final_g)

        z, xbc_l, dt3_l, dtt3_l, pool_l, q, k, v = _mixer_inputs(
            h, batch, seq, n1, sh1, sc1, w_perm, conv_w[i], conv_b[i])
        yf, yb, _, _ = scan(xbc_l, dt3_l, dtt3_l, init_f=st_f, init_b=st_b)
        pool_y = _pool_mixer(pool_l, pool_w[i], pool_scale[i])
        na = _neighbourhood_attention(q, k, v, kc, vc, _na_bias(na_rpb[i]))
        h = _mix_out(yf.reshape(n, -1), yb.reshape(n, -1), xbc_l.reshape(n, -1), z,
                     pool_y.reshape(n, -1), na.reshape(n, -1), h, seq,
                     d_skip[i], ssd_norm_g[i], g1, w_out_bf)
        h = _peer_ffn_residual(h, seq, n2, sh2, sc2, g2, wq_heads, keys_bf, u8, u_inv, vt_pk, final_g,
                               final_norm=i == DEPTH - 1)

    return h.reshape(batch, seq, d)
```

```python
import functools
import math

import jax
import jax.numpy as jnp
from jax import lax
from jax.experimental import pallas as pl
from jax.experimental.pallas import tpu as pltpu

F32 = jnp.float32
BF16 = jnp.bfloat16
F8 = jnp.float8_e4m3fn
HIGHEST = lax.Precision.HIGHEST

D_MODEL = 1024
DEPTH = 2
GRID_W = 64
RMS_EPS = 1e-6

SSD_HEAD_DIM = 64
SSD_HEADS = 16
SSD_GROUPS = 2
SSD_STATE = 128
SSD_CHUNK = 128
SSD_WIDTH = SSD_HEADS * SSD_HEAD_DIM
SSD_XBC = SSD_WIDTH + 2 * SSD_GROUPS * SSD_STATE

POOL_WINDOWS = (2, 4, 8, 16)
POOL_GROUP_DIM = 128
POOL_WIDTH = POOL_GROUP_DIM * len(POOL_WINDOWS)
POOL_PAD = 8

NA_HEAD_DIM = 64
NA_HEADS = 8
NA_WIDTH = NA_HEADS * NA_HEAD_DIM
NA_KH = 8
NA_KW = 16

PEER_HEADS = 8
PEER_NKEYS = 128
PEER_TOPK = 16
PEER_SUB = 128

LANES = 128
VMEM_LIMIT_BYTES = 56 * 1024 * 1024

_PROJ_SEGS = (SSD_WIDTH, SSD_XBC, POOL_WIDTH, NA_WIDTH, NA_WIDTH, NA_WIDTH, LANES)
_PROJ_DTYPES = (F32, F32, F32, BF16, BF16, BF16, F32)


def _params(*sem):
    return pltpu.CompilerParams(dimension_semantics=sem, vmem_limit_bytes=VMEM_LIMIT_BYTES)


def _rms_mod(x, g, scale, shift):
    ms = jnp.mean(x * x, axis=-1, keepdims=True)
    return (x * lax.rsqrt(ms + RMS_EPS)) * g * (1.0 + scale) + shift


def _silu(x):
    return x * jax.nn.sigmoid(x)


def _softplus(x):
    return jnp.maximum(x, 0.0) + jnp.log1p(jnp.exp(-jnp.abs(x)))


def _dot_nt(a, b):
    return lax.dot_general(a, b, (((1,), (1,)), ((), ())), preferred_element_type=F32)


def _batch_map(block_rows, seq_len, n_rows):
    if n_rows == 1:
        return lambda i, *_: (0, 0, 0)
    return lambda i, *_: ((i * block_rows) // seq_len, 0, 0)


def _mod_kernel(c_ref, w_ref, b_ref, o_ref):
    s = _silu(c_ref[...])
    o_ref[0] = jnp.dot(s, w_ref[0], precision=HIGHEST, preferred_element_type=F32) + b_ref[0]


def _modulation(c8, ada_w, ada_b):
    depth, d, six_d = ada_w.shape
    tn = 1024
    return pl.pallas_call(
        _mod_kernel,
        grid=(depth, six_d // tn),
        in_specs=[
            pl.BlockSpec((8, d), lambda l, j: (0, 0)),
            pl.BlockSpec((1, d, tn), lambda l, j: (l, 0, j)),
            pl.BlockSpec((1, 1, tn), lambda l, j: (l, 0, j)),
        ],
        out_specs=pl.BlockSpec((1, 8, tn), lambda l, j: (l, 0, j)),
        out_shape=jax.ShapeDtypeStruct((depth, 8, six_d), F32),
        compiler_params=_params("parallel", "parallel"),
        name="adaln_mod",
    )(c8, ada_w, ada_b.reshape(depth, 1, six_d))


def _proj_kernel(h_ref, g_ref, sh_ref, sc_ref, w_ref, *out_refs):
    u = _rms_mod(h_ref[...], g_ref[...], sc_ref[0], sh_ref[0]).astype(BF16)
    off = 0
    for o_ref, width in zip(out_refs, _PROJ_SEGS):
        o_ref[...] = jnp.dot(u, w_ref[:, off:off + width], preferred_element_type=F32).astype(o_ref.dtype)
        off += width


def _project(h2d, seq_len, g, shift, scale, w_perm):
    n, d = h2d.shape
    t = 256
    total = sum(_PROJ_SEGS)
    nb = shift.shape[0]
    row_map = lambda i: (i, 0)
    return pl.pallas_call(
        _proj_kernel,
        grid=(n // t,),
        in_specs=[
            pl.BlockSpec((t, d), row_map),
            pl.BlockSpec((1, d), lambda i: (0, 0)),
            pl.BlockSpec((1, 1, d), _batch_map(t, seq_len, nb)),
            pl.BlockSpec((1, 1, d), _batch_map(t, seq_len, nb)),
            pl.BlockSpec((d, total), lambda i: (0, 0)),
        ],
        out_specs=[pl.BlockSpec((t, w), row_map) for w in _PROJ_SEGS],
        out_shape=[jax.ShapeDtypeStruct((n, w), dt) for w, dt in zip(_PROJ_SEGS, _PROJ_DTYPES)],
        compiler_params=_params("parallel"),
        name="in_proj",
    )(h2d, g, shift, scale, w_perm)


def _permute_w_in(w_in):
    o = 0
    z = w_in[:, o:o + SSD_WIDTH]; o += SSD_WIDTH
    xbc = w_in[:, o:o + SSD_XBC]; o += SSD_XBC
    dt = w_in[:, o:o + 2 * SSD_HEADS]; o += 2 * SSD_HEADS
    rest = w_in[:, o:]
    dt = jnp.pad(dt, ((0, 0), (0, LANES - 2 * SSD_HEADS)))
    return jnp.concatenate([z, xbc, rest, dt], axis=1).astype(BF16)


def _conv_kernel(x_ref, w_ref, b_ref, o_ref):
    x = x_ref[0]
    n = x.shape[0]
    row = lax.broadcasted_iota(jnp.int32, x.shape, 0)
    prev = jnp.where(row == 0, 0.0, pltpu.roll(x, 1, 0))
    nxt = jnp.where(row == n - 1, 0.0, pltpu.roll(x, n - 1, 0))
    y = prev * w_ref[0:1, :] + x * w_ref[1:2, :] + nxt * w_ref[2:3, :] + b_ref[...]
    o_ref[0] = _silu(y)


def _conv_silu(xbc3, conv_w, conv_b):
    b, l, c = xbc3.shape
    tc = 256
    return pl.pallas_call(
        _conv_kernel,
        grid=(b, c // tc),
        in_specs=[
            pl.BlockSpec((1, l, tc), lambda i, j: (i, 0, j)),
            pl.BlockSpec((3, tc), lambda i, j: (0, j)),
            pl.BlockSpec((1, tc), lambda i, j: (0, j)),
        ],
        out_specs=pl.BlockSpec((1, l, tc), lambda i, j: (i, 0, j)),
        out_shape=jax.ShapeDtypeStruct((b, l, c), F32),
        compiler_params=_params("parallel", "parallel"),
        name="dwconv_silu",
    )(xbc3, conv_w, conv_b.reshape(1, c))


def _ssd_chunk(reverse, xbc_ref, dt_ref, dtt_ref, dtb_row, dtb_col, alog_row, alog_col, y_ref, state_scr):
    q = SSD_CHUNK
    col0 = SSD_HEADS if reverse else 0
    dt_l = _softplus(dt_ref[0] + dtb_row[...])
    a_l = dt_l * (-jnp.exp(alog_row[...]))
    dt_t = _softplus(dtt_ref[0] + dtb_col[...])
    a_t = dt_t * (-jnp.exp(alog_col[...]))
    row = lax.broadcasted_iota(jnp.int32, (q, q), 0)
    col = lax.broadcasted_iota(jnp.int32, (q, q), 1)
    lower = (row >= col).astype(F32)
    upper = (row <= col).astype(F32)
    cs_l = jnp.dot(lower, a_l, precision=HIGHEST, preferred_element_type=F32)
    cs_t = jnp.dot(a_t, upper, precision=HIGHEST, preferred_element_type=F32)
    tot_l = cs_l[q - 1:q, :]
    if reverse:
        p_l, p_t = cs_l - a_l, cs_t - a_t
        tri = row <= col
    else:
        p_l, p_t = cs_l, cs_t
        tri = row >= col

    tot_t = cs_t[:, q - 1:q]
    to_end_t = jnp.exp(p_t) if reverse else jnp.exp(tot_t - p_t)
    w_t = dt_t * to_end_t
    left = lax.broadcasted_iota(jnp.int32, (q, 2 * SSD_HEAD_DIM), 1) < SSD_HEAD_DIM

    xbc = xbc_ref[0]
    heads_per_group = SSD_HEADS // SSD_GROUPS
    for g in range(SSD_GROUPS):
        b_g = xbc[:, SSD_WIDTH + g * SSD_STATE:SSD_WIDTH + (g + 1) * SSD_STATE]
        c_g = xbc[:, SSD_WIDTH + (SSD_GROUPS + g) * SSD_STATE:SSD_WIDTH + (SSD_GROUPS + g + 1) * SSD_STATE]
        c_bf = c_g.astype(BF16)
        cb = _dot_nt(c_bf, b_g.astype(BF16))
        bt = b_g.T
        for pair in range(g * heads_per_group // 2, (g + 1) * heads_per_group // 2):
            lanes = slice(pair * 2 * SSD_HEAD_DIM, (pair + 1) * 2 * SSD_HEAD_DIM)
            xp = xbc[:, lanes]
            x_bd = jnp.concatenate([jnp.where(left, xp, 0.0), jnp.where(left, 0.0, xp)], axis=0).astype(BF16)
            decay_tiles, state_tiles, pcol_tiles = [], [], []
            for k in (col0 + 2 * pair, col0 + 2 * pair + 1):
                pcol_b = jnp.broadcast_to(p_l[:, k:k + 1], (q, q))
                prow = p_t[k:k + 1, :]
                seg = (prow - pcol_b) if reverse else (pcol_b - prow)
                lmat = jnp.exp(jnp.where(tri, seg, -jnp.inf)) * dt_t[k:k + 1, :]
                decay_tiles.append((cb * lmat).astype(BF16))
                state_tiles.append((bt * w_t[k:k + 1, :]).astype(BF16))
                pcol_tiles.append(pcol_b)
            k0 = col0 + 2 * pair
            tot_pair = jnp.where(left[0:1], tot_l[:, k0:k0 + 1], tot_l[:, k0 + 1:k0 + 2])
            pcol_pair = jnp.where(left, pcol_tiles[0], pcol_tiles[1])
            in_decay = jnp.exp(tot_pair - pcol_pair) if reverse else jnp.exp(pcol_pair)
            s_prev = state_scr[pair]
            y_diag = jnp.dot(jnp.concatenate(decay_tiles, axis=1), x_bd, preferred_element_type=F32)
            y_off = jnp.dot(c_bf, s_prev.astype(BF16), preferred_element_type=F32) * in_decay
            state_scr[pair] = jnp.exp(tot_pair) * s_prev + jnp.dot(jnp.concatenate(state_tiles, axis=1), x_bd,
                                                                   preferred_element_type=F32)
            y_ref[0, :, lanes] = y_diag + y_off


def _ssd_kernel(xf_ref, dtf_ref, dttf_ref, xb_ref, dtb_ref, dttb_ref, dtb_row, dtb_col, alog_row, alog_col,
                initf_ref, initb_ref, yf_ref, yb_ref, finf_ref, finb_ref, statef_scr, stateb_scr):
    c = pl.program_id(1)

    @pl.when(c == 0)
    def _():
        statef_scr[...] = initf_ref[0]
        stateb_scr[...] = initb_ref[0]

    params = (dtb_row, dtb_col, alog_row, alog_col)
    _ssd_chunk(False, xf_ref, dtf_ref, dttf_ref, *params, yf_ref, statef_scr)
    _ssd_chunk(True, xb_ref, dtb_ref, dttb_ref, *params, yb_ref, stateb_scr)

    @pl.when(c == pl.num_programs(1) - 1)
    def _():
        finf_ref[0] = statef_scr[...]
        finb_ref[0] = stateb_scr[...]


def _ssd_scan(xbc_act, dt3, dtt3, dtb, alog, init_f, init_b):
    b, l, _ = xbc_act.shape
    nc = l // SSD_CHUNK
    dtb_row = jnp.pad(dtb.reshape(1, -1), ((0, 0), (0, LANES - 2 * SSD_HEADS)))
    alog_row = jnp.pad(alog.reshape(1, -1), ((0, 0), (0, LANES - 2 * SSD_HEADS)))
    small = lambda shape: pl.BlockSpec(shape, lambda i, c: (0, 0))
    st_shape = (b, SSD_HEADS // 2, SSD_STATE, 2 * SSD_HEAD_DIM)
    st_spec = pl.BlockSpec((1,) + st_shape[1:], lambda i, c: (i, 0, 0, 0))
    fwd, bwd = (lambda c: c), (lambda c: nc - 1 - c)
    chunk_specs = lambda cm: [
        pl.BlockSpec((1, SSD_CHUNK, SSD_XBC), lambda i, c: (i, cm(c), 0)),
        pl.BlockSpec((1, SSD_CHUNK, LANES), lambda i, c: (i, cm(c), 0)),
        pl.BlockSpec((1, 2 * SSD_HEADS, SSD_CHUNK), lambda i, c: (i, 0, cm(c))),
    ]
    y_spec = lambda cm: pl.BlockSpec((1, SSD_CHUNK, SSD_WIDTH), lambda i, c: (i, cm(c), 0))
    y_shape = jax.ShapeDtypeStruct((b, l, SSD_WIDTH), F32)
    return pl.pallas_call(
        _ssd_kernel,
        grid=(b, nc),
        in_specs=chunk_specs(fwd) + chunk_specs(bwd)
        + [small((1, LANES)), small((2 * SSD_HEADS, 1)), small((1, LANES)), small((2 * SSD_HEADS, 1)), st_spec, st_spec],
        out_specs=[y_spec(fwd), y_spec(bwd), st_spec, st_spec],
        out_shape=[y_shape, y_shape, jax.ShapeDtypeStruct(st_shape, F32), jax.ShapeDtypeStruct(st_shape, F32)],
        scratch_shapes=[pltpu.VMEM(st_shape[1:], F32), pltpu.VMEM(st_shape[1:], F32)],
        compiler_params=_params("parallel", "arbitrary"),
        name="ssd_scan",
    )(xbc_act, dt3, dtt3, xbc_act, dt3, dtt3, dtb_row, dtb.reshape(-1, 1), alog_row, alog.reshape(-1, 1),
      init_f, init_b)


def _pool_kernel(x_ref, w_ref, sc_ref, o_ref, pad_scr):
    n = x_ref.shape[1]
    zeros = jnp.zeros((POOL_PAD, POOL_GROUP_DIM), F32)
    pad_scr[0:POOL_PAD, :] = zeros
    pad_scr[n + POOL_PAD:n + 2 * POOL_PAD, :] = zeros
    t = lax.broadcasted_iota(jnp.int32, (n, 1), 0)
    for gi, w in enumerate(POOL_WINDOWS):
        sl = slice(gi * POOL_GROUP_DIM, (gi + 1) * POOL_GROUP_DIM)
        x = x_ref[0, :, sl]
        pad_scr[POOL_PAD:n + POOL_PAD, :] = x
        acc = jnp.zeros_like(x)
        for o in range(-(w // 2), w - w // 2):
            acc = acc + pad_scr[POOL_PAD + o:POOL_PAD + o + n, :]
        lo = jnp.maximum(t - w // 2, 0)
        hi = jnp.minimum(t + (w - w // 2 - 1), n - 1)
        pooled = acc / (hi - lo + 1).astype(F32) - x
        y = jnp.dot(pooled.astype(BF16), w_ref[gi].astype(BF16), preferred_element_type=F32)
        o_ref[0, :, sl] = (y * sc_ref[:, sl]).astype(o_ref.dtype)


def _pool_mixer(u3, w_pool, scale):
    b, l, c = u3.shape
    return pl.pallas_call(
        _pool_kernel,
        grid=(b,),
        in_specs=[
            pl.BlockSpec((1, l, c), lambda i: (i, 0, 0)),
            pl.BlockSpec(w_pool.shape, lambda i: (0, 0, 0)),
            pl.BlockSpec((1, c), lambda i: (0, 0)),
        ],
        out_specs=pl.BlockSpec((1, l, c), lambda i: (i, 0, 0)),
        out_shape=jax.ShapeDtypeStruct((b, l, c), BF16),
        scratch_shapes=[pltpu.VMEM((l + 2 * POOL_PAD, POOL_GROUP_DIM), F32)],
        compiler_params=_params("parallel"),
        name="pool_mixer",
    )(u3, w_pool, scale.reshape(1, c))


def _na_bias_kernel(rpb_ref, o_ref):
    h = pl.program_id(0)
    qi = lax.broadcasted_iota(jnp.int32, (GRID_W, LANES), 0)
    lane = lax.broadcasted_iota(jnp.int32, (GRID_W, LANES), 1)
    ki = lane % GRID_W
    second = lane >= GRID_W
    start = jnp.clip(qi - NA_KW // 2, 0, GRID_W - NA_KW)
    in_window = (ki >= start) & (ki < start + NA_KW)
    dc = jnp.clip(ki - qi, -(NA_KW - 1), NA_KW - 1) + NA_KW - 1
    for dr in range(o_ref.shape[1]):
        val = jnp.zeros((GRID_W, LANES), F32)
        for j in range(2 * NA_KW - 1):
            pick = jnp.where(second, rpb_ref[h, dr + 1, j], rpb_ref[h, dr, j])
            val = jnp.where(dc == j, pick, val)
        o_ref[0, dr] = jnp.where(in_window, val, -jnp.inf)


def _na_bias(rpb):
    nh, ndr, ndc = rpb.shape
    return pl.pallas_call(
        _na_bias_kernel,
        grid=(nh,),
        in_specs=[pl.BlockSpec(memory_space=pltpu.SMEM)],
        out_specs=pl.BlockSpec((1, ndr - 1, GRID_W, LANES), lambda h: (h, 0, 0, 0)),
        out_shape=jax.ShapeDtypeStruct((nh, ndr - 1, GRID_W, LANES), F32),
        compiler_params=_params("parallel"),
        name="na_bias",
    )(rpb)


def _na_kernel(kh, q_ref, k_ref, v_ref, kc_ref, vc_ref, bias_ref, o_ref):
    r = pl.program_id(1)
    rows = pl.num_programs(1)
    r0 = jnp.clip(r - kh // 2, 0, rows - kh)
    start = pl.multiple_of(r0 * GRID_W, GRID_W)
    kblk = k_ref[0, pl.ds(start, kh * GRID_W), :]
    vblk = v_ref[0, pl.ds(start, kh * GRID_W), :]
    dr0 = r0 - r + NA_KH - 1
    scale = NA_HEAD_DIM ** -0.5
    pair_w = 2 * NA_HEAD_DIM
    halves = []
    for pair in range(NA_HEADS // 2):
        lanes = slice(pair * pair_w, (pair + 1) * pair_w)
        left_q = lax.broadcasted_iota(jnp.int32, (GRID_W, pair_w), 1) < NA_HEAD_DIM
        q_pair = q_ref[0, :, lanes] * scale
        k_pair, kc_pair = kblk[:, lanes], kc_ref[0, :, lanes]
        for side in range(2):
            keep = left_q if side == 0 else jnp.logical_not(left_q)
            q_h = jnp.where(keep, q_pair, 0.0)
            h = 2 * pair + side
            bias = jnp.concatenate([bias_ref[h, dr0 + 2 * j] for j in range(kh // 2)], axis=1)
            halves.append((_dot_nt(q_h, k_pair) + bias, _dot_nt(q_h, kc_pair)))
    probs = []
    for s_loc, s_ctx in halves:
        m = jnp.maximum(jnp.max(s_loc, axis=-1, keepdims=True), jnp.max(s_ctx, axis=-1, keepdims=True))
        p_loc = jnp.exp(s_loc - m)
        p_ctx = jnp.exp(s_ctx - m)
        inv = 1.0 / (jnp.sum(p_loc, axis=-1, keepdims=True) + jnp.sum(p_ctx, axis=-1, keepdims=True))
        probs.append((p_loc.astype(BF16), p_ctx.astype(BF16), inv))
    for pair in range(NA_HEADS // 2):
        lanes = slice(pair * pair_w, (pair + 1) * pair_w)
        v_pair, vc_pair = vblk[:, lanes], vc_ref[0, :, lanes]
        left_v = lax.broadcasted_iota(jnp.int32, v_pair.shape, 1) < NA_HEAD_DIM
        left_c = lax.broadcasted_iota(jnp.int32, vc_pair.shape, 1) < NA_HEAD_DIM
        out = None
        for side in range(2):
            p_loc, p_ctx, inv = probs[2 * pair + side]
            keep_v = left_v if side == 0 else jnp.logical_not(left_v)
            keep_c = left_c if side == 0 else jnp.logical_not(left_c)
            acc = jnp.dot(p_loc, jnp.where(keep_v, v_pair, 0.0), preferred_element_type=F32)
            acc = acc + jnp.dot(p_ctx, jnp.where(keep_c, vc_pair, 0.0), preferred_element_type=F32)
            out = acc * inv if out is None else out + acc * inv
        o_ref[0, :, lanes] = out.astype(o_ref.dtype)


def _neighbourhood_attention(q3, k3, v3, kc3, vc3, bias):
    b, s, c = q3.shape
    rows = s // GRID_W
    kh = min(NA_KH, rows)
    lc = kc3.shape[1]
    full = lambda n: pl.BlockSpec((1, n, c), lambda i, r: (i, 0, 0))
    return pl.pallas_call(
        functools.partial(_na_kernel, kh),
        grid=(b, rows),
        in_specs=[
            pl.BlockSpec((1, GRID_W, c), lambda i, r: (i, r, 0)),
            full(s), full(s), full(lc), full(lc),
            pl.BlockSpec(bias.shape, lambda i, r: (0, 0, 0, 0)),
        ],
        out_specs=pl.BlockSpec((1, GRID_W, c), lambda i, r: (i, r, 0)),
        out_shape=jax.ShapeDtypeStruct((b, s, c), BF16),
        compiler_params=_params("parallel", "arbitrary"),
        name="na_attention",
    )(q3, k3, v3, kc3, vc3, bias)


def _ctx_attn_kernel(q_ref, k_ref, v_ref, o_ref):
    scale = NA_HEAD_DIM ** -0.5
    for h in range(NA_HEADS):
        sl = slice(h * NA_HEAD_DIM, (h + 1) * NA_HEAD_DIM)
        s = _dot_nt(q_ref[0, :, sl] * scale, k_ref[0, :, sl])
        p = jnp.exp(s - jnp.max(s, axis=-1, keepdims=True))
        den = jnp.sum(p, axis=-1, keepdims=True)
        o_ref[0, :, sl] = (jnp.dot(p.astype(BF16), v_ref[0, :, sl], preferred_element_type=F32) / den).astype(o_ref.dtype)


def _context_attention(q3, k3, v3):
    b, l, c = q3.shape
    spec = pl.BlockSpec((1, l, c), lambda i: (i, 0, 0))
    return pl.pallas_call(
        _ctx_attn_kernel,
        grid=(b,),
        in_specs=[spec, spec, spec],
        out_specs=spec,
        out_shape=jax.ShapeDtypeStruct((b, l, c), BF16),
        compiler_params=_params("parallel"),
        name="ctx_attention",
    )(q3, k3, v3)


def _mix_out_kernel(yf_ref, yb_ref, xs_ref, z_ref, pool_ref, na_ref, h_ref, dsk_ref, ng_ref, g1_ref, w_ref, o_ref):
    y = yf_ref[...] + yb_ref[...] + dsk_ref[...] * xs_ref[...]
    y = y * _silu(z_ref[...])
    ms = jnp.mean(y * y, axis=-1, keepdims=True)
    yn = (y * lax.rsqrt(ms + RMS_EPS)) * ng_ref[...]
    mix = jnp.dot(yn.astype(BF16), w_ref[0:SSD_WIDTH, :], preferred_element_type=F32)
    mix = mix + jnp.dot(pool_ref[...].astype(BF16), w_ref[SSD_WIDTH:SSD_WIDTH + POOL_WIDTH, :],
                        preferred_element_type=F32)
    mix = mix + jnp.dot(na_ref[...].astype(BF16), w_ref[SSD_WIDTH + POOL_WIDTH:, :], preferred_element_type=F32)
    o_ref[...] = h_ref[...] + g1_ref[0] * mix


def _mix_out(yf, yb, xbc_act2d, z, pool_y, na_y, h2d, seq_len, d_skip, norm_g, g1, w_out_bf):
    n, d = h2d.shape
    t = 256
    row = lambda w: pl.BlockSpec((t, w), lambda i: (i, 0))
    vec = lambda w: pl.BlockSpec((1, w), lambda i: (0, 0))
    return pl.pallas_call(
        _mix_out_kernel,
        grid=(n // t,),
        in_specs=[
            row(SSD_WIDTH), row(SSD_WIDTH), row(SSD_WIDTH), row(SSD_WIDTH), row(POOL_WIDTH), row(NA_WIDTH), row(d),
            vec(SSD_WIDTH), vec(SSD_WIDTH),
            pl.BlockSpec((1, 1, d), _batch_map(t, seq_len, g1.shape[0])),
            pl.BlockSpec(w_out_bf.shape, lambda i: (0, 0)),
        ],
        out_specs=row(d),
        out_shape=jax.ShapeDtypeStruct((n, d), F32),
        compiler_params=_params("parallel"),
        name="mix_out",
    )(yf, yb, xbc_act2d, z, pool_y, na_y, h2d, jnp.repeat(d_skip, SSD_HEAD_DIM).reshape(1, -1),
      norm_g.reshape(1, -1), g1, w_out_bf)


_CAND_ROWS = 16 + 8 * 7 + 8


def _batcher_pairs(n):
    pairs = []

    def merge(lo, m, r):
        step = 2 * r
        if step < m:
            merge(lo, m, step)
            merge(lo + r, m, step)
            pairs.extend((i, i + r) for i in range(lo + r, lo + m - r, step))
        else:
            pairs.append((lo, lo + r))

    def sort(lo, m):
        if m > 1:
            sort(lo, m // 2)
            sort(lo + m // 2, m // 2)
            merge(lo, m, 1)

    sort(0, n)
    return tuple(pairs)


_SORT16 = _batcher_pairs(PEER_TOPK)
_BITONIC16 = tuple((i, i + d) for d in (8, 4, 2, 1) for i in range(PEER_TOPK) if not i & d)


def _exchange(x, pairs):
    for i, j in pairs:
        x[i], x[j] = jnp.maximum(x[i], x[j]), jnp.minimum(x[i], x[j])


def _sorted_top16(s):
    tiles = []
    for l0 in range(0, s.shape[1], LANES):
        x = [s[8 * v:8 * v + 8, l0:l0 + LANES] for v in range(PEER_TOPK)]
        _exchange(x, _SORT16)
        for shift in (4, 2, 1):
            x = [jnp.maximum(x[i], pltpu.roll(x[PEER_TOPK - 1 - i], shift, 0)) for i in range(PEER_TOPK)]
            _exchange(x, _BITONIC16)
        tiles.append(jnp.concatenate([xi[0:1] for xi in x], axis=0))
    return jnp.concatenate(tiles, axis=1)


def _count_above(s, v):
    r = lambda j: v[j:j + 1]
    pick = jnp.where
    a = s < r(7)
    b = s < pick(a, r(11), r(3))
    c = s < pick(a, pick(b, r(13), r(9)), pick(b, r(5), r(1)))
    d = s < pick(a, pick(b, pick(c, r(14), r(12)), pick(c, r(10), r(8))),
                 pick(b, pick(c, r(6), r(4)), pick(c, r(2), r(0))))
    return (pick(a, 8.0, 0.0) + pick(b, 4.0, 0.0) + pick(c, 2.0, 0.0) + pick(d, 1.0, 0.0)
            + pick(s < r(15), 1.0, 0.0))


def _top16_pair_fast(s1, s2):
    v1, v2 = _sorted_top16(s1), _sorted_top16(s2)
    rank2 = _count_above(s2, v2)

    def check(s, v):
        n = jnp.sum(jnp.where(s >= v[PEER_TOPK - 1:PEER_TOPK], 1.0, 0.0), axis=0, keepdims=True)
        strict = jnp.min(v[:-1] - v[1:], axis=0, keepdims=True) > 0.0
        return jnp.where(strict, n, 0.0)

    return v1, check(s1, v1), v2, rank2, check(s2, v2)


def _first_counts_fast(s1, v1, sel):
    inf = jnp.inf
    rows_v1 = jnp.concatenate([jnp.broadcast_to(v1[0:1], (PEER_TOPK, v1.shape[1]))]
                              + [jnp.broadcast_to(v1[j:j + 1], (8, v1.shape[1])) for j in range(1, 8)]
                              + [v1[8:16]], axis=0)
    bound = jnp.where(sel > 0.0, rows_v1, inf)
    low = bound[0:8]
    for j in range(1, 8):
        low = jnp.minimum(low, bound[16 + 8 * (j - 1):16 + 8 * j])
    tail = jnp.min(bound[64 + 8:64 + 16], axis=0, keepdims=True)
    t = [jnp.minimum(low[0:1], tail)] + [low[k:k + 1] for k in range(1, 8)]
    pick = jnp.where
    a = s1 >= t[3]
    b = s1 >= pick(a, t[5], t[1])
    c = s1 >= pick(a, pick(b, t[6], t[4]), pick(b, t[2], t[0]))
    cnt = pick(a, 4.0, 0.0) + pick(b, 2.0, 0.0) + pick(c, 1.0, 0.0) + pick(s1 >= t[7], 1.0, 0.0)
    n_high = jnp.sum(sel[8:16], axis=0, keepdims=True)
    return cnt + jnp.where(s1 >= v1[0:1], n_high, 0.0)


def _first_counts_exact(rank1, sel):
    cnt = jnp.zeros(rank1.shape, F32)
    for j in range(8):
        lo = 0 if j == 0 else 16 + 8 * (j - 1)
        n_j = jnp.sum(sel[lo:lo + (16 if j == 0 else 8)], axis=0, keepdims=True)
        cnt = cnt + jnp.where(rank1 == float(j), n_j, 0.0)
    for j in range(8, 16):
        cnt = cnt + jnp.where(rank1 == float(j), sel[64 + j:65 + j], 0.0)
    return cnt


def _top16_pair_exact(s1, s2):
    n, t = s1.shape
    iota = lax.broadcasted_iota(jnp.int32, (n, t), 0).astype(F32)
    row16 = lax.broadcasted_iota(jnp.int32, (PEER_TOPK, t), 0)

    def pick(work, rank, vals, j):
        m = jnp.max(work, axis=0, keepdims=True)
        idx = jnp.min(jnp.where(work == m, iota, float(n)), axis=0, keepdims=True)
        sel = iota == idx
        return (jnp.where(sel, -jnp.inf, work), jnp.where(sel, lax.convert_element_type(j, F32), rank),
                jnp.where(row16 == j, m, vals))

    def body(j, carry):
        a, b = carry
        return pick(*a, j), pick(*b, j)

    start = lambda s: (s, jnp.full((n, t), float(PEER_TOPK), F32), jnp.zeros((PEER_TOPK, t), F32))
    (_, rank1, v1), (_, rank2, v2) = lax.fori_loop(0, PEER_TOPK, body, (start(s1), start(s2)))
    return rank1, v1, rank2, v2


def _select16_fast(cand):
    rows, t = cand.shape
    padded = jnp.concatenate([cand, jnp.full((PEER_NKEYS - rows, t), -jnp.inf, F32)], axis=0)
    m = _sorted_top16(padded)[PEER_TOPK - 1:PEER_TOPK]
    taken = jnp.where(cand >= m, 1.0, 0.0)
    return taken, jnp.sum(taken, axis=0, keepdims=True)


def _select16_exact(cand):
    iota = lax.broadcasted_iota(jnp.int32, cand.shape, 0).astype(F32)

    def body(_, carry):
        work, sel_acc = carry
        m = jnp.max(work, axis=0, keepdims=True)
        idx = jnp.min(jnp.where(work == m, iota, float(_CAND_ROWS)), axis=0, keepdims=True)
        sel = iota == idx
        return jnp.where(sel, -jnp.inf, work), jnp.where(sel, 1.0, sel_acc)

    return lax.fori_loop(0, PEER_TOPK, body, (cand, jnp.zeros(cand.shape, F32)))[1]


def _candidate_sums(v1, v2):
    blocks = [v1[0:1] + v2]
    for j in range(1, 8):
        blocks.append(v1[j:j + 1] + v2[0:8])
    blocks.append(v1[8:16] + v2[0:1])
    return jnp.concatenate(blocks, axis=0)


def _any_not_16(*counts):
    return jnp.max(sum(jnp.abs(c - float(PEER_TOPK)) for c in counts)) > 0.0


def _bf16_bits(x):
    return pltpu.bitcast(x.astype(BF16).astype(F32), jnp.uint32)


def _pack_row_pairs(x, scr):
    n, t = x.shape
    for j in range(t // LANES):
        scr[j] = x[:, j * LANES:(j + 1) * LANES]
    words = []
    for j in range(t // LANES):
        even = scr[j, pl.ds(0, n // 2, stride=2), :]
        odd = scr[j, pl.ds(1, n // 2, stride=2), :]
        words.append((_bf16_bits(even) >> 16) | _bf16_bits(odd))
    return jnp.concatenate(words, axis=1)


def _pack_same(x):
    w = _bf16_bits(x)
    return w | (w >> 16)


_F8_TARGET_EXP = 6


def _pow2_scale(amax):
    bits = pltpu.bitcast(jnp.maximum(amax, 2.0 ** -100), jnp.int32)
    exponent = (bits >> 23) - 127
    return pltpu.bitcast((_F8_TARGET_EXP - exponent + 127) << 23, F32)


def _peer_score_kernel(h_ref, g_ref, sh_ref, sc_ref, wq_ref, wq_next_ref, keys_ref,
                       x_out, xinv_out, cnt_out, e1_out, rk_out, e2_out,
                       u_scr, q_scr, pair_scr, cnt_scr, rank2_scr, top_scr):
    hd = pl.program_id(1)

    @pl.when(hd == 0)
    def _():
        u = _rms_mod(h_ref[...], g_ref[...], sc_ref[0], sh_ref[0])
        u_scr[...] = u.astype(BF16)
        scale = _pow2_scale(jnp.max(jnp.abs(u), axis=-1, keepdims=True))
        x_out[...] = (u * scale).astype(F8)
        xinv_out[...] = 1.0 / scale
        q_scr[...] = jnp.dot(u_scr[...], wq_ref[0], preferred_element_type=F32).astype(BF16)

    q = q_scr[...]
    s1 = _dot_nt(keys_ref[0, 0], q[:, :PEER_SUB])
    s2 = _dot_nt(keys_ref[0, 1], q[:, PEER_SUB:])
    v1, n1, v2, rank2, n2 = _top16_pair_fast(s1, s2)
    cand = _candidate_sums(v1, v2)
    sel, n_sel = _select16_fast(cand)
    cnt_scr[...] = _first_counts_fast(s1, v1, sel)
    rank2_scr[...] = rank2
    top_scr[0:1] = v1[0:1]
    top_scr[1:2] = v2[0:1]
    top_scr[2:3] = jnp.sum(sel * jnp.exp(cand - cand[0:1]), axis=0, keepdims=True)
    q_scr[...] = jnp.dot(u_scr[...], wq_next_ref[0], preferred_element_type=F32).astype(BF16)

    @pl.when(_any_not_16(n1, n2, n_sel))
    def _():
        rank1_x, v1_x, rank2_x, v2_x = _top16_pair_exact(s1, s2)
        cand_x = _candidate_sums(v1_x, v2_x)
        sel_x = _select16_exact(cand_x)
        cnt_scr[...] = _first_counts_exact(rank1_x, sel_x)
        rank2_scr[...] = rank2_x
        top_scr[0:1] = v1_x[0:1]
        top_scr[1:2] = v2_x[0:1]
        top_scr[2:3] = jnp.sum(sel_x * jnp.exp(cand_x - cand_x[0:1]), axis=0, keepdims=True)

    cnt_out[0] = _pack_same(cnt_scr[...])
    e1_out[0] = _pack_same(jnp.exp(s1 - top_scr[0:1]))
    rk_out[0] = _pack_row_pairs(rank2_scr[...], pair_scr)
    e2_out[0] = _pack_row_pairs(jnp.exp(s2 - top_scr[1:2]) / top_scr[2:3], pair_scr)


def _peer_scores(h2d, seq_len, g, shift, scale, wq_heads, keys_bf):
    n, d = h2d.shape
    nb = shift.shape[0]
    t = 512
    assert n % t == 0 and (nb == 1 or seq_len % t == 0)
    bm = _batch_map(t, seq_len, nb)
    mod_spec = pl.BlockSpec((1, 1, d), lambda i, hd: bm(i))
    first_out = pl.BlockSpec((1, PEER_NKEYS, t), lambda i, hd: (hd, 0, i))
    first_shape = jax.ShapeDtypeStruct((PEER_HEADS, PEER_NKEYS, n), jnp.uint32)
    second_out = pl.BlockSpec((1, PEER_NKEYS // 2, t), lambda i, hd: (hd, 0, i))
    second_shape = jax.ShapeDtypeStruct((PEER_HEADS, PEER_NKEYS // 2, n), jnp.uint32)
    return pl.pallas_call(
        _peer_score_kernel,
        grid=(n // t, PEER_HEADS),
        in_specs=[
            pl.BlockSpec((t, d), lambda i, hd: (i, 0)),
            pl.BlockSpec((1, d), lambda i, hd: (0, 0)),
            mod_spec, mod_spec,
            pl.BlockSpec((1, d, 2 * PEER_SUB), lambda i, hd: (hd, 0, 0)),
            pl.BlockSpec((1, d, 2 * PEER_SUB), lambda i, hd: (jnp.minimum(hd + 1, PEER_HEADS - 1), 0, 0)),
            pl.BlockSpec((1, 2, PEER_NKEYS, PEER_SUB), lambda i, hd: (hd, 0, 0, 0)),
        ],
        out_specs=[pl.BlockSpec((t, d), lambda i, hd: (i, 0)), pl.BlockSpec((t, 1), lambda i, hd: (i, 0)),
                   first_out, first_out, second_out, second_out],
        out_shape=[jax.ShapeDtypeStruct((n, d), F8), jax.ShapeDtypeStruct((n, 1), F32),
                   first_shape, first_shape, second_shape, second_shape],
        scratch_shapes=[pltpu.VMEM((t, d), BF16), pltpu.VMEM((t, 2 * PEER_SUB), BF16),
                        pltpu.VMEM((t // LANES, PEER_NKEYS, LANES), F32),
                        pltpu.VMEM((PEER_NKEYS, t), F32), pltpu.VMEM((PEER_NKEYS, t), F32),
                        pltpu.VMEM((8, t), F32)],
        compiler_params=_params("parallel", "arbitrary"),
        name="peer_scores",
    )(h2d, g, shift, scale, wq_heads, wq_heads, keys_bf)


_PEER_EC = 1024
_INV_SQRT2 = 1.0 / math.sqrt(2.0)


def _as_bf16_rows(words):
    return pltpu.bitcast(words, BF16)


_PEER_A_GROUP = 4


def _peer_expert_kernel(n_chunks, final_norm, x_ref, u_ref, inv_ref, vt_ref, cnt_ref, e1_ref, rk_ref, e2_ref, h_ref,
                        g2_ref, fg_ref, o_ref,
                        ht0_scr, ht1_scr, g0_scr, g1_scr, acc_scr):
    s = pl.program_id(0)
    t = h_ref.shape[0]
    p2 = s - 2
    c2 = lax.rem(jnp.maximum(p2, 0), n_chunks)
    pack = 2 * 8
    n_a = _PEER_EC // PEER_NKEYS

    @pl.when(s == 0)
    def _():
        ht1_scr[...] = jnp.zeros_like(ht1_scr)
        g0_scr[...] = jnp.zeros_like(g0_scr)
        g1_scr[...] = jnp.zeros_like(g1_scr)

    @pl.when((p2 <= 0) | (c2 == 0))
    def _():
        acc_scr[...] = jnp.zeros_like(acc_scr)

    def step(ht_new, ht_old, g_new, g_old):
        for tc in range(t // LANES):
            ls = slice(tc * LANES, (tc + 1) * LANES)
            half_inv = 0.5 * inv_ref[:, ls]
            c_inv = _INV_SQRT2 * inv_ref[:, ls]
            for a0 in range(0, n_a, _PEER_A_GROUP):
                w = [[None] * (PEER_NKEYS // pack) for _ in range(_PEER_A_GROUP)]
                for hd in range(PEER_HEADS):
                    rows1 = [(_as_bf16_rows(jnp.broadcast_to(cnt_ref[hd, a0 + i:a0 + i + 1, ls], (8, LANES))),
                              _as_bf16_rows(jnp.broadcast_to(e1_ref[hd, a0 + i:a0 + i + 1, ls], (8, LANES))))
                             for i in range(_PEER_A_GROUP)]
                    for r in range(PEER_NKEYS // pack):
                        rk = _as_bf16_rows(rk_ref[hd, r * 8:(r + 1) * 8, ls])
                        e2 = _as_bf16_rows(e2_ref[hd, r * 8:(r + 1) * 8, ls])
                        for i, (cn, e1) in enumerate(rows1):
                            term = jnp.where(rk < cn, e2, 0.0) * e1
                            w[i][r] = term if w[i][r] is None else w[i][r] + term
                for i in range(_PEER_A_GROUP):
                    for r in range(PEER_NKEYS // pack):
                        row0 = (a0 + i) * PEER_NKEYS + r * pack
                        hs = ht_old[row0:row0 + pack, ls]
                        act = (hs * half_inv) * (1.0 + lax.erf(hs * c_inv))
                        g_new[row0 // 2:(row0 + pack) // 2, ls] = pltpu.bitcast(w[i][r] * act.astype(BF16), jnp.uint32)
        ht_new[...] = _dot_nt(u_ref[...], x_ref[...])
        acc_scr[...] += jnp.dot(_as_bf16_rows(vt_ref[...]), _as_bf16_rows(g_old[...]),
                                preferred_element_type=F32)

    @pl.when(s % 2 == 0)
    def _():
        step(ht0_scr, ht1_scr, g1_scr, g0_scr)

    @pl.when(s % 2 == 1)
    def _():
        step(ht1_scr, ht0_scr, g0_scr, g1_scr)

    @pl.when((p2 >= 0) & (c2 == n_chunks - 1))
    def _():
        y = h_ref[...] + g2_ref[0] * acc_scr[...].T
        if final_norm:
            y = (y * lax.rsqrt(jnp.mean(y * y, axis=-1, keepdims=True) + RMS_EPS)) * fg_ref[...]
        o_ref[...] = y


def _peer_experts(x8, u8, inv_row, vt_pk, cnt, e1, rk, e2, h2d, seq_len, g2, final_g, final_norm):
    n, d = h2d.shape
    t = 512
    n_chunks = u8.shape[0] // _PEER_EC
    total = (n // t) * n_chunks
    ea = _PEER_EC // PEER_NKEYS

    def pair(p):
        p = jnp.clip(p, 0, total - 1)
        return p // n_chunks, lax.rem(p, n_chunks)

    blk = lambda lag: (lambda s: pair(s - lag)[0])
    chk = lambda lag: (lambda s: pair(s - lag)[1])
    tok = pl.BlockSpec((PEER_HEADS, PEER_NKEYS // 2, t), lambda s: (0, 0, blk(1)(s)))
    first = pl.BlockSpec((PEER_HEADS, ea, t), lambda s: (0, chk(1)(s), blk(1)(s)))
    bm = _batch_map(t, seq_len, g2.shape[0])
    return pl.pallas_call(
        functools.partial(_peer_expert_kernel, n_chunks, final_norm),
        grid=(total + 2,),
        in_specs=[
            pl.BlockSpec((t, d), lambda s: (blk(0)(s), 0)),
            pl.BlockSpec((_PEER_EC, d), lambda s: (chk(0)(s), 0)),
            pl.BlockSpec((1, t), lambda s: (0, blk(1)(s))),
            pl.BlockSpec((d // 2, _PEER_EC), lambda s: (0, chk(2)(s))),
            first, first, tok, tok,
            pl.BlockSpec((t, d), lambda s: (blk(2)(s), 0)),
            pl.BlockSpec((1, 1, d), lambda s: bm(blk(2)(s))),
            pl.BlockSpec((1, d), lambda s: (0, 0)),
        ],
        out_specs=pl.BlockSpec((t, d), lambda s: (blk(2)(s), 0)),
        out_shape=jax.ShapeDtypeStruct((n, d), F32),
        scratch_shapes=[pltpu.VMEM((_PEER_EC, t), F32), pltpu.VMEM((_PEER_EC, t), F32),
                        pltpu.VMEM((_PEER_EC // 2, t), jnp.uint32), pltpu.VMEM((_PEER_EC // 2, t), jnp.uint32),
                        pltpu.VMEM((d, t), F32)],
        compiler_params=_params("arbitrary"),
        name="peer_experts",
    )(x8, u8, inv_row, vt_pk, cnt, e1, rk, e2, h2d, g2, final_g.reshape(1, d))


def _pack_rows_kernel(transpose, x_ref, o_ref):
    x = x_ref[0].T if transpose else x_ref[0]
    o_ref[...] = pltpu.bitcast(x.astype(BF16), jnp.uint32)


def _pack_bf16_rows(stack, layer, transpose=False):
    tile = 512
    if transpose:
        _, c, r = stack.shape
        in_spec = pl.BlockSpec((1, tile, r), lambda i: (layer, i, 0))
        out_spec = pl.BlockSpec((r // 2, tile), lambda i: (0, i))
        steps = c // tile
    else:
        _, r, c = stack.shape
        in_spec = pl.BlockSpec((1, tile, c), lambda i: (layer, i, 0))
        out_spec = pl.BlockSpec((tile // 2, c), lambda i: (i, 0))
        steps = r // tile
    return pl.pallas_call(
        functools.partial(_pack_rows_kernel, transpose),
        grid=(steps,),
        in_specs=[in_spec],
        out_specs=out_spec,
        out_shape=jax.ShapeDtypeStruct((r // 2, c), jnp.uint32),
        compiler_params=_params("parallel"),
        name="pack_bf16_rows_t" if transpose else "pack_bf16_rows",
    )(stack)


def _table_absmax_kernel(x_ref, o_ref):
    @pl.when(pl.program_id(0) == 0)
    def _():
        o_ref[...] = jnp.zeros_like(o_ref)

    m = jnp.max(jnp.max(jnp.abs(x_ref[0]), axis=0, keepdims=True), axis=1, keepdims=True)
    o_ref[...] = jnp.maximum(o_ref[...], m)


def _quantize_kernel(s_ref, x_ref, o_ref):
    o_ref[...] = (x_ref[0] * s_ref[0]).astype(F8)


def _quantize_table(stack, layer):
    _, r, c = stack.shape
    tile = 512
    in_spec = pl.BlockSpec((1, tile, c), lambda i: (layer, i, 0))
    amax = pl.pallas_call(
        _table_absmax_kernel,
        grid=(r // tile,),
        in_specs=[in_spec],
        out_specs=pl.BlockSpec((8, LANES), lambda i: (0, 0)),
        out_shape=jax.ShapeDtypeStruct((8, LANES), F32),
        compiler_params=_params("arbitrary"),
        name="table_absmax",
    )(stack)[0, 0]
    exponent = jnp.floor(jnp.log2(jnp.maximum(amax, 2.0 ** -100)))
    scale = jnp.exp2(_F8_TARGET_EXP - exponent)
    q = pl.pallas_call(
        _quantize_kernel,
        grid=(r // tile,),
        in_specs=[pl.BlockSpec(memory_space=pltpu.SMEM), in_spec],
        out_specs=pl.BlockSpec((tile, c), lambda i: (i, 0)),
        out_shape=jax.ShapeDtypeStruct((r, c), F8),
        compiler_params=_params("parallel"),
        name="quantize_table",
    )(scale.reshape(1), stack)
    return q, 1.0 / scale


def _peer_ffn_residual(h2d, seq_len, norm_g, shift, scale, gate, wq_heads, keys_bf, u8, u_inv, vt_pk, final_g,
                       final_norm=False):
    x8, x_inv, cnt, e1, rk, e2 = _peer_scores(h2d, seq_len, norm_g, shift, scale, wq_heads, keys_bf)
    inv_row = x_inv.reshape(1, -1) * u_inv
    return _peer_experts(x8, u8, inv_row, vt_pk, cnt, e1, rk, e2, h2d, seq_len, gate, final_g, final_norm)


def _mixer_inputs(h2d, batch, seq_len, norm_g, shift, scale, w_perm, conv_w, conv_b):
    z, xbc, pool_u, q, k, v, dt = _project(h2d, seq_len, norm_g, shift, scale, w_perm)
    xbc_act = _conv_silu(xbc.reshape(batch, seq_len, SSD_XBC), conv_w, conv_b)
    dt3 = dt.reshape(batch, seq_len, LANES)
    dtt3 = jnp.swapaxes(dt3[:, :, :2 * SSD_HEADS], 1, 2)
    r3 = lambda a: a.reshape(batch, seq_len, a.shape[-1])
    return z, xbc_act, dt3, dtt3, r3(pool_u), r3(q), r3(k), r3(v)


def kernel(x, c, ctx, c_ctx, ada_w, ada_b, norm1_g, w_in, conv_w, conv_b, a_log, dt_bias, d_skip, ssd_norm_g, pool_w, pool_scale, na_rpb, w_out, norm2_g, peer_wq, peer_keys, peer_u, peer_v, final_g):
    batch, seq, d = x.shape
    ctx_len = ctx.shape[1]
    n, nc = batch * seq, batch * ctx_len
    h = x.reshape(n, d)
    hc = ctx.reshape(nc, d)

    c8 = jnp.concatenate([c, c_ctx[None], jnp.zeros((8 - batch - 1, d), F32)], axis=0)
    mod = _modulation(c8, ada_w, ada_b)

    for i in range(DEPTH):
        need_ctx_out = i < DEPTH - 1
        lat = [mod[i, :batch, j * d:(j + 1) * d].reshape(batch, 1, d) for j in range(6)]
        cx = [mod[i, batch:batch + 1, j * d:(j + 1) * d].reshape(1, 1, d) for j in range(6)]
        sh1, sc1, g1, sh2, sc2, g2 = lat
        csh1, csc1, cg1, csh2, csc2, cg2 = cx

        w_perm = _permute_w_in(w_in[i])
        w_out_bf = w_out[i].astype(BF16)
        n1 = norm1_g[i].reshape(1, d)
        n2 = norm2_g[i].reshape(1, d)
        zero_state = jnp.zeros((batch, SSD_HEADS // 2, SSD_STATE, 2 * SSD_HEAD_DIM), F32)
        scan = functools.partial(_ssd_scan, dtb=dt_bias[i], alog=a_log[i])
        wq_heads = peer_wq[i].reshape(d, PEER_HEADS, 2 * PEER_SUB).transpose(1, 0, 2).astype(BF16)
        keys_bf = peer_keys[i].astype(BF16)
        u8, u_inv = _quantize_table(peer_u, i)
        vt_pk = _pack_bf16_rows(peer_v, i, transpose=True)

        zc, xbc_c, dt3_c, dtt3_c, pool_c, qc, kc, vc = _mixer_inputs(
            hc, batch, ctx_len, n1, csh1, csc1, w_perm, conv_w[i], conv_b[i])
        yf_c, yb_c, st_f, st_b = scan(xbc_c, dt3_c, dtt3_c, init_f=zero_state, init_b=zero_state)
        if need_ctx_out:
            pool_yc = _pool_mixer(pool_c, pool_w[i], pool_scale[i])
            att_c = _context_attention(qc, kc, vc)
            hc = _mix_out(yf_c.reshape(nc, -1), yb_c.reshape(nc, -1), xbc_c.reshape(nc, -1), zc,
                          pool_yc.reshape(nc, -1), att_c.reshape(nc, -1), hc, ctx_len,
                          d_skip[i], ssd_norm_g[i], cg1, w_out_bf)
            hc = _peer_ffn_residual(hc, ctx_len, n2, csh2, csc2, cg2, wq_heads, keys_bf, u8, u_inv, vt_pk, final_g)

        z, xbc_l, dt3_l, dtt3_l, pool_l, q, k, v = _mixer_inputs(
            h, batch, seq, n1, sh1, sc1, w_perm, conv_w[i], conv_b[i])
        yf, yb, _, _ = scan(xbc_l, dt3_l, dtt3_l, init_f=st_f, init_b=st_b)
        pool_y = _pool_mixer(pool_l, pool_w[i], pool_scale[i])
        na = _neighbourhood_attention(q, k, v, kc, vc, _na_bias(na_rpb[i]))
        h = _mix_out(yf.reshape(n, -1), yb.reshape(n, -1), xbc_l.reshape(n, -1), z,
                     pool_y.reshape(n, -1), na.reshape(n, -1), h, seq,
                     d_skip[i], ssd_norm_g[i], g1, w_out_bf)
        h = _peer_ffn_residual(h, seq, n2, sh2, sc2, g2, wq_heads, keys_bf, u8, u_inv, vt_pk, final_g,
                               final_norm=i == DEPTH - 1)

    return h.reshape(batch, seq, d)
```

```python
import functools
import math

import jax
import jax.numpy as jnp
from jax import lax
from jax.experimental import pallas as pl
from jax.experimental.pallas import tpu as pltpu

F32 = jnp.float32
BF16 = jnp.bfloat16
F8 = jnp.float8_e4m3fn
HIGHEST = lax.Precision.HIGHEST

D_MODEL = 1024
DEPTH = 2
GRID_W = 64
RMS_EPS = 1e-6

SSD_HEAD_DIM = 64
SSD_HEADS = 16
SSD_GROUPS = 2
SSD_STATE = 128
SSD_CHUNK = 128
SSD_WIDTH = SSD_HEADS * SSD_HEAD_DIM
SSD_XBC = SSD_WIDTH + 2 * SSD_GROUPS * SSD_STATE

POOL_WINDOWS = (2, 4, 8, 16)
POOL_GROUP_DIM = 128
POOL_WIDTH = POOL_GROUP_DIM * len(POOL_WINDOWS)
POOL_PAD = 8

NA_HEAD_DIM = 64
NA_HEADS = 8
NA_WIDTH = NA_HEADS * NA_HEAD_DIM
NA_KH = 8
NA_KW = 16

PEER_HEADS = 8
PEER_NKEYS = 128
PEER_TOPK = 16
PEER_SUB = 128

LANES = 128
VMEM_LIMIT_BYTES = 56 * 1024 * 1024

_PROJ_SEGS = (SSD_WIDTH, SSD_XBC, POOL_WIDTH, NA_WIDTH, NA_WIDTH, NA_WIDTH, LANES)
_PROJ_DTYPES = (F32, F32, F32, BF16, BF16, BF16, F32)


def _params(*sem):
    return pltpu.CompilerParams(dimension_semantics=sem, vmem_limit_bytes=VMEM_LIMIT_BYTES)


def _rms_mod(x, g, scale, shift):
    ms = jnp.mean(x * x, axis=-1, keepdims=True)
    return (x * lax.rsqrt(ms + RMS_EPS)) * g * (1.0 + scale) + shift


def _silu(x):
    return x * jax.nn.sigmoid(x)


def _softplus(x):
    return jnp.maximum(x, 0.0) + jnp.log1p(jnp.exp(-jnp.abs(x)))


def _dot_nt(a, b):
    return lax.dot_general(a, b, (((1,), (1,)), ((), ())), preferred_element_type=F32)


def _batch_map(block_rows, seq_len, n_rows):
    if n_rows == 1:
        return lambda i, *_: (0, 0, 0)
    return lambda i, *_: ((i * block_rows) // seq_len, 0, 0)


def _mod_kernel(c_ref, w_ref, b_ref, o_ref):
    s = _silu(c_ref[...])
    o_ref[0] = jnp.dot(s, w_ref[0], precision=HIGHEST, preferred_element_type=F32) + b_ref[0]


def _modulation(c8, ada_w, ada_b):
    depth, d, six_d = ada_w.shape
    tn = 1024
    return pl.pallas_call(
        _mod_kernel,
        grid=(depth, six_d // tn),
        in_specs=[
            pl.BlockSpec((8, d), lambda l, j: (0, 0)),
            pl.BlockSpec((1, d, tn), lambda l, j: (l, 0, j)),
            pl.BlockSpec((1, 1, tn), lambda l, j: (l, 0, j)),
        ],
        out_specs=pl.BlockSpec((1, 8, tn), lambda l, j: (l, 0, j)),
        out_shape=jax.ShapeDtypeStruct((depth, 8, six_d), F32),
        compiler_params=_params("parallel", "parallel"),
        name="adaln_mod",
    )(c8, ada_w, ada_b.reshape(depth, 1, six_d))


def _proj_kernel(h_ref, g_ref, sh_ref, sc_ref, w_ref, *out_refs):
    u = _rms_mod(h_ref[...], g_ref[...], sc_ref[0], sh_ref[0]).astype(BF16)
    off = 0
    for o_ref, width in zip(out_refs, _PROJ_SEGS):
        o_ref[...] = jnp.dot(u, w_ref[:, off:off + width], preferred_element_type=F32).astype(o_ref.dtype)
        off += width


def _project(h2d, seq_len, g, shift, scale, w_perm):
    n, d = h2d.shape
    t = 512
    total = sum(_PROJ_SEGS)
    nb = shift.shape[0]
    assert n % t == 0 and (nb == 1 or seq_len % t == 0)
    row_map = lambda i: (i, 0)
    return pl.pallas_call(
        _proj_kernel,
        grid=(n // t,),
        in_specs=[
            pl.BlockSpec((t, d), row_map),
            pl.BlockSpec((1, d), lambda i: (0, 0)),
            pl.BlockSpec((1, 1, d), _batch_map(t, seq_len, nb)),
            pl.BlockSpec((1, 1, d), _batch_map(t, seq_len, nb)),
            pl.BlockSpec((d, total), lambda i: (0, 0)),
        ],
        out_specs=[pl.BlockSpec((t, w), row_map) for w in _PROJ_SEGS],
        out_shape=[jax.ShapeDtypeStruct((n, w), dt) for w, dt in zip(_PROJ_SEGS, _PROJ_DTYPES)],
        compiler_params=_params("parallel"),
        name="in_proj",
    )(h2d, g, shift, scale, w_perm)


def _permute_w_in(w_in):
    o = 0
    z = w_in[:, o:o + SSD_WIDTH]; o += SSD_WIDTH
    xbc = w_in[:, o:o + SSD_XBC]; o += SSD_XBC
    dt = w_in[:, o:o + 2 * SSD_HEADS]; o += 2 * SSD_HEADS
    rest = w_in[:, o:]
    dt = jnp.pad(dt, ((0, 0), (0, LANES - 2 * SSD_HEADS)))
    return jnp.concatenate([z, xbc, rest, dt], axis=1).astype(BF16)


def _conv_kernel(x_ref, w_ref, b_ref, o_ref):
    x = x_ref[0]
    n = x.shape[0]
    row = lax.broadcasted_iota(jnp.int32, x.shape, 0)
    prev = jnp.where(row == 0, 0.0, pltpu.roll(x, 1, 0))
    nxt = jnp.where(row == n - 1, 0.0, pltpu.roll(x, n - 1, 0))
    y = prev * w_ref[0:1, :] + x * w_ref[1:2, :] + nxt * w_ref[2:3, :] + b_ref[...]
    o_ref[0] = _silu(y)


def _conv_silu(xbc3, conv_w, conv_b):
    b, l, c = xbc3.shape
    tc = 256
    return pl.pallas_call(
        _conv_kernel,
        grid=(b, c // tc),
        in_specs=[
            pl.BlockSpec((1, l, tc), lambda i, j: (i, 0, j)),
            pl.BlockSpec((3, tc), lambda i, j: (0, j)),
            pl.BlockSpec((1, tc), lambda i, j: (0, j)),
        ],
        out_specs=pl.BlockSpec((1, l, tc), lambda i, j: (i, 0, j)),
        out_shape=jax.ShapeDtypeStruct((b, l, c), F32),
        compiler_params=_params("parallel", "parallel"),
        name="dwconv_silu",
    )(xbc3, conv_w, conv_b.reshape(1, c))


def _ssd_chunk(reverse, xbc_ref, dt_ref, dtt_ref, dtb_row, dtb_col, alog_row, alog_col, y_ref, state_scr):
    q = SSD_CHUNK
    col0 = SSD_HEADS if reverse else 0
    dt_l = _softplus(dt_ref[0] + dtb_row[...])
    a_l = dt_l * (-jnp.exp(alog_row[...]))
    dt_t = _softplus(dtt_ref[0] + dtb_col[...])
    a_t = dt_t * (-jnp.exp(alog_col[...]))
    row = lax.broadcasted_iota(jnp.int32, (q, q), 0)
    col = lax.broadcasted_iota(jnp.int32, (q, q), 1)
    lower = (row >= col).astype(F32)
    upper = (row <= col).astype(F32)
    cs_l = jnp.dot(lower, a_l, precision=HIGHEST, preferred_element_type=F32)
    cs_t = jnp.dot(a_t, upper, precision=HIGHEST, preferred_element_type=F32)
    tot_l = cs_l[q - 1:q, :]
    if reverse:
        p_l, p_t = cs_l - a_l, cs_t - a_t
        tri = row <= col
    else:
        p_l, p_t = cs_l, cs_t
        tri = row >= col

    tot_t = cs_t[:, q - 1:q]
    to_end_t = jnp.exp(p_t) if reverse else jnp.exp(tot_t - p_t)
    w_t = dt_t * to_end_t
    left = lax.broadcasted_iota(jnp.int32, (q, 2 * SSD_HEAD_DIM), 1) < SSD_HEAD_DIM

    xbc = xbc_ref[0]
    heads_per_group = SSD_HEADS // SSD_GROUPS
    for g in range(SSD_GROUPS):
        b_g = xbc[:, SSD_WIDTH + g * SSD_STATE:SSD_WIDTH + (g + 1) * SSD_STATE]
        c_g = xbc[:, SSD_WIDTH + (SSD_GROUPS + g) * SSD_STATE:SSD_WIDTH + (SSD_GROUPS + g + 1) * SSD_STATE]
        c_bf = c_g.astype(BF16)
        cb = _dot_nt(c_bf, b_g.astype(BF16))
        bt = b_g.T
        for pair in range(g * heads_per_group // 2, (g + 1) * heads_per_group // 2):
            lanes = slice(pair * 2 * SSD_HEAD_DIM, (pair + 1) * 2 * SSD_HEAD_DIM)
            xp = xbc[:, lanes]
            x_bd = jnp.concatenate([jnp.where(left, xp, 0.0), jnp.where(left, 0.0, xp)], axis=0).astype(BF16)
            decay_tiles, state_tiles, pcol_tiles = [], [], []
            for k in (col0 + 2 * pair, col0 + 2 * pair + 1):
                pcol_b = jnp.broadcast_to(p_l[:, k:k + 1], (q, q))
                prow = p_t[k:k + 1, :]
                seg = (prow - pcol_b) if reverse else (pcol_b - prow)
                lmat = jnp.exp(jnp.where(tri, seg, -jnp.inf)) * dt_t[k:k + 1, :]
                decay_tiles.append((cb * lmat).astype(BF16))
                state_tiles.append((bt * w_t[k:k + 1, :]).astype(BF16))
                pcol_tiles.append(pcol_b)
            k0 = col0 + 2 * pair
            tot_pair = jnp.where(left[0:1], tot_l[:, k0:k0 + 1], tot_l[:, k0 + 1:k0 + 2])
            pcol_pair = jnp.where(left, pcol_tiles[0], pcol_tiles[1])
            in_decay = jnp.exp(tot_pair - pcol_pair) if reverse else jnp.exp(pcol_pair)
            s_prev = state_scr[pair]
            y_diag = jnp.dot(jnp.concatenate(decay_tiles, axis=1), x_bd, preferred_element_type=F32)
            y_off = jnp.dot(c_bf, s_prev.astype(BF16), preferred_element_type=F32) * in_decay
            state_scr[pair] = jnp.exp(tot_pair) * s_prev + jnp.dot(jnp.concatenate(state_tiles, axis=1), x_bd,
                                                                   preferred_element_type=F32)
            y_ref[0, :, lanes] = y_diag + y_off


def _ssd_kernel(xf_ref, dtf_ref, dttf_ref, xb_ref, dtb_ref, dttb_ref, dtb_row, dtb_col, alog_row, alog_col,
                initf_ref, initb_ref, yf_ref, yb_ref, finf_ref, finb_ref, statef_scr, stateb_scr):
    c = pl.program_id(1)

    @pl.when(c == 0)
    def _():
        statef_scr[...] = initf_ref[0]
        stateb_scr[...] = initb_ref[0]

    params = (dtb_row, dtb_col, alog_row, alog_col)
    _ssd_chunk(False, xf_ref, dtf_ref, dttf_ref, *params, yf_ref, statef_scr)
    _ssd_chunk(True, xb_ref, dtb_ref, dttb_ref, *params, yb_ref, stateb_scr)

    @pl.when(c == pl.num_programs(1) - 1)
    def _():
        finf_ref[0] = statef_scr[...]
        finb_ref[0] = stateb_scr[...]


def _ssd_scan(xbc_act, dt3, dtt3, dtb, alog, init_f, init_b):
    b, l, _ = xbc_act.shape
    nc = l // SSD_CHUNK
    dtb_row = jnp.pad(dtb.reshape(1, -1), ((0, 0), (0, LANES - 2 * SSD_HEADS)))
    alog_row = jnp.pad(alog.reshape(1, -1), ((0, 0), (0, LANES - 2 * SSD_HEADS)))
    small = lambda shape: pl.BlockSpec(shape, lambda i, c: (0, 0))
    st_shape = (b, SSD_HEADS // 2, SSD_STATE, 2 * SSD_HEAD_DIM)
    st_spec = pl.BlockSpec((1,) + st_shape[1:], lambda i, c: (i, 0, 0, 0))
    fwd, bwd = (lambda c: c), (lambda c: nc - 1 - c)
    chunk_specs = lambda cm: [
        pl.BlockSpec((1, SSD_CHUNK, SSD_XBC), lambda i, c: (i, cm(c), 0)),
        pl.BlockSpec((1, SSD_CHUNK, LANES), lambda i, c: (i, cm(c), 0)),
        pl.BlockSpec((1, 2 * SSD_HEADS, SSD_CHUNK), lambda i, c: (i, 0, cm(c))),
    ]
    y_spec = lambda cm: pl.BlockSpec((1, SSD_CHUNK, SSD_WIDTH), lambda i, c: (i, cm(c), 0))
    y_shape = jax.ShapeDtypeStruct((b, l, SSD_WIDTH), F32)
    return pl.pallas_call(
        _ssd_kernel,
        grid=(b, nc),
        in_specs=chunk_specs(fwd) + chunk_specs(bwd)
        + [small((1, LANES)), small((2 * SSD_HEADS, 1)), small((1, LANES)), small((2 * SSD_HEADS, 1)), st_spec, st_spec],
        out_specs=[y_spec(fwd), y_spec(bwd), st_spec, st_spec],
        out_shape=[y_shape, y_shape, jax.ShapeDtypeStruct(st_shape, F32), jax.ShapeDtypeStruct(st_shape, F32)],
        scratch_shapes=[pltpu.VMEM(st_shape[1:], F32), pltpu.VMEM(st_shape[1:], F32)],
        compiler_params=_params("parallel", "arbitrary"),
        name="ssd_scan",
    )(xbc_act, dt3, dtt3, xbc_act, dt3, dtt3, dtb_row, dtb.reshape(-1, 1), alog_row, alog.reshape(-1, 1),
      init_f, init_b)


def _pool_kernel(x_ref, w_ref, sc_ref, o_ref, pad_scr):
    n = x_ref.shape[1]
    zeros = jnp.zeros((POOL_PAD, POOL_GROUP_DIM), F32)
    pad_scr[0:POOL_PAD, :] = zeros
    pad_scr[n + POOL_PAD:n + 2 * POOL_PAD, :] = zeros
    t = lax.broadcasted_iota(jnp.int32, (n, 1), 0)
    for gi, w in enumerate(POOL_WINDOWS):
        sl = slice(gi * POOL_GROUP_DIM, (gi + 1) * POOL_GROUP_DIM)
        x = x_ref[0, :, sl]
        pad_scr[POOL_PAD:n + POOL_PAD, :] = x
        acc = jnp.zeros_like(x)
        for o in range(-(w // 2), w - w // 2):
            acc = acc + pad_scr[POOL_PAD + o:POOL_PAD + o + n, :]
        lo = jnp.maximum(t - w // 2, 0)
        hi = jnp.minimum(t + (w - w // 2 - 1), n - 1)
        pooled = acc / (hi - lo + 1).astype(F32) - x
        y = jnp.dot(pooled.astype(BF16), w_ref[gi].astype(BF16), preferred_element_type=F32)
        o_ref[0, :, sl] = (y * sc_ref[:, sl]).astype(o_ref.dtype)


def _pool_mixer(u3, w_pool, scale):
    b, l, c = u3.shape
    return pl.pallas_call(
        _pool_kernel,
        grid=(b,),
        in_specs=[
            pl.BlockSpec((1, l, c), lambda i: (i, 0, 0)),
            pl.BlockSpec(w_pool.shape, lambda i: (0, 0, 0)),
            pl.BlockSpec((1, c), lambda i: (0, 0)),
        ],
        out_specs=pl.BlockSpec((1, l, c), lambda i: (i, 0, 0)),
        out_shape=jax.ShapeDtypeStruct((b, l, c), BF16),
        scratch_shapes=[pltpu.VMEM((l + 2 * POOL_PAD, POOL_GROUP_DIM), F32)],
        compiler_params=_params("parallel"),
        name="pool_mixer",
    )(u3, w_pool, scale.reshape(1, c))


def _na_bias_kernel(rpb_ref, o_ref):
    h = pl.program_id(0)
    qi = lax.broadcasted_iota(jnp.int32, (GRID_W, LANES), 0)
    lane = lax.broadcasted_iota(jnp.int32, (GRID_W, LANES), 1)
    ki = lane % GRID_W
    second = lane >= GRID_W
    start = jnp.clip(qi - NA_KW // 2, 0, GRID_W - NA_KW)
    in_window = (ki >= start) & (ki < start + NA_KW)
    dc = jnp.clip(ki - qi, -(NA_KW - 1), NA_KW - 1) + NA_KW - 1
    for dr in range(o_ref.shape[1]):
        val = jnp.zeros((GRID_W, LANES), F32)
        for j in range(2 * NA_KW - 1):
            pick = jnp.where(second, rpb_ref[h, dr + 1, j], rpb_ref[h, dr, j])
            val = jnp.where(dc == j, pick, val)
        o_ref[0, dr] = jnp.where(in_window, val, -jnp.inf)


def _na_bias(rpb):
    nh, ndr, ndc = rpb.shape
    return pl.pallas_call(
        _na_bias_kernel,
        grid=(nh,),
        in_specs=[pl.BlockSpec(memory_space=pltpu.SMEM)],
        out_specs=pl.BlockSpec((1, ndr - 1, GRID_W, LANES), lambda h: (h, 0, 0, 0)),
        out_shape=jax.ShapeDtypeStruct((nh, ndr - 1, GRID_W, LANES), F32),
        compiler_params=_params("parallel"),
        name="na_bias",
    )(rpb)


def _na_kernel(kh, q_ref, k_ref, v_ref, kc_ref, vc_ref, bias_ref, o_ref):
    r = pl.program_id(1)
    rows = pl.num_programs(1)
    r0 = jnp.clip(r - kh // 2, 0, rows - kh)
    start = pl.multiple_of(r0 * GRID_W, GRID_W)
    kblk = k_ref[0, pl.ds(start, kh * GRID_W), :]
    vblk = v_ref[0, pl.ds(start, kh * GRID_W), :]
    dr0 = r0 - r + NA_KH - 1
    scale = NA_HEAD_DIM ** -0.5
    pair_w = 2 * NA_HEAD_DIM
    halves = []
    for pair in range(NA_HEADS // 2):
        lanes = slice(pair * pair_w, (pair + 1) * pair_w)
        left_q = lax.broadcasted_iota(jnp.int32, (GRID_W, pair_w), 1) < NA_HEAD_DIM
        q_pair = q_ref[0, :, lanes] * scale
        k_pair, kc_pair = kblk[:, lanes], kc_ref[0, :, lanes]
        for side in range(2):
            keep = left_q if side == 0 else jnp.logical_not(left_q)
            q_h = jnp.where(keep, q_pair, 0.0)
            h = 2 * pair + side
            bias = jnp.concatenate([bias_ref[h, dr0 + 2 * j] for j in range(kh // 2)], axis=1)
            halves.append((_dot_nt(q_h, k_pair) + bias, _dot_nt(q_h, kc_pair)))
    probs = []
    for s_loc, s_ctx in halves:
        m = jnp.maximum(jnp.max(s_loc, axis=-1, keepdims=True), jnp.max(s_ctx, axis=-1, keepdims=True))
        p_loc = jnp.exp(s_loc - m)
        p_ctx = jnp.exp(s_ctx - m)
        inv = 1.0 / (jnp.sum(p_loc, axis=-1, keepdims=True) + jnp.sum(p_ctx, axis=-1, keepdims=True))
        probs.append((p_loc.astype(BF16), p_ctx.astype(BF16), inv))
    for pair in range(NA_HEADS // 2):
        lanes = slice(pair * pair_w, (pair + 1) * pair_w)
        v_pair, vc_pair = vblk[:, lanes], vc_ref[0, :, lanes]
        left_v = lax.broadcasted_iota(jnp.int32, v_pair.shape, 1) < NA_HEAD_DIM
        left_c = lax.broadcasted_iota(jnp.int32, vc_pair.shape, 1) < NA_HEAD_DIM
        out = None
        for side in range(2):
            p_loc, p_ctx, inv = probs[2 * pair + side]
            keep_v = left_v if side == 0 else jnp.logical_not(left_v)
            keep_c = left_c if side == 0 else jnp.logical_not(left_c)
            acc = jnp.dot(p_loc, jnp.where(keep_v, v_pair, 0.0), preferred_element_type=F32)
            acc = acc + jnp.dot(p_ctx, jnp.where(keep_c, vc_pair, 0.0), preferred_element_type=F32)
            out = acc * inv if out is None else out + acc * inv
        o_ref[0, :, lanes] = out.astype(o_ref.dtype)


def _neighbourhood_attention(q3, k3, v3, kc3, vc3, bias):
    b, s, c = q3.shape
    rows = s // GRID_W
    kh = min(NA_KH, rows)
    lc = kc3.shape[1]
    full = lambda n: pl.BlockSpec((1, n, c), lambda i, r: (i, 0, 0))
    return pl.pallas_call(
        functools.partial(_na_kernel, kh),
        grid=(b, rows),
        in_specs=[
            pl.BlockSpec((1, GRID_W, c), lambda i, r: (i, r, 0)),
            full(s), full(s), full(lc), full(lc),
            pl.BlockSpec(bias.shape, lambda i, r: (0, 0, 0, 0)),
        ],
        out_specs=pl.BlockSpec((1, GRID_W, c), lambda i, r: (i, r, 0)),
        out_shape=jax.ShapeDtypeStruct((b, s, c), BF16),
        compiler_params=_params("parallel", "arbitrary"),
        name="na_attention",
    )(q3, k3, v3, kc3, vc3, bias)


def _ctx_attn_kernel(q_ref, k_ref, v_ref, o_ref):
    scale = NA_HEAD_DIM ** -0.5
    for h in range(NA_HEADS):
        sl = slice(h * NA_HEAD_DIM, (h + 1) * NA_HEAD_DIM)
        s = _dot_nt(q_ref[0, :, sl] * scale, k_ref[0, :, sl])
        p = jnp.exp(s - jnp.max(s, axis=-1, keepdims=True))
        den = jnp.sum(p, axis=-1, keepdims=True)
        o_ref[0, :, sl] = (jnp.dot(p.astype(BF16), v_ref[0, :, sl], preferred_element_type=F32) / den).astype(o_ref.dtype)


def _context_attention(q3, k3, v3):
    b, l, c = q3.shape
    spec = pl.BlockSpec((1, l, c), lambda i: (i, 0, 0))
    return pl.pallas_call(
        _ctx_attn_kernel,
        grid=(b,),
        in_specs=[spec, spec, spec],
        out_specs=spec,
        out_shape=jax.ShapeDtypeStruct((b, l, c), BF16),
        compiler_params=_params("parallel"),
        name="ctx_attention",
    )(q3, k3, v3)


def _mix_out_kernel(yf_ref, yb_ref, xs_ref, z_ref, pool_ref, na_ref, h_ref, dsk_ref, ng_ref, g1_ref, w_ref, o_ref):
    y = yf_ref[...] + yb_ref[...] + dsk_ref[...] * xs_ref[...]
    y = y * _silu(z_ref[...])
    ms = jnp.mean(y * y, axis=-1, keepdims=True)
    yn = (y * lax.rsqrt(ms + RMS_EPS)) * ng_ref[...]
    mix = jnp.dot(yn.astype(BF16), w_ref[0:SSD_WIDTH, :], preferred_element_type=F32)
    mix = mix + jnp.dot(pool_ref[...].astype(BF16), w_ref[SSD_WIDTH:SSD_WIDTH + POOL_WIDTH, :],
                        preferred_element_type=F32)
    mix = mix + jnp.dot(na_ref[...].astype(BF16), w_ref[SSD_WIDTH + POOL_WIDTH:, :], preferred_element_type=F32)
    o_ref[...] = h_ref[...] + g1_ref[0] * mix


def _mix_out(yf, yb, xbc_act2d, z, pool_y, na_y, h2d, seq_len, d_skip, norm_g, g1, w_out_bf):
    n, d = h2d.shape
    t = 256
    row = lambda w: pl.BlockSpec((t, w), lambda i: (i, 0))
    vec = lambda w: pl.BlockSpec((1, w), lambda i: (0, 0))
    return pl.pallas_call(
        _mix_out_kernel,
        grid=(n // t,),
        in_specs=[
            row(SSD_WIDTH), row(SSD_WIDTH), row(SSD_WIDTH), row(SSD_WIDTH), row(POOL_WIDTH), row(NA_WIDTH), row(d),
            vec(SSD_WIDTH), vec(SSD_WIDTH),
            pl.BlockSpec((1, 1, d), _batch_map(t, seq_len, g1.shape[0])),
            pl.BlockSpec(w_out_bf.shape, lambda i: (0, 0)),
        ],
        out_specs=row(d),
        out_shape=jax.ShapeDtypeStruct((n, d), F32),
        compiler_params=_params("parallel"),
        name="mix_out",
    )(yf, yb, xbc_act2d, z, pool_y, na_y, h2d, jnp.repeat(d_skip, SSD_HEAD_DIM).reshape(1, -1),
      norm_g.reshape(1, -1), g1, w_out_bf)


_CAND_ROWS = 16 + 8 * 7 + 8


def _batcher_pairs(n):
    pairs = []

    def merge(lo, m, r):
        step = 2 * r
        if step < m:
            merge(lo, m, step)
            merge(lo + r, m, step)
            pairs.extend((i, i + r) for i in range(lo + r, lo + m - r, step))
        else:
            pairs.append((lo, lo + r))

    def sort(lo, m):
        if m > 1:
            sort(lo, m // 2)
            sort(lo + m // 2, m // 2)
            merge(lo, m, 1)

    sort(0, n)
    return tuple(pairs)


_SORT16 = _batcher_pairs(PEER_TOPK)
_BITONIC16 = tuple((i, i + d) for d in (8, 4, 2, 1) for i in range(PEER_TOPK) if not i & d)


def _exchange(x, pairs):
    for i, j in pairs:
        x[i], x[j] = jnp.maximum(x[i], x[j]), jnp.minimum(x[i], x[j])


def _sorted_top16(s):
    tiles = []
    for l0 in range(0, s.shape[1], LANES):
        x = [s[8 * v:8 * v + 8, l0:l0 + LANES] for v in range(PEER_TOPK)]
        _exchange(x, _SORT16)
        for shift in (4, 2, 1):
            x = [jnp.maximum(x[i], pltpu.roll(x[PEER_TOPK - 1 - i], shift, 0)) for i in range(PEER_TOPK)]
            _exchange(x, _BITONIC16)
        tiles.append(jnp.concatenate([xi[0:1] for xi in x], axis=0))
    return jnp.concatenate(tiles, axis=1)


def _count_above(s, v):
    r = lambda j: v[j:j + 1]
    pick = jnp.where
    a = s < r(7)
    b = s < pick(a, r(11), r(3))
    c = s < pick(a, pick(b, r(13), r(9)), pick(b, r(5), r(1)))
    d = s < pick(a, pick(b, pick(c, r(14), r(12)), pick(c, r(10), r(8))),
                 pick(b, pick(c, r(6), r(4)), pick(c, r(2), r(0))))
    return (pick(a, 8.0, 0.0) + pick(b, 4.0, 0.0) + pick(c, 2.0, 0.0) + pick(d, 1.0, 0.0)
            + pick(s < r(15), 1.0, 0.0))


def _top16_pair_fast(s1, s2):
    v1, v2 = _sorted_top16(s1), _sorted_top16(s2)
    rank2 = _count_above(s2, v2)

    def check(s, v):
        n = jnp.sum(jnp.where(s >= v[PEER_TOPK - 1:PEER_TOPK], 1.0, 0.0), axis=0, keepdims=True)
        strict = jnp.min(v[:-1] - v[1:], axis=0, keepdims=True) > 0.0
        return jnp.where(strict, n, 0.0)

    return v1, check(s1, v1), v2, rank2, check(s2, v2)


def _first_counts_fast(s1, v1, sel):
    inf = jnp.inf
    rows_v1 = jnp.concatenate([jnp.broadcast_to(v1[0:1], (PEER_TOPK, v1.shape[1]))]
                              + [jnp.broadcast_to(v1[j:j + 1], (8, v1.shape[1])) for j in range(1, 8)]
                              + [v1[8:16]], axis=0)
    bound = jnp.where(sel > 0.0, rows_v1, inf)
    low = bound[0:8]
    for j in range(1, 8):
        low = jnp.minimum(low, bound[16 + 8 * (j - 1):16 + 8 * j])
    tail = jnp.min(bound[64 + 8:64 + 16], axis=0, keepdims=True)
    t = [jnp.minimum(low[0:1], tail)] + [low[k:k + 1] for k in range(1, 8)]
    pick = jnp.where
    a = s1 >= t[3]
    b = s1 >= pick(a, t[5], t[1])
    c = s1 >= pick(a, pick(b, t[6], t[4]), pick(b, t[2], t[0]))
    cnt = pick(a, 4.0, 0.0) + pick(b, 2.0, 0.0) + pick(c, 1.0, 0.0) + pick(s1 >= t[7], 1.0, 0.0)
    n_high = jnp.sum(sel[8:16], axis=0, keepdims=True)
    return cnt + jnp.where(s1 >= v1[0:1], n_high, 0.0)


def _first_counts_exact(rank1, sel):
    cnt = jnp.zeros(rank1.shape, F32)
    for j in range(8):
        lo = 0 if j == 0 else 16 + 8 * (j - 1)
        n_j = jnp.sum(sel[lo:lo + (16 if j == 0 else 8)], axis=0, keepdims=True)
        cnt = cnt + jnp.where(rank1 == float(j), n_j, 0.0)
    for j in range(8, 16):
        cnt = cnt + jnp.where(rank1 == float(j), sel[64 + j:65 + j], 0.0)
    return cnt


def _top16_pair_exact(s1, s2):
    n, t = s1.shape
    iota = lax.broadcasted_iota(jnp.int32, (n, t), 0).astype(F32)
    row16 = lax.broadcasted_iota(jnp.int32, (PEER_TOPK, t), 0)

    def pick(work, rank, vals, j):
        m = jnp.max(work, axis=0, keepdims=True)
        idx = jnp.min(jnp.where(work == m, iota, float(n)), axis=0, keepdims=True)
        sel = iota == idx
        return (jnp.where(sel, -jnp.inf, work), jnp.where(sel, lax.convert_element_type(j, F32), rank),
                jnp.where(row16 == j, m, vals))

    def body(j, carry):
        a, b = carry
        return pick(*a, j), pick(*b, j)

    start = lambda s: (s, jnp.full((n, t), float(PEER_TOPK), F32), jnp.zeros((PEER_TOPK, t), F32))
    (_, rank1, v1), (_, rank2, v2) = lax.fori_loop(0, PEER_TOPK, body, (start(s1), start(s2)))
    return rank1, v1, rank2, v2


def _select16_fast(cand):
    rows, t = cand.shape
    padded = jnp.concatenate([cand, jnp.full((PEER_NKEYS - rows, t), -jnp.inf, F32)], axis=0)
    m = _sorted_top16(padded)[PEER_TOPK - 1:PEER_TOPK]
    taken = jnp.where(cand >= m, 1.0, 0.0)
    return taken, jnp.sum(taken, axis=0, keepdims=True)


def _select16_exact(cand):
    iota = lax.broadcasted_iota(jnp.int32, cand.shape, 0).astype(F32)

    def body(_, carry):
        work, sel_acc = carry
        m = jnp.max(work, axis=0, keepdims=True)
        idx = jnp.min(jnp.where(work == m, iota, float(_CAND_ROWS)), axis=0, keepdims=True)
        sel = iota == idx
        return jnp.where(sel, -jnp.inf, work), jnp.where(sel, 1.0, sel_acc)

    return lax.fori_loop(0, PEER_TOPK, body, (cand, jnp.zeros(cand.shape, F32)))[1]


def _candidate_sums(v1, v2):
    blocks = [v1[0:1] + v2]
    for j in range(1, 8):
        blocks.append(v1[j:j + 1] + v2[0:8])
    blocks.append(v1[8:16] + v2[0:1])
    return jnp.concatenate(blocks, axis=0)


def _any_not_16(*counts):
    return jnp.max(sum(jnp.abs(c - float(PEER_TOPK)) for c in counts)) > 0.0


def _bf16_bits(x):
    return pltpu.bitcast(x.astype(BF16).astype(F32), jnp.uint32)


def _pack_row_pairs(x, scr):
    n, t = x.shape
    for j in range(t // LANES):
        scr[j] = x[:, j * LANES:(j + 1) * LANES]
    words = []
    for j in range(t // LANES):
        even = scr[j, pl.ds(0, n // 2, stride=2), :]
        odd = scr[j, pl.ds(1, n // 2, stride=2), :]
        words.append((_bf16_bits(even) >> 16) | _bf16_bits(odd))
    return jnp.concatenate(words, axis=1)


def _pack_same(x):
    w = _bf16_bits(x)
    return w | (w >> 16)


_F8_TARGET_EXP = 6


def _pow2_scale(amax):
    bits = pltpu.bitcast(jnp.maximum(amax, 2.0 ** -100), jnp.int32)
    exponent = (bits >> 23) - 127
    return pltpu.bitcast((_F8_TARGET_EXP - exponent + 127) << 23, F32)


def _peer_score_kernel(h_ref, g_ref, sh_ref, sc_ref, wq_ref, wq_next_ref, keys_ref,
                       x_out, xinv_out, cnt_out, e1_out, rk_out, e2_out,
                       u_scr, q_scr, pair_scr, cnt_scr, rank2_scr, top_scr):
    hd = pl.program_id(1)

    @pl.when(hd == 0)
    def _():
        u = _rms_mod(h_ref[...], g_ref[...], sc_ref[0], sh_ref[0])
        u_scr[...] = u.astype(BF16)
        scale = _pow2_scale(jnp.max(jnp.abs(u), axis=-1, keepdims=True))
        x_out[...] = (u * scale).astype(F8)
        xinv_out[...] = 1.0 / scale
        q_scr[...] = jnp.dot(u_scr[...], wq_ref[0], preferred_element_type=F32).astype(BF16)

    q = q_scr[...]
    s1 = _dot_nt(keys_ref[0, 0], q[:, :PEER_SUB])
    s2 = _dot_nt(keys_ref[0, 1], q[:, PEER_SUB:])
    v1, n1, v2, rank2, n2 = _top16_pair_fast(s1, s2)
    cand = _candidate_sums(v1, v2)
    sel, n_sel = _select16_fast(cand)
    cnt_scr[...] = _first_counts_fast(s1, v1, sel)
    rank2_scr[...] = rank2
    top_scr[0:1] = v1[0:1]
    top_scr[1:2] = v2[0:1]
    top_scr[2:3] = jnp.sum(sel * jnp.exp(cand - cand[0:1]), axis=0, keepdims=True)
    q_scr[...] = jnp.dot(u_scr[...], wq_next_ref[0], preferred_element_type=F32).astype(BF16)

    @pl.when(_any_not_16(n1, n2, n_sel))
    def _():
        rank1_x, v1_x, rank2_x, v2_x = _top16_pair_exact(s1, s2)
        cand_x = _candidate_sums(v1_x, v2_x)
        sel_x = _select16_exact(cand_x)
        cnt_scr[...] = _first_counts_exact(rank1_x, sel_x)
        rank2_scr[...] = rank2_x
        top_scr[0:1] = v1_x[0:1]
        top_scr[1:2] = v2_x[0:1]
        top_scr[2:3] = jnp.sum(sel_x * jnp.exp(cand_x - cand_x[0:1]), axis=0, keepdims=True)

    cnt_out[0] = _pack_same(cnt_scr[...])
    e1_out[0] = _pack_same(jnp.exp(s1 - top_scr[0:1]))
    rk_out[0] = _pack_row_pairs(rank2_scr[...], pair_scr)
    e2_out[0] = _pack_row_pairs(jnp.exp(s2 - top_scr[1:2]) / top_scr[2:3], pair_scr)


def _peer_scores(h2d, seq_len, g, shift, scale, wq_heads, keys_bf):
    n, d = h2d.shape
    nb = shift.shape[0]
    t = 512
    assert n % t == 0 and (nb == 1 or seq_len % t == 0)
    bm = _batch_map(t, seq_len, nb)
    mod_spec = pl.BlockSpec((1, 1, d), lambda i, hd: bm(i))
    first_out = pl.BlockSpec((1, PEER_NKEYS, t), lambda i, hd: (hd, 0, i))
    first_shape = jax.ShapeDtypeStruct((PEER_HEADS, PEER_NKEYS, n), jnp.uint32)
    second_out = pl.BlockSpec((1, PEER_NKEYS // 2, t), lambda i, hd: (hd, 0, i))
    second_shape = jax.ShapeDtypeStruct((PEER_HEADS, PEER_NKEYS // 2, n), jnp.uint32)
    return pl.pallas_call(
        _peer_score_kernel,
        grid=(n // t, PEER_HEADS),
        in_specs=[
            pl.BlockSpec((t, d), lambda i, hd: (i, 0)),
            pl.BlockSpec((1, d), lambda i, hd: (0, 0)),
            mod_spec, mod_spec,
            pl.BlockSpec((1, d, 2 * PEER_SUB), lambda i, hd: (hd, 0, 0)),
            pl.BlockSpec((1, d, 2 * PEER_SUB), lambda i, hd: (jnp.minimum(hd + 1, PEER_HEADS - 1), 0, 0)),
            pl.BlockSpec((1, 2, PEER_NKEYS, PEER_SUB), lambda i, hd: (hd, 0, 0, 0)),
        ],
        out_specs=[pl.BlockSpec((t, d), lambda i, hd: (i, 0)), pl.BlockSpec((t, 1), lambda i, hd: (i, 0)),
                   first_out, first_out, second_out, second_out],
        out_shape=[jax.ShapeDtypeStruct((n, d), F8), jax.ShapeDtypeStruct((n, 1), F32),
                   first_shape, first_shape, second_shape, second_shape],
        scratch_shapes=[pltpu.VMEM((t, d), BF16), pltpu.VMEM((t, 2 * PEER_SUB), BF16),
                        pltpu.VMEM((t // LANES, PEER_NKEYS, LANES), F32),
                        pltpu.VMEM((PEER_NKEYS, t), F32), pltpu.VMEM((PEER_NKEYS, t), F32),
                        pltpu.VMEM((8, t), F32)],
        compiler_params=_params("parallel", "arbitrary"),
        name="peer_scores",
    )(h2d, g, shift, scale, wq_heads, wq_heads, keys_bf)


_PEER_EC = 1024
_INV_SQRT2 = 1.0 / math.sqrt(2.0)


def _as_bf16_rows(words):
    return pltpu.bitcast(words, BF16)


_PEER_A_GROUP = 4


def _peer_expert_kernel(n_chunks, final_norm, x_ref, u_ref, inv_ref, vt_ref, cnt_ref, e1_ref, rk_ref, e2_ref, h_ref,
                        g2_ref, fg_ref, o_ref,
                        ht0_scr, ht1_scr, g0_scr, g1_scr, acc_scr):
    s = pl.program_id(0)
    t = h_ref.shape[0]
    p2 = s - 2
    c2 = lax.rem(jnp.maximum(p2, 0), n_chunks)
    pack = 2 * 8
    n_a = _PEER_EC // PEER_NKEYS

    @pl.when(s == 0)
    def _():
        ht1_scr[...] = jnp.zeros_like(ht1_scr)
        g0_scr[...] = jnp.zeros_like(g0_scr)
        g1_scr[...] = jnp.zeros_like(g1_scr)

    @pl.when((p2 <= 0) | (c2 == 0))
    def _():
        acc_scr[...] = jnp.zeros_like(acc_scr)

    def step(ht_new, ht_old, g_new, g_old):
        for tc in range(t // LANES):
            ls = slice(tc * LANES, (tc + 1) * LANES)
            half_inv = 0.5 * inv_ref[:, ls]
            c_inv = _INV_SQRT2 * inv_ref[:, ls]
            for a0 in range(0, n_a, _PEER_A_GROUP):
                w = [[None] * (PEER_NKEYS // pack) for _ in range(_PEER_A_GROUP)]
                for hd in range(PEER_HEADS):
                    rows1 = [(_as_bf16_rows(jnp.broadcast_to(cnt_ref[hd, a0 + i:a0 + i + 1, ls], (8, LANES))),
                              _as_bf16_rows(jnp.broadcast_to(e1_ref[hd, a0 + i:a0 + i + 1, ls], (8, LANES))))
                             for i in range(_PEER_A_GROUP)]
                    for r in range(PEER_NKEYS // pack):
                        rk = _as_bf16_rows(rk_ref[hd, r * 8:(r + 1) * 8, ls])
                        e2 = _as_bf16_rows(e2_ref[hd, r * 8:(r + 1) * 8, ls])
                        for i, (cn, e1) in enumerate(rows1):
                            term = jnp.where(rk < cn, e2, 0.0) * e1
                            w[i][r] = term if w[i][r] is None else w[i][r] + term
                for i in range(_PEER_A_GROUP):
                    for r in range(PEER_NKEYS // pack):
                        row0 = (a0 + i) * PEER_NKEYS + r * pack
                        hs = ht_old[row0:row0 + pack, ls]
                        act = (hs * half_inv) * (1.0 + lax.erf(hs * c_inv))
                        g_new[row0 // 2:(row0 + pack) // 2, ls] = pltpu.bitcast(w[i][r] * act.astype(BF16), jnp.uint32)
        ht_new[...] = _dot_nt(u_ref[...], x_ref[...])
        acc_scr[...] += jnp.dot(_as_bf16_rows(vt_ref[...]), _as_bf16_rows(g_old[...]),
                                preferred_element_type=F32)

    @pl.when(s % 2 == 0)
    def _():
        step(ht0_scr, ht1_scr, g1_scr, g0_scr)

    @pl.when(s % 2 == 1)
    def _():
        step(ht1_scr, ht0_scr, g0_scr, g1_scr)

    @pl.when((p2 >= 0) & (c2 == n_chunks - 1))
    def _():
        y = h_ref[...] + g2_ref[0] * acc_scr[...].T
        if final_norm:
            y = (y * lax.rsqrt(jnp.mean(y * y, axis=-1, keepdims=True) + RMS_EPS)) * fg_ref[...]
        o_ref[...] = y


def _peer_experts(x8, u8, inv_row, vt_pk, cnt, e1, rk, e2, h2d, seq_len, g2, final_g, final_norm):
    n, d = h2d.shape
    t = 512
    n_chunks = u8.shape[0] // _PEER_EC
    total = (n // t) * n_chunks
    ea = _PEER_EC // PEER_NKEYS

    def pair(p):
        p = jnp.clip(p, 0, total - 1)
        return p // n_chunks, lax.rem(p, n_chunks)

    blk = lambda lag: (lambda s: pair(s - lag)[0])
    chk = lambda lag: (lambda s: pair(s - lag)[1])
    tok = pl.BlockSpec((PEER_HEADS, PEER_NKEYS // 2, t), lambda s: (0, 0, blk(1)(s)))
    first = pl.BlockSpec((PEER_HEADS, ea, t), lambda s: (0, chk(1)(s), blk(1)(s)))
    bm = _batch_map(t, seq_len, g2.shape[0])
    return pl.pallas_call(
        functools.partial(_peer_expert_kernel, n_chunks, final_norm),
        grid=(total + 2,),
        in_specs=[
            pl.BlockSpec((t, d), lambda s: (blk(0)(s), 0)),
            pl.BlockSpec((_PEER_EC, d), lambda s: (chk(0)(s), 0)),
            pl.BlockSpec((1, t), lambda s: (0, blk(1)(s))),
            pl.BlockSpec((d // 2, _PEER_EC), lambda s: (0, chk(2)(s))),
            first, first, tok, tok,
            pl.BlockSpec((t, d), lambda s: (blk(2)(s), 0)),
            pl.BlockSpec((1, 1, d), lambda s: bm(blk(2)(s))),
            pl.BlockSpec((1, d), lambda s: (0, 0)),
        ],
        out_specs=pl.BlockSpec((t, d), lambda s: (blk(2)(s), 0)),
        out_shape=jax.ShapeDtypeStruct((n, d), F32),
        scratch_shapes=[pltpu.VMEM((_PEER_EC, t), F32), pltpu.VMEM((_PEER_EC, t), F32),
                        pltpu.VMEM((_PEER_EC // 2, t), jnp.uint32), pltpu.VMEM((_PEER_EC // 2, t), jnp.uint32),
                        pltpu.VMEM((d, t), F32)],
        compiler_params=_params("arbitrary"),
        name="peer_experts",
    )(x8, u8, inv_row, vt_pk, cnt, e1, rk, e2, h2d, g2, final_g.reshape(1, d))


def _pack_rows_kernel(transpose, x_ref, o_ref):
    x = x_ref[0].T if transpose else x_ref[0]
    o_ref[...] = pltpu.bitcast(x.astype(BF16), jnp.uint32)


def _pack_bf16_rows(stack, layer, transpose=False):
    tile = 512
    if transpose:
        _, c, r = stack.shape
        in_spec = pl.BlockSpec((1, tile, r), lambda i: (layer, i, 0))
        out_spec = pl.BlockSpec((r // 2, tile), lambda i: (0, i))
        steps = c // tile
    else:
        _, r, c = stack.shape
        in_spec = pl.BlockSpec((1, tile, c), lambda i: (layer, i, 0))
        out_spec = pl.BlockSpec((tile // 2, c), lambda i: (i, 0))
        steps = r // tile
    return pl.pallas_call(
        functools.partial(_pack_rows_kernel, transpose),
        grid=(steps,),
        in_specs=[in_spec],
        out_specs=out_spec,
        out_shape=jax.ShapeDtypeStruct((r // 2, c), jnp.uint32),
        compiler_params=_params("parallel"),
        name="pack_bf16_rows_t" if transpose else "pack_bf16_rows",
    )(stack)


def _table_absmax_kernel(x_ref, o_ref):
    @pl.when(pl.program_id(0) == 0)
    def _():
        o_ref[...] = jnp.zeros_like(o_ref)

    m = jnp.max(jnp.max(jnp.abs(x_ref[0]), axis=0, keepdims=True), axis=1, keepdims=True)
    o_ref[...] = jnp.maximum(o_ref[...], m)


def _quantize_kernel(s_ref, x_ref, o_ref):
    o_ref[...] = (x_ref[0] * s_ref[0]).astype(F8)


def _quantize_table(stack, layer):
    _, r, c = stack.shape
    tile = 512
    in_spec = pl.BlockSpec((1, tile, c), lambda i: (layer, i, 0))
    amax = pl.pallas_call(
        _table_absmax_kernel,
        grid=(r // tile,),
        in_specs=[in_spec],
        out_specs=pl.BlockSpec((8, LANES), lambda i: (0, 0)),
        out_shape=jax.ShapeDtypeStruct((8, LANES), F32),
        compiler_params=_params("arbitrary"),
        name="table_absmax",
    )(stack)[0, 0]
    exponent = jnp.floor(jnp.log2(jnp.maximum(amax, 2.0 ** -100)))
    scale = jnp.exp2(_F8_TARGET_EXP - exponent)
    q = pl.pallas_call(
        _quantize_kernel,
        grid=(r // tile,),
        in_specs=[pl.BlockSpec(memory_space=pltpu.SMEM), in_spec],
        out_specs=pl.BlockSpec((tile, c), lambda i: (i, 0)),
        out_shape=jax.ShapeDtypeStruct((r, c), F8),
        compiler_params=_params("parallel"),
        name="quantize_table",
    )(scale.reshape(1), stack)
    return q, 1.0 / scale


def _peer_ffn_residual(h2d, seq_len, norm_g, shift, scale, gate, wq_heads, keys_bf, u8, u_inv, vt_pk, final_g,
                       final_norm=False):
    x8, x_inv, cnt, e1, rk, e2 = _peer_scores(h2d, seq_len, norm_g, shift, scale, wq_heads, keys_bf)
    inv_row = x_inv.reshape(1, -1) * u_inv
    return _peer_experts(x8, u8, inv_row, vt_pk, cnt, e1, rk, e2, h2d, seq_len, gate, final_g, final_norm)


def _mixer_inputs(h2d, batch, seq_len, norm_g, shift, scale, w_perm, conv_w, conv_b):
    z, xbc, pool_u, q, k, v, dt = _project(h2d, seq_len, norm_g, shift, scale, w_perm)
    xbc_act = _conv_silu(xbc.reshape(batch, seq_len, SSD_XBC), conv_w, conv_b)
    dt3 = dt.reshape(batch, seq_len, LANES)
    dtt3 = jnp.swapaxes(dt3[:, :, :2 * SSD_HEADS], 1, 2)
    r3 = lambda a: a.reshape(batch, seq_len, a.shape[-1])
    return z, xbc_act, dt3, dtt3, r3(pool_u), r3(q), r3(k), r3(v)


def kernel(x, c, ctx, c_ctx, ada_w, ada_b, norm1_g, w_in, conv_w, conv_b, a_log, dt_bias, d_skip, ssd_norm_g, pool_w, pool_scale, na_rpb, w_out, norm2_g, peer_wq, peer_keys, peer_u, peer_v, final_g):
    batch, seq, d = x.shape
    ctx_len = ctx.shape[1]
    n, nc = batch * seq, batch * ctx_len
    h = x.reshape(n, d)
    hc = ctx.reshape(nc, d)

    c8 = jnp.concatenate([c, c_ctx[None], jnp.zeros((8 - batch - 1, d), F32)], axis=0)
    mod = _modulation(c8, ada_w, ada_b)

    for i in range(DEPTH):
        need_ctx_out = i < DEPTH - 1
        lat = [mod[i, :batch, j * d:(j + 1) * d].reshape(batch, 1, d) for j in range(6)]
        cx = [mod[i, batch:batch + 1, j * d:(j + 1) * d].reshape(1, 1, d) for j in range(6)]
        sh1, sc1, g1, sh2, sc2, g2 = lat
        csh1, csc1, cg1, csh2, csc2, cg2 = cx

        w_perm = _permute_w_in(w_in[i])
        w_out_bf = w_out[i].astype(BF16)
        n1 = norm1_g[i].reshape(1, d)
        n2 = norm2_g[i].reshape(1, d)
        zero_state = jnp.zeros((batch, SSD_HEADS // 2, SSD_STATE, 2 * SSD_HEAD_DIM), F32)
        scan = functools.partial(_ssd_scan, dtb=dt_bias[i], alog=a_log[i])
        wq_heads = peer_wq[i].reshape(d, PEER_HEADS, 2 * PEER_SUB).transpose(1, 0, 2).astype(BF16)
        keys_bf = peer_keys[i].astype(BF16)
        u8, u_inv = _quantize_table(peer_u, i)
        vt_pk = _pack_bf16_rows(peer_v, i, transpose=True)

        zc, xbc_c, dt3_c, dtt3_c, pool_c, qc, kc, vc = _mixer_inputs(
            hc, batch, ctx_len, n1, csh1, csc1, w_perm, conv_w[i], conv_b[i])
        yf_c, yb_c, st_f, st_b = scan(xbc_c, dt3_c, dtt3_c, init_f=zero_state, init_b=zero_state)
        if need_ctx_out:
            pool_yc = _pool_mixer(pool_c, pool_w[i], pool_scale[i])
            att_c = _context_attention(qc, kc, vc)
            hc = _mix_out(yf_c.reshape(nc, -1), yb_c.reshape(nc, -1), xbc_c.reshape(nc, -1), zc,
                          pool_yc.reshape(nc, -1), att_c.reshape(nc, -1), hc, ctx_len,
                          d_skip[i], ssd_norm_g[i], cg1, w_out_bf)
            hc = _peer_ffn_residual(hc, ctx_len, n2, csh2, csc2, cg2, wq_heads, keys_bf, u8, u_inv, vt_pk, final_g)

        z, xbc_l, dt3_l, dtt3_l, pool_l, q, k, v = _mixer_inputs(
            h, batch, seq, n1, sh1, sc1, w_perm, conv_w[i], conv_b[i])
        yf, yb, _, _ = scan(xbc_l, dt3_l, dtt3_l, init_f=st_f, init_b=st_b)
        pool_y = _pool_mixer(pool_l, pool_w[i], pool_scale[i])
        na = _neighbourhood_attention(q, k, v, kc, vc, _na_bias(na_rpb[i]))
        h = _mix_out(yf.reshape(n, -1), yb.reshape(n, -1), xbc_l.reshape(n, -1), z,
                     pool_y.reshape(n, -1), na.reshape(n, -1), h, seq,
                     d_skip[i], ssd_norm_g[i], g1, w_out_bf)
        h = _peer_ffn_residual(h, seq, n2, sh2, sc2, g2, wq_heads, keys_bf, u8, u_inv, vt_pk, final_g,
                               final_norm=i == DEPTH - 1)

    return h.reshape(batch, seq, d)
```

```python
import functools
import math

import jax
import jax.numpy as jnp
from jax import lax
from jax.experimental import pallas as pl
from jax.experimental.pallas import tpu as pltpu

F32 = jnp.float32
BF16 = jnp.bfloat16
F8 = jnp.float8_e4m3fn
HIGHEST = lax.Precision.HIGHEST

D_MODEL = 1024
DEPTH = 2
GRID_W = 64
RMS_EPS = 1e-6

SSD_HEAD_DIM = 64
SSD_HEADS = 16
SSD_GROUPS = 2
SSD_STATE = 128
SSD_CHUNK = 128
SSD_WIDTH = SSD_HEADS * SSD_HEAD_DIM
SSD_XBC = SSD_WIDTH + 2 * SSD_GROUPS * SSD_STATE

POOL_WINDOWS = (2, 4, 8, 16)
POOL_GROUP_DIM = 128
POOL_WIDTH = POOL_GROUP_DIM * len(POOL_WINDOWS)
POOL_PAD = 8

NA_HEAD_DIM = 64
NA_HEADS = 8
NA_WIDTH = NA_HEADS * NA_HEAD_DIM
NA_KH = 8
NA_KW = 16

PEER_HEADS = 8
PEER_NKEYS = 128
PEER_TOPK = 16
PEER_SUB = 128

LANES = 128
VMEM_LIMIT_BYTES = 56 * 1024 * 1024

_PROJ_SEGS = (SSD_WIDTH, SSD_XBC, POOL_WIDTH, NA_WIDTH, NA_WIDTH, NA_WIDTH, LANES)
_PROJ_DTYPES = (F32, F32, F32, BF16, BF16, BF16, F32)


def _params(*sem):
    return pltpu.CompilerParams(dimension_semantics=sem, vmem_limit_bytes=VMEM_LIMIT_BYTES)


def _rms_mod(x, g, scale, shift):
    ms = jnp.mean(x * x, axis=-1, keepdims=True)
    return (x * lax.rsqrt(ms + RMS_EPS)) * g * (1.0 + scale) + shift


def _silu(x):
    return x * jax.nn.sigmoid(x)


def _softplus(x):
    return jnp.maximum(x, 0.0) + jnp.log1p(jnp.exp(-jnp.abs(x)))


def _dot_nt(a, b):
    return lax.dot_general(a, b, (((1,), (1,)), ((), ())), preferred_element_type=F32)


def _batch_map(block_rows, seq_len, n_rows):
    if n_rows == 1:
        return lambda i, *_: (0, 0, 0)
    return lambda i, *_: ((i * block_rows) // seq_len, 0, 0)


def _mod_kernel(c_ref, w_ref, b_ref, o_ref):
    s = _silu(c_ref[...])
    o_ref[0] = jnp.dot(s, w_ref[0], precision=HIGHEST, preferred_element_type=F32) + b_ref[0]


def _modulation(c8, ada_w, ada_b):
    depth, d, six_d = ada_w.shape
    tn = 1024
    return pl.pallas_call(
        _mod_kernel,
        grid=(depth, six_d // tn),
        in_specs=[
            pl.BlockSpec((8, d), lambda l, j: (0, 0)),
            pl.BlockSpec((1, d, tn), lambda l, j: (l, 0, j)),
            pl.BlockSpec((1, 1, tn), lambda l, j: (l, 0, j)),
        ],
        out_specs=pl.BlockSpec((1, 8, tn), lambda l, j: (l, 0, j)),
        out_shape=jax.ShapeDtypeStruct((depth, 8, six_d), F32),
        compiler_params=_params("parallel", "parallel"),
        name="adaln_mod",
    )(c8, ada_w, ada_b.reshape(depth, 1, six_d))


def _proj_kernel(h_ref, g_ref, sh_ref, sc_ref, w_ref, *out_refs):
    u = _rms_mod(h_ref[...], g_ref[...], sc_ref[0], sh_ref[0]).astype(BF16)
    off = 0
    for o_ref, width in zip(out_refs, _PROJ_SEGS):
        o_ref[...] = jnp.dot(u, w_ref[:, off:off + width], preferred_element_type=F32).astype(o_ref.dtype)
        off += width


def _project(h2d, seq_len, g, shift, scale, w_perm):
    n, d = h2d.shape
    t = 512
    total = sum(_PROJ_SEGS)
    nb = shift.shape[0]
    assert n % t == 0 and (nb == 1 or seq_len % t == 0)
    row_map = lambda i: (i, 0)
    return pl.pallas_call(
        _proj_kernel,
        grid=(n // t,),
        in_specs=[
            pl.BlockSpec((t, d), row_map),
            pl.BlockSpec((1, d), lambda i: (0, 0)),
            pl.BlockSpec((1, 1, d), _batch_map(t, seq_len, nb)),
            pl.BlockSpec((1, 1, d), _batch_map(t, seq_len, nb)),
            pl.BlockSpec((d, total), lambda i: (0, 0)),
        ],
        out_specs=[pl.BlockSpec((t, w), row_map) for w in _PROJ_SEGS],
        out_shape=[jax.ShapeDtypeStruct((n, w), dt) for w, dt in zip(_PROJ_SEGS, _PROJ_DTYPES)],
        compiler_params=_params("parallel"),
        name="in_proj",
    )(h2d, g, shift, scale, w_perm)


def _permute_w_in(w_in):
    o = 0
    z = w_in[:, o:o + SSD_WIDTH]; o += SSD_WIDTH
    xbc = w_in[:, o:o + SSD_XBC]; o += SSD_XBC
    dt = w_in[:, o:o + 2 * SSD_HEADS]; o += 2 * SSD_HEADS
    rest = w_in[:, o:]
    dt = jnp.pad(dt, ((0, 0), (0, LANES - 2 * SSD_HEADS)))
    return jnp.concatenate([z, xbc, rest, dt], axis=1).astype(BF16)


def _conv_kernel(x_ref, w_ref, b_ref, o_ref):
    x = x_ref[0]
    n = x.shape[0]
    row = lax.broadcasted_iota(jnp.int32, x.shape, 0)
    prev = jnp.where(row == 0, 0.0, pltpu.roll(x, 1, 0))
    nxt = jnp.where(row == n - 1, 0.0, pltpu.roll(x, n - 1, 0))
    y = prev * w_ref[0:1, :] + x * w_ref[1:2, :] + nxt * w_ref[2:3, :] + b_ref[...]
    o_ref[0] = _silu(y)


def _conv_silu(xbc3, conv_w, conv_b):
    b, l, c = xbc3.shape
    tc = 256
    return pl.pallas_call(
        _conv_kernel,
        grid=(b, c // tc),
        in_specs=[
            pl.BlockSpec((1, l, tc), lambda i, j: (i, 0, j)),
            pl.BlockSpec((3, tc), lambda i, j: (0, j)),
            pl.BlockSpec((1, tc), lambda i, j: (0, j)),
        ],
        out_specs=pl.BlockSpec((1, l, tc), lambda i, j: (i, 0, j)),
        out_shape=jax.ShapeDtypeStruct((b, l, c), F32),
        compiler_params=_params("parallel", "parallel"),
        name="dwconv_silu",
    )(xbc3, conv_w, conv_b.reshape(1, c))


def _ssd_chunk(reverse, xbc_ref, dt_ref, dtt_ref, dtb_row, dtb_col, alog_row, alog_col, y_ref, state_scr):
    q = SSD_CHUNK
    col0 = SSD_HEADS if reverse else 0
    dt_l = _softplus(dt_ref[0] + dtb_row[...])
    a_l = dt_l * (-jnp.exp(alog_row[...]))
    dt_t = _softplus(dtt_ref[0] + dtb_col[...])
    a_t = dt_t * (-jnp.exp(alog_col[...]))
    row = lax.broadcasted_iota(jnp.int32, (q, q), 0)
    col = lax.broadcasted_iota(jnp.int32, (q, q), 1)
    lower = (row >= col).astype(F32)
    upper = (row <= col).astype(F32)
    cs_l = jnp.dot(lower, a_l, precision=HIGHEST, preferred_element_type=F32)
    cs_t = jnp.dot(a_t, upper, precision=HIGHEST, preferred_element_type=F32)
    tot_l = cs_l[q - 1:q, :]
    if reverse:
        p_l, p_t = cs_l - a_l, cs_t - a_t
        tri = row <= col
    else:
        p_l, p_t = cs_l, cs_t
        tri = row >= col

    tot_t = cs_t[:, q - 1:q]
    to_end_t = jnp.exp(p_t) if reverse else jnp.exp(tot_t - p_t)
    w_t = dt_t * to_end_t
    left = lax.broadcasted_iota(jnp.int32, (q, 2 * SSD_HEAD_DIM), 1) < SSD_HEAD_DIM

    xbc = xbc_ref[0]
    heads_per_group = SSD_HEADS // SSD_GROUPS
    for g in range(SSD_GROUPS):
        b_g = xbc[:, SSD_WIDTH + g * SSD_STATE:SSD_WIDTH + (g + 1) * SSD_STATE]
        c_g = xbc[:, SSD_WIDTH + (SSD_GROUPS + g) * SSD_STATE:SSD_WIDTH + (SSD_GROUPS + g + 1) * SSD_STATE]
        c_bf = c_g.astype(BF16)
        cb = _dot_nt(c_bf, b_g.astype(BF16))
        bt = b_g.T
        for pair in range(g * heads_per_group // 2, (g + 1) * heads_per_group // 2):
            lanes = slice(pair * 2 * SSD_HEAD_DIM, (pair + 1) * 2 * SSD_HEAD_DIM)
            xp = xbc[:, lanes]
            x_bd = jnp.concatenate([jnp.where(left, xp, 0.0), jnp.where(left, 0.0, xp)], axis=0).astype(BF16)
            decay_tiles, state_tiles, pcol_tiles = [], [], []
            for k in (col0 + 2 * pair, col0 + 2 * pair + 1):
                pcol_b = jnp.broadcast_to(p_l[:, k:k + 1], (q, q))
                prow = p_t[k:k + 1, :]
                seg = (prow - pcol_b) if reverse else (pcol_b - prow)
                lmat = jnp.exp(jnp.where(tri, seg, -jnp.inf)) * dt_t[k:k + 1, :]
                decay_tiles.append((cb * lmat).astype(BF16))
                state_tiles.append((bt * w_t[k:k + 1, :]).astype(BF16))
                pcol_tiles.append(pcol_b)
            k0 = col0 + 2 * pair
            tot_pair = jnp.where(left[0:1], tot_l[:, k0:k0 + 1], tot_l[:, k0 + 1:k0 + 2])
            pcol_pair = jnp.where(left, pcol_tiles[0], pcol_tiles[1])
            in_decay = jnp.exp(tot_pair - pcol_pair) if reverse else jnp.exp(pcol_pair)
            s_prev = state_scr[pair]
            y_diag = jnp.dot(jnp.concatenate(decay_tiles, axis=1), x_bd, preferred_element_type=F32)
            y_off = jnp.dot(c_bf, s_prev.astype(BF16), preferred_element_type=F32) * in_decay
            state_scr[pair] = jnp.exp(tot_pair) * s_prev + jnp.dot(jnp.concatenate(state_tiles, axis=1), x_bd,
                                                                   preferred_element_type=F32)
            y_ref[0, :, lanes] = y_diag + y_off


def _ssd_kernel(xf_ref, dtf_ref, dttf_ref, xb_ref, dtb_ref, dttb_ref, dtb_row, dtb_col, alog_row, alog_col,
                initf_ref, initb_ref, yf_ref, yb_ref, finf_ref, finb_ref, statef_scr, stateb_scr):
    c = pl.program_id(1)

    @pl.when(c == 0)
    def _():
        statef_scr[...] = initf_ref[0]
        stateb_scr[...] = initb_ref[0]

    params = (dtb_row, dtb_col, alog_row, alog_col)
    _ssd_chunk(False, xf_ref, dtf_ref, dttf_ref, *params, yf_ref, statef_scr)
    _ssd_chunk(True, xb_ref, dtb_ref, dttb_ref, *params, yb_ref, stateb_scr)

    @pl.when(c == pl.num_programs(1) - 1)
    def _():
        finf_ref[0] = statef_scr[...]
        finb_ref[0] = stateb_scr[...]


def _ssd_scan(xbc_act, dt3, dtt3, dtb, alog, init_f, init_b):
    b, l, _ = xbc_act.shape
    nc = l // SSD_CHUNK
    dtb_row = jnp.pad(dtb.reshape(1, -1), ((0, 0), (0, LANES - 2 * SSD_HEADS)))
    alog_row = jnp.pad(alog.reshape(1, -1), ((0, 0), (0, LANES - 2 * SSD_HEADS)))
    small = lambda shape: pl.BlockSpec(shape, lambda i, c: (0, 0))
    st_shape = (b, SSD_HEADS // 2, SSD_STATE, 2 * SSD_HEAD_DIM)
    st_spec = pl.BlockSpec((1,) + st_shape[1:], lambda i, c: (i, 0, 0, 0))
    fwd, bwd = (lambda c: c), (lambda c: nc - 1 - c)
    chunk_specs = lambda cm: [
        pl.BlockSpec((1, SSD_CHUNK, SSD_XBC), lambda i, c: (i, cm(c), 0)),
        pl.BlockSpec((1, SSD_CHUNK, LANES), lambda i, c: (i, cm(c), 0)),
        pl.BlockSpec((1, 2 * SSD_HEADS, SSD_CHUNK), lambda i, c: (i, 0, cm(c))),
    ]
    y_spec = lambda cm: pl.BlockSpec((1, SSD_CHUNK, SSD_WIDTH), lambda i, c: (i, cm(c), 0))
    y_shape = jax.ShapeDtypeStruct((b, l, SSD_WIDTH), F32)
    return pl.pallas_call(
        _ssd_kernel,
        grid=(b, nc),
        in_specs=chunk_specs(fwd) + chunk_specs(bwd)
        + [small((1, LANES)), small((2 * SSD_HEADS, 1)), small((1, LANES)), small((2 * SSD_HEADS, 1)), st_spec, st_spec],
        out_specs=[y_spec(fwd), y_spec(bwd), st_spec, st_spec],
        out_shape=[y_shape, y_shape, jax.ShapeDtypeStruct(st_shape, F32), jax.ShapeDtypeStruct(st_shape, F32)],
        scratch_shapes=[pltpu.VMEM(st_shape[1:], F32), pltpu.VMEM(st_shape[1:], F32)],
        compiler_params=_params("parallel", "arbitrary"),
        name="ssd_scan",
    )(xbc_act, dt3, dtt3, xbc_act, dt3, dtt3, dtb_row, dtb.reshape(-1, 1), alog_row, alog.reshape(-1, 1),
      init_f, init_b)


def _pool_kernel(x_ref, w_ref, sc_ref, o_ref, pad_scr):
    n = x_ref.shape[1]
    zeros = jnp.zeros((POOL_PAD, POOL_GROUP_DIM), F32)
    pad_scr[0:POOL_PAD, :] = zeros
    pad_scr[n + POOL_PAD:n + 2 * POOL_PAD, :] = zeros
    t = lax.broadcasted_iota(jnp.int32, (n, 1), 0)
    for gi, w in enumerate(POOL_WINDOWS):
        sl = slice(gi * POOL_GROUP_DIM, (gi + 1) * POOL_GROUP_DIM)
        x = x_ref[0, :, sl]
        pad_scr[POOL_PAD:n + POOL_PAD, :] = x
        acc = jnp.zeros_like(x)
        for o in range(-(w // 2), w - w // 2):
            acc = acc + pad_scr[POOL_PAD + o:POOL_PAD + o + n, :]
        lo = jnp.maximum(t - w // 2, 0)
        hi = jnp.minimum(t + (w - w // 2 - 1), n - 1)
        pooled = acc / (hi - lo + 1).astype(F32) - x
        y = jnp.dot(pooled.astype(BF16), w_ref[gi].astype(BF16), preferred_element_type=F32)
        o_ref[0, :, sl] = (y * sc_ref[:, sl]).astype(o_ref.dtype)


def _pool_mixer(u3, w_pool, scale):
    b, l, c = u3.shape
    return pl.pallas_call(
        _pool_kernel,
        grid=(b,),
        in_specs=[
            pl.BlockSpec((1, l, c), lambda i: (i, 0, 0)),
            pl.BlockSpec(w_pool.shape, lambda i: (0, 0, 0)),
            pl.BlockSpec((1, c), lambda i: (0, 0)),
        ],
        out_specs=pl.BlockSpec((1, l, c), lambda i: (i, 0, 0)),
        out_shape=jax.ShapeDtypeStruct((b, l, c), BF16),
        scratch_shapes=[pltpu.VMEM((l + 2 * POOL_PAD, POOL_GROUP_DIM), F32)],
        compiler_params=_params("parallel"),
        name="pool_mixer",
    )(u3, w_pool, scale.reshape(1, c))


def _na_bias_kernel(rpb_ref, o_ref):
    h = pl.program_id(0)
    qi = lax.broadcasted_iota(jnp.int32, (GRID_W, LANES), 0)
    lane = lax.broadcasted_iota(jnp.int32, (GRID_W, LANES), 1)
    ki = lane % GRID_W
    second = lane >= GRID_W
    start = jnp.clip(qi - NA_KW // 2, 0, GRID_W - NA_KW)
    in_window = (ki >= start) & (ki < start + NA_KW)
    dc = jnp.clip(ki - qi, -(NA_KW - 1), NA_KW - 1) + NA_KW - 1
    for dr in range(o_ref.shape[1]):
        val = jnp.zeros((GRID_W, LANES), F32)
        for j in range(2 * NA_KW - 1):
            pick = jnp.where(second, rpb_ref[h, dr + 1, j], rpb_ref[h, dr, j])
            val = jnp.where(dc == j, pick, val)
        o_ref[0, dr] = jnp.where(in_window, val, -jnp.inf)


def _na_bias(rpb):
    nh, ndr, ndc = rpb.shape
    return pl.pallas_call(
        _na_bias_kernel,
        grid=(nh,),
        in_specs=[pl.BlockSpec(memory_space=pltpu.SMEM)],
        out_specs=pl.BlockSpec((1, ndr - 1, GRID_W, LANES), lambda h: (h, 0, 0, 0)),
        out_shape=jax.ShapeDtypeStruct((nh, ndr - 1, GRID_W, LANES), F32),
        compiler_params=_params("parallel"),
        name="na_bias",
    )(rpb)


def _na_kernel(kh, q_ref, k_ref, v_ref, kc_ref, vc_ref, bias_ref, o_ref):
    r = pl.program_id(1)
    rows = pl.num_programs(1)
    r0 = jnp.clip(r - kh // 2, 0, rows - kh)
    start = pl.multiple_of(r0 * GRID_W, GRID_W)
    kblk = k_ref[0, pl.ds(start, kh * GRID_W), :]
    vblk = v_ref[0, pl.ds(start, kh * GRID_W), :]
    dr0 = r0 - r + NA_KH - 1
    scale = NA_HEAD_DIM ** -0.5
    pair_w = 2 * NA_HEAD_DIM
    halves = []
    for pair in range(NA_HEADS // 2):
        lanes = slice(pair * pair_w, (pair + 1) * pair_w)
        left_q = lax.broadcasted_iota(jnp.int32, (GRID_W, pair_w), 1) < NA_HEAD_DIM
        q_pair = q_ref[0, :, lanes] * scale
        k_pair, kc_pair = kblk[:, lanes], kc_ref[0, :, lanes]
        for side in range(2):
            keep = left_q if side == 0 else jnp.logical_not(left_q)
            q_h = jnp.where(keep, q_pair, 0.0)
            h = 2 * pair + side
            bias = jnp.concatenate([bias_ref[h, dr0 + 2 * j] for j in range(kh // 2)], axis=1)
            halves.append((_dot_nt(q_h, k_pair) + bias, _dot_nt(q_h, kc_pair)))
    probs = []
    for s_loc, s_ctx in halves:
        m = jnp.maximum(jnp.max(s_loc, axis=-1, keepdims=True), jnp.max(s_ctx, axis=-1, keepdims=True))
        p_loc = jnp.exp(s_loc - m)
        p_ctx = jnp.exp(s_ctx - m)
        inv = 1.0 / (jnp.sum(p_loc, axis=-1, keepdims=True) + jnp.sum(p_ctx, axis=-1, keepdims=True))
        probs.append((p_loc.astype(BF16), p_ctx.astype(BF16), inv))
    for pair in range(NA_HEADS // 2):
        lanes = slice(pair * pair_w, (pair + 1) * pair_w)
        v_pair, vc_pair = vblk[:, lanes], vc_ref[0, :, lanes]
        left_v = lax.broadcasted_iota(jnp.int32, v_pair.shape, 1) < NA_HEAD_DIM
        left_c = lax.broadcasted_iota(jnp.int32, vc_pair.shape, 1) < NA_HEAD_DIM
        out = None
        for side in range(2):
            p_loc, p_ctx, inv = probs[2 * pair + side]
            keep_v = left_v if side == 0 else jnp.logical_not(left_v)
            keep_c = left_c if side == 0 else jnp.logical_not(left_c)
            acc = jnp.dot(p_loc, jnp.where(keep_v, v_pair, 0.0), preferred_element_type=F32)
            acc = acc + jnp.dot(p_ctx, jnp.where(keep_c, vc_pair, 0.0), preferred_element_type=F32)
            out = acc * inv if out is None else out + acc * inv
        o_ref[0, :, lanes] = out.astype(o_ref.dtype)


def _neighbourhood_attention(q3, k3, v3, kc3, vc3, bias):
    b, s, c = q3.shape
    rows = s // GRID_W
    kh = min(NA_KH, rows)
    lc = kc3.shape[1]
    full = lambda n: pl.BlockSpec((1, n, c), lambda i, r: (i, 0, 0))
    return pl.pallas_call(
        functools.partial(_na_kernel, kh),
        grid=(b, rows),
        in_specs=[
            pl.BlockSpec((1, GRID_W, c), lambda i, r: (i, r, 0)),
            full(s), full(s), full(lc), full(lc),
            pl.BlockSpec(bias.shape, lambda i, r: (0, 0, 0, 0)),
        ],
        out_specs=pl.BlockSpec((1, GRID_W, c), lambda i, r: (i, r, 0)),
        out_shape=jax.ShapeDtypeStruct((b, s, c), BF16),
        compiler_params=_params("parallel", "arbitrary"),
        name="na_attention",
    )(q3, k3, v3, kc3, vc3, bias)


def _ctx_attn_kernel(q_ref, k_ref, v_ref, o_ref):
    scale = NA_HEAD_DIM ** -0.5
    for h in range(NA_HEADS):
        sl = slice(h * NA_HEAD_DIM, (h + 1) * NA_HEAD_DIM)
        s = _dot_nt(q_ref[0, :, sl] * scale, k_ref[0, :, sl])
        p = jnp.exp(s - jnp.max(s, axis=-1, keepdims=True))
        den = jnp.sum(p, axis=-1, keepdims=True)
        o_ref[0, :, sl] = (jnp.dot(p.astype(BF16), v_ref[0, :, sl], preferred_element_type=F32) / den).astype(o_ref.dtype)


def _context_attention(q3, k3, v3):
    b, l, c = q3.shape
    spec = pl.BlockSpec((1, l, c), lambda i: (i, 0, 0))
    return pl.pallas_call(
        _ctx_attn_kernel,
        grid=(b,),
        in_specs=[spec, spec, spec],
        out_specs=spec,
        out_shape=jax.ShapeDtypeStruct((b, l, c), BF16),
        compiler_params=_params("parallel"),
        name="ctx_attention",
    )(q3, k3, v3)


def _mix_out_kernel(yf_ref, yb_ref, xs_ref, z_ref, pool_ref, na_ref, h_ref, dsk_ref, ng_ref, g1_ref, w_ref, o_ref):
    y = yf_ref[...] + yb_ref[...] + dsk_ref[...] * xs_ref[...]
    y = y * _silu(z_ref[...])
    ms = jnp.mean(y * y, axis=-1, keepdims=True)
    yn = (y * lax.rsqrt(ms + RMS_EPS)) * ng_ref[...]
    mix = jnp.dot(yn.astype(BF16), w_ref[0:SSD_WIDTH, :], preferred_element_type=F32)
    mix = mix + jnp.dot(pool_ref[...].astype(BF16), w_ref[SSD_WIDTH:SSD_WIDTH + POOL_WIDTH, :],
                        preferred_element_type=F32)
    mix = mix + jnp.dot(na_ref[...].astype(BF16), w_ref[SSD_WIDTH + POOL_WIDTH:, :], preferred_element_type=F32)
    o_ref[...] = h_ref[...] + g1_ref[0] * mix


def _mix_out(yf, yb, xbc_act2d, z, pool_y, na_y, h2d, seq_len, d_skip, norm_g, g1, w_out_bf):
    n, d = h2d.shape
    t = 256
    row = lambda w: pl.BlockSpec((t, w), lambda i: (i, 0))
    vec = lambda w: pl.BlockSpec((1, w), lambda i: (0, 0))
    return pl.pallas_call(
        _mix_out_kernel,
        grid=(n // t,),
        in_specs=[
            row(SSD_WIDTH), row(SSD_WIDTH), row(SSD_WIDTH), row(SSD_WIDTH), row(POOL_WIDTH), row(NA_WIDTH), row(d),
            vec(SSD_WIDTH), vec(SSD_WIDTH),
            pl.BlockSpec((1, 1, d), _batch_map(t, seq_len, g1.shape[0])),
            pl.BlockSpec(w_out_bf.shape, lambda i: (0, 0)),
        ],
        out_specs=row(d),
        out_shape=jax.ShapeDtypeStruct((n, d), F32),
        compiler_params=_params("parallel"),
        name="mix_out",
    )(yf, yb, xbc_act2d, z, pool_y, na_y, h2d, jnp.repeat(d_skip, SSD_HEAD_DIM).reshape(1, -1),
      norm_g.reshape(1, -1), g1, w_out_bf)


_CAND_ROWS = 16 + 8 * 7 + 8


def _batcher_pairs(n):
    pairs = []

    def merge(lo, m, r):
        step = 2 * r
        if step < m:
            merge(lo, m, step)
            merge(lo + r, m, step)
            pairs.extend((i, i + r) for i in range(lo + r, lo + m - r, step))
        else:
            pairs.append((lo, lo + r))

    def sort(lo, m):
        if m > 1:
            sort(lo, m // 2)
            sort(lo + m // 2, m // 2)
            merge(lo, m, 1)

    sort(0, n)
    return tuple(pairs)


_SORT16 = _batcher_pairs(PEER_TOPK)
_BITONIC16 = tuple((i, i + d) for d in (8, 4, 2, 1) for i in range(PEER_TOPK) if not i & d)


def _exchange(x, pairs):
    for i, j in pairs:
        x[i], x[j] = jnp.maximum(x[i], x[j]), jnp.minimum(x[i], x[j])


def _sorted_top16(s):
    tiles = []
    for l0 in range(0, s.shape[1], LANES):
        x = [s[8 * v:8 * v + 8, l0:l0 + LANES] for v in range(PEER_TOPK)]
        _exchange(x, _SORT16)
        for shift in (4, 2, 1):
            x = [jnp.maximum(x[i], pltpu.roll(x[PEER_TOPK - 1 - i], shift, 0)) for i in range(PEER_TOPK)]
            _exchange(x, _BITONIC16)
        tiles.append(jnp.concatenate([xi[0:1] for xi in x], axis=0))
    return jnp.concatenate(tiles, axis=1)


def _count_above(s, v):
    r = lambda j: v[j:j + 1]
    pick = jnp.where
    a = s < r(7)
    b = s < pick(a, r(11), r(3))
    c = s < pick(a, pick(b, r(13), r(9)), pick(b, r(5), r(1)))
    d = s < pick(a, pick(b, pick(c, r(14), r(12)), pick(c, r(10), r(8))),
                 pick(b, pick(c, r(6), r(4)), pick(c, r(2), r(0))))
    return (pick(a, 8.0, 0.0) + pick(b, 4.0, 0.0) + pick(c, 2.0, 0.0) + pick(d, 1.0, 0.0)
            + pick(s < r(15), 1.0, 0.0))


def _top16_pair_fast(s1, s2):
    v1, v2 = _sorted_top16(s1), _sorted_top16(s2)
    rank2 = _count_above(s2, v2)

    def check(s, v):
        n = jnp.sum(jnp.where(s >= v[PEER_TOPK - 1:PEER_TOPK], 1.0, 0.0), axis=0, keepdims=True)
        strict = jnp.min(v[:-1] - v[1:], axis=0, keepdims=True) > 0.0
        return jnp.where(strict, n, 0.0)

    return v1, check(s1, v1), v2, rank2, check(s2, v2)


def _first_counts_fast(s1, v1, sel):
    inf = jnp.inf
    rows_v1 = jnp.concatenate([jnp.broadcast_to(v1[0:1], (PEER_TOPK, v1.shape[1]))]
                              + [jnp.broadcast_to(v1[j:j + 1], (8, v1.shape[1])) for j in range(1, 8)]
                              + [v1[8:16]], axis=0)
    bound = jnp.where(sel > 0.0, rows_v1, inf)
    low = bound[0:8]
    for j in range(1, 8):
        low = jnp.minimum(low, bound[16 + 8 * (j - 1):16 + 8 * j])
    tail = jnp.min(bound[64 + 8:64 + 16], axis=0, keepdims=True)
    t = [jnp.minimum(low[0:1], tail)] + [low[k:k + 1] for k in range(1, 8)]
    pick = jnp.where
    a = s1 >= t[3]
    b = s1 >= pick(a, t[5], t[1])
    c = s1 >= pick(a, pick(b, t[6], t[4]), pick(b, t[2], t[0]))
    cnt = pick(a, 4.0, 0.0) + pick(b, 2.0, 0.0) + pick(c, 1.0, 0.0) + pick(s1 >= t[7], 1.0, 0.0)
    n_high = jnp.sum(sel[8:16], axis=0, keepdims=True)
    return cnt + jnp.where(s1 >= v1[0:1], n_high, 0.0)


def _first_counts_exact(rank1, sel):
    cnt = jnp.zeros(rank1.shape, F32)
    for j in range(8):
        lo = 0 if j == 0 else 16 + 8 * (j - 1)
        n_j = jnp.sum(sel[lo:lo + (16 if j == 0 else 8)], axis=0, keepdims=True)
        cnt = cnt + jnp.where(rank1 == float(j), n_j, 0.0)
    for j in range(8, 16):
        cnt = cnt + jnp.where(rank1 == float(j), sel[64 + j:65 + j], 0.0)
    return cnt


def _top16_pair_exact(s1, s2):
    n, t = s1.shape
    iota = lax.broadcasted_iota(jnp.int32, (n, t), 0).astype(F32)
    row16 = lax.broadcasted_iota(jnp.int32, (PEER_TOPK, t), 0)

    def pick(work, rank, vals, j):
        m = jnp.max(work, axis=0, keepdims=True)
        idx = jnp.min(jnp.where(work == m, iota, float(n)), axis=0, keepdims=True)
        sel = iota == idx
        return (jnp.where(sel, -jnp.inf, work), jnp.where(sel, lax.convert_element_type(j, F32), rank),
                jnp.where(row16 == j, m, vals))

    def body(j, carry):
        a, b = carry
        return pick(*a, j), pick(*b, j)

    start = lambda s: (s, jnp.full((n, t), float(PEER_TOPK), F32), jnp.zeros((PEER_TOPK, t), F32))
    (_, rank1, v1), (_, rank2, v2) = lax.fori_loop(0, PEER_TOPK, body, (start(s1), start(s2)))
    return rank1, v1, rank2, v2


def _select16_fast(cand):
    rows, t = cand.shape
    padded = jnp.concatenate([cand, jnp.full((PEER_NKEYS - rows, t), -jnp.inf, F32)], axis=0)
    m = _sorted_top16(padded)[PEER_TOPK - 1:PEER_TOPK]
    taken = jnp.where(cand >= m, 1.0, 0.0)
    return taken, jnp.sum(taken, axis=0, keepdims=True)


def _select16_exact(cand):
    iota = lax.broadcasted_iota(jnp.int32, cand.shape, 0).astype(F32)

    def body(_, carry):
        work, sel_acc = carry
        m = jnp.max(work, axis=0, keepdims=True)
        idx = jnp.min(jnp.where(work == m, iota, float(_CAND_ROWS)), axis=0, keepdims=True)
        sel = iota == idx
        return jnp.where(sel, -jnp.inf, work), jnp.where(sel, 1.0, sel_acc)

    return lax.fori_loop(0, PEER_TOPK, body, (cand, jnp.zeros(cand.shape, F32)))[1]


def _candidate_sums(v1, v2):
    blocks = [v1[0:1] + v2]
    for j in range(1, 8):
        blocks.append(v1[j:j + 1] + v2[0:8])
    blocks.append(v1[8:16] + v2[0:1])
    return jnp.concatenate(blocks, axis=0)


def _any_not_16(*counts):
    return jnp.max(sum(jnp.abs(c - float(PEER_TOPK)) for c in counts)) > 0.0


def _bf16_bits(x):
    return pltpu.bitcast(x.astype(BF16).astype(F32), jnp.uint32)


def _pack_row_pairs(x, scr):
    n, t = x.shape
    for j in range(t // LANES):
        scr[j] = x[:, j * LANES:(j + 1) * LANES]
    words = []
    for j in range(t // LANES):
        even = scr[j, pl.ds(0, n // 2, stride=2), :]
        odd = scr[j, pl.ds(1, n // 2, stride=2), :]
        words.append((_bf16_bits(even) >> 16) | _bf16_bits(odd))
    return jnp.concatenate(words, axis=1)


def _pack_same(x):
    w = _bf16_bits(x)
    return w | (w >> 16)


_F8_TARGET_EXP = 6


def _pow2_scale(amax):
    bits = pltpu.bitcast(jnp.maximum(amax, 2.0 ** -100), jnp.int32)
    exponent = (bits >> 23) - 127
    return pltpu.bitcast((_F8_TARGET_EXP - exponent + 127) << 23, F32)


def _peer_score_kernel(h_ref, g_ref, sh_ref, sc_ref, wq_ref, wq_next_ref, keys_ref,
                       x_out, xinv_out, cnt_out, e1_out, rk_out, e2_out,
                       u_scr, q_scr, pair_scr, cnt_scr, rank2_scr, top_scr):
    hd = pl.program_id(1)

    @pl.when(hd == 0)
    def _():
        u = _rms_mod(h_ref[...], g_ref[...], sc_ref[0], sh_ref[0])
        u_scr[...] = u.astype(BF16)
        scale = _pow2_scale(jnp.max(jnp.abs(u), axis=-1, keepdims=True))
        x_out[...] = (u * scale).astype(F8)
        xinv_out[...] = 1.0 / scale
        q_scr[...] = jnp.dot(u_scr[...], wq_ref[0], preferred_element_type=F32).astype(BF16)

    q = q_scr[...]
    s1 = _dot_nt(keys_ref[0, 0], q[:, :PEER_SUB])
    s2 = _dot_nt(keys_ref[0, 1], q[:, PEER_SUB:])
    v1, n1, v2, rank2, n2 = _top16_pair_fast(s1, s2)
    cand = _candidate_sums(v1, v2)
    sel, n_sel = _select16_fast(cand)
    cnt_scr[...] = _first_counts_fast(s1, v1, sel)
    rank2_scr[...] = rank2
    top_scr[0:1] = v1[0:1]
    top_scr[1:2] = v2[0:1]
    top_scr[2:3] = jnp.sum(sel * jnp.exp(cand - cand[0:1]), axis=0, keepdims=True)
    q_scr[...] = jnp.dot(u_scr[...], wq_next_ref[0], preferred_element_type=F32).astype(BF16)

    @pl.when(_any_not_16(n1, n2, n_sel))
    def _():
        rank1_x, v1_x, rank2_x, v2_x = _top16_pair_exact(s1, s2)
        cand_x = _candidate_sums(v1_x, v2_x)
        sel_x = _select16_exact(cand_x)
        cnt_scr[...] = _first_counts_exact(rank1_x, sel_x)
        rank2_scr[...] = rank2_x
        top_scr[0:1] = v1_x[0:1]
        top_scr[1:2] = v2_x[0:1]
        top_scr[2:3] = jnp.sum(sel_x * jnp.exp(cand_x - cand_x[0:1]), axis=0, keepdims=True)

    cnt_out[0] = _pack_same(cnt_scr[...])
    e1_out[0] = _pack_same(jnp.exp(s1 - top_scr[0:1]))
    rk_out[0] = _pack_row_pairs(rank2_scr[...], pair_scr)
    e2_out[0] = _pack_row_pairs(jnp.exp(s2 - top_scr[1:2]) / top_scr[2:3], pair_scr)


def _peer_scores(h2d, seq_len, g, shift, scale, wq_heads, keys_bf):
    n, d = h2d.shape
    nb = shift.shape[0]
    t = 512
    assert n % t == 0 and (nb == 1 or seq_len % t == 0)
    bm = _batch_map(t, seq_len, nb)
    mod_spec = pl.BlockSpec((1, 1, d), lambda i, hd: bm(i))
    first_out = pl.BlockSpec((1, PEER_NKEYS, t), lambda i, hd: (hd, 0, i))
    first_shape = jax.ShapeDtypeStruct((PEER_HEADS, PEER_NKEYS, n), jnp.uint32)
    second_out = pl.BlockSpec((1, PEER_NKEYS // 2, t), lambda i, hd: (hd, 0, i))
    second_shape = jax.ShapeDtypeStruct((PEER_HEADS, PEER_NKEYS // 2, n), jnp.uint32)
    return pl.pallas_call(
        _peer_score_kernel,
        grid=(n // t, PEER_HEADS),
        in_specs=[
            pl.BlockSpec((t, d), lambda i, hd: (i, 0)),
            pl.BlockSpec((1, d), lambda i, hd: (0, 0)),
            mod_spec, mod_spec,
            pl.BlockSpec((1, d, 2 * PEER_SUB), lambda i, hd: (hd, 0, 0)),
            pl.BlockSpec((1, d, 2 * PEER_SUB), lambda i, hd: (jnp.minimum(hd + 1, PEER_HEADS - 1), 0, 0)),
            pl.BlockSpec((1, 2, PEER_NKEYS, PEER_SUB), lambda i, hd: (hd, 0, 0, 0)),
        ],
        out_specs=[pl.BlockSpec((t, d), lambda i, hd: (i, 0)), pl.BlockSpec((t, 1), lambda i, hd: (i, 0)),
                   first_out, first_out, second_out, second_out],
        out_shape=[jax.ShapeDtypeStruct((n, d), F8), jax.ShapeDtypeStruct((n, 1), F32),
                   first_shape, first_shape, second_shape, second_shape],
        scratch_shapes=[pltpu.VMEM((t, d), BF16), pltpu.VMEM((t, 2 * PEER_SUB), BF16),
                        pltpu.VMEM((t // LANES, PEER_NKEYS, LANES), F32),
                        pltpu.VMEM((PEER_NKEYS, t), F32), pltpu.VMEM((PEER_NKEYS, t), F32),
                        pltpu.VMEM((8, t), F32)],
        compiler_params=_params("parallel", "arbitrary"),
        name="peer_scores",
    )(h2d, g, shift, scale, wq_heads, wq_heads, keys_bf)


_PEER_EC = 1024
_INV_SQRT2 = 1.0 / math.sqrt(2.0)


def _as_bf16_rows(words):
    return pltpu.bitcast(words, BF16)


_PEER_A_GROUP = 4


def _peer_expert_kernel(n_chunks, total, final_norm, x_ref, u_ref, inv_ref, uinv_ref, vt_ref, cnt_ref, e1_ref, rk_ref,
                        e2_ref, h_ref,
                        g2_ref, fg_ref, o_ref,
                        ht0_scr, ht1_scr, g0_scr, g1_scr, acc_scr):
    s = pl.program_id(0)
    t = h_ref.shape[0]
    p2 = s - 2
    c2 = lax.rem(jnp.maximum(p2, 0), n_chunks)
    pack = 2 * 8
    gate_chunk = lax.rem(jnp.clip(s - 1, 0, total - 1), n_chunks)
    n_a = _PEER_EC // PEER_NKEYS

    @pl.when(s == 0)
    def _():
        ht1_scr[...] = jnp.zeros_like(ht1_scr)
        g0_scr[...] = jnp.zeros_like(g0_scr)
        g1_scr[...] = jnp.zeros_like(g1_scr)

    @pl.when((p2 <= 0) | (c2 == 0))
    def _():
        acc_scr[...] = jnp.zeros_like(acc_scr)

    def step(ht_new, ht_old, g_new, g_old):
        for tc in range(t // LANES):
            ls = slice(tc * LANES, (tc + 1) * LANES)
            for a0 in range(0, n_a, _PEER_A_GROUP):
                u_inv = uinv_ref[gate_chunk * (_PEER_EC // _QUANT_ROWS) + a0 * PEER_NKEYS // _QUANT_ROWS]
                half_inv = (0.5 * u_inv) * inv_ref[:, ls]
                c_inv = (_INV_SQRT2 * u_inv) * inv_ref[:, ls]
                w = [[None] * (PEER_NKEYS // pack) for _ in range(_PEER_A_GROUP)]
                for hd in range(PEER_HEADS):
                    rows1 = [(_as_bf16_rows(jnp.broadcast_to(cnt_ref[hd, a0 + i:a0 + i + 1, ls], (8, LANES))),
                              _as_bf16_rows(jnp.broadcast_to(e1_ref[hd, a0 + i:a0 + i + 1, ls], (8, LANES))))
                             for i in range(_PEER_A_GROUP)]
                    for r in range(PEER_NKEYS // pack):
                        rk = _as_bf16_rows(rk_ref[hd, r * 8:(r + 1) * 8, ls])
                        e2 = _as_bf16_rows(e2_ref[hd, r * 8:(r + 1) * 8, ls])
                        for i, (cn, e1) in enumerate(rows1):
                            term = jnp.where(rk < cn, e2, 0.0) * e1
                            w[i][r] = term if w[i][r] is None else w[i][r] + term
                for i in range(_PEER_A_GROUP):
                    for r in range(PEER_NKEYS // pack):
                        row0 = (a0 + i) * PEER_NKEYS + r * pack
                        hs = ht_old[row0:row0 + pack, ls]
                        act = (hs * half_inv) * (1.0 + lax.erf(hs * c_inv))
                        g_new[row0 // 2:(row0 + pack) // 2, ls] = pltpu.bitcast(w[i][r] * act.astype(BF16), jnp.uint32)
        ht_new[...] = _dot_nt(u_ref[...], x_ref[...])
        acc_scr[...] += jnp.dot(_as_bf16_rows(vt_ref[...]), _as_bf16_rows(g_old[...]),
                                preferred_element_type=F32)

    @pl.when(s % 2 == 0)
    def _():
        step(ht0_scr, ht1_scr, g1_scr, g0_scr)

    @pl.when(s % 2 == 1)
    def _():
        step(ht1_scr, ht0_scr, g0_scr, g1_scr)

    @pl.when((p2 >= 0) & (c2 == n_chunks - 1))
    def _():
        y = h_ref[...] + g2_ref[0] * acc_scr[...].T
        if final_norm:
            y = (y * lax.rsqrt(jnp.mean(y * y, axis=-1, keepdims=True) + RMS_EPS)) * fg_ref[...]
        o_ref[...] = y


def _peer_experts(x8, u8, inv_row, u_inv, vt_pk, cnt, e1, rk, e2, h2d, seq_len, g2, final_g, final_norm):
    n, d = h2d.shape
    t = 512
    n_chunks = u8.shape[0] // _PEER_EC
    total = (n // t) * n_chunks
    ea = _PEER_EC // PEER_NKEYS

    def pair(p):
        p = jnp.clip(p, 0, total - 1)
        return p // n_chunks, lax.rem(p, n_chunks)

    blk = lambda lag: (lambda s: pair(s - lag)[0])
    chk = lambda lag: (lambda s: pair(s - lag)[1])
    tok = pl.BlockSpec((PEER_HEADS, PEER_NKEYS // 2, t), lambda s: (0, 0, blk(1)(s)))
    first = pl.BlockSpec((PEER_HEADS, ea, t), lambda s: (0, chk(1)(s), blk(1)(s)))
    bm = _batch_map(t, seq_len, g2.shape[0])
    return pl.pallas_call(
        functools.partial(_peer_expert_kernel, n_chunks, total, final_norm),
        grid=(total + 2,),
        in_specs=[
            pl.BlockSpec((t, d), lambda s: (blk(0)(s), 0)),
            pl.BlockSpec((_PEER_EC, d), lambda s: (chk(0)(s), 0)),
            pl.BlockSpec((1, t), lambda s: (0, blk(1)(s))),
            pl.BlockSpec(memory_space=pltpu.SMEM),
            pl.BlockSpec((d // 2, _PEER_EC), lambda s: (0, chk(2)(s))),
            first, first, tok, tok,
            pl.BlockSpec((t, d), lambda s: (blk(2)(s), 0)),
            pl.BlockSpec((1, 1, d), lambda s: bm(blk(2)(s))),
            pl.BlockSpec((1, d), lambda s: (0, 0)),
        ],
        out_specs=pl.BlockSpec((t, d), lambda s: (blk(2)(s), 0)),
        out_shape=jax.ShapeDtypeStruct((n, d), F32),
        scratch_shapes=[pltpu.VMEM((_PEER_EC, t), F32), pltpu.VMEM((_PEER_EC, t), F32),
                        pltpu.VMEM((_PEER_EC // 2, t), jnp.uint32), pltpu.VMEM((_PEER_EC // 2, t), jnp.uint32),
                        pltpu.VMEM((d, t), F32)],
        compiler_params=_params("arbitrary"),
        name="peer_experts",
    )(x8, u8, inv_row, u_inv, vt_pk, cnt, e1, rk, e2, h2d, g2, final_g.reshape(1, d))


def _pack_rows_kernel(transpose, x_ref, o_ref):
    x = x_ref[0].T if transpose else x_ref[0]
    o_ref[...] = pltpu.bitcast(x.astype(BF16), jnp.uint32)


def _pack_bf16_rows(stack, layer, transpose=False):
    tile = 512
    if transpose:
        _, c, r = stack.shape
        in_spec = pl.BlockSpec((1, tile, r), lambda i: (layer, i, 0))
        out_spec = pl.BlockSpec((r // 2, tile), lambda i: (0, i))
        steps = c // tile
    else:
        _, r, c = stack.shape
        in_spec = pl.BlockSpec((1, tile, c), lambda i: (layer, i, 0))
        out_spec = pl.BlockSpec((tile // 2, c), lambda i: (i, 0))
        steps = r // tile
    return pl.pallas_call(
        functools.partial(_pack_rows_kernel, transpose),
        grid=(steps,),
        in_specs=[in_spec],
        out_specs=out_spec,
        out_shape=jax.ShapeDtypeStruct((r // 2, c), jnp.uint32),
        compiler_params=_params("parallel"),
        name="pack_bf16_rows_t" if transpose else "pack_bf16_rows",
    )(stack)


_QUANT_ROWS = 512


def _quantize_kernel(x_ref, o_ref, inv_ref):
    x = x_ref[0]
    scale = _pow2_scale(jnp.max(jnp.max(jnp.abs(x), axis=0, keepdims=True), axis=1, keepdims=True))
    o_ref[...] = (x * scale).astype(F8)
    inv_ref[0] = jnp.broadcast_to(1.0 / scale, inv_ref.shape[1:])


def _quantize_table(stack, layer):
    _, r, c = stack.shape
    blocks = r // _QUANT_ROWS
    q, inv = pl.pallas_call(
        _quantize_kernel,
        grid=(blocks,),
        in_specs=[pl.BlockSpec((1, _QUANT_ROWS, c), lambda i: (layer, i, 0))],
        out_specs=[pl.BlockSpec((_QUANT_ROWS, c), lambda i: (i, 0)), pl.BlockSpec((1, 8, LANES), lambda i: (i, 0, 0))],
        out_shape=[jax.ShapeDtypeStruct((r, c), F8), jax.ShapeDtypeStruct((blocks, 8, LANES), F32)],
        compiler_params=_params("parallel"),
        name="quantize_table",
    )(stack)
    return q, inv[:, 0, 0]


def _peer_ffn_residual(h2d, seq_len, norm_g, shift, scale, gate, wq_heads, keys_bf, u8, u_inv, vt_pk, final_g,
                       final_norm=False):
    x8, x_inv, cnt, e1, rk, e2 = _peer_scores(h2d, seq_len, norm_g, shift, scale, wq_heads, keys_bf)
    return _peer_experts(x8, u8, x_inv.reshape(1, -1), u_inv, vt_pk, cnt, e1, rk, e2, h2d, seq_len, gate, final_g, final_norm)


def _mixer_inputs(h2d, batch, seq_len, norm_g, shift, scale, w_perm, conv_w, conv_b):
    z, xbc, pool_u, q, k, v, dt = _project(h2d, seq_len, norm_g, shift, scale, w_perm)
    xbc_act = _conv_silu(xbc.reshape(batch, seq_len, SSD_XBC), conv_w, conv_b)
    dt3 = dt.reshape(batch, seq_len, LANES)
    dtt3 = jnp.swapaxes(dt3[:, :, :2 * SSD_HEADS], 1, 2)
    r3 = lambda a: a.reshape(batch, seq_len, a.shape[-1])
    return z, xbc_act, dt3, dtt3, r3(pool_u), r3(q), r3(k), r3(v)


def kernel(x, c, ctx, c_ctx, ada_w, ada_b, norm1_g, w_in, conv_w, conv_b, a_log, dt_bias, d_skip, ssd_norm_g, pool_w, pool_scale, na_rpb, w_out, norm2_g, peer_wq, peer_keys, peer_u, peer_v, final_g):
    batch, seq, d = x.shape
    ctx_len = ctx.shape[1]
    n, nc = batch * seq, batch * ctx_len
    h = x.reshape(n, d)
    hc = ctx.reshape(nc, d)

    c8 = jnp.concatenate([c, c_ctx[None], jnp.zeros((8 - batch - 1, d), F32)], axis=0)
    mod = _modulation(c8, ada_w, ada_b)

    for i in range(DEPTH):
        need_ctx_out = i < DEPTH - 1
        lat = [mod[i, :batch, j * d:(j + 1) * d].reshape(batch, 1, d) for j in range(6)]
        cx = [mod[i, batch:batch + 1, j * d:(j + 1) * d].reshape(1, 1, d) for j in range(6)]
        sh1, sc1, g1, sh2, sc2, g2 = lat
        csh1, csc1, cg1, csh2, csc2, cg2 = cx

        w_perm = _permute_w_in(w_in[i])
        w_out_bf = w_out[i].astype(BF16)
        n1 = norm1_g[i].reshape(1, d)
        n2 = norm2_g[i].reshape(1, d)
        zero_state = jnp.zeros((batch, SSD_HEADS // 2, SSD_STATE, 2 * SSD_HEAD_DIM), F32)
        scan = functools.partial(_ssd_scan, dtb=dt_bias[i], alog=a_log[i])
        wq_heads = peer_wq[i].reshape(d, PEER_HEADS, 2 * PEER_SUB).transpose(1, 0, 2).astype(BF16)
        keys_bf = peer_keys[i].astype(BF16)
        u8, u_inv = _quantize_table(peer_u, i)
        vt_pk = _pack_bf16_rows(peer_v, i, transpose=True)

        zc, xbc_c, dt3_c, dtt3_c, pool_c, qc, kc, vc = _mixer_inputs(
            hc, batch, ctx_len, n1, csh1, csc1, w_perm, conv_w[i], conv_b[i])
        yf_c, yb_c, st_f, st_b = scan(xbc_c, dt3_c, dtt3_c, init_f=zero_state, init_b=zero_state)
        if need_ctx_out:
            pool_yc = _pool_mixer(pool_c, pool_w[i], pool_scale[i])
            att_c = _context_attention(qc, kc, vc)
            hc = _mix_out(yf_c.reshape(nc, -1), yb_c.reshape(nc, -1), xbc_c.reshape(nc, -1), zc,
                          pool_yc.reshape(nc, -1), att_c.reshape(nc, -1), hc, ctx_len,
                          d_skip[i], ssd_norm_g[i], cg1, w_out_bf)
            hc = _peer_ffn_residual(hc, ctx_len, n2, csh2, csc2, cg2, wq_heads, keys_bf, u8, u_inv, vt_pk, final_g)

        z, xbc_l, dt3_l, dtt3_l, pool_l, q, k, v = _mixer_inputs(
            h, batch, seq, n1, sh1, sc1, w_perm, conv_w[i], conv_b[i])
        yf, yb, _, _ = scan(xbc_l, dt3_l, dtt3_l, init_f=st_f, init_b=st_b)
        pool_y = _pool_mixer(pool_l, pool_w[i], pool_scale[i])
        na = _neighbourhood_attention(q, k, v, kc, vc, _na_bias(na_rpb[i]))
        h = _mix_out(yf.reshape(n, -1), yb.reshape(n, -1), xbc_l.reshape(n, -1), z,
                     pool_y.reshape(n, -1), na.reshape(n, -1), h, seq,
                     d_skip[i], ssd_norm_g[i], g1, w_out_bf)
        h = _peer_ffn_residual(h, seq, n2, sh2, sc2, g2, wq_heads, keys_bf, u8, u_inv, vt_pk, final_g,
                               final_norm=i == DEPTH - 1)

    return h.reshape(batch, seq, d)
```

```python
import functools
import math

import jax
import jax.numpy as jnp
from jax import lax
from jax.experimental import pallas as pl
from jax.experimental.pallas import tpu as pltpu

F32 = jnp.float32
BF16 = jnp.bfloat16
F8 = jnp.float8_e4m3fn
HIGHEST = lax.Precision.HIGHEST

D_MODEL = 1024
DEPTH = 2
GRID_W = 64
RMS_EPS = 1e-6

SSD_HEAD_DIM = 64
SSD_HEADS = 16
SSD_GROUPS = 2
SSD_STATE = 128
SSD_CHUNK = 128
SSD_WIDTH = SSD_HEADS * SSD_HEAD_DIM
SSD_XBC = SSD_WIDTH + 2 * SSD_GROUPS * SSD_STATE

POOL_WINDOWS = (2, 4, 8, 16)
POOL_GROUP_DIM = 128
POOL_WIDTH = POOL_GROUP_DIM * len(POOL_WINDOWS)
POOL_PAD = 8

NA_HEAD_DIM = 64
NA_HEADS = 8
NA_WIDTH = NA_HEADS * NA_HEAD_DIM
NA_KH = 8
NA_KW = 16

PEER_HEADS = 8
PEER_NKEYS = 128
PEER_TOPK = 16
PEER_SUB = 128

LANES = 128
VMEM_LIMIT_BYTES = 56 * 1024 * 1024

_PROJ_SEGS = (SSD_WIDTH, SSD_XBC, POOL_WIDTH, NA_WIDTH, NA_WIDTH, NA_WIDTH, LANES)
_PROJ_DTYPES = (F32, F32, F32, BF16, BF16, BF16, F32)


def _params(*sem):
    return pltpu.CompilerParams(dimension_semantics=sem, vmem_limit_bytes=VMEM_LIMIT_BYTES)


def _rms_mod(x, g, scale, shift):
    ms = jnp.mean(x * x, axis=-1, keepdims=True)
    return (x * lax.rsqrt(ms + RMS_EPS)) * g * (1.0 + scale) + shift


def _silu(x):
    return x * jax.nn.sigmoid(x)


def _softplus(x):
    return jnp.maximum(x, 0.0) + jnp.log1p(jnp.exp(-jnp.abs(x)))


def _dot_nt(a, b):
    return lax.dot_general(a, b, (((1,), (1,)), ((), ())), preferred_element_type=F32)


def _batch_map(block_rows, seq_len, n_rows):
    if n_rows == 1:
        return lambda i, *_: (0, 0, 0)
    return lambda i, *_: ((i * block_rows) // seq_len, 0, 0)


def _mod_kernel(c_ref, w_ref, b_ref, o_ref):
    s = _silu(c_ref[...])
    o_ref[0] = jnp.dot(s, w_ref[0], precision=HIGHEST, preferred_element_type=F32) + b_ref[0]


def _modulation(c8, ada_w, ada_b):
    depth, d, six_d = ada_w.shape
    tn = 1024
    return pl.pallas_call(
        _mod_kernel,
        grid=(depth, six_d // tn),
        in_specs=[
            pl.BlockSpec((8, d), lambda l, j: (0, 0)),
            pl.BlockSpec((1, d, tn), lambda l, j: (l, 0, j)),
            pl.BlockSpec((1, 1, tn), lambda l, j: (l, 0, j)),
        ],
        out_specs=pl.BlockSpec((1, 8, tn), lambda l, j: (l, 0, j)),
        out_shape=jax.ShapeDtypeStruct((depth, 8, six_d), F32),
        compiler_params=_params("parallel", "parallel"),
        name="adaln_mod",
    )(c8, ada_w, ada_b.reshape(depth, 1, six_d))


def _proj_kernel(h_ref, g_ref, sh_ref, sc_ref, w_ref, *out_refs):
    u = _rms_mod(h_ref[...], g_ref[...], sc_ref[0], sh_ref[0]).astype(BF16)
    off = 0
    for o_ref, width in zip(out_refs, _PROJ_SEGS):
        o_ref[...] = jnp.dot(u, w_ref[:, off:off + width], preferred_element_type=F32).astype(o_ref.dtype)
        off += width


def _project(h2d, seq_len, g, shift, scale, w_perm):
    n, d = h2d.shape
    t = 512
    total = sum(_PROJ_SEGS)
    nb = shift.shape[0]
    assert n % t == 0 and (nb == 1 or seq_len % t == 0)
    row_map = lambda i: (i, 0)
    return pl.pallas_call(
        _proj_kernel,
        grid=(n // t,),
        in_specs=[
            pl.BlockSpec((t, d), row_map),
            pl.BlockSpec((1, d), lambda i: (0, 0)),
            pl.BlockSpec((1, 1, d), _batch_map(t, seq_len, nb)),
            pl.BlockSpec((1, 1, d), _batch_map(t, seq_len, nb)),
            pl.BlockSpec((d, total), lambda i: (0, 0)),
        ],
        out_specs=[pl.BlockSpec((t, w), row_map) for w in _PROJ_SEGS],
        out_shape=[jax.ShapeDtypeStruct((n, w), dt) for w, dt in zip(_PROJ_SEGS, _PROJ_DTYPES)],
        compiler_params=_params("parallel"),
        name="in_proj",
    )(h2d, g, shift, scale, w_perm)


def _permute_w_in(w_in):
    o = 0
    z = w_in[:, o:o + SSD_WIDTH]; o += SSD_WIDTH
    xbc = w_in[:, o:o + SSD_XBC]; o += SSD_XBC
    dt = w_in[:, o:o + 2 * SSD_HEADS]; o += 2 * SSD_HEADS
    rest = w_in[:, o:]
    dt = jnp.pad(dt, ((0, 0), (0, LANES - 2 * SSD_HEADS)))
    return jnp.concatenate([z, xbc, rest, dt], axis=1).astype(BF16)


def _conv_kernel(x_ref, w_ref, b_ref, o_ref):
    x = x_ref[0]
    n = x.shape[0]
    row = lax.broadcasted_iota(jnp.int32, x.shape, 0)
    prev = jnp.where(row == 0, 0.0, pltpu.roll(x, 1, 0))
    nxt = jnp.where(row == n - 1, 0.0, pltpu.roll(x, n - 1, 0))
    y = prev * w_ref[0:1, :] + x * w_ref[1:2, :] + nxt * w_ref[2:3, :] + b_ref[...]
    o_ref[0] = _silu(y)


def _conv_silu(xbc3, conv_w, conv_b):
    b, l, c = xbc3.shape
    tc = 256
    return pl.pallas_call(
        _conv_kernel,
        grid=(b, c // tc),
        in_specs=[
            pl.BlockSpec((1, l, tc), lambda i, j: (i, 0, j)),
            pl.BlockSpec((3, tc), lambda i, j: (0, j)),
            pl.BlockSpec((1, tc), lambda i, j: (0, j)),
        ],
        out_specs=pl.BlockSpec((1, l, tc), lambda i, j: (i, 0, j)),
        out_shape=jax.ShapeDtypeStruct((b, l, c), F32),
        compiler_params=_params("parallel", "parallel"),
        name="dwconv_silu",
    )(xbc3, conv_w, conv_b.reshape(1, c))


def _ssd_chunk(reverse, xbc_ref, dt_ref, dtt_ref, dtb_row, dtb_col, alog_row, alog_col, y_ref, state_scr):
    q = SSD_CHUNK
    col0 = SSD_HEADS if reverse else 0
    dt_l = _softplus(dt_ref[0] + dtb_row[...])
    a_l = dt_l * (-jnp.exp(alog_row[...]))
    dt_t = _softplus(dtt_ref[0] + dtb_col[...])
    a_t = dt_t * (-jnp.exp(alog_col[...]))
    row = lax.broadcasted_iota(jnp.int32, (q, q), 0)
    col = lax.broadcasted_iota(jnp.int32, (q, q), 1)
    lower = (row >= col).astype(F32)
    upper = (row <= col).astype(F32)
    cs_l = jnp.dot(lower, a_l, precision=HIGHEST, preferred_element_type=F32)
    cs_t = jnp.dot(a_t, upper, precision=HIGHEST, preferred_element_type=F32)
    tot_l = cs_l[q - 1:q, :]
    if reverse:
        p_l, p_t = cs_l - a_l, cs_t - a_t
        tri = row <= col
    else:
        p_l, p_t = cs_l, cs_t
        tri = row >= col

    tot_t = cs_t[:, q - 1:q]
    to_end_t = jnp.exp(p_t) if reverse else jnp.exp(tot_t - p_t)
    w_t = dt_t * to_end_t
    left = lax.broadcasted_iota(jnp.int32, (q, 2 * SSD_HEAD_DIM), 1) < SSD_HEAD_DIM

    xbc = xbc_ref[0]
    heads_per_group = SSD_HEADS // SSD_GROUPS
    for g in range(SSD_GROUPS):
        b_g = xbc[:, SSD_WIDTH + g * SSD_STATE:SSD_WIDTH + (g + 1) * SSD_STATE]
        c_g = xbc[:, SSD_WIDTH + (SSD_GROUPS + g) * SSD_STATE:SSD_WIDTH + (SSD_GROUPS + g + 1) * SSD_STATE]
        c_bf = c_g.astype(BF16)
        cb = _dot_nt(c_bf, b_g.astype(BF16))
        bt = b_g.T
        for pair in range(g * heads_per_group // 2, (g + 1) * heads_per_group // 2):
            lanes = slice(pair * 2 * SSD_HEAD_DIM, (pair + 1) * 2 * SSD_HEAD_DIM)
            xp = xbc[:, lanes]
            x_bd = jnp.concatenate([jnp.where(left, xp, 0.0), jnp.where(left, 0.0, xp)], axis=0).astype(BF16)
            decay_tiles, state_tiles, pcol_tiles = [], [], []
            for k in (col0 + 2 * pair, col0 + 2 * pair + 1):
                pcol_b = jnp.broadcast_to(p_l[:, k:k + 1], (q, q))
                prow = p_t[k:k + 1, :]
                seg = (prow - pcol_b) if reverse else (pcol_b - prow)
                lmat = jnp.exp(jnp.where(tri, seg, -jnp.inf)) * dt_t[k:k + 1, :]
                decay_tiles.append((cb * lmat).astype(BF16))
                state_tiles.append((bt * w_t[k:k + 1, :]).astype(BF16))
                pcol_tiles.append(pcol_b)
            k0 = col0 + 2 * pair
            tot_pair = jnp.where(left[0:1], tot_l[:, k0:k0 + 1], tot_l[:, k0 + 1:k0 + 2])
            pcol_pair = jnp.where(left, pcol_tiles[0], pcol_tiles[1])
            in_decay = jnp.exp(tot_pair - pcol_pair) if reverse else jnp.exp(pcol_pair)
            s_prev = state_scr[pair]
            y_diag = jnp.dot(jnp.concatenate(decay_tiles, axis=1), x_bd, preferred_element_type=F32)
            y_off = jnp.dot(c_bf, s_prev.astype(BF16), preferred_element_type=F32) * in_decay
            state_scr[pair] = jnp.exp(tot_pair) * s_prev + jnp.dot(jnp.concatenate(state_tiles, axis=1), x_bd,
                                                                   preferred_element_type=F32)
            y_ref[0, :, lanes] = y_diag + y_off


def _ssd_kernel(xf_ref, dtf_ref, dttf_ref, xb_ref, dtb_ref, dttb_ref, dtb_row, dtb_col, alog_row, alog_col,
                initf_ref, initb_ref, yf_ref, yb_ref, finf_ref, finb_ref, statef_scr, stateb_scr):
    c = pl.program_id(1)

    @pl.when(c == 0)
    def _():
        statef_scr[...] = initf_ref[0]
        stateb_scr[...] = initb_ref[0]

    params = (dtb_row, dtb_col, alog_row, alog_col)
    _ssd_chunk(False, xf_ref, dtf_ref, dttf_ref, *params, yf_ref, statef_scr)
    _ssd_chunk(True, xb_ref, dtb_ref, dttb_ref, *params, yb_ref, stateb_scr)

    @pl.when(c == pl.num_programs(1) - 1)
    def _():
        finf_ref[0] = statef_scr[...]
        finb_ref[0] = stateb_scr[...]


def _ssd_scan(xbc_act, dt3, dtt3, dtb, alog, init_f, init_b):
    b, l, _ = xbc_act.shape
    nc = l // SSD_CHUNK
    dtb_row = jnp.pad(dtb.reshape(1, -1), ((0, 0), (0, LANES - 2 * SSD_HEADS)))
    alog_row = jnp.pad(alog.reshape(1, -1), ((0, 0), (0, LANES - 2 * SSD_HEADS)))
    small = lambda shape: pl.BlockSpec(shape, lambda i, c: (0, 0))
    st_shape = (b, SSD_HEADS // 2, SSD_STATE, 2 * SSD_HEAD_DIM)
    st_spec = pl.BlockSpec((1,) + st_shape[1:], lambda i, c: (i, 0, 0, 0))
    fwd, bwd = (lambda c: c), (lambda c: nc - 1 - c)
    chunk_specs = lambda cm: [
        pl.BlockSpec((1, SSD_CHUNK, SSD_XBC), lambda i, c: (i, cm(c), 0)),
        pl.BlockSpec((1, SSD_CHUNK, LANES), lambda i, c: (i, cm(c), 0)),
        pl.BlockSpec((1, 2 * SSD_HEADS, SSD_CHUNK), lambda i, c: (i, 0, cm(c))),
    ]
    y_spec = lambda cm: pl.BlockSpec((1, SSD_CHUNK, SSD_WIDTH), lambda i, c: (i, cm(c), 0))
    y_shape = jax.ShapeDtypeStruct((b, l, SSD_WIDTH), F32)
    return pl.pallas_call(
        _ssd_kernel,
        grid=(b, nc),
        in_specs=chunk_specs(fwd) + chunk_specs(bwd)
        + [small((1, LANES)), small((2 * SSD_HEADS, 1)), small((1, LANES)), small((2 * SSD_HEADS, 1)), st_spec, st_spec],
        out_specs=[y_spec(fwd), y_spec(bwd), st_spec, st_spec],
        out_shape=[y_shape, y_shape, jax.ShapeDtypeStruct(st_shape, F32), jax.ShapeDtypeStruct(st_shape, F32)],
        scratch_shapes=[pltpu.VMEM(st_shape[1:], F32), pltpu.VMEM(st_shape[1:], F32)],
        compiler_params=_params("parallel", "arbitrary"),
        name="ssd_scan",
    )(xbc_act, dt3, dtt3, xbc_act, dt3, dtt3, dtb_row, dtb.reshape(-1, 1), alog_row, alog.reshape(-1, 1),
      init_f, init_b)


def _pool_kernel(x_ref, w_ref, sc_ref, o_ref, pad_scr):
    n = x_ref.shape[1]
    zeros = jnp.zeros((POOL_PAD, POOL_GROUP_DIM), F32)
    pad_scr[0:POOL_PAD, :] = zeros
    pad_scr[n + POOL_PAD:n + 2 * POOL_PAD, :] = zeros
    t = lax.broadcasted_iota(jnp.int32, (n, 1), 0)
    for gi, w in enumerate(POOL_WINDOWS):
        sl = slice(gi * POOL_GROUP_DIM, (gi + 1) * POOL_GROUP_DIM)
        x = x_ref[0, :, sl]
        pad_scr[POOL_PAD:n + POOL_PAD, :] = x
        acc = jnp.zeros_like(x)
        for o in range(-(w // 2), w - w // 2):
            acc = acc + pad_scr[POOL_PAD + o:POOL_PAD + o + n, :]
        lo = jnp.maximum(t - w // 2, 0)
        hi = jnp.minimum(t + (w - w // 2 - 1), n - 1)
        pooled = acc / (hi - lo + 1).astype(F32) - x
        y = jnp.dot(pooled.astype(BF16), w_ref[gi].astype(BF16), preferred_element_type=F32)
        o_ref[0, :, sl] = (y * sc_ref[:, sl]).astype(o_ref.dtype)


def _pool_mixer(u3, w_pool, scale):
    b, l, c = u3.shape
    return pl.pallas_call(
        _pool_kernel,
        grid=(b,),
        in_specs=[
            pl.BlockSpec((1, l, c), lambda i: (i, 0, 0)),
            pl.BlockSpec(w_pool.shape, lambda i: (0, 0, 0)),
            pl.BlockSpec((1, c), lambda i: (0, 0)),
        ],
        out_specs=pl.BlockSpec((1, l, c), lambda i: (i, 0, 0)),
        out_shape=jax.ShapeDtypeStruct((b, l, c), BF16),
        scratch_shapes=[pltpu.VMEM((l + 2 * POOL_PAD, POOL_GROUP_DIM), F32)],
        compiler_params=_params("parallel"),
        name="pool_mixer",
    )(u3, w_pool, scale.reshape(1, c))


def _na_bias_kernel(rpb_ref, o_ref):
    h = pl.program_id(0)
    qi = lax.broadcasted_iota(jnp.int32, (GRID_W, LANES), 0)
    lane = lax.broadcasted_iota(jnp.int32, (GRID_W, LANES), 1)
    ki = lane % GRID_W
    second = lane >= GRID_W
    start = jnp.clip(qi - NA_KW // 2, 0, GRID_W - NA_KW)
    in_window = (ki >= start) & (ki < start + NA_KW)
    dc = jnp.clip(ki - qi, -(NA_KW - 1), NA_KW - 1) + NA_KW - 1
    for dr in range(o_ref.shape[1]):
        val = jnp.zeros((GRID_W, LANES), F32)
        for j in range(2 * NA_KW - 1):
            pick = jnp.where(second, rpb_ref[h, dr + 1, j], rpb_ref[h, dr, j])
            val = jnp.where(dc == j, pick, val)
        o_ref[0, dr] = jnp.where(in_window, val, -jnp.inf)


def _na_bias(rpb):
    nh, ndr, ndc = rpb.shape
    return pl.pallas_call(
        _na_bias_kernel,
        grid=(nh,),
        in_specs=[pl.BlockSpec(memory_space=pltpu.SMEM)],
        out_specs=pl.BlockSpec((1, ndr - 1, GRID_W, LANES), lambda h: (h, 0, 0, 0)),
        out_shape=jax.ShapeDtypeStruct((nh, ndr - 1, GRID_W, LANES), F32),
        compiler_params=_params("parallel"),
        name="na_bias",
    )(rpb)


def _na_kernel(kh, q_ref, k_ref, v_ref, kc_ref, vc_ref, bias_ref, o_ref):
    r = pl.program_id(1)
    rows = pl.num_programs(1)
    r0 = jnp.clip(r - kh // 2, 0, rows - kh)
    start = pl.multiple_of(r0 * GRID_W, GRID_W)
    kblk = k_ref[0, pl.ds(start, kh * GRID_W), :]
    vblk = v_ref[0, pl.ds(start, kh * GRID_W), :]
    dr0 = r0 - r + NA_KH - 1
    scale = NA_HEAD_DIM ** -0.5
    pair_w = 2 * NA_HEAD_DIM
    halves = []
    for pair in range(NA_HEADS // 2):
        lanes = slice(pair * pair_w, (pair + 1) * pair_w)
        left_q = lax.broadcasted_iota(jnp.int32, (GRID_W, pair_w), 1) < NA_HEAD_DIM
        q_pair = q_ref[0, :, lanes] * scale
        k_pair, kc_pair = kblk[:, lanes], kc_ref[0, :, lanes]
        for side in range(2):
            keep = left_q if side == 0 else jnp.logical_not(left_q)
            q_h = jnp.where(keep, q_pair, 0.0)
            h = 2 * pair + side
            bias = jnp.concatenate([bias_ref[h, dr0 + 2 * j] for j in range(kh // 2)], axis=1)
            halves.append((_dot_nt(q_h, k_pair) + bias, _dot_nt(q_h, kc_pair)))
    probs = []
    for s_loc, s_ctx in halves:
        m = jnp.maximum(jnp.max(s_loc, axis=-1, keepdims=True), jnp.max(s_ctx, axis=-1, keepdims=True))
        p_loc = jnp.exp(s_loc - m)
        p_ctx = jnp.exp(s_ctx - m)
        inv = 1.0 / (jnp.sum(p_loc, axis=-1, keepdims=True) + jnp.sum(p_ctx, axis=-1, keepdims=True))
        probs.append((p_loc.astype(BF16), p_ctx.astype(BF16), inv))
    for pair in range(NA_HEADS // 2):
        lanes = slice(pair * pair_w, (pair + 1) * pair_w)
        v_pair, vc_pair = vblk[:, lanes], vc_ref[0, :, lanes]
        left_v = lax.broadcasted_iota(jnp.int32, v_pair.shape, 1) < NA_HEAD_DIM
        left_c = lax.broadcasted_iota(jnp.int32, vc_pair.shape, 1) < NA_HEAD_DIM
        out = None
        for side in range(2):
            p_loc, p_ctx, inv = probs[2 * pair + side]
            keep_v = left_v if side == 0 else jnp.logical_not(left_v)
            keep_c = left_c if side == 0 else jnp.logical_not(left_c)
            acc = jnp.dot(p_loc, jnp.where(keep_v, v_pair, 0.0), preferred_element_type=F32)
            acc = acc + jnp.dot(p_ctx, jnp.where(keep_c, vc_pair, 0.0), preferred_element_type=F32)
            out = acc * inv if out is None else out + acc * inv
        o_ref[0, :, lanes] = out.astype(o_ref.dtype)


def _neighbourhood_attention(q3, k3, v3, kc3, vc3, bias):
    b, s, c = q3.shape
    rows = s // GRID_W
    kh = min(NA_KH, rows)
    lc = kc3.shape[1]
    full = lambda n: pl.BlockSpec((1, n, c), lambda i, r: (i, 0, 0))
    return pl.pallas_call(
        functools.partial(_na_kernel, kh),
        grid=(b, rows),
        in_specs=[
            pl.BlockSpec((1, GRID_W, c), lambda i, r: (i, r, 0)),
            full(s), full(s), full(lc), full(lc),
            pl.BlockSpec(bias.shape, lambda i, r: (0, 0, 0, 0)),
        ],
        out_specs=pl.BlockSpec((1, GRID_W, c), lambda i, r: (i, r, 0)),
        out_shape=jax.ShapeDtypeStruct((b, s, c), BF16),
        compiler_params=_params("parallel", "arbitrary"),
        name="na_attention",
    )(q3, k3, v3, kc3, vc3, bias)


def _ctx_attn_kernel(q_ref, k_ref, v_ref, o_ref):
    scale = NA_HEAD_DIM ** -0.5
    for h in range(NA_HEADS):
        sl = slice(h * NA_HEAD_DIM, (h + 1) * NA_HEAD_DIM)
        s = _dot_nt(q_ref[0, :, sl] * scale, k_ref[0, :, sl])
        p = jnp.exp(s - jnp.max(s, axis=-1, keepdims=True))
        den = jnp.sum(p, axis=-1, keepdims=True)
        o_ref[0, :, sl] = (jnp.dot(p.astype(BF16), v_ref[0, :, sl], preferred_element_type=F32) / den).astype(o_ref.dtype)


def _context_attention(q3, k3, v3):
    b, l, c = q3.shape
    spec = pl.BlockSpec((1, l, c), lambda i: (i, 0, 0))
    return pl.pallas_call(
        _ctx_attn_kernel,
        grid=(b,),
        in_specs=[spec, spec, spec],
        out_specs=spec,
        out_shape=jax.ShapeDtypeStruct((b, l, c), BF16),
        compiler_params=_params("parallel"),
        name="ctx_attention",
    )(q3, k3, v3)


def _mix_out_kernel(yf_ref, yb_ref, xs_ref, z_ref, pool_ref, na_ref, h_ref, dsk_ref, ng_ref, g1_ref, w_ref, o_ref):
    y = yf_ref[...] + yb_ref[...] + dsk_ref[...] * xs_ref[...]
    y = y * _silu(z_ref[...])
    ms = jnp.mean(y * y, axis=-1, keepdims=True)
    yn = (y * lax.rsqrt(ms + RMS_EPS)) * ng_ref[...]
    mix = jnp.dot(yn.astype(BF16), w_ref[0:SSD_WIDTH, :], preferred_element_type=F32)
    mix = mix + jnp.dot(pool_ref[...].astype(BF16), w_ref[SSD_WIDTH:SSD_WIDTH + POOL_WIDTH, :],
                        preferred_element_type=F32)
    mix = mix + jnp.dot(na_ref[...].astype(BF16), w_ref[SSD_WIDTH + POOL_WIDTH:, :], preferred_element_type=F32)
    o_ref[...] = h_ref[...] + g1_ref[0] * mix


def _mix_out(yf, yb, xbc_act2d, z, pool_y, na_y, h2d, seq_len, d_skip, norm_g, g1, w_out_bf):
    n, d = h2d.shape
    t = 512
    assert n % t == 0 and (g1.shape[0] == 1 or seq_len % t == 0)
    row = lambda w: pl.BlockSpec((t, w), lambda i: (i, 0))
    vec = lambda w: pl.BlockSpec((1, w), lambda i: (0, 0))
    return pl.pallas_call(
        _mix_out_kernel,
        grid=(n // t,),
        in_specs=[
            row(SSD_WIDTH), row(SSD_WIDTH), row(SSD_WIDTH), row(SSD_WIDTH), row(POOL_WIDTH), row(NA_WIDTH), row(d),
            vec(SSD_WIDTH), vec(SSD_WIDTH),
            pl.BlockSpec((1, 1, d), _batch_map(t, seq_len, g1.shape[0])),
            pl.BlockSpec(w_out_bf.shape, lambda i: (0, 0)),
        ],
        out_specs=row(d),
        out_shape=jax.ShapeDtypeStruct((n, d), F32),
        compiler_params=_params("parallel"),
        name="mix_out",
    )(yf, yb, xbc_act2d, z, pool_y, na_y, h2d, jnp.repeat(d_skip, SSD_HEAD_DIM).reshape(1, -1),
      norm_g.reshape(1, -1), g1, w_out_bf)


_CAND_ROWS = 16 + 8 * 7 + 8


def _batcher_pairs(n):
    pairs = []

    def merge(lo, m, r):
        step = 2 * r
        if step < m:
            merge(lo, m, step)
            merge(lo + r, m, step)
            pairs.extend((i, i + r) for i in range(lo + r, lo + m - r, step))
        else:
            pairs.append((lo, lo + r))

    def sort(lo, m):
        if m > 1:
            sort(lo, m // 2)
            sort(lo + m // 2, m // 2)
            merge(lo, m, 1)

    sort(0, n)
    return tuple(pairs)


_SORT16 = _batcher_pairs(PEER_TOPK)
_BITONIC16 = tuple((i, i + d) for d in (8, 4, 2, 1) for i in range(PEER_TOPK) if not i & d)


def _exchange(x, pairs):
    for i, j in pairs:
        x[i], x[j] = jnp.maximum(x[i], x[j]), jnp.minimum(x[i], x[j])


def _sorted_top16(s):
    tiles = []
    for l0 in range(0, s.shape[1], LANES):
        x = [s[8 * v:8 * v + 8, l0:l0 + LANES] for v in range(PEER_TOPK)]
        _exchange(x, _SORT16)
        for shift in (4, 2, 1):
            x = [jnp.maximum(x[i], pltpu.roll(x[PEER_TOPK - 1 - i], shift, 0)) for i in range(PEER_TOPK)]
            _exchange(x, _BITONIC16)
        tiles.append(jnp.concatenate([xi[0:1] for xi in x], axis=0))
    return jnp.concatenate(tiles, axis=1)


def _count_above(s, v):
    r = lambda j: v[j:j + 1]
    pick = jnp.where
    a = s < r(7)
    b = s < pick(a, r(11), r(3))
    c = s < pick(a, pick(b, r(13), r(9)), pick(b, r(5), r(1)))
    d = s < pick(a, pick(b, pick(c, r(14), r(12)), pick(c, r(10), r(8))),
                 pick(b, pick(c, r(6), r(4)), pick(c, r(2), r(0))))
    return (pick(a, 8.0, 0.0) + pick(b, 4.0, 0.0) + pick(c, 2.0, 0.0) + pick(d, 1.0, 0.0)
            + pick(s < r(15), 1.0, 0.0))


def _top16_pair_fast(s1, s2):
    v1, v2 = _sorted_top16(s1), _sorted_top16(s2)
    rank2 = _count_above(s2, v2)

    def check(s, v):
        n = jnp.sum(jnp.where(s >= v[PEER_TOPK - 1:PEER_TOPK], 1.0, 0.0), axis=0, keepdims=True)
        strict = jnp.min(v[:-1] - v[1:], axis=0, keepdims=True) > 0.0
        return jnp.where(strict, n, 0.0)

    return v1, check(s1, v1), v2, rank2, check(s2, v2)


def _first_counts_fast(s1, v1, sel):
    inf = jnp.inf
    rows_v1 = jnp.concatenate([jnp.broadcast_to(v1[0:1], (PEER_TOPK, v1.shape[1]))]
                              + [jnp.broadcast_to(v1[j:j + 1], (8, v1.shape[1])) for j in range(1, 8)]
                              + [v1[8:16]], axis=0)
    bound = jnp.where(sel > 0.0, rows_v1, inf)
    low = bound[0:8]
    for j in range(1, 8):
        low = jnp.minimum(low, bound[16 + 8 * (j - 1):16 + 8 * j])
    tail = jnp.min(bound[64 + 8:64 + 16], axis=0, keepdims=True)
    t = [jnp.minimum(low[0:1], tail)] + [low[k:k + 1] for k in range(1, 8)]
    pick = jnp.where
    a = s1 >= t[3]
    b = s1 >= pick(a, t[5], t[1])
    c = s1 >= pick(a, pick(b, t[6], t[4]), pick(b, t[2], t[0]))
    cnt = pick(a, 4.0, 0.0) + pick(b, 2.0, 0.0) + pick(c, 1.0, 0.0) + pick(s1 >= t[7], 1.0, 0.0)
    n_high = jnp.sum(sel[8:16], axis=0, keepdims=True)
    return cnt + jnp.where(s1 >= v1[0:1], n_high, 0.0)


def _first_counts_exact(rank1, sel):
    cnt = jnp.zeros(rank1.shape, F32)
    for j in range(8):
        lo = 0 if j == 0 else 16 + 8 * (j - 1)
        n_j = jnp.sum(sel[lo:lo + (16 if j == 0 else 8)], axis=0, keepdims=True)
        cnt = cnt + jnp.where(rank1 == float(j), n_j, 0.0)
    for j in range(8, 16):
        cnt = cnt + jnp.where(rank1 == float(j), sel[64 + j:65 + j], 0.0)
    return cnt


def _top16_pair_exact(s1, s2):
    n, t = s1.shape
    iota = lax.broadcasted_iota(jnp.int32, (n, t), 0).astype(F32)
    row16 = lax.broadcasted_iota(jnp.int32, (PEER_TOPK, t), 0)

    def pick(work, rank, vals, j):
        m = jnp.max(work, axis=0, keepdims=True)
        idx = jnp.min(jnp.where(work == m, iota, float(n)), axis=0, keepdims=True)
        sel = iota == idx
        return (jnp.where(sel, -jnp.inf, work), jnp.where(sel, lax.convert_element_type(j, F32), rank),
                jnp.where(row16 == j, m, vals))

    def body(j, carry):
        a, b = carry
        return pick(*a, j), pick(*b, j)

    start = lambda s: (s, jnp.full((n, t), float(PEER_TOPK), F32), jnp.zeros((PEER_TOPK, t), F32))
    (_, rank1, v1), (_, rank2, v2) = lax.fori_loop(0, PEER_TOPK, body, (start(s1), start(s2)))
    return rank1, v1, rank2, v2


def _select16_fast(cand):
    rows, t = cand.shape
    padded = jnp.concatenate([cand, jnp.full((PEER_NKEYS - rows, t), -jnp.inf, F32)], axis=0)
    m = _sorted_top16(padded)[PEER_TOPK - 1:PEER_TOPK]
    taken = jnp.where(cand >= m, 1.0, 0.0)
    return taken, jnp.sum(taken, axis=0, keepdims=True)


def _select16_exact(cand):
    iota = lax.broadcasted_iota(jnp.int32, cand.shape, 0).astype(F32)

    def body(_, carry):
        work, sel_acc = carry
        m = jnp.max(work, axis=0, keepdims=True)
        idx = jnp.min(jnp.where(work == m, iota, float(_CAND_ROWS)), axis=0, keepdims=True)
        sel = iota == idx
        return jnp.where(sel, -jnp.inf, work), jnp.where(sel, 1.0, sel_acc)

    return lax.fori_loop(0, PEER_TOPK, body, (cand, jnp.zeros(cand.shape, F32)))[1]


def _candidate_sums(v1, v2):
    blocks = [v1[0:1] + v2]
    for j in range(1, 8):
        blocks.append(v1[j:j + 1] + v2[0:8])
    blocks.append(v1[8:16] + v2[0:1])
    return jnp.concatenate(blocks, axis=0)


def _any_not_16(*counts):
    return jnp.max(sum(jnp.abs(c - float(PEER_TOPK)) for c in counts)) > 0.0


def _bf16_bits(x):
    return pltpu.bitcast(x.astype(BF16).astype(F32), jnp.uint32)


def _pack_row_pairs(x, scr):
    n, t = x.shape
    for j in range(t // LANES):
        scr[j] = x[:, j * LANES:(j + 1) * LANES]
    words = []
    for j in range(t // LANES):
        even = scr[j, pl.ds(0, n // 2, stride=2), :]
        odd = scr[j, pl.ds(1, n // 2, stride=2), :]
        words.append((_bf16_bits(even) >> 16) | _bf16_bits(odd))
    return jnp.concatenate(words, axis=1)


def _pack_same(x):
    w = _bf16_bits(x)
    return w | (w >> 16)


_F8_TARGET_EXP = 6


def _pow2_scale(amax):
    bits = pltpu.bitcast(jnp.maximum(amax, 2.0 ** -100), jnp.int32)
    exponent = (bits >> 23) - 127
    return pltpu.bitcast((_F8_TARGET_EXP - exponent + 127) << 23, F32)


def _peer_score_kernel(h_ref, g_ref, sh_ref, sc_ref, wq_ref, wq_next_ref, keys_ref,
                       x_out, xinv_out, cnt_out, e1_out, rk_out, e2_out,
                       u_scr, q_scr, pair_scr, cnt_scr, rank2_scr, top_scr):
    hd = pl.program_id(1)

    @pl.when(hd == 0)
    def _():
        u = _rms_mod(h_ref[...], g_ref[...], sc_ref[0], sh_ref[0])
        u_scr[...] = u.astype(BF16)
        scale = _pow2_scale(jnp.max(jnp.abs(u), axis=-1, keepdims=True))
        x_out[...] = (u * scale).astype(F8)
        xinv_out[...] = 1.0 / scale
        q_scr[...] = jnp.dot(u_scr[...], wq_ref[0], preferred_element_type=F32).astype(BF16)

    q = q_scr[...]
    s1 = _dot_nt(keys_ref[0, 0], q[:, :PEER_SUB])
    s2 = _dot_nt(keys_ref[0, 1], q[:, PEER_SUB:])
    v1, n1, v2, rank2, n2 = _top16_pair_fast(s1, s2)
    cand = _candidate_sums(v1, v2)
    sel, n_sel = _select16_fast(cand)
    cnt_scr[...] = _first_counts_fast(s1, v1, sel)
    rank2_scr[...] = rank2
    top_scr[0:1] = v1[0:1]
    top_scr[1:2] = v2[0:1]
    top_scr[2:3] = jnp.sum(sel * jnp.exp(cand - cand[0:1]), axis=0, keepdims=True)
    q_scr[...] = jnp.dot(u_scr[...], wq_next_ref[0], preferred_element_type=F32).astype(BF16)

    @pl.when(_any_not_16(n1, n2, n_sel))
    def _():
        rank1_x, v1_x, rank2_x, v2_x = _top16_pair_exact(s1, s2)
        cand_x = _candidate_sums(v1_x, v2_x)
        sel_x = _select16_exact(cand_x)
        cnt_scr[...] = _first_counts_exact(rank1_x, sel_x)
        rank2_scr[...] = rank2_x
        top_scr[0:1] = v1_x[0:1]
        top_scr[1:2] = v2_x[0:1]
        top_scr[2:3] = jnp.sum(sel_x * jnp.exp(cand_x - cand_x[0:1]), axis=0, keepdims=True)

    cnt_out[0] = _pack_same(cnt_scr[...])
    e1_out[0] = _pack_same(jnp.exp(s1 - top_scr[0:1]))
    rk_out[0] = _pack_row_pairs(rank2_scr[...], pair_scr)
    e2_out[0] = _pack_row_pairs(jnp.exp(s2 - top_scr[1:2]) / top_scr[2:3], pair_scr)


def _peer_scores(h2d, seq_len, g, shift, scale, wq_heads, keys_bf):
    n, d = h2d.shape
    nb = shift.shape[0]
    t = 512
    assert n % t == 0 and (nb == 1 or seq_len % t == 0)
    bm = _batch_map(t, seq_len, nb)
    mod_spec = pl.BlockSpec((1, 1, d), lambda i, hd: bm(i))
    first_out = pl.BlockSpec((1, PEER_NKEYS, t), lambda i, hd: (hd, 0, i))
    first_shape = jax.ShapeDtypeStruct((PEER_HEADS, PEER_NKEYS, n), jnp.uint32)
    second_out = pl.BlockSpec((1, PEER_NKEYS // 2, t), lambda i, hd: (hd, 0, i))
    second_shape = jax.ShapeDtypeStruct((PEER_HEADS, PEER_NKEYS // 2, n), jnp.uint32)
    return pl.pallas_call(
        _peer_score_kernel,
        grid=(n // t, PEER_HEADS),
        in_specs=[
            pl.BlockSpec((t, d), lambda i, hd: (i, 0)),
            pl.BlockSpec((1, d), lambda i, hd: (0, 0)),
            mod_spec, mod_spec,
            pl.BlockSpec((1, d, 2 * PEER_SUB), lambda i, hd: (hd, 0, 0)),
            pl.BlockSpec((1, d, 2 * PEER_SUB), lambda i, hd: (jnp.minimum(hd + 1, PEER_HEADS - 1), 0, 0)),
            pl.BlockSpec((1, 2, PEER_NKEYS, PEER_SUB), lambda i, hd: (hd, 0, 0, 0)),
        ],
        out_specs=[pl.BlockSpec((t, d), lambda i, hd: (i, 0)), pl.BlockSpec((t, 1), lambda i, hd: (i, 0)),
                   first_out, first_out, second_out, second_out],
        out_shape=[jax.ShapeDtypeStruct((n, d), F8), jax.ShapeDtypeStruct((n, 1), F32),
                   first_shape, first_shape, second_shape, second_shape],
        scratch_shapes=[pltpu.VMEM((t, d), BF16), pltpu.VMEM((t, 2 * PEER_SUB), BF16),
                        pltpu.VMEM((t // LANES, PEER_NKEYS, LANES), F32),
                        pltpu.VMEM((PEER_NKEYS, t), F32), pltpu.VMEM((PEER_NKEYS, t), F32),
                        pltpu.VMEM((8, t), F32)],
        compiler_params=_params("parallel", "arbitrary"),
        name="peer_scores",
    )(h2d, g, shift, scale, wq_heads, wq_heads, keys_bf)


_PEER_EC = 1024
_INV_SQRT2 = 1.0 / math.sqrt(2.0)


def _as_bf16_rows(words):
    return pltpu.bitcast(words, BF16)


_PEER_A_GROUP = 4


def _peer_expert_kernel(n_chunks, total, final_norm, x_ref, u_ref, inv_ref, uinv_ref, vt_ref, cnt_ref, e1_ref, rk_ref,
                        e2_ref, h_ref,
                        g2_ref, fg_ref, o_ref,
                        ht0_scr, ht1_scr, g0_scr, g1_scr, acc_scr):
    s = pl.program_id(0)
    t = h_ref.shape[0]
    p2 = s - 2
    c2 = lax.rem(jnp.maximum(p2, 0), n_chunks)
    pack = 2 * 8
    gate_chunk = lax.rem(jnp.clip(s - 1, 0, total - 1), n_chunks)
    n_a = _PEER_EC // PEER_NKEYS

    @pl.when(s == 0)
    def _():
        ht1_scr[...] = jnp.zeros_like(ht1_scr)
        g0_scr[...] = jnp.zeros_like(g0_scr)
        g1_scr[...] = jnp.zeros_like(g1_scr)

    @pl.when((p2 <= 0) | (c2 == 0))
    def _():
        acc_scr[...] = jnp.zeros_like(acc_scr)

    def step(ht_new, ht_old, g_new, g_old):
        for tc in range(t // LANES):
            ls = slice(tc * LANES, (tc + 1) * LANES)
            for a0 in range(0, n_a, _PEER_A_GROUP):
                u_inv = uinv_ref[gate_chunk * (_PEER_EC // _QUANT_ROWS) + a0 * PEER_NKEYS // _QUANT_ROWS]
                half_inv = (0.5 * u_inv) * inv_ref[:, ls]
                c_inv = (_INV_SQRT2 * u_inv) * inv_ref[:, ls]
                w = [[None] * (PEER_NKEYS // pack) for _ in range(_PEER_A_GROUP)]
                for hd in range(PEER_HEADS):
                    rows1 = [(_as_bf16_rows(jnp.broadcast_to(cnt_ref[hd, a0 + i:a0 + i + 1, ls], (8, LANES))),
                              _as_bf16_rows(jnp.broadcast_to(e1_ref[hd, a0 + i:a0 + i + 1, ls], (8, LANES))))
                             for i in range(_PEER_A_GROUP)]
                    for r in range(PEER_NKEYS // pack):
                        rk = _as_bf16_rows(rk_ref[hd, r * 8:(r + 1) * 8, ls])
                        e2 = _as_bf16_rows(e2_ref[hd, r * 8:(r + 1) * 8, ls])
                        for i, (cn, e1) in enumerate(rows1):
                            term = jnp.where(rk < cn, e2, 0.0) * e1
                            w[i][r] = term if w[i][r] is None else w[i][r] + term
                for i in range(_PEER_A_GROUP):
                    for r in range(PEER_NKEYS // pack):
                        row0 = (a0 + i) * PEER_NKEYS + r * pack
                        hs = ht_old[row0:row0 + pack, ls]
                        act = (hs * half_inv) * (1.0 + lax.erf(hs * c_inv))
                        g_new[row0 // 2:(row0 + pack) // 2, ls] = pltpu.bitcast(w[i][r] * act.astype(BF16), jnp.uint32)
        ht_new[...] = _dot_nt(u_ref[...], x_ref[...])
        acc_scr[...] += jnp.dot(_as_bf16_rows(vt_ref[...]), _as_bf16_rows(g_old[...]),
                                preferred_element_type=F32)

    @pl.when(s % 2 == 0)
    def _():
        step(ht0_scr, ht1_scr, g1_scr, g0_scr)

    @pl.when(s % 2 == 1)
    def _():
        step(ht1_scr, ht0_scr, g0_scr, g1_scr)

    @pl.when((p2 >= 0) & (c2 == n_chunks - 1))
    def _():
        y = h_ref[...] + g2_ref[0] * acc_scr[...].T
        if final_norm:
            y = (y * lax.rsqrt(jnp.mean(y * y, axis=-1, keepdims=True) + RMS_EPS)) * fg_ref[...]
        o_ref[...] = y


def _peer_experts(x8, u8, inv_row, u_inv, vt_pk, cnt, e1, rk, e2, h2d, seq_len, g2, final_g, final_norm):
    n, d = h2d.shape
    t = 512
    n_chunks = u8.shape[0] // _PEER_EC
    total = (n // t) * n_chunks
    ea = _PEER_EC // PEER_NKEYS

    def pair(p):
        p = jnp.clip(p, 0, total - 1)
        return p // n_chunks, lax.rem(p, n_chunks)

    blk = lambda lag: (lambda s: pair(s - lag)[0])
    chk = lambda lag: (lambda s: pair(s - lag)[1])
    tok = pl.BlockSpec((PEER_HEADS, PEER_NKEYS // 2, t), lambda s: (0, 0, blk(1)(s)))
    first = pl.BlockSpec((PEER_HEADS, ea, t), lambda s: (0, chk(1)(s), blk(1)(s)))
    bm = _batch_map(t, seq_len, g2.shape[0])
    return pl.pallas_call(
        functools.partial(_peer_expert_kernel, n_chunks, total, final_norm),
        grid=(total + 2,),
        in_specs=[
            pl.BlockSpec((t, d), lambda s: (blk(0)(s), 0)),
            pl.BlockSpec((_PEER_EC, d), lambda s: (chk(0)(s), 0)),
            pl.BlockSpec((1, t), lambda s: (0, blk(1)(s))),
            pl.BlockSpec(memory_space=pltpu.SMEM),
            pl.BlockSpec((d // 2, _PEER_EC), lambda s: (0, chk(2)(s))),
            first, first, tok, tok,
            pl.BlockSpec((t, d), lambda s: (blk(2)(s), 0)),
            pl.BlockSpec((1, 1, d), lambda s: bm(blk(2)(s))),
            pl.BlockSpec((1, d), lambda s: (0, 0)),
        ],
        out_specs=pl.BlockSpec((t, d), lambda s: (blk(2)(s), 0)),
        out_shape=jax.ShapeDtypeStruct((n, d), F32),
        scratch_shapes=[pltpu.VMEM((_PEER_EC, t), F32), pltpu.VMEM((_PEER_EC, t), F32),
                        pltpu.VMEM((_PEER_EC // 2, t), jnp.uint32), pltpu.VMEM((_PEER_EC // 2, t), jnp.uint32),
                        pltpu.VMEM((d, t), F32)],
        compiler_params=_params("arbitrary"),
        name="peer_experts",
    )(x8, u8, inv_row, u_inv, vt_pk, cnt, e1, rk, e2, h2d, g2, final_g.reshape(1, d))


def _pack_rows_kernel(transpose, x_ref, o_ref):
    x = x_ref[0].T if transpose else x_ref[0]
    o_ref[...] = pltpu.bitcast(x.astype(BF16), jnp.uint32)


def _pack_bf16_rows(stack, layer, transpose=False):
    tile = 512
    if transpose:
        _, c, r = stack.shape
        in_spec = pl.BlockSpec((1, tile, r), lambda i: (layer, i, 0))
        out_spec = pl.BlockSpec((r // 2, tile), lambda i: (0, i))
        steps = c // tile
    else:
        _, r, c = stack.shape
        in_spec = pl.BlockSpec((1, tile, c), lambda i: (layer, i, 0))
        out_spec = pl.BlockSpec((tile // 2, c), lambda i: (i, 0))
        steps = r // tile
    return pl.pallas_call(
        functools.partial(_pack_rows_kernel, transpose),
        grid=(steps,),
        in_specs=[in_spec],
        out_specs=out_spec,
        out_shape=jax.ShapeDtypeStruct((r // 2, c), jnp.uint32),
        compiler_params=_params("parallel"),
        name="pack_bf16_rows_t" if transpose else "pack_bf16_rows",
    )(stack)


_QUANT_ROWS = 512


def _quantize_kernel(x_ref, o_ref, inv_ref):
    x = x_ref[0]
    scale = _pow2_scale(jnp.max(jnp.max(jnp.abs(x), axis=0, keepdims=True), axis=1, keepdims=True))
    o_ref[...] = (x * scale).astype(F8)
    inv_ref[0] = jnp.broadcast_to(1.0 / scale, inv_ref.shape[1:])


def _quantize_table(stack, layer):
    _, r, c = stack.shape
    blocks = r // _QUANT_ROWS
    q, inv = pl.pallas_call(
        _quantize_kernel,
        grid=(blocks,),
        in_specs=[pl.BlockSpec((1, _QUANT_ROWS, c), lambda i: (layer, i, 0))],
        out_specs=[pl.BlockSpec((_QUANT_ROWS, c), lambda i: (i, 0)), pl.BlockSpec((1, 8, LANES), lambda i: (i, 0, 0))],
        out_shape=[jax.ShapeDtypeStruct((r, c), F8), jax.ShapeDtypeStruct((blocks, 8, LANES), F32)],
        compiler_params=_params("parallel"),
        name="quantize_table",
    )(stack)
    return q, inv[:, 0, 0]


def _peer_ffn_residual(h2d, seq_len, norm_g, shift, scale, gate, wq_heads, keys_bf, u8, u_inv, vt_pk, final_g,
                       final_norm=False):
    x8, x_inv, cnt, e1, rk, e2 = _peer_scores(h2d, seq_len, norm_g, shift, scale, wq_heads, keys_bf)
    return _peer_experts(x8, u8, x_inv.reshape(1, -1), u_inv, vt_pk, cnt, e1, rk, e2, h2d, seq_len, gate, final_g, final_norm)


def _mixer_inputs(h2d, batch, seq_len, norm_g, shift, scale, w_perm, conv_w, conv_b):
    z, xbc, pool_u, q, k, v, dt = _project(h2d, seq_len, norm_g, shift, scale, w_perm)
    xbc_act = _conv_silu(xbc.reshape(batch, seq_len, SSD_XBC), conv_w, conv_b)
    dt3 = dt.reshape(batch, seq_len, LANES)
    dtt3 = jnp.swapaxes(dt3[:, :, :2 * SSD_HEADS], 1, 2)
    r3 = lambda a: a.reshape(batch, seq_len, a.shape[-1])
    return z, xbc_act, dt3, dtt3, r3(pool_u), r3(q), r3(k), r3(v)


def kernel(x, c, ctx, c_ctx, ada_w, ada_b, norm1_g, w_in, conv_w, conv_b, a_log, dt_bias, d_skip, ssd_norm_g, pool_w, pool_scale, na_rpb, w_out, norm2_g, peer_wq, peer_keys, peer_u, peer_v, final_g):
    batch, seq, d = x.shape
    ctx_len = ctx.shape[1]
    n, nc = batch * seq, batch * ctx_len
    h = x.reshape(n, d)
    hc = ctx.reshape(nc, d)

    c8 = jnp.concatenate([c, c_ctx[None], jnp.zeros((8 - batch - 1, d), F32)], axis=0)
    mod = _modulation(c8, ada_w, ada_b)

    for i in range(DEPTH):
        need_ctx_out = i < DEPTH - 1
        lat = [mod[i, :batch, j * d:(j + 1) * d].reshape(batch, 1, d) for j in range(6)]
        cx = [mod[i, batch:batch + 1, j * d:(j + 1) * d].reshape(1, 1, d) for j in range(6)]
        sh1, sc1, g1, sh2, sc2, g2 = lat
        csh1, csc1, cg1, csh2, csc2, cg2 = cx

        w_perm = _permute_w_in(w_in[i])
        w_out_bf = w_out[i].astype(BF16)
        n1 = norm1_g[i].reshape(1, d)
        n2 = norm2_g[i].reshape(1, d)
        zero_state = jnp.zeros((batch, SSD_HEADS // 2, SSD_STATE, 2 * SSD_HEAD_DIM), F32)
        scan = functools.partial(_ssd_scan, dtb=dt_bias[i], alog=a_log[i])
        wq_heads = peer_wq[i].reshape(d, PEER_HEADS, 2 * PEER_SUB).transpose(1, 0, 2).astype(BF16)
        keys_bf = peer_keys[i].astype(BF16)
        u8, u_inv = _quantize_table(peer_u, i)
        vt_pk = _pack_bf16_rows(peer_v, i, transpose=True)

        zc, xbc_c, dt3_c, dtt3_c, pool_c, qc, kc, vc = _mixer_inputs(
            hc, batch, ctx_len, n1, csh1, csc1, w_perm, conv_w[i], conv_b[i])
        yf_c, yb_c, st_f, st_b = scan(xbc_c, dt3_c, dtt3_c, init_f=zero_state, init_b=zero_state)
        if need_ctx_out:
            pool_yc = _pool_mixer(pool_c, pool_w[i], pool_scale[i])
            att_c = _context_attention(qc, kc, vc)
            hc = _mix_out(yf_c.reshape(nc, -1), yb_c.reshape(nc, -1), xbc_c.reshape(nc, -1), zc,
                          pool_yc.reshape(nc, -1), att_c.reshape(nc, -1), hc, ctx_len,
                          d_skip[i], ssd_norm_g[i], cg1, w_out_bf)
            hc = _peer_ffn_residual(hc, ctx_len, n2, csh2, csc2, cg2, wq_heads, keys_bf, u8, u_inv, vt_pk, final_g)

        z, xbc_l, dt3_l, dtt3_l, pool_l, q, k, v = _mixer_inputs(
            h, batch, seq, n1, sh1, sc1, w_perm, conv_w[i], conv_b[i])
        yf, yb, _, _ = scan(xbc_l, dt3_l, dtt3_l, init_f=st_f, init_b=st_b)
        pool_y = _pool_mixer(pool_l, pool_w[i], pool_scale[i])
        na = _neighbourhood_attention(q, k, v, kc, vc, _na_bias(na_rpb[i]))
        h = _mix_out(yf.reshape(n, -1), yb.reshape(n, -1), xbc_l.reshape(n, -1), z,
                     pool_y.reshape(n, -1), na.reshape(n, -1), h, seq,
                     d_skip[i], ssd_norm_g[i], g1, w_out_bf)
        h = _peer_ffn_residual(h, seq, n2, sh2, sc2, g2, wq_heads, keys_bf, u8, u_inv, vt_pk, final_g,
                               final_norm=i == DEPTH - 1)

    return h.reshape(batch, seq, d)
```

```python
import functools
import math

import jax
import jax.numpy as jnp
from jax import lax
from jax.experimental import pallas as pl
from jax.experimental.pallas import tpu as pltpu

F32 = jnp.float32
BF16 = jnp.bfloat16
F8 = jnp.float8_e4m3fn
HIGHEST = lax.Precision.HIGHEST

D_MODEL = 1024
DEPTH = 2
GRID_W = 64
RMS_EPS = 1e-6

SSD_HEAD_DIM = 64
SSD_HEADS = 16
SSD_GROUPS = 2
SSD_STATE = 128
SSD_CHUNK = 128
SSD_WIDTH = SSD_HEADS * SSD_HEAD_DIM
SSD_XBC = SSD_WIDTH + 2 * SSD_GROUPS * SSD_STATE

POOL_WINDOWS = (2, 4, 8, 16)
POOL_GROUP_DIM = 128
POOL_WIDTH = POOL_GROUP_DIM * len(POOL_WINDOWS)
POOL_PAD = 8

NA_HEAD_DIM = 64
NA_HEADS = 8
NA_WIDTH = NA_HEADS * NA_HEAD_DIM
NA_KH = 8
NA_KW = 16

PEER_HEADS = 8
PEER_NKEYS = 128
PEER_TOPK = 16
PEER_SUB = 128

LANES = 128
VMEM_LIMIT_BYTES = 56 * 1024 * 1024

_PROJ_SEGS = (SSD_WIDTH, SSD_XBC, POOL_WIDTH, NA_WIDTH, NA_WIDTH, NA_WIDTH, LANES)
_PROJ_DTYPES = (F32, F32, F32, BF16, BF16, BF16, F32)


def _params(*sem):
    return pltpu.CompilerParams(dimension_semantics=sem, vmem_limit_bytes=VMEM_LIMIT_BYTES)


def _rms_mod(x, g, scale, shift):
    ms = jnp.mean(x * x, axis=-1, keepdims=True)
    return (x * lax.rsqrt(ms + RMS_EPS)) * g * (1.0 + scale) + shift


def _silu(x):
    return x * jax.nn.sigmoid(x)


def _softplus(x):
    return jnp.maximum(x, 0.0) + jnp.log1p(jnp.exp(-jnp.abs(x)))


def _dot_nt(a, b):
    return lax.dot_general(a, b, (((1,), (1,)), ((), ())), preferred_element_type=F32)


def _batch_map(block_rows, seq_len, n_rows):
    if n_rows == 1:
        return lambda i, *_: (0, 0, 0)
    return lambda i, *_: ((i * block_rows) // seq_len, 0, 0)


def _mod_kernel(c_ref, w_ref, b_ref, o_ref):
    s = _silu(c_ref[...])
    o_ref[0] = jnp.dot(s, w_ref[0], precision=HIGHEST, preferred_element_type=F32) + b_ref[0]


def _modulation(c8, ada_w, ada_b):
    depth, d, six_d = ada_w.shape
    tn = 1024
    return pl.pallas_call(
        _mod_kernel,
        grid=(depth, six_d // tn),
        in_specs=[
            pl.BlockSpec((8, d), lambda l, j: (0, 0)),
            pl.BlockSpec((1, d, tn), lambda l, j: (l, 0, j)),
            pl.BlockSpec((1, 1, tn), lambda l, j: (l, 0, j)),
        ],
        out_specs=pl.BlockSpec((1, 8, tn), lambda l, j: (l, 0, j)),
        out_shape=jax.ShapeDtypeStruct((depth, 8, six_d), F32),
        compiler_params=_params("parallel", "parallel"),
        name="adaln_mod",
    )(c8, ada_w, ada_b.reshape(depth, 1, six_d))


def _proj_kernel(h_ref, g_ref, sh_ref, sc_ref, w_ref, *out_refs):
    u = _rms_mod(h_ref[...], g_ref[...], sc_ref[0], sh_ref[0]).astype(BF16)
    off = 0
    for o_ref, width in zip(out_refs, _PROJ_SEGS):
        o_ref[...] = jnp.dot(u, w_ref[:, off:off + width], preferred_element_type=F32).astype(o_ref.dtype)
        off += width


def _project(h2d, seq_len, g, shift, scale, w_perm):
    n, d = h2d.shape
    t = 512
    total = sum(_PROJ_SEGS)
    nb = shift.shape[0]
    assert n % t == 0 and (nb == 1 or seq_len % t == 0)
    row_map = lambda i: (i, 0)
    return pl.pallas_call(
        _proj_kernel,
        grid=(n // t,),
        in_specs=[
            pl.BlockSpec((t, d), row_map),
            pl.BlockSpec((1, d), lambda i: (0, 0)),
            pl.BlockSpec((1, 1, d), _batch_map(t, seq_len, nb)),
            pl.BlockSpec((1, 1, d), _batch_map(t, seq_len, nb)),
            pl.BlockSpec((d, total), lambda i: (0, 0)),
        ],
        out_specs=[pl.BlockSpec((t, w), row_map) for w in _PROJ_SEGS],
        out_shape=[jax.ShapeDtypeStruct((n, w), dt) for w, dt in zip(_PROJ_SEGS, _PROJ_DTYPES)],
        compiler_params=_params("parallel"),
        name="in_proj",
    )(h2d, g, shift, scale, w_perm)


def _permute_w_in(w_in):
    o = 0
    z = w_in[:, o:o + SSD_WIDTH]; o += SSD_WIDTH
    xbc = w_in[:, o:o + SSD_XBC]; o += SSD_XBC
    dt = w_in[:, o:o + 2 * SSD_HEADS]; o += 2 * SSD_HEADS
    rest = w_in[:, o:]
    dt = jnp.pad(dt, ((0, 0), (0, LANES - 2 * SSD_HEADS)))
    return jnp.concatenate([z, xbc, rest, dt], axis=1).astype(BF16)


def _conv_kernel(x_ref, w_ref, b_ref, o_ref):
    x = x_ref[0]
    n = x.shape[0]
    row = lax.broadcasted_iota(jnp.int32, x.shape, 0)
    prev = jnp.where(row == 0, 0.0, pltpu.roll(x, 1, 0))
    nxt = jnp.where(row == n - 1, 0.0, pltpu.roll(x, n - 1, 0))
    y = prev * w_ref[0:1, :] + x * w_ref[1:2, :] + nxt * w_ref[2:3, :] + b_ref[...]
    o_ref[0] = _silu(y)


def _conv_silu(xbc3, conv_w, conv_b):
    b, l, c = xbc3.shape
    tc = 512
    return pl.pallas_call(
        _conv_kernel,
        grid=(b, c // tc),
        in_specs=[
            pl.BlockSpec((1, l, tc), lambda i, j: (i, 0, j)),
            pl.BlockSpec((3, tc), lambda i, j: (0, j)),
            pl.BlockSpec((1, tc), lambda i, j: (0, j)),
        ],
        out_specs=pl.BlockSpec((1, l, tc), lambda i, j: (i, 0, j)),
        out_shape=jax.ShapeDtypeStruct((b, l, c), F32),
        compiler_params=_params("parallel", "parallel"),
        name="dwconv_silu",
    )(xbc3, conv_w, conv_b.reshape(1, c))


def _ssd_chunk(reverse, xbc_ref, dt_ref, dtt_ref, dtb_row, dtb_col, alog_row, alog_col, y_ref, state_scr):
    q = SSD_CHUNK
    col0 = SSD_HEADS if reverse else 0
    dt_l = _softplus(dt_ref[0] + dtb_row[...])
    a_l = dt_l * (-jnp.exp(alog_row[...]))
    dt_t = _softplus(dtt_ref[0] + dtb_col[...])
    a_t = dt_t * (-jnp.exp(alog_col[...]))
    row = lax.broadcasted_iota(jnp.int32, (q, q), 0)
    col = lax.broadcasted_iota(jnp.int32, (q, q), 1)
    lower = (row >= col).astype(F32)
    upper = (row <= col).astype(F32)
    cs_l = jnp.dot(lower, a_l, precision=HIGHEST, preferred_element_type=F32)
    cs_t = jnp.dot(a_t, upper, precision=HIGHEST, preferred_element_type=F32)
    tot_l = cs_l[q - 1:q, :]
    if reverse:
        p_l, p_t = cs_l - a_l, cs_t - a_t
        tri = row <= col
    else:
        p_l, p_t = cs_l, cs_t
        tri = row >= col

    tot_t = cs_t[:, q - 1:q]
    to_end_t = jnp.exp(p_t) if reverse else jnp.exp(tot_t - p_t)
    w_t = dt_t * to_end_t
    left = lax.broadcasted_iota(jnp.int32, (q, 2 * SSD_HEAD_DIM), 1) < SSD_HEAD_DIM

    xbc = xbc_ref[0]
    heads_per_group = SSD_HEADS // SSD_GROUPS
    for g in range(SSD_GROUPS):
        b_g = xbc[:, SSD_WIDTH + g * SSD_STATE:SSD_WIDTH + (g + 1) * SSD_STATE]
        c_g = xbc[:, SSD_WIDTH + (SSD_GROUPS + g) * SSD_STATE:SSD_WIDTH + (SSD_GROUPS + g + 1) * SSD_STATE]
        c_bf = c_g.astype(BF16)
        cb = _dot_nt(c_bf, b_g.astype(BF16))
        bt = b_g.T
        for pair in range(g * heads_per_group // 2, (g + 1) * heads_per_group // 2):
            lanes = slice(pair * 2 * SSD_HEAD_DIM, (pair + 1) * 2 * SSD_HEAD_DIM)
            xp = xbc[:, lanes]
            x_bd = jnp.concatenate([jnp.where(left, xp, 0.0), jnp.where(left, 0.0, xp)], axis=0).astype(BF16)
            decay_tiles, state_tiles, pcol_tiles = [], [], []
            for k in (col0 + 2 * pair, col0 + 2 * pair + 1):
                pcol_b = jnp.broadcast_to(p_l[:, k:k + 1], (q, q))
                prow = p_t[k:k + 1, :]
                seg = (prow - pcol_b) if reverse else (pcol_b - prow)
                lmat = jnp.exp(jnp.where(tri, seg, -jnp.inf)) * dt_t[k:k + 1, :]
                decay_tiles.append((cb * lmat).astype(BF16))
                state_tiles.append((bt * w_t[k:k + 1, :]).astype(BF16))
                pcol_tiles.append(pcol_b)
            k0 = col0 + 2 * pair
            tot_pair = jnp.where(left[0:1], tot_l[:, k0:k0 + 1], tot_l[:, k0 + 1:k0 + 2])
            pcol_pair = jnp.where(left, pcol_tiles[0], pcol_tiles[1])
            in_decay = jnp.exp(tot_pair - pcol_pair) if reverse else jnp.exp(pcol_pair)
            s_prev = state_scr[pair]
            y_diag = jnp.dot(jnp.concatenate(decay_tiles, axis=1), x_bd, preferred_element_type=F32)
            y_off = jnp.dot(c_bf, s_prev.astype(BF16), preferred_element_type=F32) * in_decay
            state_scr[pair] = jnp.exp(tot_pair) * s_prev + jnp.dot(jnp.concatenate(state_tiles, axis=1), x_bd,
                                                                   preferred_element_type=F32)
            y_ref[0, :, lanes] = y_diag + y_off


def _ssd_kernel(xf_ref, dtf_ref, dttf_ref, xb_ref, dtb_ref, dttb_ref, dtb_row, dtb_col, alog_row, alog_col,
                initf_ref, initb_ref, yf_ref, yb_ref, finf_ref, finb_ref, statef_scr, stateb_scr):
    c = pl.program_id(1)

    @pl.when(c == 0)
    def _():
        statef_scr[...] = initf_ref[0]
        stateb_scr[...] = initb_ref[0]

    params = (dtb_row, dtb_col, alog_row, alog_col)
    _ssd_chunk(False, xf_ref, dtf_ref, dttf_ref, *params, yf_ref, statef_scr)
    _ssd_chunk(True, xb_ref, dtb_ref, dttb_ref, *params, yb_ref, stateb_scr)

    @pl.when(c == pl.num_programs(1) - 1)
    def _():
        finf_ref[0] = statef_scr[...]
        finb_ref[0] = stateb_scr[...]


def _ssd_scan(xbc_act, dt3, dtt3, dtb, alog, init_f, init_b):
    b, l, _ = xbc_act.shape
    nc = l // SSD_CHUNK
    dtb_row = jnp.pad(dtb.reshape(1, -1), ((0, 0), (0, LANES - 2 * SSD_HEADS)))
    alog_row = jnp.pad(alog.reshape(1, -1), ((0, 0), (0, LANES - 2 * SSD_HEADS)))
    small = lambda shape: pl.BlockSpec(shape, lambda i, c: (0, 0))
    st_shape = (b, SSD_HEADS // 2, SSD_STATE, 2 * SSD_HEAD_DIM)
    st_spec = pl.BlockSpec((1,) + st_shape[1:], lambda i, c: (i, 0, 0, 0))
    fwd, bwd = (lambda c: c), (lambda c: nc - 1 - c)
    chunk_specs = lambda cm: [
        pl.BlockSpec((1, SSD_CHUNK, SSD_XBC), lambda i, c: (i, cm(c), 0)),
        pl.BlockSpec((1, SSD_CHUNK, LANES), lambda i, c: (i, cm(c), 0)),
        pl.BlockSpec((1, 2 * SSD_HEADS, SSD_CHUNK), lambda i, c: (i, 0, cm(c))),
    ]
    y_spec = lambda cm: pl.BlockSpec((1, SSD_CHUNK, SSD_WIDTH), lambda i, c: (i, cm(c), 0))
    y_shape = jax.ShapeDtypeStruct((b, l, SSD_WIDTH), F32)
    return pl.pallas_call(
        _ssd_kernel,
        grid=(b, nc),
        in_specs=chunk_specs(fwd) + chunk_specs(bwd)
        + [small((1, LANES)), small((2 * SSD_HEADS, 1)), small((1, LANES)), small((2 * SSD_HEADS, 1)), st_spec, st_spec],
        out_specs=[y_spec(fwd), y_spec(bwd), st_spec, st_spec],
        out_shape=[y_shape, y_shape, jax.ShapeDtypeStruct(st_shape, F32), jax.ShapeDtypeStruct(st_shape, F32)],
        scratch_shapes=[pltpu.VMEM(st_shape[1:], F32), pltpu.VMEM(st_shape[1:], F32)],
        compiler_params=_params("parallel", "arbitrary"),
        name="ssd_scan",
    )(xbc_act, dt3, dtt3, xbc_act, dt3, dtt3, dtb_row, dtb.reshape(-1, 1), alog_row, alog.reshape(-1, 1),
      init_f, init_b)


def _pool_kernel(x_ref, w_ref, sc_ref, o_ref, pad_scr):
    n = x_ref.shape[1]
    zeros = jnp.zeros((POOL_PAD, POOL_GROUP_DIM), F32)
    pad_scr[0:POOL_PAD, :] = zeros
    pad_scr[n + POOL_PAD:n + 2 * POOL_PAD, :] = zeros
    t = lax.broadcasted_iota(jnp.int32, (n, 1), 0)
    for gi, w in enumerate(POOL_WINDOWS):
        sl = slice(gi * POOL_GROUP_DIM, (gi + 1) * POOL_GROUP_DIM)
        x = x_ref[0, :, sl]
        pad_scr[POOL_PAD:n + POOL_PAD, :] = x
        acc = jnp.zeros_like(x)
        for o in range(-(w // 2), w - w // 2):
            acc = acc + pad_scr[POOL_PAD + o:POOL_PAD + o + n, :]
        lo = jnp.maximum(t - w // 2, 0)
        hi = jnp.minimum(t + (w - w // 2 - 1), n - 1)
        pooled = acc / (hi - lo + 1).astype(F32) - x
        y = jnp.dot(pooled.astype(BF16), w_ref[gi].astype(BF16), preferred_element_type=F32)
        o_ref[0, :, sl] = (y * sc_ref[:, sl]).astype(o_ref.dtype)


def _pool_mixer(u3, w_pool, scale):
    b, l, c = u3.shape
    return pl.pallas_call(
        _pool_kernel,
        grid=(b,),
        in_specs=[
            pl.BlockSpec((1, l, c), lambda i: (i, 0, 0)),
            pl.BlockSpec(w_pool.shape, lambda i: (0, 0, 0)),
            pl.BlockSpec((1, c), lambda i: (0, 0)),
        ],
        out_specs=pl.BlockSpec((1, l, c), lambda i: (i, 0, 0)),
        out_shape=jax.ShapeDtypeStruct((b, l, c), BF16),
        scratch_shapes=[pltpu.VMEM((l + 2 * POOL_PAD, POOL_GROUP_DIM), F32)],
        compiler_params=_params("parallel"),
        name="pool_mixer",
    )(u3, w_pool, scale.reshape(1, c))


def _na_bias_kernel(rpb_ref, o_ref):
    h = pl.program_id(0)
    qi = lax.broadcasted_iota(jnp.int32, (GRID_W, LANES), 0)
    lane = lax.broadcasted_iota(jnp.int32, (GRID_W, LANES), 1)
    ki = lane % GRID_W
    second = lane >= GRID_W
    start = jnp.clip(qi - NA_KW // 2, 0, GRID_W - NA_KW)
    in_window = (ki >= start) & (ki < start + NA_KW)
    dc = jnp.clip(ki - qi, -(NA_KW - 1), NA_KW - 1) + NA_KW - 1
    for dr in range(o_ref.shape[1]):
        val = jnp.zeros((GRID_W, LANES), F32)
        for j in range(2 * NA_KW - 1):
            pick = jnp.where(second, rpb_ref[h, dr + 1, j], rpb_ref[h, dr, j])
            val = jnp.where(dc == j, pick, val)
        o_ref[0, dr] = jnp.where(in_window, val, -jnp.inf)


def _na_bias(rpb):
    nh, ndr, ndc = rpb.shape
    return pl.pallas_call(
        _na_bias_kernel,
        grid=(nh,),
        in_specs=[pl.BlockSpec(memory_space=pltpu.SMEM)],
        out_specs=pl.BlockSpec((1, ndr - 1, GRID_W, LANES), lambda h: (h, 0, 0, 0)),
        out_shape=jax.ShapeDtypeStruct((nh, ndr - 1, GRID_W, LANES), F32),
        compiler_params=_params("parallel"),
        name="na_bias",
    )(rpb)


def _na_kernel(kh, q_ref, k_ref, v_ref, kc_ref, vc_ref, bias_ref, o_ref):
    r = pl.program_id(1)
    rows = pl.num_programs(1)
    r0 = jnp.clip(r - kh // 2, 0, rows - kh)
    start = pl.multiple_of(r0 * GRID_W, GRID_W)
    kblk = k_ref[0, pl.ds(start, kh * GRID_W), :]
    vblk = v_ref[0, pl.ds(start, kh * GRID_W), :]
    dr0 = r0 - r + NA_KH - 1
    scale = NA_HEAD_DIM ** -0.5
    pair_w = 2 * NA_HEAD_DIM
    halves = []
    for pair in range(NA_HEADS // 2):
        lanes = slice(pair * pair_w, (pair + 1) * pair_w)
        left_q = lax.broadcasted_iota(jnp.int32, (GRID_W, pair_w), 1) < NA_HEAD_DIM
        q_pair = q_ref[0, :, lanes] * scale
        k_pair, kc_pair = kblk[:, lanes], kc_ref[0, :, lanes]
        for side in range(2):
            keep = left_q if side == 0 else jnp.logical_not(left_q)
            q_h = jnp.where(keep, q_pair, 0.0)
            h = 2 * pair + side
            bias = jnp.concatenate([bias_ref[h, dr0 + 2 * j] for j in range(kh // 2)], axis=1)
            halves.append((_dot_nt(q_h, k_pair) + bias, _dot_nt(q_h, kc_pair)))
    probs = []
    for s_loc, s_ctx in halves:
        m = jnp.maximum(jnp.max(s_loc, axis=-1, keepdims=True), jnp.max(s_ctx, axis=-1, keepdims=True))
        p_loc = jnp.exp(s_loc - m)
        p_ctx = jnp.exp(s_ctx - m)
        inv = 1.0 / (jnp.sum(p_loc, axis=-1, keepdims=True) + jnp.sum(p_ctx, axis=-1, keepdims=True))
        probs.append((p_loc.astype(BF16), p_ctx.astype(BF16), inv))
    for pair in range(NA_HEADS // 2):
        lanes = slice(pair * pair_w, (pair + 1) * pair_w)
        v_pair, vc_pair = vblk[:, lanes], vc_ref[0, :, lanes]
        left_v = lax.broadcasted_iota(jnp.int32, v_pair.shape, 1) < NA_HEAD_DIM
        left_c = lax.broadcasted_iota(jnp.int32, vc_pair.shape, 1) < NA_HEAD_DIM
        out = None
        for side in range(2):
            p_loc, p_ctx, inv = probs[2 * pair + side]
            keep_v = left_v if side == 0 else jnp.logical_not(left_v)
            keep_c = left_c if side == 0 else jnp.logical_not(left_c)
            acc = jnp.dot(p_loc, jnp.where(keep_v, v_pair, 0.0), preferred_element_type=F32)
            acc = acc + jnp.dot(p_ctx, jnp.where(keep_c, vc_pair, 0.0), preferred_element_type=F32)
            out = acc * inv if out is None else out + acc * inv
        o_ref[0, :, lanes] = out.astype(o_ref.dtype)


def _neighbourhood_attention(q3, k3, v3, kc3, vc3, bias):
    b, s, c = q3.shape
    rows = s // GRID_W
    kh = min(NA_KH, rows)
    lc = kc3.shape[1]
    full = lambda n: pl.BlockSpec((1, n, c), lambda i, r: (i, 0, 0))
    return pl.pallas_call(
        functools.partial(_na_kernel, kh),
        grid=(b, rows),
        in_specs=[
            pl.BlockSpec((1, GRID_W, c), lambda i, r: (i, r, 0)),
            full(s), full(s), full(lc), full(lc),
            pl.BlockSpec(bias.shape, lambda i, r: (0, 0, 0, 0)),
        ],
        out_specs=pl.BlockSpec((1, GRID_W, c), lambda i, r: (i, r, 0)),
        out_shape=jax.ShapeDtypeStruct((b, s, c), BF16),
        compiler_params=_params("parallel", "arbitrary"),
        name="na_attention",
    )(q3, k3, v3, kc3, vc3, bias)


def _ctx_attn_kernel(q_ref, k_ref, v_ref, o_ref):
    scale = NA_HEAD_DIM ** -0.5
    for h in range(NA_HEADS):
        sl = slice(h * NA_HEAD_DIM, (h + 1) * NA_HEAD_DIM)
        s = _dot_nt(q_ref[0, :, sl] * scale, k_ref[0, :, sl])
        p = jnp.exp(s - jnp.max(s, axis=-1, keepdims=True))
        den = jnp.sum(p, axis=-1, keepdims=True)
        o_ref[0, :, sl] = (jnp.dot(p.astype(BF16), v_ref[0, :, sl], preferred_element_type=F32) / den).astype(o_ref.dtype)


def _context_attention(q3, k3, v3):
    b, l, c = q3.shape
    spec = pl.BlockSpec((1, l, c), lambda i: (i, 0, 0))
    return pl.pallas_call(
        _ctx_attn_kernel,
        grid=(b,),
        in_specs=[spec, spec, spec],
        out_specs=spec,
        out_shape=jax.ShapeDtypeStruct((b, l, c), BF16),
        compiler_params=_params("parallel"),
        name="ctx_attention",
    )(q3, k3, v3)


def _mix_out_kernel(yf_ref, yb_ref, xs_ref, z_ref, pool_ref, na_ref, h_ref, dsk_ref, ng_ref, g1_ref, w_ref, o_ref):
    y = yf_ref[...] + yb_ref[...] + dsk_ref[...] * xs_ref[...]
    y = y * _silu(z_ref[...])
    ms = jnp.mean(y * y, axis=-1, keepdims=True)
    yn = (y * lax.rsqrt(ms + RMS_EPS)) * ng_ref[...]
    mix = jnp.dot(yn.astype(BF16), w_ref[0:SSD_WIDTH, :], preferred_element_type=F32)
    mix = mix + jnp.dot(pool_ref[...].astype(BF16), w_ref[SSD_WIDTH:SSD_WIDTH + POOL_WIDTH, :],
                        preferred_element_type=F32)
    mix = mix + jnp.dot(na_ref[...].astype(BF16), w_ref[SSD_WIDTH + POOL_WIDTH:, :], preferred_element_type=F32)
    o_ref[...] = h_ref[...] + g1_ref[0] * mix


def _mix_out(yf, yb, xbc_act2d, z, pool_y, na_y, h2d, seq_len, d_skip, norm_g, g1, w_out_bf):
    n, d = h2d.shape
    t = 512
    assert n % t == 0 and (g1.shape[0] == 1 or seq_len % t == 0)
    row = lambda w: pl.BlockSpec((t, w), lambda i: (i, 0))
    vec = lambda w: pl.BlockSpec((1, w), lambda i: (0, 0))
    return pl.pallas_call(
        _mix_out_kernel,
        grid=(n // t,),
        in_specs=[
            row(SSD_WIDTH), row(SSD_WIDTH), row(SSD_WIDTH), row(SSD_WIDTH), row(POOL_WIDTH), row(NA_WIDTH), row(d),
            vec(SSD_WIDTH), vec(SSD_WIDTH),
            pl.BlockSpec((1, 1, d), _batch_map(t, seq_len, g1.shape[0])),
            pl.BlockSpec(w_out_bf.shape, lambda i: (0, 0)),
        ],
        out_specs=row(d),
        out_shape=jax.ShapeDtypeStruct((n, d), F32),
        compiler_params=_params("parallel"),
        name="mix_out",
    )(yf, yb, xbc_act2d, z, pool_y, na_y, h2d, jnp.repeat(d_skip, SSD_HEAD_DIM).reshape(1, -1),
      norm_g.reshape(1, -1), g1, w_out_bf)


_CAND_ROWS = 16 + 8 * 7 + 8


def _batcher_pairs(n):
    pairs = []

    def merge(lo, m, r):
        step = 2 * r
        if step < m:
            merge(lo, m, step)
            merge(lo + r, m, step)
            pairs.extend((i, i + r) for i in range(lo + r, lo + m - r, step))
        else:
            pairs.append((lo, lo + r))

    def sort(lo, m):
        if m > 1:
            sort(lo, m // 2)
            sort(lo + m // 2, m // 2)
            merge(lo, m, 1)

    sort(0, n)
    return tuple(pairs)


_SORT16 = _batcher_pairs(PEER_TOPK)
_BITONIC16 = tuple((i, i + d) for d in (8, 4, 2, 1) for i in range(PEER_TOPK) if not i & d)


def _exchange(x, pairs):
    for i, j in pairs:
        x[i], x[j] = jnp.maximum(x[i], x[j]), jnp.minimum(x[i], x[j])


def _sorted_top16(s):
    tiles = []
    for l0 in range(0, s.shape[1], LANES):
        x = [s[8 * v:8 * v + 8, l0:l0 + LANES] for v in range(PEER_TOPK)]
        _exchange(x, _SORT16)
        for shift in (4, 2, 1):
            x = [jnp.maximum(x[i], pltpu.roll(x[PEER_TOPK - 1 - i], shift, 0)) for i in range(PEER_TOPK)]
            _exchange(x, _BITONIC16)
        tiles.append(jnp.concatenate([xi[0:1] for xi in x], axis=0))
    return jnp.concatenate(tiles, axis=1)


def _count_above(s, v):
    r = lambda j: v[j:j + 1]
    pick = jnp.where
    a = s < r(7)
    b = s < pick(a, r(11), r(3))
    c = s < pick(a, pick(b, r(13), r(9)), pick(b, r(5), r(1)))
    d = s < pick(a, pick(b, pick(c, r(14), r(12)), pick(c, r(10), r(8))),
                 pick(b, pick(c, r(6), r(4)), pick(c, r(2), r(0))))
    return (pick(a, 8.0, 0.0) + pick(b, 4.0, 0.0) + pick(c, 2.0, 0.0) + pick(d, 1.0, 0.0)
            + pick(s < r(15), 1.0, 0.0))


def _top16_pair_fast(s1, s2):
    v1, v2 = _sorted_top16(s1), _sorted_top16(s2)
    rank2 = _count_above(s2, v2)

    def check(s, v):
        n = jnp.sum(jnp.where(s >= v[PEER_TOPK - 1:PEER_TOPK], 1.0, 0.0), axis=0, keepdims=True)
        strict = jnp.min(v[:-1] - v[1:], axis=0, keepdims=True) > 0.0
        return jnp.where(strict, n, 0.0)

    return v1, check(s1, v1), v2, rank2, check(s2, v2)


def _first_counts_fast(s1, v1, sel):
    inf = jnp.inf
    rows_v1 = jnp.concatenate([jnp.broadcast_to(v1[0:1], (PEER_TOPK, v1.shape[1]))]
                              + [jnp.broadcast_to(v1[j:j + 1], (8, v1.shape[1])) for j in range(1, 8)]
                              + [v1[8:16]], axis=0)
    bound = jnp.where(sel > 0.0, rows_v1, inf)
    low = bound[0:8]
    for j in range(1, 8):
        low = jnp.minimum(low, bound[16 + 8 * (j - 1):16 + 8 * j])
    tail = jnp.min(bound[64 + 8:64 + 16], axis=0, keepdims=True)
    t = [jnp.minimum(low[0:1], tail)] + [low[k:k + 1] for k in range(1, 8)]
    pick = jnp.where
    a = s1 >= t[3]
    b = s1 >= pick(a, t[5], t[1])
    c = s1 >= pick(a, pick(b, t[6], t[4]), pick(b, t[2], t[0]))
    cnt = pick(a, 4.0, 0.0) + pick(b, 2.0, 0.0) + pick(c, 1.0, 0.0) + pick(s1 >= t[7], 1.0, 0.0)
    n_high = jnp.sum(sel[8:16], axis=0, keepdims=True)
    return cnt + jnp.where(s1 >= v1[0:1], n_high, 0.0)


def _first_counts_exact(rank1, sel):
    cnt = jnp.zeros(rank1.shape, F32)
    for j in range(8):
        lo = 0 if j == 0 else 16 + 8 * (j - 1)
        n_j = jnp.sum(sel[lo:lo + (16 if j == 0 else 8)], axis=0, keepdims=True)
        cnt = cnt + jnp.where(rank1 == float(j), n_j, 0.0)
    for j in range(8, 16):
        cnt = cnt + jnp.where(rank1 == float(j), sel[64 + j:65 + j], 0.0)
    return cnt


def _top16_pair_exact(s1, s2):
    n, t = s1.shape
    iota = lax.broadcasted_iota(jnp.int32, (n, t), 0).astype(F32)
    row16 = lax.broadcasted_iota(jnp.int32, (PEER_TOPK, t), 0)

    def pick(work, rank, vals, j):
        m = jnp.max(work, axis=0, keepdims=True)
        idx = jnp.min(jnp.where(work == m, iota, float(n)), axis=0, keepdims=True)
        sel = iota == idx
        return (jnp.where(sel, -jnp.inf, work), jnp.where(sel, lax.convert_element_type(j, F32), rank),
                jnp.where(row16 == j, m, vals))

    def body(j, carry):
        a, b = carry
        return pick(*a, j), pick(*b, j)

    start = lambda s: (s, jnp.full((n, t), float(PEER_TOPK), F32), jnp.zeros((PEER_TOPK, t), F32))
    (_, rank1, v1), (_, rank2, v2) = lax.fori_loop(0, PEER_TOPK, body, (start(s1), start(s2)))
    return rank1, v1, rank2, v2


def _select16_fast(cand):
    rows, t = cand.shape
    padded = jnp.concatenate([cand, jnp.full((PEER_NKEYS - rows, t), -jnp.inf, F32)], axis=0)
    m = _sorted_top16(padded)[PEER_TOPK - 1:PEER_TOPK]
    taken = jnp.where(cand >= m, 1.0, 0.0)
    return taken, jnp.sum(taken, axis=0, keepdims=True)


def _select16_exact(cand):
    iota = lax.broadcasted_iota(jnp.int32, cand.shape, 0).astype(F32)

    def body(_, carry):
        work, sel_acc = carry
        m = jnp.max(work, axis=0, keepdims=True)
        idx = jnp.min(jnp.where(work == m, iota, float(_CAND_ROWS)), axis=0, keepdims=True)
        sel = iota == idx
        return jnp.where(sel, -jnp.inf, work), jnp.where(sel, 1.0, sel_acc)

    return lax.fori_loop(0, PEER_TOPK, body, (cand, jnp.zeros(cand.shape, F32)))[1]


def _candidate_sums(v1, v2):
    blocks = [v1[0:1] + v2]
    for j in range(1, 8):
        blocks.append(v1[j:j + 1] + v2[0:8])
    blocks.append(v1[8:16] + v2[0:1])
    return jnp.concatenate(blocks, axis=0)


def _any_not_16(*counts):
    return jnp.max(sum(jnp.abs(c - float(PEER_TOPK)) for c in counts)) > 0.0


def _bf16_bits(x):
    return pltpu.bitcast(x.astype(BF16).astype(F32), jnp.uint32)


def _pack_row_pairs(x, scr):
    n, t = x.shape
    for j in range(t // LANES):
        scr[j] = x[:, j * LANES:(j + 1) * LANES]
    words = []
    for j in range(t // LANES):
        even = scr[j, pl.ds(0, n // 2, stride=2), :]
        odd = scr[j, pl.ds(1, n // 2, stride=2), :]
        words.append((_bf16_bits(even) >> 16) | _bf16_bits(odd))
    return jnp.concatenate(words, axis=1)


def _pack_same(x):
    w = _bf16_bits(x)
    return w | (w >> 16)


_F8_TARGET_EXP = 6


def _pow2_scale(amax):
    bits = pltpu.bitcast(jnp.maximum(amax, 2.0 ** -100), jnp.int32)
    exponent = (bits >> 23) - 127
    return pltpu.bitcast((_F8_TARGET_EXP - exponent + 127) << 23, F32)


def _peer_score_kernel(h_ref, g_ref, sh_ref, sc_ref, wq_ref, wq_next_ref, keys_ref,
                       x_out, xinv_out, cnt_out, e1_out, rk_out, e2_out,
                       u_scr, q_scr, pair_scr, cnt_scr, rank2_scr, top_scr):
    hd = pl.program_id(1)

    @pl.when(hd == 0)
    def _():
        u = _rms_mod(h_ref[...], g_ref[...], sc_ref[0], sh_ref[0])
        u_scr[...] = u.astype(BF16)
        scale = _pow2_scale(jnp.max(jnp.abs(u), axis=-1, keepdims=True))
        x_out[...] = (u * scale).astype(F8)
        xinv_out[...] = 1.0 / scale
        q_scr[...] = jnp.dot(u_scr[...], wq_ref[0], preferred_element_type=F32).astype(BF16)

    q = q_scr[...]
    s1 = _dot_nt(keys_ref[0, 0], q[:, :PEER_SUB])
    s2 = _dot_nt(keys_ref[0, 1], q[:, PEER_SUB:])
    v1, n1, v2, rank2, n2 = _top16_pair_fast(s1, s2)
    cand = _candidate_sums(v1, v2)
    sel, n_sel = _select16_fast(cand)
    cnt_scr[...] = _first_counts_fast(s1, v1, sel)
    rank2_scr[...] = rank2
    top_scr[0:1] = v1[0:1]
    top_scr[1:2] = v2[0:1]
    top_scr[2:3] = jnp.sum(sel * jnp.exp(cand - cand[0:1]), axis=0, keepdims=True)
    q_scr[...] = jnp.dot(u_scr[...], wq_next_ref[0], preferred_element_type=F32).astype(BF16)

    @pl.when(_any_not_16(n1, n2, n_sel))
    def _():
        rank1_x, v1_x, rank2_x, v2_x = _top16_pair_exact(s1, s2)
        cand_x = _candidate_sums(v1_x, v2_x)
        sel_x = _select16_exact(cand_x)
        cnt_scr[...] = _first_counts_exact(rank1_x, sel_x)
        rank2_scr[...] = rank2_x
        top_scr[0:1] = v1_x[0:1]
        top_scr[1:2] = v2_x[0:1]
        top_scr[2:3] = jnp.sum(sel_x * jnp.exp(cand_x - cand_x[0:1]), axis=0, keepdims=True)

    cnt_out[0] = _pack_same(cnt_scr[...])
    e1_out[0] = _pack_same(jnp.exp(s1 - top_scr[0:1]))
    rk_out[0] = _pack_row_pairs(rank2_scr[...], pair_scr)
    e2_out[0] = _pack_row_pairs(jnp.exp(s2 - top_scr[1:2]) / top_scr[2:3], pair_scr)


def _peer_scores(h2d, seq_len, g, shift, scale, wq_heads, keys_bf):
    n, d = h2d.shape
    nb = shift.shape[0]
    t = 512
    assert n % t == 0 and (nb == 1 or seq_len % t == 0)
    bm = _batch_map(t, seq_len, nb)
    mod_spec = pl.BlockSpec((1, 1, d), lambda i, hd: bm(i))
    first_out = pl.BlockSpec((1, PEER_NKEYS, t), lambda i, hd: (hd, 0, i))
    first_shape = jax.ShapeDtypeStruct((PEER_HEADS, PEER_NKEYS, n), jnp.uint32)
    second_out = pl.BlockSpec((1, PEER_NKEYS // 2, t), lambda i, hd: (hd, 0, i))
    second_shape = jax.ShapeDtypeStruct((PEER_HEADS, PEER_NKEYS // 2, n), jnp.uint32)
    return pl.pallas_call(
        _peer_score_kernel,
        grid=(n // t, PEER_HEADS),
        in_specs=[
            pl.BlockSpec((t, d), lambda i, hd: (i, 0)),
            pl.BlockSpec((1, d), lambda i, hd: (0, 0)),
            mod_spec, mod_spec,
            pl.BlockSpec((1, d, 2 * PEER_SUB), lambda i, hd: (hd, 0, 0)),
            pl.BlockSpec((1, d, 2 * PEER_SUB), lambda i, hd: (jnp.minimum(hd + 1, PEER_HEADS - 1), 0, 0)),
            pl.BlockSpec((1, 2, PEER_NKEYS, PEER_SUB), lambda i, hd: (hd, 0, 0, 0)),
        ],
        out_specs=[pl.BlockSpec((t, d), lambda i, hd: (i, 0)), pl.BlockSpec((t, 1), lambda i, hd: (i, 0)),
                   first_out, first_out, second_out, second_out],
        out_shape=[jax.ShapeDtypeStruct((n, d), F8), jax.ShapeDtypeStruct((n, 1), F32),
                   first_shape, first_shape, second_shape, second_shape],
        scratch_shapes=[pltpu.VMEM((t, d), BF16), pltpu.VMEM((t, 2 * PEER_SUB), BF16),
                        pltpu.VMEM((t // LANES, PEER_NKEYS, LANES), F32),
                        pltpu.VMEM((PEER_NKEYS, t), F32), pltpu.VMEM((PEER_NKEYS, t), F32),
                        pltpu.VMEM((8, t), F32)],
        compiler_params=_params("parallel", "arbitrary"),
        name="peer_scores",
    )(h2d, g, shift, scale, wq_heads, wq_heads, keys_bf)


_PEER_EC = 1024
_INV_SQRT2 = 1.0 / math.sqrt(2.0)


def _as_bf16_rows(words):
    return pltpu.bitcast(words, BF16)


_PEER_A_GROUP = 4


def _peer_expert_kernel(n_chunks, total, final_norm, x_ref, u_ref, inv_ref, uinv_ref, vt_ref, cnt_ref, e1_ref, rk_ref,
                        e2_ref, h_ref,
                        g2_ref, fg_ref, o_ref,
                        ht0_scr, ht1_scr, g0_scr, g1_scr, acc_scr):
    s = pl.program_id(0)
    t = h_ref.shape[0]
    p2 = s - 2
    c2 = lax.rem(jnp.maximum(p2, 0), n_chunks)
    pack = 2 * 8
    gate_chunk = lax.rem(jnp.clip(s - 1, 0, total - 1), n_chunks)
    n_a = _PEER_EC // PEER_NKEYS

    @pl.when(s == 0)
    def _():
        ht1_scr[...] = jnp.zeros_like(ht1_scr)
        g0_scr[...] = jnp.zeros_like(g0_scr)
        g1_scr[...] = jnp.zeros_like(g1_scr)

    @pl.when((p2 <= 0) | (c2 == 0))
    def _():
        acc_scr[...] = jnp.zeros_like(acc_scr)

    def step(ht_new, ht_old, g_new, g_old):
        for tc in range(t // LANES):
            ls = slice(tc * LANES, (tc + 1) * LANES)
            for a0 in range(0, n_a, _PEER_A_GROUP):
                u_inv = uinv_ref[gate_chunk * (_PEER_EC // _QUANT_ROWS) + a0 * PEER_NKEYS // _QUANT_ROWS]
                half_inv = (0.5 * u_inv) * inv_ref[:, ls]
                c_inv = (_INV_SQRT2 * u_inv) * inv_ref[:, ls]
                w = [[None] * (PEER_NKEYS // pack) for _ in range(_PEER_A_GROUP)]
                for hd in range(PEER_HEADS):
                    rows1 = [(_as_bf16_rows(jnp.broadcast_to(cnt_ref[hd, a0 + i:a0 + i + 1, ls], (8, LANES))),
                              _as_bf16_rows(jnp.broadcast_to(e1_ref[hd, a0 + i:a0 + i + 1, ls], (8, LANES))))
                             for i in range(_PEER_A_GROUP)]
                    for r in range(PEER_NKEYS // pack):
                        rk = _as_bf16_rows(rk_ref[hd, r * 8:(r + 1) * 8, ls])
                        e2 = _as_bf16_rows(e2_ref[hd, r * 8:(r + 1) * 8, ls])
                        for i, (cn, e1) in enumerate(rows1):
                            term = jnp.where(rk < cn, e2, 0.0) * e1
                            w[i][r] = term if w[i][r] is None else w[i][r] + term
                for i in range(_PEER_A_GROUP):
                    for r in range(PEER_NKEYS // pack):
                        row0 = (a0 + i) * PEER_NKEYS + r * pack
                        hs = ht_old[row0:row0 + pack, ls]
                        act = (hs * half_inv) * (1.0 + lax.erf(hs * c_inv))
                        g_new[row0 // 2:(row0 + pack) // 2, ls] = pltpu.bitcast(w[i][r] * act.astype(BF16), jnp.uint32)
        ht_new[...] = _dot_nt(u_ref[...], x_ref[...])
        acc_scr[...] += jnp.dot(_as_bf16_rows(vt_ref[...]), _as_bf16_rows(g_old[...]),
                                preferred_element_type=F32)

    @pl.when(s % 2 == 0)
    def _():
        step(ht0_scr, ht1_scr, g1_scr, g0_scr)

    @pl.when(s % 2 == 1)
    def _():
        step(ht1_scr, ht0_scr, g0_scr, g1_scr)

    @pl.when((p2 >= 0) & (c2 == n_chunks - 1))
    def _():
        y = h_ref[...] + g2_ref[0] * acc_scr[...].T
        if final_norm:
            y = (y * lax.rsqrt(jnp.mean(y * y, axis=-1, keepdims=True) + RMS_EPS)) * fg_ref[...]
        o_ref[...] = y


def _peer_experts(x8, u8, inv_row, u_inv, vt_pk, cnt, e1, rk, e2, h2d, seq_len, g2, final_g, final_norm):
    n, d = h2d.shape
    t = 512
    n_chunks = u8.shape[0] // _PEER_EC
    total = (n // t) * n_chunks
    ea = _PEER_EC // PEER_NKEYS

    def pair(p):
        p = jnp.clip(p, 0, total - 1)
        return p // n_chunks, lax.rem(p, n_chunks)

    blk = lambda lag: (lambda s: pair(s - lag)[0])
    chk = lambda lag: (lambda s: pair(s - lag)[1])
    tok = pl.BlockSpec((PEER_HEADS, PEER_NKEYS // 2, t), lambda s: (0, 0, blk(1)(s)))
    first = pl.BlockSpec((PEER_HEADS, ea, t), lambda s: (0, chk(1)(s), blk(1)(s)))
    bm = _batch_map(t, seq_len, g2.shape[0])
    return pl.pallas_call(
        functools.partial(_peer_expert_kernel, n_chunks, total, final_norm),
        grid=(total + 2,),
        in_specs=[
            pl.BlockSpec((t, d), lambda s: (blk(0)(s), 0)),
            pl.BlockSpec((_PEER_EC, d), lambda s: (chk(0)(s), 0)),
            pl.BlockSpec((1, t), lambda s: (0, blk(1)(s))),
            pl.BlockSpec(memory_space=pltpu.SMEM),
            pl.BlockSpec((d // 2, _PEER_EC), lambda s: (0, chk(2)(s))),
            first, first, tok, tok,
            pl.BlockSpec((t, d), lambda s: (blk(2)(s), 0)),
            pl.BlockSpec((1, 1, d), lambda s: bm(blk(2)(s))),
            pl.BlockSpec((1, d), lambda s: (0, 0)),
        ],
        out_specs=pl.BlockSpec((t, d), lambda s: (blk(2)(s), 0)),
        out_shape=jax.ShapeDtypeStruct((n, d), F32),
        scratch_shapes=[pltpu.VMEM((_PEER_EC, t), F32), pltpu.VMEM((_PEER_EC, t), F32),
                        pltpu.VMEM((_PEER_EC // 2, t), jnp.uint32), pltpu.VMEM((_PEER_EC // 2, t), jnp.uint32),
                        pltpu.VMEM((d, t), F32)],
        compiler_params=_params("arbitrary"),
        name="peer_experts",
    )(x8, u8, inv_row, u_inv, vt_pk, cnt, e1, rk, e2, h2d, g2, final_g.reshape(1, d))


def _pack_rows_kernel(transpose, x_ref, o_ref):
    x = x_ref[0].T if transpose else x_ref[0]
    o_ref[...] = pltpu.bitcast(x.astype(BF16), jnp.uint32)


def _pack_bf16_rows(stack, layer, transpose=False):
    tile = 512
    if transpose:
        _, c, r = stack.shape
        in_spec = pl.BlockSpec((1, tile, r), lambda i: (layer, i, 0))
        out_spec = pl.BlockSpec((r // 2, tile), lambda i: (0, i))
        steps = c // tile
    else:
        _, r, c = stack.shape
        in_spec = pl.BlockSpec((1, tile, c), lambda i: (layer, i, 0))
        out_spec = pl.BlockSpec((tile // 2, c), lambda i: (i, 0))
        steps = r // tile
    return pl.pallas_call(
        functools.partial(_pack_rows_kernel, transpose),
        grid=(steps,),
        in_specs=[in_spec],
        out_specs=out_spec,
        out_shape=jax.ShapeDtypeStruct((r // 2, c), jnp.uint32),
        compiler_params=_params("parallel"),
        name="pack_bf16_rows_t" if transpose else "pack_bf16_rows",
    )(stack)


_QUANT_ROWS = 512


def _quantize_kernel(x_ref, o_ref, inv_ref):
    x = x_ref[0]
    scale = _pow2_scale(jnp.max(jnp.max(jnp.abs(x), axis=0, keepdims=True), axis=1, keepdims=True))
    o_ref[...] = (x * scale).astype(F8)
    inv_ref[0] = jnp.broadcast_to(1.0 / scale, inv_ref.shape[1:])


def _quantize_table(stack, layer):
    _, r, c = stack.shape
    blocks = r // _QUANT_ROWS
    q, inv = pl.pallas_call(
        _quantize_kernel,
        grid=(blocks,),
        in_specs=[pl.BlockSpec((1, _QUANT_ROWS, c), lambda i: (layer, i, 0))],
        out_specs=[pl.BlockSpec((_QUANT_ROWS, c), lambda i: (i, 0)), pl.BlockSpec((1, 8, LANES), lambda i: (i, 0, 0))],
        out_shape=[jax.ShapeDtypeStruct((r, c), F8), jax.ShapeDtypeStruct((blocks, 8, LANES), F32)],
        compiler_params=_params("parallel"),
        name="quantize_table",
    )(stack)
    return q, inv[:, 0, 0]


def _peer_ffn_residual(h2d, seq_len, norm_g, shift, scale, gate, wq_heads, keys_bf, u8, u_inv, vt_pk, final_g,
                       final_norm=False):
    x8, x_inv, cnt, e1, rk, e2 = _peer_scores(h2d, seq_len, norm_g, shift, scale, wq_heads, keys_bf)
    return _peer_experts(x8, u8, x_inv.reshape(1, -1), u_inv, vt_pk, cnt, e1, rk, e2, h2d, seq_len, gate, final_g, final_norm)


def _mixer_inputs(h2d, batch, seq_len, norm_g, shift, scale, w_perm, conv_w, conv_b):
    z, xbc, pool_u, q, k, v, dt = _project(h2d, seq_len, norm_g, shift, scale, w_perm)
    xbc_act = _conv_silu(xbc.reshape(batch, seq_len, SSD_XBC), conv_w, conv_b)
    dt3 = dt.reshape(batch, seq_len, LANES)
    dtt3 = jnp.swapaxes(dt3[:, :, :2 * SSD_HEADS], 1, 2)
    r3 = lambda a: a.reshape(batch, seq_len, a.shape[-1])
    return z, xbc_act, dt3, dtt3, r3(pool_u), r3(q), r3(k), r3(v)


def kernel(x, c, ctx, c_ctx, ada_w, ada_b, norm1_g, w_in, conv_w, conv_b, a_log, dt_bias, d_skip, ssd_norm_g, pool_w, pool_scale, na_rpb, w_out, norm2_g, peer_wq, peer_keys, peer_u, peer_v, final_g):
    batch, seq, d = x.shape
    ctx_len = ctx.shape[1]
    n, nc = batch * seq, batch * ctx_len
    h = x.reshape(n, d)
    hc = ctx.reshape(nc, d)

    c8 = jnp.concatenate([c, c_ctx[None], jnp.zeros((8 - batch - 1, d), F32)], axis=0)
    mod = _modulation(c8, ada_w, ada_b)

    for i in range(DEPTH):
        need_ctx_out = i < DEPTH - 1
        lat = [mod[i, :batch, j * d:(j + 1) * d].reshape(batch, 1, d) for j in range(6)]
        cx = [mod[i, batch:batch + 1, j * d:(j + 1) * d].reshape(1, 1, d) for j in range(6)]
        sh1, sc1, g1, sh2, sc2, g2 = lat
        csh1, csc1, cg1, csh2, csc2, cg2 = cx

        w_perm = _permute_w_in(w_in[i])
        w_out_bf = w_out[i].astype(BF16)
        n1 = norm1_g[i].reshape(1, d)
        n2 = norm2_g[i].reshape(1, d)
        zero_state = jnp.zeros((batch, SSD_HEADS // 2, SSD_STATE, 2 * SSD_HEAD_DIM), F32)
        scan = functools.partial(_ssd_scan, dtb=dt_bias[i], alog=a_log[i])
        wq_heads = peer_wq[i].reshape(d, PEER_HEADS, 2 * PEER_SUB).transpose(1, 0, 2).astype(BF16)
        keys_bf = peer_keys[i].astype(BF16)
        u8, u_inv = _quantize_table(peer_u, i)
        vt_pk = _pack_bf16_rows(peer_v, i, transpose=True)

        zc, xbc_c, dt3_c, dtt3_c, pool_c, qc, kc, vc = _mixer_inputs(
            hc, batch, ctx_len, n1, csh1, csc1, w_perm, conv_w[i], conv_b[i])
        yf_c, yb_c, st_f, st_b = scan(xbc_c, dt3_c, dtt3_c, init_f=zero_state, init_b=zero_state)
        if need_ctx_out:
            pool_yc = _pool_mixer(pool_c, pool_w[i], pool_scale[i])
            att_c = _context_attention(qc, kc, vc)
            hc = _mix_out(yf_c.reshape(nc, -1), yb_c.reshape(nc, -1), xbc_c.reshape(nc, -1), zc,
                          pool_yc.reshape(nc, -1), att_c.reshape(nc, -1), hc, ctx_len,
                          d_skip[i], ssd_norm_g[i], cg1, w_out_bf)
            hc = _peer_ffn_residual(hc, ctx_len, n2, csh2, csc2, cg2, wq_heads, keys_bf, u8, u_inv, vt_pk, final_g)

        z, xbc_l, dt3_l, dtt3_l, pool_l, q, k, v = _mixer_inputs(
            h, batch, seq, n1, sh1, sc1, w_perm, conv_w[i], conv_b[i])
        yf, yb, _, _ = scan(xbc_l, dt3_l, dtt3_l, init_f=st_f, init_b=st_b)
        pool_y = _pool_mixer(pool_l, pool_w[i], pool_scale[i])
        na = _neighbourhood_attention(q, k, v, kc, vc, _na_bias(na_rpb[i]))
        h = _mix_out(yf.reshape(n, -1), yb.reshape(n, -1), xbc_l.reshape(n, -1), z,
                     pool_y.reshape(n, -1), na.reshape(n, -1), h, seq,
                     d_skip[i], ssd_norm_g[i], g1, w_out_bf)
        h = _peer_ffn_residual(h, seq, n2, sh2, sc2, g2, wq_heads, keys_bf, u8, u_inv, vt_pk, final_g,
                               final_norm=i == DEPTH - 1)

    return h.reshape(batch, seq, d)
```
